```python
import math
import jax
import jax.numpy as jnp
from jax import lax
import numpy as np

D_MODEL = 1024
BATCH = 8
SEQ = 2048
DEPTH = 2
DEC_BATCH = 128
DEC_SEQ = 4
PAST_LEN = 16384
PAGE_SIZE = 128

D_MIX = D_MODEL
HEAD_DIM = 64
A_HEADS = 6
B_HEADS = 6
C_HEADS = 4
A_WIDTH = A_HEADS * HEAD_DIM
B_WIDTH = B_HEADS * HEAD_DIM
C_WIDTH = C_HEADS * HEAD_DIM
A_DECAY_LORA = 32
A_AAA_LORA = 32
A_GATE_LORA = 64
A_COLS = 3 * A_WIDTH + A_DECAY_LORA + A_AAA_LORA + A_GATE_LORA
A_SPLITS = (A_WIDTH, 2 * A_WIDTH, 3 * A_WIDTH, 3 * A_WIDTH + A_DECAY_LORA, 3 * A_WIDTH + A_DECAY_LORA + A_AAA_LORA)
B_QKV = 3 * B_WIDTH
B_COLS = B_QKV + 2 * B_HEADS + B_WIDTH
C_COLS = 4 * C_WIDTH
IN_COLS = A_COLS + B_COLS + C_COLS
CONV_K = 4
CHUNK = 64
FF_HIDDEN = ((8 * D_MODEL // 3 + 255) // 256) * 256
N_ADA = 6
RMS_EPS = 1e-6
L2_EPS = 1e-6
GN_EPS = 64e-5

kernel_name = 'hybrid_rwkv7_gdn_hgrn2_adaln_step'


def rms_norm(x, w, eps=RMS_EPS):
    xf = x.astype(jnp.float32)
    y = xf * lax.rsqrt(jnp.mean(xf * xf, axis=-1, keepdims=True) + eps)
    return (y * w.astype(jnp.float32)).astype(x.dtype)


def l2_normalize(x, eps=L2_EPS):
    xf = x.astype(jnp.float32)
    return (xf * lax.rsqrt(jnp.sum(xf * xf, axis=-1, keepdims=True) + eps)).astype(x.dtype)


def group_norm_heads(y, w, b, eps=GN_EPS):
    B, T, H, D = y.shape
    mu = jnp.mean(y, axis=-1, keepdims=True)
    var = jnp.mean(jnp.square(y - mu), axis=-1, keepdims=True)
    yn = ((y - mu) * lax.rsqrt(var + eps)).reshape(B, T, H * D)
    return yn * w.astype(jnp.float32) + b.astype(jnp.float32)


def causal_depthwise_conv(x, buf, w):
    T = x.shape[1]
    xp = jnp.concatenate([buf.astype(x.dtype), x], axis=1)
    y = sum(xp[:, j:j + T] * w[j] for j in range(CONV_K))
    return y, xp[:, -(CONV_K - 1):]


def pad_time(a, C):
    pad = (-a.shape[1]) % C
    if pad == 0:
        return a
    widths = [(0, 0)] * a.ndim
    widths[1] = (0, pad)
    return jnp.pad(a, widths)


def to_chunks(a, C):
    B, Tp = a.shape[:2]
    a = a.reshape((B, Tp // C, C) + a.shape[2:])
    return jnp.swapaxes(jnp.swapaxes(a, 2, 3), 0, 1)


def from_chunks(o, T):
    N, B, H, C, D = o.shape
    return jnp.swapaxes(jnp.swapaxes(o, 0, 1), 2, 3).reshape(B, N * C, H, D)[:, :T]


def rwkv7_mixer(p_a, p_prev, S0, mu, w0, w2, a0, a2, g2, k_k, k_a, r_k, ln_w, ln_b):
    B, T, _ = p_a.shape
    f32 = jnp.float32
    prev = jnp.concatenate([p_prev.astype(p_a.dtype), p_a[:, :-1]], axis=1)
    xs = p_a + mu * (prev - p_a)
    r, k, v, xw, xa, xg = jnp.split(xs, A_SPLITS, axis=-1)
    w = -jax.nn.softplus(-(w0 + jnp.tanh(xw) @ w2)) - 0.5
    decay = jnp.exp(-jnp.exp(w.astype(f32)))
    a = jax.nn.sigmoid(a0 + xa @ a2)
    g = jax.nn.sigmoid(xg) @ g2
    hd = lambda t: t.reshape(B, T, A_HEADS, HEAD_DIM)
    kk = l2_normalize(hd(k * k_k))
    k = k * (1 + (a - 1) * k_a)
    r_h, k_h, v_h, a_h = hd(r), hd(k), hd(v), hd(a)
    seq = tuple(jnp.swapaxes(t.astype(f32), 0, 1) for t in (r_h, hd(decay), k_h, v_h, -kk, kk * a_h))

    def step(S, inp):
        rt, dt, kt, vt, at, bt = inp
        sa = jnp.einsum('bhvk,bhk->bhv', S, at)
        S = S * dt[:, :, None, :] + sa[..., None] * bt[:, :, None, :] + vt[..., None] * kt[:, :, None, :]
        return S, jnp.einsum('bhvk,bhk->bhv', S, rt)

    S, y = lax.scan(step, S0.astype(f32), seq)
    y = group_norm_heads(jnp.swapaxes(y, 0, 1), ln_w, ln_b)
    bonus = (jnp.sum(r_h * k_h * r_k, axis=-1, keepdims=True) * v_h).reshape(B, T, A_WIDTH)
    out = (y + bonus.astype(f32)) * g.astype(f32)
    return out.astype(p_a.dtype), S.astype(S0.dtype)


def gated_delta_chunked(q, k, v, g, beta, S0):
    T = q.shape[1]
    C = min(CHUNK, T)
    f32 = jnp.float32
    scale = q.shape[-1] ** -0.5
    qc, kc, vc, gc, bc = (to_chunks(pad_time(t, C), C) for t in
                          (q.astype(f32) * scale, k.astype(f32), v.astype(f32), g.astype(f32), beta.astype(f32)))
    gcum = jnp.cumsum(gc, axis=-1)
    tril = jnp.tril(jnp.ones((C, C), bool))
    strict = jnp.tril(jnp.ones((C, C), bool), -1)
    eye = jnp.eye(C, dtype=f32)
    diff = gcum[..., :, None] - gcum[..., None, :]
    decay = jnp.where(tril, jnp.exp(jnp.where(tril, diff, 0.0)), 0.0)
    kb = kc * bc[..., None]
    A = jnp.where(strict, jnp.einsum('nbhid,nbhjd->nbhij', kb, kc) * decay, 0.0)
    IA = A + eye
    u = lax.linalg.triangular_solve(IA, vc * bc[..., None], left_side=True, lower=True, unit_diagonal=True)
    w = lax.linalg.triangular_solve(IA, kb * jnp.exp(gcum)[..., None], left_side=True, lower=True, unit_diagonal=True)

    def step(S, inp):
        qi, ki, ui, wi, gi, di = inp
        v_new = ui - jnp.einsum('bhcd,bhde->bhce', wi, S)
        o_inter = jnp.einsum('bhcd,bhde->bhce', qi * jnp.exp(gi)[..., None], S)
        qk = jnp.einsum('bhid,bhjd->bhij', qi, ki) * di
        o = o_inter + jnp.einsum('bhij,bhje->bhie', qk, v_new)
        gl = gi[..., -1]
        S = S * jnp.exp(gl)[..., None, None] + jnp.einsum(
            'bhcd,bhce->bhde', ki * jnp.exp(gl[..., None] - gi)[..., None], v_new)
        return S, o

    S, o = lax.scan(step, S0.astype(f32), (qc, kc, u, w, gcum, decay))
    return from_chunks(o, T), S.astype(S0.dtype)


def gated_deltanet_mixer(p_b, conv_buf, S0, conv_w, A_log, dt_bias, norm_w):
    B, T, _ = p_b.shape
    f32 = jnp.float32
    qkv, beta_pre, a_pre, z = jnp.split(p_b, [B_QKV, B_QKV + B_HEADS, B_QKV + 2 * B_HEADS], axis=-1)
    qkv, new_buf = causal_depthwise_conv(qkv, conv_buf, conv_w)
    q, k, v = jnp.split(jax.nn.silu(qkv), 3, axis=-1)
    hd = lambda t: t.reshape(B, T, B_HEADS, HEAD_DIM)
    beta = jax.nn.sigmoid(beta_pre.astype(f32))
    g = -jnp.exp(A_log.astype(f32)) * jax.nn.softplus(a_pre.astype(f32) + dt_bias.astype(f32))
    o, S = gated_delta_chunked(l2_normalize(hd(q)), l2_normalize(hd(k)), hd(v), g, beta, S0)
    o = rms_norm(o, norm_w) * jax.nn.silu(hd(z)).astype(f32)
    return o.reshape(B, T, B_WIDTH).astype(p_b.dtype), new_buf, S


def hgrn2_chunked(q, k, v, logf, S0):
    T = q.shape[1]
    C = min(CHUNK, T)
    f32 = jnp.float32
    qc, kc, vc, fc = (to_chunks(pad_time(t.astype(f32), C), C) for t in (q, k, v, logf))
    bcum = jnp.cumsum(fc, axis=-2)
    tril = jnp.tril(jnp.ones((C, C), bool))[:, :, None]

    def step(S, inp):
        qi, ki, vi, bi = inp
        o_inter = jnp.einsum('bhtd,bhde->bhte', qi * jnp.exp(bi), S)
        diff = bi[:, :, :, None, :] - bi[:, :, None, :, :]
        dec = jnp.where(tril, jnp.exp(jnp.where(tril, diff, 0.0)), 0.0)
        A = jnp.einsum('bhtd,bhsd,bhtsd->bhts', qi, ki, dec)
        o = o_inter + jnp.einsum('bhts,bhse->bhte', A, vi)
        bl = bi[:, :, -1:, :]
        S = jnp.exp(bl[:, :, 0, :])[..., None] * S + jnp.einsum('bhsd,bhse->bhde', ki * jnp.exp(bl - bi), vi)
        return S, o

    S, o = lax.scan(step, S0.astype(f32), (qc, kc, vc, bcum))
    return from_chunks(o, T), S.astype(S0.dtype)


def hgrn2_mixer(p_c, S0, lb, norm_w):
    B, T, _ = p_c.shape
    f32 = jnp.float32
    q, f, i, z = jnp.split(p_c, 4, axis=-1)
    ff = f.astype(f32)
    logf = jnp.log(lb + (1.0 - lb) * jax.nn.sigmoid(ff))
    k = (1.0 - lb) * jax.nn.sigmoid(-ff)
    hd = lambda t: t.reshape(B, T, C_HEADS, HEAD_DIM)
    o, S = hgrn2_chunked(hd(jax.nn.silu(q)), hd(k), hd(i), hd(logf), S0)
    o = rms_norm(o, norm_w) * jax.nn.sigmoid(hd(z)).astype(f32)
    return o.reshape(B, T, C_WIDTH).astype(p_c.dtype), S


def trunk_layer(x, c_act, h_prev, S_a, conv_b, S_b, S_c,
                w_ada, b_ada, norm_mix_w, w_in,
                rwkv_mu, rwkv_w0, rwkv_w2, rwkv_a0, rwkv_a2, rwkv_g2, rwkv_k_k, rwkv_k_a, rwkv_r_k,
                rwkv_ln_w, rwkv_ln_b,
                gdn_conv_w, gdn_A_log, gdn_dt_bias, gdn_norm_w,
                lb, hgrn_norm_w,
                w_out, norm_ffn_w, w_ffn_in, w_ffn_out):
    mod = (c_act @ w_ada + b_ada)[:, None, :]
    shift_m, scale_m, gate_m, shift_f, scale_f, gate_f = jnp.split(mod, N_ADA, axis=-1)
    h = rms_norm(x, norm_mix_w) * (1 + scale_m) + shift_m
    p = h @ w_in
    p_a, p_b, p_c = jnp.split(p, [A_COLS, A_COLS + B_COLS], axis=-1)
    p_prev = (h_prev.astype(h.dtype) @ w_in[:, :A_COLS])[:, None, :]
    y_a, S_a = rwkv7_mixer(p_a, p_prev, S_a, rwkv_mu, rwkv_w0, rwkv_w2, rwkv_a0, rwkv_a2, rwkv_g2,
                           rwkv_k_k, rwkv_k_a, rwkv_r_k, rwkv_ln_w, rwkv_ln_b)
    y_b, conv_b, S_b = gated_deltanet_mixer(p_b, conv_b, S_b, gdn_conv_w, gdn_A_log, gdn_dt_bias, gdn_norm_w)
    y_c, S_c = hgrn2_mixer(p_c, S_c, lb, hgrn_norm_w)
    x = x + gate_m * (jnp.concatenate([y_a, y_b, y_c], axis=-1) @ w_out)
    h2 = rms_norm(x, norm_ffn_w) * (1 + scale_f) + shift_f
    gate, up = jnp.split(h2 @ w_ffn_in, 2, axis=-1)
    x = x + gate_f * ((jax.nn.silu(gate) * up) @ w_ffn_out)
    return x, (h[:, -1].astype(h_prev.dtype), S_a, conv_b.astype(h_prev.dtype), S_b, S_c)


def setup_inputs(seed: int = 0) -> dict:
    key = jax.random.key(seed)
    ks = jax.random.split(key, 36)
    f32 = jnp.float32
    L = DEPTH

    def nrm(i, shape, scale):
        return scale * jax.random.normal(ks[i], shape, f32)

    def uni(i, shape, lo, hi):
        return jax.random.uniform(ks[i], shape, f32, lo, hi)

    dt = jnp.exp(uni(27, (L, B_HEADS), math.log(1e-3), math.log(1e-1)))
    return {
        'x_prompt': nrm(0, (BATCH, SEQ, D_MODEL), 1.0),
        'x_sample': nrm(1, (DEC_BATCH, DEC_SEQ, D_MODEL), 1.0),
        'c_prompt': nrm(2, (BATCH, D_MODEL), 1.0),
        'c_sample': nrm(3, (DEC_BATCH, D_MODEL), 1.0),
        'state_rwkv_shift': nrm(4, (L, DEC_BATCH, D_MODEL), 1.0),
        'state_rwkv': nrm(5, (L, DEC_BATCH, A_HEADS, HEAD_DIM, HEAD_DIM), 0.5),
        'state_gdn_conv': nrm(6, (L, DEC_BATCH, CONV_K - 1, B_QKV), 1.0),
        'state_gdn': nrm(7, (L, DEC_BATCH, B_HEADS, HEAD_DIM, HEAD_DIM), 0.1),
        'state_hgrn': nrm(8, (L, DEC_BATCH, C_HEADS, HEAD_DIM, HEAD_DIM), 0.5),
        'w_ada': nrm(9, (L, D_MODEL, N_ADA * D_MODEL), 0.5 * D_MODEL ** -0.5),
        'b_ada': nrm(10, (L, N_ADA * D_MODEL), 0.01),
        'norm_mix_w': 1.0 + nrm(11, (L, D_MODEL), 0.05),
        'w_in': nrm(12, (L, D_MODEL, IN_COLS), D_MODEL ** -0.5),
        'rwkv_mu': uni(13, (L, A_COLS), 0.0, 1.0),
        'rwkv_w0': -1.0 + nrm(14, (L, A_WIDTH), 0.5),
        'rwkv_w2': nrm(15, (L, A_DECAY_LORA, A_WIDTH), A_DECAY_LORA ** -0.5),
        'rwkv_a0': nrm(16, (L, A_WIDTH), 0.1),
        'rwkv_a2': nrm(17, (L, A_AAA_LORA, A_WIDTH), A_AAA_LORA ** -0.5),
        'rwkv_g2': nrm(18, (L, A_GATE_LORA, A_WIDTH), A_GATE_LORA ** -0.5),
        'rwkv_k_k': 0.85 + nrm(19, (L, A_WIDTH), 0.05),
        'rwkv_k_a': 1.0 + nrm(20, (L, A_WIDTH), 0.05),
        'rwkv_r_k': nrm(21, (L, A_HEADS, HEAD_DIM), 0.1),
        'rwkv_ln_w': 1.0 + nrm(22, (L, A_WIDTH), 0.05),
        'rwkv_ln_b': nrm(23, (L, A_WIDTH), 0.01),
        'gdn_conv_w': nrm(24, (L, CONV_K, B_QKV), CONV_K ** -0.5),
        'gdn_A_log': jnp.log(uni(25, (L, B_HEADS), 1.0, 16.0)),
        'gdn_dt_bias': dt + jnp.log(-jnp.expm1(-dt)),
        'gdn_norm_w': 1.0 + nrm(26, (L, HEAD_DIM), 0.05),
        'hgrn_lb_logits': nrm(28, (L, C_WIDTH), 1.0),
        'hgrn_norm_w': 1.0 + nrm(29, (L, HEAD_DIM), 0.05),
        'w_out': nrm(30, (L, D_MIX, D_MODEL), D_MIX ** -0.5),
        'norm_ffn_w': 1.0 + nrm(31, (L, D_MODEL), 0.05),
        'w_ffn_in': nrm(32, (L, D_MODEL, 2 * FF_HIDDEN), D_MODEL ** -0.5),
        'w_ffn_out': nrm(33, (L, FF_HIDDEN, D_MODEL), FF_HIDDEN ** -0.5),
        'final_norm_w': 1.0 + nrm(34, (D_MODEL,), 0.05),
    }


def reference(x_prompt, x_sample, c_prompt, c_sample,
              state_rwkv_shift, state_rwkv, state_gdn_conv, state_gdn, state_hgrn,
              w_ada, b_ada, norm_mix_w, w_in,
              rwkv_mu, rwkv_w0, rwkv_w2, rwkv_a0, rwkv_a2, rwkv_g2, rwkv_k_k, rwkv_k_a, rwkv_r_k,
              rwkv_ln_w, rwkv_ln_b,
              gdn_conv_w, gdn_A_log, gdn_dt_bias, gdn_norm_w,
              hgrn_lb_logits, hgrn_norm_w,
              w_out, norm_ffn_w, w_ffn_in, w_ffn_out, final_norm_w):
    gam = jax.nn.softmax(hgrn_lb_logits.astype(jnp.float32), axis=0)
    lb_all = jnp.cumsum(gam, axis=0) - gam[0]

    def run_trunk(x, c, shift0, rwkv0, conv0, gdn0, hgrn0):
        c_act = jax.nn.silu(c)
        new = ([], [], [], [], [])
        for l in range(DEPTH):
            x, st = trunk_layer(
                x, c_act, shift0[l], rwkv0[l], conv0[l], gdn0[l], hgrn0[l],
                w_ada[l], b_ada[l], norm_mix_w[l], w_in[l],
                rwkv_mu[l], rwkv_w0[l], rwkv_w2[l], rwkv_a0[l], rwkv_a2[l], rwkv_g2[l],
                rwkv_k_k[l], rwkv_k_a[l], rwkv_r_k[l], rwkv_ln_w[l], rwkv_ln_b[l],
                gdn_conv_w[l], gdn_A_log[l], gdn_dt_bias[l], gdn_norm_w[l],
                lb_all[l], hgrn_norm_w[l],
                w_out[l], norm_ffn_w[l], w_ffn_in[l], w_ffn_out[l])
            for acc, s in zip(new, st):
                acc.append(s)
        y = rms_norm(x, final_norm_w)
        return y, [jnp.stack(acc) for acc in new]

    B = x_prompt.shape[0]
    y_prompt, (p_shift, p_rwkv, p_conv, p_gdn, p_hgrn) = run_trunk(
        x_prompt, c_prompt,
        jnp.zeros((DEPTH, B, D_MODEL), state_rwkv_shift.dtype),
        jnp.zeros((DEPTH, B, A_HEADS, HEAD_DIM, HEAD_DIM), state_rwkv.dtype),
        jnp.zeros((DEPTH, B, CONV_K - 1, B_QKV), state_gdn_conv.dtype),
        jnp.zeros((DEPTH, B, B_HEADS, HEAD_DIM, HEAD_DIM), state_gdn.dtype),
        jnp.zeros((DEPTH, B, C_HEADS, HEAD_DIM, HEAD_DIM), state_hgrn.dtype))
    y_sample, (s_shift, s_rwkv, s_conv, s_gdn, s_hgrn) = run_trunk(
        x_sample, c_sample, state_rwkv_shift, state_rwkv, state_gdn_conv, state_gdn, state_hgrn)
    return (y_prompt, y_sample, p_shift, p_rwkv, p_conv, p_gdn, p_hgrn, s_shift, s_rwkv, s_conv, s_gdn, s_hgrn)
```

```python
import functools

import numpy as np
import jax
import jax.numpy as jnp
from jax import lax
from jax.experimental import pallas as pl
from jax.experimental.pallas import tpu as pltpu

F32 = jnp.float32
BF16 = jnp.bfloat16

HEAD = 64
PAIR = 2 * HEAD
CHUNK = 64
HBLK = 16
CONV_K = 4
RMS_EPS = 1e-6
L2_EPS = 1e-6
GN_EPS = 64e-5
VMEM_LIMIT = 56 * 1024 * 1024

_NN = (((1,), (0,)), ((), ()))
_NT = (((1,), (1,)), ((), ()))
_TN = (((0,), (0,)), ((), ()))


def _dot(a, b, dims=_NN):
    return lax.dot_general(a.astype(BF16), b.astype(BF16), dims, preferred_element_type=F32)


def _dot_f32(a, b, dims=_NN):
    return lax.dot_general(a, b, dims, precision=lax.Precision.HIGHEST, preferred_element_type=F32)


def _dot_sel(x, m, pieces=3):
    acc = None
    rem = x
    for i in range(pieces):
        p = rem.astype(BF16)
        t = lax.dot_general(p, m, _NN, preferred_element_type=F32)
        acc = t if acc is None else acc + t
        if i + 1 < pieces:
            rem = rem - p.astype(F32)
    return acc


def _sigmoid(x):
    return jax.nn.sigmoid(x)


def _softplus(x):
    return jnp.maximum(x, 0.0) + jnp.log1p(jnp.exp(-jnp.abs(x)))


def _iota(shape, dim):
    return lax.broadcasted_iota(jnp.int32, shape, dim)


def _stack_heads(x):
    is_a = _iota((1, PAIR), 1) < HEAD
    return jnp.concatenate([jnp.where(is_a, x, 0.0), jnp.where(is_a, 0.0, x)], axis=0)


def _fold_heads(x, c):
    return x[0:c] + x[c:2 * c]


def _pair_blockdiag_mask():
    return (_iota((PAIR, 1), 0) >> 6) == (_iota((1, PAIR), 1) >> 6)


def _load_pair_state(s_ref, bi, pr):
    sa = s_ref[bi, 2 * pr]
    sb = s_ref[bi, 2 * pr + 1]
    z = jnp.zeros((HEAD, HEAD), F32)
    return jnp.concatenate(
        [jnp.concatenate([sa, z], axis=1), jnp.concatenate([z, sb], axis=1)], axis=0)


def _store_pair_state(s_ref, bi, pr, s):
    s_ref[bi, 2 * pr] = s[0:HEAD, 0:HEAD]
    s_ref[bi, 2 * pr + 1] = s[HEAD:PAIR, HEAD:PAIR]


def _neumann_inverse(n, c):
    size = n.shape[0]
    eye = (_iota((size, 1), 0) == _iota((1, size), 1)).astype(F32)
    inv = eye + n
    pw = n
    span = 2
    while span < c:
        pw = _dot_f32(pw, pw)
        inv = inv + _dot_f32(inv, pw)
        span *= 2
    return inv


def _pad_rows(x, rows):
    if x.shape[0] == rows:
        return x
    return jnp.concatenate([x, jnp.zeros((rows - x.shape[0], x.shape[1]), x.dtype)], axis=0)


def _ada_kernel(c_ref, w_ref, b_ref, o_ref):
    c = c_ref[...]
    o_ref[0] = _dot(c * _sigmoid(c), w_ref[0]) + b_ref[0]


def _ada_call(c_all, w_ada, b_ada):
    nl, d, n6 = w_ada.shape
    rows = c_all.shape[0]
    tn = n6 // 4
    return pl.pallas_call(
        _ada_kernel,
        grid=(nl, n6 // tn),
        in_specs=[
            pl.BlockSpec((rows, d), lambda l, j: (0, 0)),
            pl.BlockSpec((1, d, tn), lambda l, j: (l, 0, j)),
            pl.BlockSpec((1, 1, tn), lambda l, j: (l, 0, j)),
        ],
        out_specs=pl.BlockSpec((1, rows, tn), lambda l, j: (l, 0, j)),
        out_shape=jax.ShapeDtypeStruct((nl, rows, n6), F32),
        compiler_params=pltpu.CompilerParams(
            dimension_semantics=("parallel", "parallel"), vmem_limit_bytes=VMEM_LIMIT),
        name="ada",
    )(c_all, w_ada, b_ada.reshape(nl, 1, n6))


def _norm_mod(x, nw, scale, shift):
    y = x * lax.rsqrt(jnp.mean(x * x, axis=-1, keepdims=True) + RMS_EPS) * nw
    return y * (1.0 + scale) + shift


def _inproj_kernel(x_ref, nw_ref, sc_ref, sh_ref, wa_ref, wb_ref, wc_ref, pa_ref, pb_ref, pc_ref):
    h = _norm_mod(x_ref[...], nw_ref[...], sc_ref[0], sh_ref[0]).astype(BF16)
    pa_ref[...] = jnp.dot(h, wa_ref[...], preferred_element_type=F32)
    pb_ref[...] = jnp.dot(h, wb_ref[...], preferred_element_type=F32)
    pc_ref[...] = jnp.dot(h, wc_ref[...], preferred_element_type=F32)


def _mod_spec(mod, tm, tiles_per_seq):
    if mod.shape[1] == 1:
        return pl.BlockSpec((1, 1, mod.shape[2]), lambda i, *_: (i // tiles_per_seq, 0, 0))
    return pl.BlockSpec((1, tm, mod.shape[2]), lambda i, *_: (i, 0, 0))


def _inproj_call(x, nw, scale, shift, wa, wb, wc, tm, tiles_per_seq):
    rows, d = x.shape
    full = lambda a: pl.BlockSpec(a.shape, lambda i: (0, 0))
    outs = [jax.ShapeDtypeStruct((rows, w.shape[1]), F32) for w in (wa, wb, wc)]
    return pl.pallas_call(
        _inproj_kernel,
        grid=(rows // tm,),
        in_specs=[
            pl.BlockSpec((tm, d), lambda i: (i, 0)),
            full(nw),
            _mod_spec(scale, tm, tiles_per_seq),
            _mod_spec(shift, tm, tiles_per_seq),
            full(wa), full(wb), full(wc),
        ],
        out_specs=[pl.BlockSpec((tm, w.shape[1]), lambda i: (i, 0)) for w in (wa, wb, wc)],
        out_shape=outs,
        compiler_params=pltpu.CompilerParams(
            dimension_semantics=("parallel",), vmem_limit_bytes=VMEM_LIMIT),
        name="inproj",
    )(x, nw, scale, shift, wa, wb, wc)


def _lastrow_kernel(x_ref, nw_ref, sc_ref, sh_ref, hprev_ref, wa_ref, h_ref, pprev_ref):
    h_ref[...] = _norm_mod(x_ref[...], nw_ref[...], sc_ref[...], sh_ref[...])
    pprev_ref[...] = jnp.dot(hprev_ref[...].astype(BF16), wa_ref[...], preferred_element_type=F32)


def _lastrow_call(x_last, nw, scale, shift, h_prev, wa):
    b, d = x_last.shape
    return pl.pallas_call(
        _lastrow_kernel,
        out_shape=[jax.ShapeDtypeStruct((b, d), F32), jax.ShapeDtypeStruct((b, wa.shape[1]), F32)],
        compiler_params=pltpu.CompilerParams(vmem_limit_bytes=VMEM_LIMIT),
        name="lastrow",
    )(x_last, nw, scale, shift, h_prev, wa)


def _rwkv_kernel(p_ref, pprev_ref, s0_ref, mu_ref, vec_ref, wl_ref, g_ref,
                 y_ref, sout_ref, s_scr, prev_scr, *, bt, cin, t_valid):
    c = CHUNK
    ci = pl.program_id(1)
    nci = pl.num_programs(1)
    aw = 3 * PAIR
    rows = _iota((c, 1), 0)
    lane = _iota((1, PAIR), 1)
    vec = vec_ref[...]
    w0, a0, k_k, k_a, r_k, ln_w, ln_b = (vec[i:i + 1] for i in range(7))
    gmat = g_ref[...]
    tri = (rows >= _iota((1, c), 1)).astype(F32)
    r2 = _iota((2 * c, 1), 0) & (c - 1)
    c2 = _iota((1, 2 * c), 1) & (c - 1)
    strict = r2 > c2
    incl = r2 >= c2
    bd = _pair_blockdiag_mask()

    for bi in range(bt):
        @pl.when(ci == 0)
        def _():
            for pr in range(3):
                s_scr[bi, pr] = _load_pair_state(s0_ref, bi, pr)
            prev_scr[bi] = pprev_ref[bi]

        p = _pad_rows(p_ref[bi], c)
        prev = pltpu.roll(p, 1, axis=0)
        prev = jnp.where(rows == 0, prev_scr[bi], prev)
        prev_scr[bi] = p[c - 1:c]
        xs = p + mu_ref[...] * (prev - p)
        r = xs[:, 0:aw]
        k = xs[:, aw:2 * aw]
        v = xs[:, 2 * aw:3 * aw]
        tail = xs[:, 3 * aw:3 * aw + PAIR]
        act = jnp.where(lane < 32, jnp.tanh(tail), jnp.where(lane < 64, tail, _sigmoid(tail)))
        lo = _dot(act, wl_ref[...])
        w = -_softplus(-(w0 + lo[:, 0:aw])) - 0.5
        ld = -jnp.exp(w)
        a = _sigmoid(a0 + lo[:, aw:2 * aw])
        g = lo[:, 2 * aw:3 * aw]
        kk = k * k_k
        kk = kk * lax.rsqrt(_dot_sel(kk * kk, gmat) + L2_EPS)
        k = k * (1.0 + (a - 1.0) * k_a)
        if t_valid < c:
            valid = rows < t_valid
            ld = jnp.where(valid, ld, 0.0)
            kk = jnp.where(valid, kk, 0.0)
            k = jnp.where(valid, k, 0.0)
        cum = _dot_f32(tri, ld)
        cum_last = cum[c - 1:c]
        e_neg = jnp.exp(-cum)
        a_t = -kk * jnp.exp(cum - ld)
        b_t = kk * a * e_neg
        k_t = k * e_neg
        r_t = r * jnp.exp(cum)
        e_end = jnp.exp(cum_last - cum)
        b_end = kk * a * e_end
        k_end = k * e_end
        d_end = jnp.exp(cum_last)

        ys = []
        for pr in range(3):
            sl = slice(PAIR * pr, PAIR * (pr + 1))
            v_st = _stack_heads(v[:, sl])
            l4 = jnp.concatenate([_stack_heads(a_t[:, sl]), _stack_heads(r_t[:, sl])], axis=0)
            r4 = jnp.concatenate([_stack_heads(b_t[:, sl]), _stack_heads(k_t[:, sl])], axis=0)
            gs = _dot(l4, r4, _NT)
            a_ab = jnp.where(strict, gs[0:2 * c, 0:2 * c], 0.0)
            a_ak = jnp.where(strict, gs[0:2 * c, 2 * c:4 * c], 0.0)
            p_rb = jnp.where(incl, gs[2 * c:4 * c, 0:2 * c], 0.0)
            p_rk = jnp.where(incl, gs[2 * c:4 * c, 2 * c:4 * c], 0.0)
            tinv = _neumann_inverse(a_ab, c)
            s = s_scr[bi, pr]
            ars = _dot(jnp.concatenate([a_t[:, sl], r_t[:, sl]], axis=0), s, _NT)
            rhs = _stack_heads(ars[0:c]) + _dot(a_ak, v_st)
            u_st = _dot(tinv, rhs)
            y_st = _dot(jnp.concatenate([p_rb, p_rk], axis=1),
                        jnp.concatenate([u_st, v_st], axis=0))
            ys.append(ars[c:2 * c] + _fold_heads(y_st, c))
            uv = jnp.concatenate([_fold_heads(u_st, c), v[:, sl]], axis=0)
            bk = jnp.concatenate([b_end[:, sl], k_end[:, sl]], axis=0)
            s_new = s * d_end[:, sl] + jnp.where(bd, _dot(uv, bk, _TN), 0.0)
            s_scr[bi, pr] = s_new

            @pl.when(ci == nci - 1)
            def _():
                _store_pair_state(sout_ref, bi, pr, s_new)

        y = jnp.concatenate(ys, axis=1)
        mean = _dot_sel(y, gmat) * (1.0 / HEAD)
        dy = y - mean
        var = _dot_sel(dy * dy, gmat) * (1.0 / HEAD)
        yn = dy * lax.rsqrt(var + GN_EPS) * ln_w + ln_b
        bonus = _dot_sel(r * k * r_k, gmat) * v
        y_ref[bi] = ((yn + bonus) * g)[0:cin]


def _rwkv_call(p_a, p_prev, s0, mu, vec, wl, gmat, *, cin, t_valid, bt):
    b, tp, acols = p_a.shape
    nh = s0.shape[1]
    aw = nh * HEAD
    nc = tp // cin
    const = lambda a: pl.BlockSpec(a.shape, lambda i, j: (0,) * a.ndim)
    kern = functools.partial(_rwkv_kernel, bt=bt, cin=cin, t_valid=t_valid)
    return pl.pallas_call(
        kern,
        grid=(b // bt, nc),
        in_specs=[
            pl.BlockSpec((bt, cin, acols), lambda i, j: (i, j, 0)),
            pl.BlockSpec((bt, 1, acols), lambda i, j: (i, 0, 0)),
            pl.BlockSpec((bt, nh, HEAD, HEAD), lambda i, j: (i, 0, 0, 0)),
            const(mu), const(vec), const(wl), const(gmat),
        ],
        out_specs=[
            pl.BlockSpec((bt, cin, aw), lambda i, j: (i, j, 0)),
            pl.BlockSpec((bt, nh, HEAD, HEAD), lambda i, j: (i, 0, 0, 0)),
        ],
        out_shape=[jax.ShapeDtypeStruct((b, tp, aw), F32),
                   jax.ShapeDtypeStruct(s0.shape, F32)],
        scratch_shapes=[pltpu.VMEM((bt, nh // 2, PAIR, PAIR), F32),
                        pltpu.VMEM((bt, 1, acols), F32)],
        compiler_params=pltpu.CompilerParams(
            dimension_semantics=("parallel", "arbitrary"), vmem_limit_bytes=VMEM_LIMIT),
        name="rwkv",
    )(p_a, p_prev.reshape(b, 1, acols), s0, mu, vec, wl, gmat)


def _gdn_kernel(p_ref, conv0_ref, s0_ref, cw_ref, tp_ref, nw_ref, g_ref, bsel_ref, gsel_ref,
                y_ref, sout_ref, s_scr, cbuf, *, bt, cin, t_valid):
    c = CHUNK
    ci = pl.program_id(1)
    nci = pl.num_programs(1)
    bw = 3 * PAIR
    qkvw = 3 * bw
    rows = _iota((c, 1), 0)
    lane = _iota((1, PAIR), 1)
    nh = 2 * 3
    gmat = g_ref[...]
    tri = (rows >= _iota((1, c), 1)).astype(F32)
    r2 = _iota((2 * c, 1), 0) & (c - 1)
    c2 = _iota((1, 2 * c), 1) & (c - 1)
    same = (_iota((2 * c, 1), 0) >> 6) == (_iota((1, 2 * c), 1) >> 6)
    strict = same & (r2 > c2)
    incl = same & (r2 >= c2)
    bd = _pair_blockdiag_mask()
    is_a = lane < HEAD
    a_log = tp_ref[0:1]
    dt_bias = tp_ref[1:2]
    is_beta = lane < nh
    is_g = (lane >= nh) & (lane < 2 * nh)

    for bi in range(bt):
        @pl.when(ci == 0)
        def _():
            for pr in range(3):
                s_scr[bi, pr] = _load_pair_state(s0_ref, bi, pr)
            cbuf[bi, 0:8] = jnp.zeros((8, qkvw), F32)
            cbuf[bi, 8 - (CONV_K - 1):8] = conv0_ref[bi]

        p = _pad_rows(p_ref[bi], c)
        cbuf[bi, 8:8 + c] = p[:, 0:qkvw]
        conv = None
        for j in range(CONV_K):
            t = cw_ref[j:j + 1] * cbuf[bi, pl.ds(8 - (CONV_K - 1) + j, c), :]
            conv = t if conv is None else conv + t
        cbuf[bi, 0:8] = cbuf[bi, c:c + 8]
        qkv = conv * _sigmoid(conv)
        q = qkv[:, 0:bw]
        k = qkv[:, bw:2 * bw]
        v = qkv[:, 2 * bw:3 * bw]
        z = p[:, qkvw:qkvw + bw]
        tail = p[:, qkvw + bw:qkvw + bw + PAIR]
        q = q * (lax.rsqrt(_dot_sel(q * q, gmat) + L2_EPS) * (HEAD ** -0.5))
        k = k * lax.rsqrt(_dot_sel(k * k, gmat) + L2_EPS)
        beta = jnp.where(is_beta, _sigmoid(tail), 0.0)
        gl = jnp.where(is_g, -jnp.exp(a_log) * _softplus(tail + dt_bias), 0.0)
        if t_valid < c:
            valid = rows < t_valid
            beta = jnp.where(valid, beta, 0.0)
            gl = jnp.where(valid, gl, 0.0)
            k = jnp.where(valid, k, 0.0)
        gcum = _dot_sel(_dot_f32(tri, gl), gsel_ref[...])
        beta = _dot_sel(beta, bsel_ref[...])
        g_last = gcum[c - 1:c]
        eg = jnp.exp(gcum)
        kb = k * beta
        kg = kb * eg
        qg = q * eg
        kd = k * jnp.exp(g_last - gcum)
        vb = v * beta
        d_end = jnp.exp(g_last)

        os_ = []
        for pr in range(3):
            sl = slice(PAIR * pr, PAIR * (pr + 1))

            def col(x):
                x = x[:, sl]
                sw = pltpu.roll(x, HEAD, axis=1)
                return jnp.concatenate([jnp.where(is_a, x, sw), jnp.where(is_a, sw, x)], axis=0)

            gcol = col(gcum)
            diff = gcol - gcol.T
            dec_s = jnp.where(strict, jnp.exp(jnp.where(strict, diff, 0.0)), 0.0)
            dec_i = jnp.where(incl, jnp.exp(jnp.where(incl, diff, 0.0)), 0.0)
            k_st = _stack_heads(k[:, sl])
            sc = _dot(jnp.concatenate([_stack_heads(kb[:, sl]), _stack_heads(q[:, sl])], axis=0),
                      k_st, _NT)
            a_m = sc[0:2 * c] * dec_s
            qk = sc[2 * c:4 * c] * dec_i
            tinv = _neumann_inverse(-a_m, c)
            s = s_scr[bi, pr]
            kqs = _dot(jnp.concatenate([kg[:, sl], qg[:, sl]], axis=0), s)
            v_new = _dot(tinv, _stack_heads(vb[:, sl] - kqs[0:c]))
            os_.append(kqs[c:2 * c] + _fold_heads(_dot(qk, v_new), c))
            s_new = s * d_end[:, sl] + jnp.where(bd, _dot(kd[:, sl], _fold_heads(v_new, c), _TN), 0.0)
            s_scr[bi, pr] = s_new

            @pl.when(ci == nci - 1)
            def _():
                _store_pair_state(sout_ref, bi, pr, s_new)

        o = jnp.concatenate(os_, axis=1)
        o = o * lax.rsqrt(_dot_sel(o * o, gmat) * (1.0 / HEAD) + RMS_EPS) * nw_ref[...]
        y_ref[bi] = (o * (z * _sigmoid(z)))[0:cin]


def _gdn_call(p_b, conv0, s0, cw, tailp, nw, gmat, bsel, gsel, *, cin, t_valid, bt):
    b, tp, bcols = p_b.shape
    nh = s0.shape[1]
    bw = nh * HEAD
    nc = tp // cin
    const = lambda a: pl.BlockSpec(a.shape, lambda i, j: (0,) * a.ndim)
    kern = functools.partial(_gdn_kernel, bt=bt, cin=cin, t_valid=t_valid)
    return pl.pallas_call(
        kern,
        grid=(b // bt, nc),
        in_specs=[
            pl.BlockSpec((bt, cin, bcols), lambda i, j: (i, j, 0)),
            pl.BlockSpec((bt, CONV_K - 1, 3 * bw), lambda i, j: (i, 0, 0)),
            pl.BlockSpec((bt, nh, HEAD, HEAD), lambda i, j: (i, 0, 0, 0)),
            const(cw), const(tailp), const(nw), const(gmat), const(bsel), const(gsel),
        ],
        out_specs=[
            pl.BlockSpec((bt, cin, bw), lambda i, j: (i, j, 0)),
            pl.BlockSpec((bt, nh, HEAD, HEAD), lambda i, j: (i, 0, 0, 0)),
        ],
        out_shape=[jax.ShapeDtypeStruct((b, tp, bw), F32),
                   jax.ShapeDtypeStruct(s0.shape, F32)],
        scratch_shapes=[pltpu.VMEM((bt, nh // 2, PAIR, PAIR), F32),
                        pltpu.VMEM((bt, CHUNK + 8, 3 * bw), F32)],
        compiler_params=pltpu.CompilerParams(
            dimension_semantics=("parallel", "arbitrary"), vmem_limit_bytes=VMEM_LIMIT),
        name="gdn",
    )(p_b, conv0, s0, cw, tailp, nw, gmat, bsel, gsel)


def _hgrn_kernel(p_ref, s0_ref, lbl_ref, nw_ref, g_ref, y_ref, sout_ref, s_scr,
                 *, bt, cin, c, t_valid, layer):
    ci = pl.program_id(1)
    nci = pl.num_programs(1)
    cw = 2 * PAIR
    rows = _iota((c, 1), 0)
    cols = _iota((1, c), 1)
    sameblk = (rows >> 4) == (cols >> 4)
    tri = (sameblk & (rows >= cols)).astype(F32)
    tot = sameblk.astype(F32)
    rows_b = _iota((HBLK, 1), 0)
    bd = _pair_blockdiag_mask()
    gmat = g_ref[...]
    gpair = gmat[0:PAIR, 0:PAIR]

    logits = lbl_ref[...]
    ex = jnp.exp(logits - jnp.max(logits, axis=0, keepdims=True))
    gam = ex / jnp.sum(ex, axis=0, keepdims=True)
    lb = jnp.sum(gam[0:layer + 1], axis=0, keepdims=True) - gam[0:1]

    for bi in range(bt):
        @pl.when(ci == 0)
        def _():
            for pr in range(2):
                s_scr[bi, pr] = _load_pair_state(s0_ref, bi, pr).T

        p = _pad_rows(p_ref[bi], c)
        qp = p[:, 0:cw]
        f = p[:, cw:2 * cw]
        v = p[:, 2 * cw:3 * cw]
        z = p[:, 3 * cw:4 * cw]
        q = qp * _sigmoid(qp)
        logf = jnp.log(lb + (1.0 - lb) * _sigmoid(f))
        k = (1.0 - lb) * _sigmoid(-f)
        if t_valid < c:
            valid = rows < t_valid
            logf = jnp.where(valid, logf, 0.0)
            k = jnp.where(valid, k, 0.0)
        bl = _dot_f32(tri, logf)
        btot = _dot_f32(tot, logf)
        q_in = q * jnp.exp(bl)
        k_out = k * jnp.exp(btot - bl)
        d_blk = jnp.exp(btot)

        o_pairs = []
        for pr in range(2):
            sl = slice(PAIR * pr, PAIR * (pr + 1))
            s = s_scr[bi, pr]
            o_blks = []
            for blk in range(c // HBLK):
                rs = slice(HBLK * blk, HBLK * (blk + 1))
                qb, kb, vb, bb = q[rs, sl], k[rs, sl], v[rs, sl], bl[rs, sl]
                o_inter = _dot(q_in[rs, sl], s, _NT)
                xs = []
                for j in range(HBLK):
                    e = jnp.exp(jnp.minimum(bb - bb[j:j + 1], 0.0))
                    xs.append(qb * e * kb[j:j + 1])
                att = _dot_sel(jnp.concatenate(xs, axis=0), gpair, pieces=2)
                acc = o_inter
                for j in range(HBLK):
                    acc = acc + jnp.where(rows_b >= j, att[HBLK * j:HBLK * (j + 1)], 0.0) * vb[j:j + 1]
                o_blks.append(acc)
                s = s * d_blk[HBLK * blk:HBLK * blk + 1, sl] + jnp.where(
                    bd, _dot(vb, k_out[rs, sl], _TN), 0.0)
            s_scr[bi, pr] = s
            o_pairs.append(jnp.concatenate(o_blks, axis=0) if len(o_blks) > 1 else o_blks[0])

            @pl.when(ci == nci - 1)
            def _():
                _store_pair_state(sout_ref, bi, pr, s.T)

        o = jnp.concatenate(o_pairs, axis=1)
        o = o * lax.rsqrt(_dot_sel(o * o, gmat) * (1.0 / HEAD) + RMS_EPS) * nw_ref[...]
        y_ref[bi] = (o * _sigmoid(z))[0:cin]


def _hgrn_call(p_c, s0, lbl, nw, gmat, *, cin, c, t_valid, bt, layer):
    b, tp, ccols = p_c.shape
    nh = s0.shape[1]
    cw = nh * HEAD
    nc = tp // cin
    const = lambda a: pl.BlockSpec(a.shape, lambda i, j: (0,) * a.ndim)
    kern = functools.partial(_hgrn_kernel, bt=bt, cin=cin, c=c, t_valid=t_valid, layer=layer)
    return pl.pallas_call(
        kern,
        grid=(b // bt, nc),
        in_specs=[
            pl.BlockSpec((bt, cin, ccols), lambda i, j: (i, j, 0)),
            pl.BlockSpec((bt, nh, HEAD, HEAD), lambda i, j: (i, 0, 0, 0)),
            const(lbl), const(nw), const(gmat),
        ],
        out_specs=[
            pl.BlockSpec((bt, cin, cw), lambda i, j: (i, j, 0)),
            pl.BlockSpec((bt, nh, HEAD, HEAD), lambda i, j: (i, 0, 0, 0)),
        ],
        out_shape=[jax.ShapeDtypeStruct((b, tp, cw), F32),
                   jax.ShapeDtypeStruct(s0.shape, F32)],
        scratch_shapes=[pltpu.VMEM((bt, nh // 2, PAIR, PAIR), F32)],
        compiler_params=pltpu.CompilerParams(
            dimension_semantics=("parallel", "arbitrary"), vmem_limit_bytes=VMEM_LIMIT),
        name="hgrn",
    )(p_c, s0, lbl, nw, gmat)


def _ffn_kernel(x_ref, ya_ref, yb_ref, yc_ref, woa_ref, wob_ref, woc_ref, gm_ref, nw_ref,
                sc_ref, sh_ref, gf_ref, wg_ref, wu_ref, wd_ref, fnw_ref,
                xo_ref, yo_ref, x1_scr, h_scr, acc_scr):
    j = pl.program_id(1)

    @pl.when(j == 0)
    def _():
        mix = (jnp.dot(ya_ref[...].astype(BF16), woa_ref[...], preferred_element_type=F32)
               + jnp.dot(yb_ref[...].astype(BF16), wob_ref[...], preferred_element_type=F32)
               + jnp.dot(yc_ref[...].astype(BF16), woc_ref[...], preferred_element_type=F32))
        x1 = x_ref[...] + gm_ref[0] * mix
        x1_scr[...] = x1
        h_scr[...] = _norm_mod(x1, nw_ref[...], sc_ref[0], sh_ref[0]).astype(BF16)
        acc_scr[...] = jnp.zeros_like(acc_scr)

    h = h_scr[...]
    gate = jnp.dot(h, wg_ref[...], preferred_element_type=F32)
    up = jnp.dot(h, wu_ref[...], preferred_element_type=F32)
    act = (gate * _sigmoid(gate) * up).astype(BF16)
    acc_scr[...] += jnp.dot(act, wd_ref[...], preferred_element_type=F32)

    @pl.when(j == pl.num_programs(1) - 1)
    def _():
        xo = x1_scr[...] + gf_ref[0] * acc_scr[...]
        xo_ref[...] = xo
        yo_ref[...] = xo * lax.rsqrt(jnp.mean(xo * xo, axis=-1, keepdims=True) + RMS_EPS) * fnw_ref[...]


def _ffn_call(x, ya, yb, yc, woa, wob, woc, gate_m, nw, scale_f, shift_f, gate_f,
              w_in, w_down, fnw, *, tm, th, tiles_per_seq):
    rows, d = x.shape
    ff = w_down.shape[0]
    nj = ff // th
    row = lambda a: pl.BlockSpec((tm, a.shape[1]), lambda i, j: (i, 0))
    const = lambda a: pl.BlockSpec(a.shape, lambda i, j: (0, 0))
    mod = lambda a: _mod_spec(a, tm, tiles_per_seq)
    return pl.pallas_call(
        _ffn_kernel,
        grid=(rows // tm, nj),
        in_specs=[
            row(x), row(ya), row(yb), row(yc), const(woa), const(wob), const(woc),
            mod(gate_m), const(nw), mod(scale_f), mod(shift_f), mod(gate_f),
            pl.BlockSpec((d, th), lambda i, j: (0, j)),
            pl.BlockSpec((d, th), lambda i, j: (0, nj + j)),
            pl.BlockSpec((th, d), lambda i, j: (j, 0)),
            const(fnw),
        ],
        out_specs=[row(x), row(x)],
        out_shape=[jax.ShapeDtypeStruct((rows, d), F32), jax.ShapeDtypeStruct((rows, d), F32)],
        scratch_shapes=[pltpu.VMEM((tm, d), F32), pltpu.VMEM((tm, d), BF16), pltpu.VMEM((tm, d), F32)],
        compiler_params=pltpu.CompilerParams(
            dimension_semantics=("parallel", "arbitrary"), vmem_limit_bytes=VMEM_LIMIT),
        name="ffn",
    )(x, ya, yb, yc, woa, wob, woc, gate_m, nw, scale_f, shift_f, gate_f, w_in, w_in, w_down, fnw)


def _block_ones(n, blk):
    i = np.arange(n) // blk
    return jnp.asarray(i[:, None] == i[None, :], BF16)


def _head_select(first_lane, nh):
    m = np.zeros((PAIR, nh * HEAD), np.float32)
    for h in range(nh):
        m[first_lane + h, h * HEAD:(h + 1) * HEAD] = 1.0
    return jnp.asarray(m, BF16)


def _lane_place(x, first_lane):
    return jnp.zeros((PAIR,), F32).at[first_lane:first_lane + x.shape[0]].set(x)


def _run_group(x, t_valid, mods, states, wts, *, tm, th, bt):
    b, tp, d = x.shape
    rows = b * tp
    per_token_mod = tp < tm
    tiles_per_seq = max(tp // tm, 1)
    cin = min(tp, CHUNK)
    shift0, rwkv0, conv0, gdn0, hgrn0 = states
    nl = len(wts)
    new = ([], [], [], [], [])
    xr = x.reshape(rows, d)
    y = None
    for l in range(nl):
        w = wts[l]
        m = mods[l]

        def mod(i, m=m):
            if per_token_mod:
                return jnp.repeat(m[:, i], tp, axis=0).reshape(rows // tm, tm, d)
            return m[:, i].reshape(b, 1, d)

        shift_m, scale_m, gate_m, shift_f, scale_f, gate_f = (mod(i) for i in range(6))
        p_a, p_b, p_c = _inproj_call(xr, w["norm_mix"], scale_m, shift_m, w["wa"], w["wb"], w["wc"],
                                     tm, tiles_per_seq)
        x_last = xr.reshape(b, tp, d)[:, t_valid - 1]
        h_last, p_prev = _lastrow_call(x_last, w["norm_mix"], m[:, 1], m[:, 0], shift0[l], w["wa"])
        p_a = p_a.reshape(b, tp, -1)
        p_b = p_b.reshape(b, tp, -1)
        p_c = p_c.reshape(b, tp, -1)
        y_a, s_a = _rwkv_call(p_a, p_prev, rwkv0[l], w["mu"], w["rwkv_vec"], w["lora"], w["g384"],
                              cin=cin, t_valid=t_valid, bt=bt)
        y_b, s_b = _gdn_call(p_b, conv0[l], gdn0[l], w["conv_w"], w["gdn_tail"], w["gdn_norm"],
                             w["g384"], w["bsel"], w["gsel"], cin=cin, t_valid=t_valid, bt=bt)
        hc = CHUNK if tp >= CHUNK else HBLK
        y_c, s_c = _hgrn_call(p_c, hgrn0[l], w["lb_logits"], w["hgrn_norm"], w["g256"],
                              cin=min(tp, hc), c=hc, t_valid=t_valid, bt=bt, layer=l)
        qkvw = conv0.shape[-1]
        keep = min(t_valid, CONV_K - 1)
        conv_new = jnp.concatenate([conv0[l], p_b[:, t_valid - keep:t_valid, :qkvw]], axis=1)[:, -(CONV_K - 1):]
        xr, y = _ffn_call(xr, y_a.reshape(rows, -1), y_b.reshape(rows, -1), y_c.reshape(rows, -1),
                          w["woa"], w["wob"], w["woc"], gate_m, w["norm_ffn"], scale_f, shift_f, gate_f,
                          w["w_ffn_in"], w["w_ffn_out"], w["final_norm"],
                          tm=tm, th=th, tiles_per_seq=tiles_per_seq)
        for acc, s in zip(new, (h_last, s_a, conv_new, s_b, s_c)):
            acc.append(s)
    y = y.reshape(b, tp, d)[:, :t_valid]
    return y, [jnp.stack(acc) for acc in new]


def kernel(x_prompt, x_sample, c_prompt, c_sample, state_rwkv_shift, state_rwkv, state_gdn_conv, state_gdn, state_hgrn, w_ada, b_ada, norm_mix_w, w_in, rwkv_mu, rwkv_w0, rwkv_w2, rwkv_a0, rwkv_a2, rwkv_g2, rwkv_k_k, rwkv_k_a, rwkv_r_k, rwkv_ln_w, rwkv_ln_b, gdn_conv_w, gdn_A_log, gdn_dt_bias, gdn_norm_w, hgrn_lb_logits, hgrn_norm_w, w_out, norm_ffn_w, w_ffn_in, w_ffn_out, final_norm_w):
    nl, d, _ = w_in.shape
    a_heads = state_rwkv.shape[2]
    b_heads = state_gdn.shape[2]
    c_heads = state_hgrn.shape[2]
    aw, bw, cw = a_heads * HEAD, b_heads * HEAD, c_heads * HEAD
    lw, la, lg = rwkv_w2.shape[1], rwkv_a2.shape[1], rwkv_g2.shape[1]
    a_cols = 3 * aw + lw + la + lg
    qkvw = 3 * bw
    b_cols = qkvw + 2 * b_heads + bw
    assert (aw, bw, cw) == (3 * PAIR, 3 * PAIR, 2 * PAIR) and lw + la + lg == PAIR and lw == 32 and la == 32

    bp, tpr, _ = x_prompt.shape
    bs, ts, _ = x_sample.shape

    g384 = _block_ones(aw, HEAD)
    g256 = _block_ones(cw, HEAD)
    bsel = _head_select(0, b_heads)
    gsel = _head_select(b_heads, b_heads)
    wts = []
    for l in range(nl):
        wi = w_in[l]
        wb_cols = wi[:, a_cols:a_cols + b_cols]
        wb = jnp.concatenate(
            [wb_cols[:, :qkvw], wb_cols[:, qkvw + 2 * b_heads:], wb_cols[:, qkvw:qkvw + 2 * b_heads],
             jnp.zeros((d, PAIR - 2 * b_heads), F32)], axis=1)
        lora = jnp.zeros((PAIR, 3 * aw), F32)
        lora = lora.at[0:lw, 0:aw].set(rwkv_w2[l])
        lora = lora.at[lw:lw + la, aw:2 * aw].set(rwkv_a2[l])
        lora = lora.at[lw + la:, 2 * aw:].set(rwkv_g2[l])
        rwkv_vec = jnp.stack([rwkv_w0[l], rwkv_a0[l], rwkv_k_k[l], rwkv_k_a[l], rwkv_r_k[l].reshape(-1),
                              rwkv_ln_w[l], rwkv_ln_b[l], jnp.zeros((aw,), F32)])
        wo = w_out[l].astype(BF16)
        wts.append(dict(
            norm_mix=norm_mix_w[l].reshape(1, d),
            wa=wi[:, :a_cols].astype(BF16), wb=wb.astype(BF16), wc=wi[:, a_cols + b_cols:].astype(BF16),
            mu=rwkv_mu[l].reshape(1, a_cols), rwkv_vec=rwkv_vec, lora=lora.astype(BF16),
            g384=g384, g256=g256, bsel=bsel, gsel=gsel,
            conv_w=jnp.concatenate([gdn_conv_w[l], jnp.zeros((8 - CONV_K, qkvw), F32)], axis=0),
            gdn_tail=jnp.zeros((8, PAIR), F32).at[0].set(_lane_place(gdn_A_log[l], b_heads))
                                              .at[1].set(_lane_place(gdn_dt_bias[l], b_heads)),
            gdn_norm=jnp.tile(gdn_norm_w[l], b_heads).reshape(1, bw),
            lb_logits=hgrn_lb_logits,
            hgrn_norm=jnp.tile(hgrn_norm_w[l], c_heads).reshape(1, cw),
            woa=wo[:aw], wob=wo[aw:aw + bw], woc=wo[aw + bw:],
            norm_ffn=norm_ffn_w[l].reshape(1, d),
            w_ffn_in=w_ffn_in[l].astype(BF16), w_ffn_out=w_ffn_out[l].astype(BF16),
            final_norm=final_norm_w.reshape(1, d),
        ))

    mod = _ada_call(jnp.concatenate([c_prompt, c_sample], axis=0), w_ada, b_ada)
    mod = mod.reshape(nl, bp + bs, 6, d)
    mods_p = [mod[l, :bp] for l in range(nl)]
    mods_s = [mod[l, bp:] for l in range(nl)]

    zeros = lambda s: jnp.zeros((nl, bp) + s.shape[2:], s.dtype)
    states_p = tuple(zeros(s) for s in (state_rwkv_shift, state_rwkv, state_gdn_conv, state_gdn, state_hgrn))
    states_s = (state_rwkv_shift, state_rwkv, state_gdn_conv, state_gdn, state_hgrn)

    ff = w_ffn_out.shape[1]
    th = ff // 2
    y_p, new_p = _run_group(x_prompt, tpr, mods_p, states_p, wts, tm=256, th=th, bt=1)
    ts_pad = 8
    xs_pad = jnp.concatenate([x_sample, jnp.zeros((bs, ts_pad - ts, d), F32)], axis=1)
    y_s, new_s = _run_group(xs_pad, ts, mods_s, states_s, wts, tm=256, th=th, bt=1)
    return (y_p, y_s, *new_p, *new_s)
```

```python
import functools

import numpy as np
import jax
import jax.numpy as jnp
from jax import lax
from jax.experimental import pallas as pl
from jax.experimental.pallas import tpu as pltpu

F32 = jnp.float32
BF16 = jnp.bfloat16

HEAD = 64
PAIR = 2 * HEAD
CHUNK = 64
HBLK = 16
CONV_K = 4
RMS_EPS = 1e-6
L2_EPS = 1e-6
GN_EPS = 64e-5
VMEM_LIMIT = 56 * 1024 * 1024

_NN = (((1,), (0,)), ((), ()))
_NT = (((1,), (1,)), ((), ()))
_TN = (((0,), (0,)), ((), ()))


def _dot(a, b, dims=_NN):
    return lax.dot_general(a.astype(BF16), b.astype(BF16), dims, preferred_element_type=F32)


def _dot_sel(x, m, pieces=3):
    acc = None
    rem = x
    for i in range(pieces):
        p = rem.astype(BF16)
        t = lax.dot_general(p, m, _NN, preferred_element_type=F32)
        acc = t if acc is None else acc + t
        if i + 1 < pieces:
            rem = rem - p.astype(F32)
    return acc


def _sigmoid(x):
    return jax.nn.sigmoid(x)


def _softplus(x):
    return jnp.maximum(x, 0.0) + jnp.log1p(jnp.exp(-jnp.abs(x)))


def _iota(shape, dim):
    return lax.broadcasted_iota(jnp.int32, shape, dim)


def _stack_heads(x):
    is_a = _iota((1, PAIR), 1) < HEAD
    return jnp.concatenate([jnp.where(is_a, x, 0.0), jnp.where(is_a, 0.0, x)], axis=0)


def _fold_heads(x, c):
    return x[0:c] + x[c:2 * c]


def _pair_blockdiag_mask():
    return (_iota((PAIR, 1), 0) >> 6) == (_iota((1, PAIR), 1) >> 6)


def _load_pair_state(s_ref, bi, pr):
    sa = s_ref[bi, 2 * pr]
    sb = s_ref[bi, 2 * pr + 1]
    z = jnp.zeros((HEAD, HEAD), F32)
    return jnp.concatenate(
        [jnp.concatenate([sa, z], axis=1), jnp.concatenate([z, sb], axis=1)], axis=0)


def _store_pair_state(s_ref, bi, pr, s):
    s_ref[bi, 2 * pr] = s[0:HEAD, 0:HEAD]
    s_ref[bi, 2 * pr + 1] = s[HEAD:PAIR, HEAD:PAIR]


def _split(x):
    hi = x.astype(BF16)
    return hi, (x - hi.astype(F32)).astype(BF16)


def _dot3(a, b, dims=_NN):
    d = lambda x, y: lax.dot_general(x, y, dims, preferred_element_type=F32)
    return d(a[0], b[0]) + d(a[0], b[1]) + d(a[1], b[0])


def _dot_left01(m, x, pieces=3):
    acc = None
    rem = x
    for i in range(pieces):
        p = rem.astype(BF16)
        t = lax.dot_general(m, p, _NN, preferred_element_type=F32)
        acc = t if acc is None else acc + t
        if i + 1 < pieces:
            rem = rem - p.astype(F32)
    return acc


def _neumann_inverse_many(ns, c):
    size = ns[0].shape[0]
    eye = (_iota((size, 1), 0) == _iota((1, size), 1)).astype(F32)
    invs = [eye + n for n in ns]
    pws = [_split(n) for n in ns]
    span = 2
    while span < c:
        pws = [_split(_dot3(pw, pw)) for pw in pws]
        invs = [inv + _dot3(_split(inv), pw) for inv, pw in zip(invs, pws)]
        span *= 2
    return invs


def _pad_rows(x, rows):
    if x.shape[0] == rows:
        return x
    return jnp.concatenate([x, jnp.zeros((rows - x.shape[0], x.shape[1]), x.dtype)], axis=0)


def _ada_kernel(c_ref, w_ref, b_ref, o_ref):
    c = c_ref[...]
    o_ref[0] = _dot(c * _sigmoid(c), w_ref[0]) + b_ref[0]


def _ada_call(c_all, w_ada, b_ada):
    nl, d, n6 = w_ada.shape
    rows = c_all.shape[0]
    tn = n6 // 4
    return pl.pallas_call(
        _ada_kernel,
        grid=(nl, n6 // tn),
        in_specs=[
            pl.BlockSpec((rows, d), lambda l, j: (0, 0)),
            pl.BlockSpec((1, d, tn), lambda l, j: (l, 0, j)),
            pl.BlockSpec((1, 1, tn), lambda l, j: (l, 0, j)),
        ],
        out_specs=pl.BlockSpec((1, rows, tn), lambda l, j: (l, 0, j)),
        out_shape=jax.ShapeDtypeStruct((nl, rows, n6), F32),
        compiler_params=pltpu.CompilerParams(
            dimension_semantics=("parallel", "parallel"), vmem_limit_bytes=VMEM_LIMIT),
        name="ada",
    )(c_all, w_ada, b_ada.reshape(nl, 1, n6))


def _norm_mod(x, nw, scale, shift):
    y = x * lax.rsqrt(jnp.mean(x * x, axis=-1, keepdims=True) + RMS_EPS) * nw
    return y * (1.0 + scale) + shift


def _inproj_kernel(x_ref, nw_ref, sc_ref, sh_ref, wa_ref, wb_ref, wc_ref, pa_ref, pb_ref, pc_ref):
    h = _norm_mod(x_ref[...], nw_ref[...], sc_ref[0], sh_ref[0]).astype(BF16)
    pa_ref[...] = jnp.dot(h, wa_ref[...], preferred_element_type=F32)
    pb_ref[...] = jnp.dot(h, wb_ref[...], preferred_element_type=F32)
    pc_ref[...] = jnp.dot(h, wc_ref[...], preferred_element_type=F32)


def _mod_spec(mod, tm, tiles_per_seq):
    if mod.shape[1] == 1:
        return pl.BlockSpec((1, 1, mod.shape[2]), lambda i, *_: (i // tiles_per_seq, 0, 0))
    return pl.BlockSpec((1, tm, mod.shape[2]), lambda i, *_: (i, 0, 0))


def _inproj_call(x, nw, scale, shift, wa, wb, wc, tm, tiles_per_seq):
    rows, d = x.shape
    full = lambda a: pl.BlockSpec(a.shape, lambda i: (0, 0))
    outs = [jax.ShapeDtypeStruct((rows, w.shape[1]), F32) for w in (wa, wb, wc)]
    return pl.pallas_call(
        _inproj_kernel,
        grid=(rows // tm,),
        in_specs=[
            pl.BlockSpec((tm, d), lambda i: (i, 0)),
            full(nw),
            _mod_spec(scale, tm, tiles_per_seq),
            _mod_spec(shift, tm, tiles_per_seq),
            full(wa), full(wb), full(wc),
        ],
        out_specs=[pl.BlockSpec((tm, w.shape[1]), lambda i: (i, 0)) for w in (wa, wb, wc)],
        out_shape=outs,
        compiler_params=pltpu.CompilerParams(
            dimension_semantics=("parallel",), vmem_limit_bytes=VMEM_LIMIT),
        name="inproj",
    )(x, nw, scale, shift, wa, wb, wc)


def _lastrow_kernel(x_ref, nw_ref, sc_ref, sh_ref, hprev_ref, wa_ref, h_ref, pprev_ref):
    h_ref[...] = _norm_mod(x_ref[...], nw_ref[...], sc_ref[...], sh_ref[...])
    pprev_ref[...] = jnp.dot(hprev_ref[...].astype(BF16), wa_ref[...], preferred_element_type=F32)


def _lastrow_call(x_last, nw, scale, shift, h_prev, wa):
    b, d = x_last.shape
    return pl.pallas_call(
        _lastrow_kernel,
        out_shape=[jax.ShapeDtypeStruct((b, d), F32), jax.ShapeDtypeStruct((b, wa.shape[1]), F32)],
        compiler_params=pltpu.CompilerParams(vmem_limit_bytes=VMEM_LIMIT),
        name="lastrow",
    )(x_last, nw, scale, shift, h_prev, wa)


def _rwkv_kernel(p_ref, pprev_ref, s0_ref, mu_ref, vec_ref, wl_ref, g_ref,
                 y_ref, sout_ref, s_scr, prev_scr, *, bt, cin, t_valid):
    c = CHUNK
    ci = pl.program_id(1)
    nci = pl.num_programs(1)
    aw = 3 * PAIR
    rows = _iota((c, 1), 0)
    lane = _iota((1, PAIR), 1)
    vec = vec_ref[...]
    w0, a0, k_k, k_a, r_k, ln_w, ln_b = (vec[i:i + 1] for i in range(7))
    gmat = g_ref[...]
    tri = (rows >= _iota((1, c), 1)).astype(BF16)
    r2 = _iota((2 * c, 1), 0) & (c - 1)
    c2 = _iota((1, 2 * c), 1) & (c - 1)
    strict = r2 > c2
    incl = r2 >= c2
    bd = _pair_blockdiag_mask()

    @pl.when(ci == 0)
    def _():
        for bi in range(bt):
            for pr in range(3):
                s_scr[bi, pr] = _load_pair_state(s0_ref, bi, pr)
            prev_scr[bi] = pprev_ref[bi]

    pre = []
    for bi in range(bt):
        p = _pad_rows(p_ref[bi], c)
        prev = pltpu.roll(p, 1, axis=0)
        prev = jnp.where(rows == 0, prev_scr[bi], prev)
        prev_scr[bi] = p[c - 1:c]
        xs = p + mu_ref[...] * (prev - p)
        r = xs[:, 0:aw]
        k = xs[:, aw:2 * aw]
        v = xs[:, 2 * aw:3 * aw]
        tail = xs[:, 3 * aw:3 * aw + PAIR]
        act = jnp.where(lane < 32, jnp.tanh(tail), jnp.where(lane < 64, tail, _sigmoid(tail)))
        lo = _dot(act, wl_ref[...])
        w = -_softplus(-(w0 + lo[:, 0:aw])) - 0.5
        ld = -jnp.exp(w)
        a = _sigmoid(a0 + lo[:, aw:2 * aw])
        g = lo[:, 2 * aw:3 * aw]
        kk = k * k_k
        kk = kk * lax.rsqrt(_dot_sel(kk * kk, gmat) + L2_EPS)
        k = k * (1.0 + (a - 1.0) * k_a)
        if t_valid < c:
            valid = rows < t_valid
            ld = jnp.where(valid, ld, 0.0)
            kk = jnp.where(valid, kk, 0.0)
            k = jnp.where(valid, k, 0.0)
        cum = _dot_left01(tri, ld)
        cum_last = cum[c - 1:c]
        e_neg = jnp.exp(-cum)
        a_t = -kk * jnp.exp(cum - ld)
        b_t = kk * a * e_neg
        k_t = k * e_neg
        r_t = r * jnp.exp(cum)
        e_end = jnp.exp(cum_last - cum)
        pre.append(dict(r=r, k=k, v=v, g=g, a_t=a_t, b_t=b_t, k_t=k_t, r_t=r_t,
                        b_end=kk * a * e_end, k_end=k * e_end, d_end=jnp.exp(cum_last)))

    chains = [(bi, pr) for bi in range(bt) for pr in range(3)]
    pair = lambda name: [pre[bi][name][:, PAIR * pr:PAIR * (pr + 1)] for bi, pr in chains]
    a_t, b_t, k_t, r_t, v = pair("a_t"), pair("b_t"), pair("k_t"), pair("r_t"), pair("v")
    b_end, k_end, d_end = pair("b_end"), pair("k_end"), pair("d_end")
    v_st = [_stack_heads(x) for x in v]
    gs = [_dot(jnp.concatenate([_stack_heads(a), _stack_heads(r)], axis=0),
               jnp.concatenate([_stack_heads(b), _stack_heads(k)], axis=0), _NT)
          for a, r, b, k in zip(a_t, r_t, b_t, k_t)]
    a_ak = [jnp.where(strict, g[0:2 * c, 2 * c:4 * c], 0.0) for g in gs]
    p_rbk = [jnp.concatenate([jnp.where(incl, g[2 * c:4 * c, 0:2 * c], 0.0),
                              jnp.where(incl, g[2 * c:4 * c, 2 * c:4 * c], 0.0)], axis=1) for g in gs]
    tinv = _neumann_inverse_many([jnp.where(strict, g[0:2 * c, 0:2 * c], 0.0) for g in gs], c)
    s = [s_scr[bi, pr] for bi, pr in chains]
    ars = [_dot(jnp.concatenate([a, r], axis=0), s_, _NT) for a, r, s_ in zip(a_t, r_t, s)]
    akv = [_dot(m, x) for m, x in zip(a_ak, v_st)]
    u_st = [_dot(t, _stack_heads(x[0:c]) + y) for t, x, y in zip(tinv, ars, akv)]
    y_st = [_dot(p, jnp.concatenate([u, x], axis=0)) for p, u, x in zip(p_rbk, u_st, v_st)]
    ds = [_dot(jnp.concatenate([_fold_heads(u, c), x], axis=0), jnp.concatenate([b, k], axis=0), _TN)
          for u, x, b, k in zip(u_st, v, b_end, k_end)]
    for i, (bi, pr) in enumerate(chains):
        s_scr[bi, pr] = s[i] * d_end[i] + jnp.where(bd, ds[i], 0.0)

    for bi in range(bt):
        r, k, v_, g = (pre[bi][n] for n in ("r", "k", "v", "g"))
        y = jnp.concatenate([ars[3 * bi + pr][c:2 * c] + _fold_heads(y_st[3 * bi + pr], c)
                             for pr in range(3)], axis=1)
        mean = _dot_sel(y, gmat) * (1.0 / HEAD)
        dy = y - mean
        var = _dot_sel(dy * dy, gmat) * (1.0 / HEAD)
        yn = dy * lax.rsqrt(var + GN_EPS) * ln_w + ln_b
        bonus = _dot_sel(r * k * r_k, gmat) * v_
        y_ref[bi] = ((yn + bonus) * g)[0:cin]

    @pl.when(ci == nci - 1)
    def _():
        for bi in range(bt):
            for pr in range(3):
                _store_pair_state(sout_ref, bi, pr, s_scr[bi, pr])


def _rwkv_call(p_a, p_prev, s0, mu, vec, wl, gmat, *, cin, t_valid, bt):
    b, tp, acols = p_a.shape
    nh = s0.shape[1]
    aw = nh * HEAD
    nc = tp // cin
    const = lambda a: pl.BlockSpec(a.shape, lambda i, j: (0,) * a.ndim)
    kern = functools.partial(_rwkv_kernel, bt=bt, cin=cin, t_valid=t_valid)
    return pl.pallas_call(
        kern,
        grid=(b // bt, nc),
        in_specs=[
            pl.BlockSpec((bt, cin, acols), lambda i, j: (i, j, 0)),
            pl.BlockSpec((bt, 1, acols), lambda i, j: (i, 0, 0)),
            pl.BlockSpec((bt, nh, HEAD, HEAD), lambda i, j: (i, 0, 0, 0)),
            const(mu), const(vec), const(wl), const(gmat),
        ],
        out_specs=[
            pl.BlockSpec((bt, cin, aw), lambda i, j: (i, j, 0)),
            pl.BlockSpec((bt, nh, HEAD, HEAD), lambda i, j: (i, 0, 0, 0)),
        ],
        out_shape=[jax.ShapeDtypeStruct((b, tp, aw), F32),
                   jax.ShapeDtypeStruct(s0.shape, F32)],
        scratch_shapes=[pltpu.VMEM((bt, nh // 2, PAIR, PAIR), F32),
                        pltpu.VMEM((bt, 1, acols), F32)],
        compiler_params=pltpu.CompilerParams(
            dimension_semantics=("parallel", "arbitrary"), vmem_limit_bytes=VMEM_LIMIT),
        name="rwkv",
    )(p_a, p_prev.reshape(b, 1, acols), s0, mu, vec, wl, gmat)


def _gdn_kernel(p_ref, conv0_ref, s0_ref, cw_ref, tp_ref, nw_ref, g_ref, bsel_ref, gsel_ref,
                y_ref, sout_ref, s_scr, cbuf, *, bt, cin, t_valid):
    c = CHUNK
    ci = pl.program_id(1)
    nci = pl.num_programs(1)
    bw = 3 * PAIR
    qkvw = 3 * bw
    rows = _iota((c, 1), 0)
    lane = _iota((1, PAIR), 1)
    nh = 2 * 3
    gmat = g_ref[...]
    tri = (rows >= _iota((1, c), 1)).astype(BF16)
    r2 = _iota((2 * c, 1), 0) & (c - 1)
    c2 = _iota((1, 2 * c), 1) & (c - 1)
    same =(_iota((2 * c, 1), 0) >> 6) == (_iota((1, 2 * c), 1) >> 6)
    strict = same & (r2 > c2)
    incl = same & (r2 >= c2)
    bd = _pair_blockdiag_mask()
    is_a = lane < HEAD
    a_log = tp_ref[0:1]
    dt_bias = tp_ref[1:2]
    is_beta = lane < nh
    is_g = (lane >= nh) & (lane < 2 * nh)

    @pl.when(ci == 0)
    def _():
        for bi in range(bt):
            for pr in range(3):
                s_scr[bi, pr] = _load_pair_state(s0_ref, bi, pr)
            cbuf[bi, 0:8] = jnp.zeros((8, qkvw), F32)
            cbuf[bi, 8 - (CONV_K - 1):8] = conv0_ref[bi]

    pre = []
    for bi in range(bt):
        p = _pad_rows(p_ref[bi], c)
        cbuf[bi, 8:8 + c] = p[:, 0:qkvw]
        conv = None
        for j in range(CONV_K):
            t = cw_ref[j:j + 1] * cbuf[bi, pl.ds(8 - (CONV_K - 1) + j, c), :]
            conv = t if conv is None else conv + t
        cbuf[bi, 0:8] = cbuf[bi, c:c + 8]
        qkv = conv * _sigmoid(conv)
        q = qkv[:, 0:bw]
        k = qkv[:, bw:2 * bw]
        v = qkv[:, 2 * bw:3 * bw]
        z = p[:, qkvw:qkvw + bw]
        tail = p[:, qkvw + bw:qkvw + bw + PAIR]
        q = q * (lax.rsqrt(_dot_sel(q * q, gmat) + L2_EPS) * (HEAD ** -0.5))
        k = k * lax.rsqrt(_dot_sel(k * k, gmat) + L2_EPS)
        beta = jnp.where(is_beta, _sigmoid(tail), 0.0)
        gl = jnp.where(is_g, -jnp.exp(a_log) * _softplus(tail + dt_bias), 0.0)
        if t_valid < c:
            valid = rows < t_valid
            beta = jnp.where(valid, beta, 0.0)
            gl = jnp.where(valid, gl, 0.0)
            k = jnp.where(valid, k, 0.0)
        gcum = _dot_sel(_dot_left01(tri, gl), gsel_ref[...])
        beta = _dot_sel(beta, bsel_ref[...])
        g_last = gcum[c - 1:c]
        eg = jnp.exp(gcum)
        kb = k * beta
        pre.append(dict(z=z, q=q, k=k, kb=kb, gcum=gcum, kg=kb * eg, qg=q * eg,
                        kd=k * jnp.exp(g_last - gcum), vb=v * beta, d_end=jnp.exp(g_last)))

    chains = [(bi, pr) for bi in range(bt) for pr in range(3)]
    pair = lambda name: [pre[bi][name][:, PAIR * pr:PAIR * (pr + 1)] for bi, pr in chains]
    q, k, kb, gcum, kg, qg = pair("q"), pair("k"), pair("kb"), pair("gcum"), pair("kg"), pair("qg")
    kd, vb, d_end = pair("kd"), pair("vb"), pair("d_end")

    def col(x):
        sw = pltpu.roll(x, HEAD, axis=1)
        return jnp.concatenate([jnp.where(is_a, x, sw), jnp.where(is_a, sw, x)], axis=0)

    gcol = [col(x) for x in gcum]
    diff = [x - x.T for x in gcol]
    dec_s = [jnp.where(strict, jnp.exp(jnp.where(strict, x, 0.0)), 0.0) for x in diff]
    dec_i = [jnp.where(incl, jnp.exp(jnp.where(incl, x, 0.0)), 0.0) for x in diff]
    sc = [_dot(jnp.concatenate([_stack_heads(x), _stack_heads(y)], axis=0), _stack_heads(w), _NT)
          for x, y, w in zip(kb, q, k)]
    tinv = _neumann_inverse_many([-(x[0:2 * c] * d) for x, d in zip(sc, dec_s)], c)
    qk = [x[2 * c:4 * c] * d for x, d in zip(sc, dec_i)]
    s = [s_scr[bi, pr] for bi, pr in chains]
    kqs = [_dot(jnp.concatenate([x, y], axis=0), s_) for x, y, s_ in zip(kg, qg, s)]
    v_new = [_dot(t, _stack_heads(x - y[0:c])) for t, x, y in zip(tinv, vb, kqs)]
    o_st = [_dot(x, y) for x, y in zip(qk, v_new)]
    ds = [_dot(x, _fold_heads(y, c), _TN) for x, y in zip(kd, v_new)]
    for i, (bi, pr) in enumerate(chains):
        s_scr[bi, pr] = s[i] * d_end[i] + jnp.where(bd, ds[i], 0.0)

    for bi in range(bt):
        z = pre[bi]["z"]
        o = jnp.concatenate([kqs[3 * bi + pr][c:2 * c] + _fold_heads(o_st[3 * bi + pr], c)
                             for pr in range(3)], axis=1)
        o = o * lax.rsqrt(_dot_sel(o * o, gmat) * (1.0 / HEAD) + RMS_EPS) * nw_ref[...]
        y_ref[bi] = (o * (z * _sigmoid(z)))[0:cin]

    @pl.when(ci == nci - 1)
    def _():
        for bi in range(bt):
            for pr in range(3):
                _store_pair_state(sout_ref, bi, pr, s_scr[bi, pr])


def _gdn_call(p_b, conv0, s0, cw, tailp, nw, gmat, bsel, gsel, *, cin, t_valid, bt):
    b, tp, bcols = p_b.shape
    nh = s0.shape[1]
    bw = nh * HEAD
    nc = tp // cin
    const = lambda a: pl.BlockSpec(a.shape, lambda i, j: (0,) * a.ndim)
    kern = functools.partial(_gdn_kernel, bt=bt, cin=cin, t_valid=t_valid)
    return pl.pallas_call(
        kern,
        grid=(b // bt, nc),
        in_specs=[
            pl.BlockSpec((bt, cin, bcols), lambda i, j: (i, j, 0)),
            pl.BlockSpec((bt, CONV_K - 1, 3 * bw), lambda i, j: (i, 0, 0)),
            pl.BlockSpec((bt, nh, HEAD, HEAD), lambda i, j: (i, 0, 0, 0)),
            const(cw), const(tailp), const(nw), const(gmat), const(bsel), const(gsel),
        ],
        out_specs=[
            pl.BlockSpec((bt, cin, bw), lambda i, j: (i, j, 0)),
            pl.BlockSpec((bt, nh, HEAD, HEAD), lambda i, j: (i, 0, 0, 0)),
        ],
        out_shape=[jax.ShapeDtypeStruct((b, tp, bw), F32),
                   jax.ShapeDtypeStruct(s0.shape, F32)],
        scratch_shapes=[pltpu.VMEM((bt, nh // 2, PAIR, PAIR), F32),
                        pltpu.VMEM((bt, CHUNK + 8, 3 * bw), F32)],
        compiler_params=pltpu.CompilerParams(
            dimension_semantics=("parallel", "arbitrary"), vmem_limit_bytes=VMEM_LIMIT),
        name="gdn",
    )(p_b, conv0, s0, cw, tailp, nw, gmat, bsel, gsel)


def _hgrn_kernel(p_ref, s0_ref, lbl_ref, nw_ref, g_ref, y_ref, sout_ref, s_scr,
                 *, bt, cin, c, t_valid, layer):
    ci = pl.program_id(1)
    nci = pl.num_programs(1)
    cw = 2 * PAIR
    rows = _iota((c, 1), 0)
    cols = _iota((1, c), 1)
    sameblk = (rows >> 4) == (cols >> 4)
    tri_tot = jnp.concatenate([sameblk & (rows >= cols), sameblk], axis=0).astype(BF16)
    rows_b = _iota((HBLK, 1), 0)
    bd = _pair_blockdiag_mask()
    gmat = g_ref[...]
    gpair = gmat[0:PAIR, 0:PAIR]

    logits = lbl_ref[...]
    ex = jnp.exp(logits - jnp.max(logits, axis=0, keepdims=True))
    gam = ex / jnp.sum(ex, axis=0, keepdims=True)
    lb = jnp.sum(gam[0:layer + 1], axis=0, keepdims=True) - gam[0:1]

    @pl.when(ci == 0)
    def _():
        for bi in range(bt):
            for pr in range(2):
                s_scr[bi, pr] = _load_pair_state(s0_ref, bi, pr).T

    pre = []
    for bi in range(bt):
        p = _pad_rows(p_ref[bi], c)
        qp = p[:, 0:cw]
        f = p[:, cw:2 * cw]
        v = p[:, 2 * cw:3 * cw]
        z = p[:, 3 * cw:4 * cw]
        q = qp * _sigmoid(qp)
        logf = jnp.log(lb + (1.0 - lb) * _sigmoid(f))
        k = (1.0 - lb) * _sigmoid(-f)
        if t_valid < c:
            valid = rows < t_valid
            logf = jnp.where(valid, logf, 0.0)
            k = jnp.where(valid, k, 0.0)
        cums = _dot_left01(tri_tot, logf)
        bl = cums[0:c]
        btot = cums[c:2 * c]
        pre.append(dict(q=q, k=k, v=v, z=z, bl=bl, q_in=q * jnp.exp(bl),
                        k_out=k * jnp.exp(btot - bl), d_blk=jnp.exp(btot)))

    chains = [(bi, pr) for bi in range(bt) for pr in range(2)]
    nblk = c // HBLK

    def blk_of(name, bi, pr, blk):
        return pre[bi][name][HBLK * blk:HBLK * (blk + 1), PAIR * pr:PAIR * (pr + 1)]

    intra = {}
    ds = {}
    for blk in range(nblk):
        for bi, pr in chains:
            qb, kb, vb, bb = (blk_of(n, bi, pr, blk) for n in ("q", "k", "v", "bl"))
            xs = []
            for j in range(HBLK):
                e = jnp.exp(jnp.minimum(bb - bb[j:j + 1], 0.0))
                xs.append(qb * e * kb[j:j + 1])
            att = _dot_sel(jnp.concatenate(xs, axis=0), gpair, pieces=2)
            acc = None
            for j in range(HBLK):
                t = jnp.where(rows_b >= j, att[HBLK * j:HBLK * (j + 1)], 0.0) * vb[j:j + 1]
                acc = t if acc is None else acc + t
            intra[bi, pr, blk] = acc
            ds[bi, pr, blk] = jnp.where(bd, _dot(vb, blk_of("k_out", bi, pr, blk), _TN), 0.0)

    s = {ch: s_scr[ch] for ch in chains}
    outs = {}
    for blk in range(nblk):
        for bi, pr in chains:
            outs[bi, pr, blk] = intra[bi, pr, blk] + _dot(blk_of("q_in", bi, pr, blk), s[bi, pr], _NT)
            s[bi, pr] = s[bi, pr] * blk_of("d_blk", bi, pr, blk)[0:1] + ds[bi, pr, blk]
    for ch in chains:
        s_scr[ch] = s[ch]

    for bi in range(bt):
        z = pre[bi]["z"]
        o = jnp.concatenate(
            [jnp.concatenate([outs[bi, pr, blk] for blk in range(nblk)], axis=0) for pr in range(2)], axis=1)
        o = o * lax.rsqrt(_dot_sel(o * o, gmat) * (1.0 / HEAD) + RMS_EPS) * nw_ref[...]
        y_ref[bi] = (o * _sigmoid(z))[0:cin]

    @pl.when(ci == nci - 1)
    def _():
        for bi in range(bt):
            for pr in range(2):
                _store_pair_state(sout_ref, bi, pr, s_scr[bi, pr].T)


def _hgrn_call(p_c, s0, lbl, nw, gmat, *, cin, c, t_valid, bt, layer):
    b, tp, ccols = p_c.shape
    nh = s0.shape[1]
    cw = nh * HEAD
    nc = tp // cin
    const = lambda a: pl.BlockSpec(a.shape, lambda i, j: (0,) * a.ndim)
    kern = functools.partial(_hgrn_kernel, bt=bt, cin=cin, c=c, t_valid=t_valid, layer=layer)
    return pl.pallas_call(
        kern,
        grid=(b // bt, nc),
        in_specs=[
            pl.BlockSpec((bt, cin, ccols), lambda i, j: (i, j, 0)),
            pl.BlockSpec((bt, nh, HEAD, HEAD), lambda i, j: (i, 0, 0, 0)),
            const(lbl), const(nw), const(gmat),
        ],
        out_specs=[
            pl.BlockSpec((bt, cin, cw), lambda i, j: (i, j, 0)),
            pl.BlockSpec((bt, nh, HEAD, HEAD), lambda i, j: (i, 0, 0, 0)),
        ],
        out_shape=[jax.ShapeDtypeStruct((b, tp, cw), F32),
                   jax.ShapeDtypeStruct(s0.shape, F32)],
        scratch_shapes=[pltpu.VMEM((bt, nh // 2, PAIR, PAIR), F32)],
        compiler_params=pltpu.CompilerParams(
            dimension_semantics=("parallel", "arbitrary"), vmem_limit_bytes=VMEM_LIMIT),
        name="hgrn",
    )(p_c, s0, lbl, nw, gmat)


def _ffn_kernel(x_ref, ya_ref, yb_ref, yc_ref, woa_ref, wob_ref, woc_ref, gm_ref, nw_ref,
                sc_ref, sh_ref, gf_ref, wg_ref, wu_ref, wd_ref, fnw_ref,
                xo_ref, yo_ref, x1_scr, h_scr, acc_scr):
    j = pl.program_id(1)

    @pl.when(j == 0)
    def _():
        mix = (jnp.dot(ya_ref[...].astype(BF16), woa_ref[...], preferred_element_type=F32)
               + jnp.dot(yb_ref[...].astype(BF16), wob_ref[...], preferred_element_type=F32)
               + jnp.dot(yc_ref[...].astype(BF16), woc_ref[...], preferred_element_type=F32))
        x1 = x_ref[...] + gm_ref[0] * mix
        x1_scr[...] = x1
        h_scr[...] = _norm_mod(x1, nw_ref[...], sc_ref[0], sh_ref[0]).astype(BF16)
        acc_scr[...] = jnp.zeros_like(acc_scr)

    h = h_scr[...]
    gate = jnp.dot(h, wg_ref[...], preferred_element_type=F32)
    up = jnp.dot(h, wu_ref[...], preferred_element_type=F32)
    act = (gate * _sigmoid(gate) * up).astype(BF16)
    acc_scr[...] += jnp.dot(act, wd_ref[...], preferred_element_type=F32)

    @pl.when(j == pl.num_programs(1) - 1)
    def _():
        xo = x1_scr[...] + gf_ref[0] * acc_scr[...]
        xo_ref[...] = xo
        yo_ref[...] = xo * lax.rsqrt(jnp.mean(xo * xo, axis=-1, keepdims=True) + RMS_EPS) * fnw_ref[...]


def _ffn_call(x, ya, yb, yc, woa, wob, woc, gate_m, nw, scale_f, shift_f, gate_f,
              w_in, w_down, fnw, *, tm, th, tiles_per_seq):
    rows, d = x.shape
    ff = w_down.shape[0]
    nj = ff // th
    row = lambda a: pl.BlockSpec((tm, a.shape[1]), lambda i, j: (i, 0))
    const = lambda a: pl.BlockSpec(a.shape, lambda i, j: (0, 0))
    mod = lambda a: _mod_spec(a, tm, tiles_per_seq)
    return pl.pallas_call(
        _ffn_kernel,
        grid=(rows // tm, nj),
        in_specs=[
            row(x), row(ya), row(yb), row(yc), const(woa), const(wob), const(woc),
            mod(gate_m), const(nw), mod(scale_f), mod(shift_f), mod(gate_f),
            pl.BlockSpec((d, th), lambda i, j: (0, j)),
            pl.BlockSpec((d, th), lambda i, j: (0, nj + j)),
            pl.BlockSpec((th, d), lambda i, j: (j, 0)),
            const(fnw),
        ],
        out_specs=[row(x), row(x)],
        out_shape=[jax.ShapeDtypeStruct((rows, d), F32), jax.ShapeDtypeStruct((rows, d), F32)],
        scratch_shapes=[pltpu.VMEM((tm, d), F32), pltpu.VMEM((tm, d), BF16), pltpu.VMEM((tm, d), F32)],
        compiler_params=pltpu.CompilerParams(
            dimension_semantics=("parallel", "arbitrary"), vmem_limit_bytes=VMEM_LIMIT),
        name="ffn",
    )(x, ya, yb, yc, woa, wob, woc, gate_m, nw, scale_f, shift_f, gate_f, w_in, w_in, w_down, fnw)


def _block_ones(n, blk):
    i = np.arange(n) // blk
    return jnp.asarray(i[:, None] == i[None, :], BF16)


def _head_select(first_lane, nh):
    m = np.zeros((PAIR, nh * HEAD), np.float32)
    for h in range(nh):
        m[first_lane + h, h * HEAD:(h + 1) * HEAD] = 1.0
    return jnp.asarray(m, BF16)


def _lane_place(x, first_lane):
    return jnp.zeros((PAIR,), F32).at[first_lane:first_lane + x.shape[0]].set(x)


def _run_group(x, t_valid, mods, states, wts, *, tm, th, bt):
    b, tp, d = x.shape
    rows = b * tp
    per_token_mod = tp < tm
    tiles_per_seq = max(tp // tm, 1)
    cin = min(tp, CHUNK)
    shift0, rwkv0, conv0, gdn0, hgrn0 = states
    nl = len(wts)
    new = ([], [], [], [], [])
    xr = x.reshape(rows, d)
    y = None
    for l in range(nl):
        w = wts[l]
        m = mods[l]

        def mod(i, m=m):
            if per_token_mod:
                return jnp.repeat(m[:, i], tp, axis=0).reshape(rows // tm, tm, d)
            return m[:, i].reshape(b, 1, d)

        shift_m, scale_m, gate_m, shift_f, scale_f, gate_f = (mod(i) for i in range(6))
        p_a, p_b, p_c = _inproj_call(xr, w["norm_mix"], scale_m, shift_m, w["wa"], w["wb"], w["wc"],
                                     tm, tiles_per_seq)
        x_last = xr.reshape(b, tp, d)[:, t_valid - 1]
        h_last, p_prev = _lastrow_call(x_last, w["norm_mix"], m[:, 1], m[:, 0], shift0[l], w["wa"])
        p_a = p_a.reshape(b, tp, -1)
        p_b = p_b.reshape(b, tp, -1)
        p_c = p_c.reshape(b, tp, -1)
        y_a, s_a = _rwkv_call(p_a, p_prev, rwkv0[l], w["mu"], w["rwkv_vec"], w["lora"], w["g384"],
                              cin=cin, t_valid=t_valid, bt=bt)
        y_b, s_b = _gdn_call(p_b, conv0[l], gdn0[l], w["conv_w"], w["gdn_tail"], w["gdn_norm"],
                             w["g384"], w["bsel"], w["gsel"], cin=cin, t_valid=t_valid, bt=bt)
        hc = CHUNK if tp >= CHUNK else HBLK
        y_c, s_c = _hgrn_call(p_c, hgrn0[l], w["lb_logits"], w["hgrn_norm"], w["g256"],
                              cin=min(tp, hc), c=hc, t_valid=t_valid, bt=bt, layer=l)
        qkvw = conv0.shape[-1]
        keep = min(t_valid, CONV_K - 1)
        conv_new = jnp.concatenate([conv0[l], p_b[:, t_valid - keep:t_valid, :qkvw]], axis=1)[:, -(CONV_K - 1):]
        xr, y = _ffn_call(xr, y_a.reshape(rows, -1), y_b.reshape(rows, -1), y_c.reshape(rows, -1),
                          w["woa"], w["wob"], w["woc"], gate_m, w["norm_ffn"], scale_f, shift_f, gate_f,
                          w["w_ffn_in"], w["w_ffn_out"], w["final_norm"],
                          tm=tm, th=th, tiles_per_seq=tiles_per_seq)
        for acc, s in zip(new, (h_last, s_a, conv_new, s_b, s_c)):
            acc.append(s)
    y = y.reshape(b, tp, d)[:, :t_valid]
    return y, [jnp.stack(acc) for acc in new]


def kernel(x_prompt, x_sample, c_prompt, c_sample, state_rwkv_shift, state_rwkv, state_gdn_conv, state_gdn, state_hgrn, w_ada, b_ada, norm_mix_w, w_in, rwkv_mu, rwkv_w0, rwkv_w2, rwkv_a0, rwkv_a2, rwkv_g2, rwkv_k_k, rwkv_k_a, rwkv_r_k, rwkv_ln_w, rwkv_ln_b, gdn_conv_w, gdn_A_log, gdn_dt_bias, gdn_norm_w, hgrn_lb_logits, hgrn_norm_w, w_out, norm_ffn_w, w_ffn_in, w_ffn_out, final_norm_w):
    nl, d, _ = w_in.shape
    a_heads = state_rwkv.shape[2]
    b_heads = state_gdn.shape[2]
    c_heads = state_hgrn.shape[2]
    aw, bw, cw = a_heads * HEAD, b_heads * HEAD, c_heads * HEAD
    lw, la, lg = rwkv_w2.shape[1], rwkv_a2.shape[1], rwkv_g2.shape[1]
    a_cols = 3 * aw + lw + la + lg
    qkvw = 3 * bw
    b_cols = qkvw + 2 * b_heads + bw
    assert (aw, bw, cw) == (3 * PAIR, 3 * PAIR, 2 * PAIR) and lw + la + lg == PAIR and lw == 32 and la == 32

    bp, tpr, _ = x_prompt.shape
    bs, ts, _ = x_sample.shape

    g384 = _block_ones(aw, HEAD)
    g256 = _block_ones(cw, HEAD)
    bsel = _head_select(0, b_heads)
    gsel = _head_select(b_heads, b_heads)
    wts = []
    for l in range(nl):
        wi = w_in[l]
        wb_cols = wi[:, a_cols:a_cols + b_cols]
        wb = jnp.concatenate(
            [wb_cols[:, :qkvw], wb_cols[:, qkvw + 2 * b_heads:], wb_cols[:, qkvw:qkvw + 2 * b_heads],
             jnp.zeros((d, PAIR - 2 * b_heads), F32)], axis=1)
        lora = jnp.zeros((PAIR, 3 * aw), F32)
        lora = lora.at[0:lw, 0:aw].set(rwkv_w2[l])
        lora = lora.at[lw:lw + la, aw:2 * aw].set(rwkv_a2[l])
        lora = lora.at[lw + la:, 2 * aw:].set(rwkv_g2[l])
        rwkv_vec = jnp.stack([rwkv_w0[l], rwkv_a0[l], rwkv_k_k[l], rwkv_k_a[l], rwkv_r_k[l].reshape(-1),
                              rwkv_ln_w[l], rwkv_ln_b[l], jnp.zeros((aw,), F32)])
        wo = w_out[l].astype(BF16)
        wts.append(dict(
            norm_mix=norm_mix_w[l].reshape(1, d),
            wa=wi[:, :a_cols].astype(BF16), wb=wb.astype(BF16), wc=wi[:, a_cols + b_cols:].astype(BF16),
            mu=rwkv_mu[l].reshape(1, a_cols), rwkv_vec=rwkv_vec, lora=lora.astype(BF16),
            g384=g384, g256=g256, bsel=bsel, gsel=gsel,
            conv_w=jnp.concatenate([gdn_conv_w[l], jnp.zeros((8 - CONV_K, qkvw), F32)], axis=0),
            gdn_tail=jnp.zeros((8, PAIR), F32).at[0].set(_lane_place(gdn_A_log[l], b_heads))
                                              .at[1].set(_lane_place(gdn_dt_bias[l], b_heads)),
            gdn_norm=jnp.tile(gdn_norm_w[l], b_heads).reshape(1, bw),
            lb_logits=hgrn_lb_logits,
            hgrn_norm=jnp.tile(hgrn_norm_w[l], c_heads).reshape(1, cw),
            woa=wo[:aw], wob=wo[aw:aw + bw], woc=wo[aw + bw:],
            norm_ffn=norm_ffn_w[l].reshape(1, d),
            w_ffn_in=w_ffn_in[l].astype(BF16), w_ffn_out=w_ffn_out[l].astype(BF16),
            final_norm=final_norm_w.reshape(1, d),
        ))

    mod = _ada_call(jnp.concatenate([c_prompt, c_sample], axis=0), w_ada, b_ada)
    mod = mod.reshape(nl, bp + bs, 6, d)
    mods_p = [mod[l, :bp] for l in range(nl)]
    mods_s = [mod[l, bp:] for l in range(nl)]

    zeros = lambda s: jnp.zeros((nl, bp) + s.shape[2:], s.dtype)
    states_p = tuple(zeros(s) for s in (state_rwkv_shift, state_rwkv, state_gdn_conv, state_gdn, state_hgrn))
    states_s = (state_rwkv_shift, state_rwkv, state_gdn_conv, state_gdn, state_hgrn)

    ff = w_ffn_out.shape[1]
    th = ff // 2
    y_p, new_p = _run_group(x_prompt, tpr, mods_p, states_p, wts, tm=256, th=th, bt=2)
    ts_pad = 8
    xs_pad = jnp.concatenate([x_sample, jnp.zeros((bs, ts_pad - ts, d), F32)], axis=1)
    y_s, new_s = _run_group(xs_pad, ts, mods_s, states_s, wts, tm=256, th=th, bt=2)
    return (y_p, y_s, *new_p, *new_s)
```

```python
import functools

import numpy as np
import jax
import jax.numpy as jnp
from jax import lax
from jax.experimental import pallas as pl
from jax.experimental.pallas import tpu as pltpu

F32 = jnp.float32
BF16 = jnp.bfloat16

HEAD = 64
PAIR = 2 * HEAD
CHUNK = 64
HBLK = 16
CONV_K = 4
RMS_EPS = 1e-6
L2_EPS = 1e-6
GN_EPS = 64e-5
VMEM_LIMIT = 56 * 1024 * 1024

_NN = (((1,), (0,)), ((), ()))
_NT = (((1,), (1,)), ((), ()))
_TN = (((0,), (0,)), ((), ()))


def _dot(a, b, dims=_NN):
    return lax.dot_general(a.astype(BF16), b.astype(BF16), dims, preferred_element_type=F32)


def _dot_sel(x, m, pieces=3):
    acc = None
    rem = x
    for i in range(pieces):
        p = rem.astype(BF16)
        t = lax.dot_general(p, m, _NN, preferred_element_type=F32)
        acc = t if acc is None else acc + t
        if i + 1 < pieces:
            rem = rem - p.astype(F32)
    return acc


def _seg_sum(x, gpair, pieces=2):
    return jnp.concatenate(
        [_dot_sel(x[:, i:i + PAIR], gpair, pieces) for i in range(0, x.shape[1], PAIR)], axis=1)


def _sigmoid(x):
    return jax.nn.sigmoid(x)


def _softplus(x):
    return jnp.maximum(x, 0.0) + jnp.log1p(jnp.exp(-jnp.abs(x)))


def _iota(shape, dim):
    return lax.broadcasted_iota(jnp.int32, shape, dim)


def _stack_heads(x):
    is_a = _iota((1, PAIR), 1) < HEAD
    return jnp.concatenate([jnp.where(is_a, x, 0.0), jnp.where(is_a, 0.0, x)], axis=0)


def _fold_heads(x, c):
    return x[0:c] + x[c:2 * c]


def _pair_blockdiag_mask():
    return (_iota((PAIR, 1), 0) >> 6) == (_iota((1, PAIR), 1) >> 6)


def _load_pair_state(s_ref, bi, pr):
    sa = s_ref[bi, 2 * pr]
    sb = s_ref[bi, 2 * pr + 1]
    z = jnp.zeros((HEAD, HEAD), F32)
    return jnp.concatenate(
        [jnp.concatenate([sa, z], axis=1), jnp.concatenate([z, sb], axis=1)], axis=0)


def _store_pair_state(s_ref, bi, pr, s):
    s_ref[bi, 2 * pr] = s[0:HEAD, 0:HEAD]
    s_ref[bi, 2 * pr + 1] = s[HEAD:PAIR, HEAD:PAIR]


def _split(x):
    hi = x.astype(BF16)
    return hi, (x - hi.astype(F32)).astype(BF16)


def _dot3(a, b, dims=_NN):
    d = lambda x, y: lax.dot_general(x, y, dims, preferred_element_type=F32)
    return d(a[0], b[0]) + d(a[0], b[1]) + d(a[1], b[0])


def _dot_left01(m, x, pieces=3):
    acc = None
    rem = x
    for i in range(pieces):
        p = rem.astype(BF16)
        t = lax.dot_general(m, p, _NN, preferred_element_type=F32)
        acc = t if acc is None else acc + t
        if i + 1 < pieces:
            rem = rem - p.astype(F32)
    return acc


def _neumann_inverse_many(ns, c):
    size = ns[0].shape[0]
    eye = (_iota((size, 1), 0) == _iota((1, size), 1)).astype(F32)
    invs = [eye + n for n in ns]
    pws = [_split(n) for n in ns]
    span = 2
    while span < c:
        pws = [_split(_dot3(pw, pw)) for pw in pws]
        invs = [inv + _dot3(_split(inv), pw) for inv, pw in zip(invs, pws)]
        span *= 2
    return invs


def _pad_rows(x, rows):
    if x.shape[0] == rows:
        return x
    return jnp.concatenate([x, jnp.zeros((rows - x.shape[0], x.shape[1]), x.dtype)], axis=0)


def _ada_kernel(c_ref, w_ref, b_ref, o_ref):
    c = c_ref[...]
    o_ref[0] = _dot(c * _sigmoid(c), w_ref[0]) + b_ref[0]


def _ada_call(c_all, w_ada, b_ada):
    nl, d, n6 = w_ada.shape
    rows = c_all.shape[0]
    tn = n6 // 4
    return pl.pallas_call(
        _ada_kernel,
        grid=(nl, n6 // tn),
        in_specs=[
            pl.BlockSpec((rows, d), lambda l, j: (0, 0)),
            pl.BlockSpec((1, d, tn), lambda l, j: (l, 0, j)),
            pl.BlockSpec((1, 1, tn), lambda l, j: (l, 0, j)),
        ],
        out_specs=pl.BlockSpec((1, rows, tn), lambda l, j: (l, 0, j)),
        out_shape=jax.ShapeDtypeStruct((nl, rows, n6), F32),
        compiler_params=pltpu.CompilerParams(
            dimension_semantics=("parallel", "parallel"), vmem_limit_bytes=VMEM_LIMIT),
        name="ada",
    )(c_all, w_ada, b_ada.reshape(nl, 1, n6))


def _norm_mod(x, nw, scale, shift):
    y = x * lax.rsqrt(jnp.mean(x * x, axis=-1, keepdims=True) + RMS_EPS) * nw
    return y * (1.0 + scale) + shift


def _inproj_kernel(x_ref, nw_ref, sc_ref, sh_ref, wa_ref, wb_ref, wc_ref, pa_ref, pb_ref, pc_ref):
    h = _norm_mod(x_ref[...], nw_ref[...], sc_ref[0], sh_ref[0]).astype(BF16)
    pa_ref[...] = lax.dot_general(h, wa_ref[...], _NT, preferred_element_type=F32)
    pb_ref[...] = lax.dot_general(h, wb_ref[...], _NT, preferred_element_type=F32)
    pc_ref[...] = lax.dot_general(h, wc_ref[...], _NT, preferred_element_type=F32)


def _mod_spec(mod, tm, tiles_per_seq):
    if mod.shape[1] == 1:
        return pl.BlockSpec((1, 1, mod.shape[2]), lambda i, *_: (i // tiles_per_seq, 0, 0))
    return pl.BlockSpec((1, tm, mod.shape[2]), lambda i, *_: (i, 0, 0))


def _inproj_call(x, nw, scale, shift, wa, wb, wc, tm, tiles_per_seq):
    rows, d = x.shape
    full = lambda a: pl.BlockSpec(a.shape, lambda i: (0, 0), pipeline_mode=pl.Buffered(1))
    outs = [jax.ShapeDtypeStruct((rows, w.shape[0]), F32) for w in (wa, wb, wc)]
    return pl.pallas_call(
        _inproj_kernel,
        grid=(rows // tm,),
        in_specs=[
            pl.BlockSpec((tm, d), lambda i: (i, 0)),
            full(nw),
            _mod_spec(scale, tm, tiles_per_seq),
            _mod_spec(shift, tm, tiles_per_seq),
            full(wa), full(wb), full(wc),
        ],
        out_specs=[pl.BlockSpec((tm, w.shape[0]), lambda i: (i, 0)) for w in (wa, wb, wc)],
        out_shape=outs,
        compiler_params=pltpu.CompilerParams(
            dimension_semantics=("parallel",), vmem_limit_bytes=VMEM_LIMIT),
        name="inproj",
    )(x, nw, scale, shift, wa, wb, wc)


def _lastrow_kernel(x_ref, nw_ref, sc_ref, sh_ref, hprev_ref, wa_ref, h_ref, pprev_ref):
    h_ref[...] = _norm_mod(x_ref[...], nw_ref[...], sc_ref[...], sh_ref[...])
    pprev_ref[...] = lax.dot_general(hprev_ref[...].astype(BF16), wa_ref[...], _NT,
                                     preferred_element_type=F32)


def _lastrow_call(x_last, nw, scale, shift, h_prev, wa):
    b, d = x_last.shape
    return pl.pallas_call(
        _lastrow_kernel,
        out_shape=[jax.ShapeDtypeStruct((b, d), F32), jax.ShapeDtypeStruct((b, wa.shape[0]), F32)],
        compiler_params=pltpu.CompilerParams(vmem_limit_bytes=VMEM_LIMIT),
        name="lastrow",
    )(x_last, nw, scale, shift, h_prev, wa)


def _rwkv_kernel(p_ref, pprev_ref, s0_ref, mu_ref, vec_ref, wl_ref, g_ref,
                 y_ref, sout_ref, s_scr, prev_scr, *, bt, cin, t_valid):
    c = CHUNK
    ci = pl.program_id(1)
    nci = pl.num_programs(1)
    aw = 3 * PAIR
    rows = _iota((c, 1), 0)
    lane = _iota((1, PAIR), 1)
    vec = vec_ref[...]
    w0, a0, k_k, k_a, r_k, ln_w, ln_b = (vec[i:i + 1] for i in range(7))
    gpair = g_ref[...]
    tri = (rows >= _iota((1, c), 1)).astype(BF16)
    r2 = _iota((2 * c, 1), 0) & (c - 1)
    c2 = _iota((1, 2 * c), 1) & (c - 1)
    strict = r2 > c2
    incl = r2 >= c2
    bd = _pair_blockdiag_mask()

    @pl.when(ci == 0)
    def _():
        for bi in range(bt):
            for pr in range(3):
                s_scr[bi, pr] = _load_pair_state(s0_ref, bi, pr)
            prev_scr[bi] = pprev_ref[bi]

    pre = []
    for bi in range(bt):
        p = _pad_rows(p_ref[bi], c)
        prev = pltpu.roll(p, 1, axis=0)
        prev = jnp.where(rows == 0, prev_scr[bi], prev)
        prev_scr[bi] = p[c - 1:c]
        xs = p + mu_ref[...] * (prev - p)
        r = xs[:, 0:aw]
        k = xs[:, aw:2 * aw]
        v = xs[:, 2 * aw:3 * aw]
        tail = xs[:, 3 * aw:3 * aw + PAIR]
        act = jnp.where(lane < 32, jnp.tanh(tail), jnp.where(lane < 64, tail, _sigmoid(tail)))
        lo = _dot(act, wl_ref[...])
        w = -_softplus(-(w0 + lo[:, 0:aw])) - 0.5
        ld = -jnp.exp(w)
        a = _sigmoid(a0 + lo[:, aw:2 * aw])
        g = lo[:, 2 * aw:3 * aw]
        kk = k * k_k
        kk = kk * lax.rsqrt(_seg_sum(kk * kk, gpair) + L2_EPS)
        k = k * (1.0 + (a - 1.0) * k_a)
        if t_valid < c:
            valid = rows < t_valid
            ld = jnp.where(valid, ld, 0.0)
            kk = jnp.where(valid, kk, 0.0)
            k = jnp.where(valid, k, 0.0)
        cum = _dot_left01(tri, ld)
        cum_last = cum[c - 1:c]
        e_neg = jnp.exp(-cum)
        a_t = -kk * jnp.exp(cum - ld)
        b_t = kk * a * e_neg
        k_t = k * e_neg
        r_t = r * jnp.exp(cum)
        e_end = jnp.exp(cum_last - cum)
        pre.append(dict(r=r, k=k, v=v, g=g, a_t=a_t, b_t=b_t, k_t=k_t, r_t=r_t,
                        b_end=kk * a * e_end, k_end=k * e_end, d_end=jnp.exp(cum_last)))

    chains = [(bi, pr) for bi in range(bt) for pr in range(3)]
    pair = lambda name: [pre[bi][name][:, PAIR * pr:PAIR * (pr + 1)] for bi, pr in chains]
    a_t, b_t, k_t, r_t, v = pair("a_t"), pair("b_t"), pair("k_t"), pair("r_t"), pair("v")
    b_end, k_end, d_end = pair("b_end"), pair("k_end"), pair("d_end")
    v_st = [_stack_heads(x) for x in v]
    gs = [_dot(jnp.concatenate([_stack_heads(a), _stack_heads(r)], axis=0),
               jnp.concatenate([_stack_heads(b), _stack_heads(k)], axis=0), _NT)
          for a, r, b, k in zip(a_t, r_t, b_t, k_t)]
    a_ak = [jnp.where(strict, g[0:2 * c, 2 * c:4 * c], 0.0) for g in gs]
    p_rbk = [jnp.concatenate([jnp.where(incl, g[2 * c:4 * c, 0:2 * c], 0.0),
                              jnp.where(incl, g[2 * c:4 * c, 2 * c:4 * c], 0.0)], axis=1) for g in gs]
    tinv = _neumann_inverse_many([jnp.where(strict, g[0:2 * c, 0:2 * c], 0.0) for g in gs], c)
    s = [s_scr[bi, pr] for bi, pr in chains]
    ars = [_dot(jnp.concatenate([a, r], axis=0), s_, _NT) for a, r, s_ in zip(a_t, r_t, s)]
    akv = [_dot(m, x) for m, x in zip(a_ak, v_st)]
    u_st = [_dot(t, _stack_heads(x[0:c]) + y) for t, x, y in zip(tinv, ars, akv)]
    y_st = [_dot(p, jnp.concatenate([u, x], axis=0)) for p, u, x in zip(p_rbk, u_st, v_st)]
    ds = [_dot(jnp.concatenate([_fold_heads(u, c), x], axis=0), jnp.concatenate([b, k], axis=0), _TN)
          for u, x, b, k in zip(u_st, v, b_end, k_end)]
    for i, (bi, pr) in enumerate(chains):
        s_scr[bi, pr] = s[i] * d_end[i] + jnp.where(bd, ds[i], 0.0)

    for bi in range(bt):
        r, k, v_, g = (pre[bi][n] for n in ("r", "k", "v", "g"))
        y = jnp.concatenate([ars[3 * bi + pr][c:2 * c] + _fold_heads(y_st[3 * bi + pr], c)
                             for pr in range(3)], axis=1)
        mean = _seg_sum(y, gpair) * (1.0 / HEAD)
        dy = y - mean
        var = _seg_sum(dy * dy, gpair) * (1.0 / HEAD)
        yn = dy * lax.rsqrt(var + GN_EPS) * ln_w + ln_b
        bonus = _seg_sum(r * k * r_k, gpair) * v_
        y_ref[bi] = ((yn + bonus) * g)[0:cin]

    @pl.when(ci == nci - 1)
    def _():
        for bi in range(bt):
            for pr in range(3):
                _store_pair_state(sout_ref, bi, pr, s_scr[bi, pr])


def _rwkv_call(p_a, p_prev, s0, mu, vec, wl, gmat, *, cin, t_valid, bt):
    b, tp, acols = p_a.shape
    nh = s0.shape[1]
    aw = nh * HEAD
    nc = tp // cin
    const = lambda a: pl.BlockSpec(a.shape, lambda i, j: (0,) * a.ndim)
    kern = functools.partial(_rwkv_kernel, bt=bt, cin=cin, t_valid=t_valid)
    return pl.pallas_call(
        kern,
        grid=(b // bt, nc),
        in_specs=[
            pl.BlockSpec((bt, cin, acols), lambda i, j: (i, j, 0)),
            pl.BlockSpec((bt, 1, acols), lambda i, j: (i, 0, 0)),
            pl.BlockSpec((bt, nh, HEAD, HEAD), lambda i, j: (i, 0, 0, 0)),
            const(mu), const(vec), const(wl), const(gmat),
        ],
        out_specs=[
            pl.BlockSpec((bt, cin, aw), lambda i, j: (i, j, 0)),
            pl.BlockSpec((bt, nh, HEAD, HEAD), lambda i, j: (i, 0, 0, 0)),
        ],
        out_shape=[jax.ShapeDtypeStruct((b, tp, aw), F32),
                   jax.ShapeDtypeStruct(s0.shape, F32)],
        scratch_shapes=[pltpu.VMEM((bt, nh // 2, PAIR, PAIR), F32),
                        pltpu.VMEM((bt, 1, acols), F32)],
        compiler_params=pltpu.CompilerParams(
            dimension_semantics=("parallel", "arbitrary"), vmem_limit_bytes=VMEM_LIMIT),
        name="rwkv",
    )(p_a, p_prev.reshape(b, 1, acols), s0, mu, vec, wl, gmat)


def _gdn_kernel(p_ref, conv0_ref, s0_ref, cw_ref, tp_ref, nw_ref, g_ref, bsel_ref, gsel_ref,
                y_ref, sout_ref, s_scr, cbuf, *, bt, cin, t_valid):
    c = CHUNK
    ci = pl.program_id(1)
    nci = pl.num_programs(1)
    bw = 3 * PAIR
    qkvw = 3 * bw
    rows = _iota((c, 1), 0)
    lane = _iota((1, PAIR), 1)
    nh = 2 * 3
    gpair = g_ref[...]
    tri = (rows >= _iota((1, c), 1)).astype(BF16)
    r2 = _iota((2 * c, 1), 0) & (c - 1)
    c2 = _iota((1, 2 * c), 1) & (c - 1)
    same =(_iota((2 * c, 1), 0) >> 6) == (_iota((1, 2 * c), 1) >> 6)
    strict = same & (r2 > c2)
    incl = same & (r2 >= c2)
    bd = _pair_blockdiag_mask()
    is_a = lane < HEAD
    a_log = tp_ref[0:1]
    dt_bias = tp_ref[1:2]
    is_beta = lane < nh
    is_g = (lane >= nh) & (lane < 2 * nh)

    @pl.when(ci == 0)
    def _():
        for bi in range(bt):
            for pr in range(3):
                s_scr[bi, pr] = _load_pair_state(s0_ref, bi, pr)
            cbuf[bi, 0:8] = jnp.zeros((8, qkvw), F32)
            cbuf[bi, 8 - (CONV_K - 1):8] = conv0_ref[bi]

    pre = []
    for bi in range(bt):
        p = _pad_rows(p_ref[bi], c)
        cbuf[bi, 8:8 + c] = p[:, 0:qkvw]
        conv = None
        for j in range(CONV_K):
            t = cw_ref[j:j + 1] * cbuf[bi, pl.ds(8 - (CONV_K - 1) + j, c), :]
            conv = t if conv is None else conv + t
        cbuf[bi, 0:8] = cbuf[bi, c:c + 8]
        qkv = conv * _sigmoid(conv)
        q = qkv[:, 0:bw]
        k = qkv[:, bw:2 * bw]
        v = qkv[:, 2 * bw:3 * bw]
        z = p[:, qkvw:qkvw + bw]
        tail = p[:, qkvw + bw:qkvw + bw + PAIR]
        q = q * (lax.rsqrt(_seg_sum(q * q, gpair) + L2_EPS) * (HEAD ** -0.5))
        k = k * lax.rsqrt(_seg_sum(k * k, gpair) + L2_EPS)
        beta = jnp.where(is_beta, _sigmoid(tail), 0.0)
        gl = jnp.where(is_g, -jnp.exp(a_log) * _softplus(tail + dt_bias), 0.0)
        if t_valid < c:
            valid = rows < t_valid
            beta = jnp.where(valid, beta, 0.0)
            gl = jnp.where(valid, gl, 0.0)
            k = jnp.where(valid, k, 0.0)
        gcum = _dot_sel(_dot_left01(tri, gl), gsel_ref[...])
        beta = _dot_sel(beta, bsel_ref[...])
        g_last = gcum[c - 1:c]
        eg = jnp.exp(gcum)
        kb = k * beta
        pre.append(dict(z=z, q=q, k=k, kb=kb, gcum=gcum, kg=kb * eg, qg=q * eg,
                        kd=k * jnp.exp(g_last - gcum), vb=v * beta, d_end=jnp.exp(g_last)))

    chains = [(bi, pr) for bi in range(bt) for pr in range(3)]
    pair = lambda name: [pre[bi][name][:, PAIR * pr:PAIR * (pr + 1)] for bi, pr in chains]
    q, k, kb, gcum, kg, qg = pair("q"), pair("k"), pair("kb"), pair("gcum"), pair("kg"), pair("qg")
    kd, vb, d_end = pair("kd"), pair("vb"), pair("d_end")

    def col(x):
        sw = pltpu.roll(x, HEAD, axis=1)
        return jnp.concatenate([jnp.where(is_a, x, sw), jnp.where(is_a, sw, x)], axis=0)

    gcol = [col(x) for x in gcum]
    diff = [x - x.T for x in gcol]
    dec_s = [jnp.where(strict, jnp.exp(jnp.where(strict, x, 0.0)), 0.0) for x in diff]
    dec_i = [jnp.where(incl, jnp.exp(jnp.where(incl, x, 0.0)), 0.0) for x in diff]
    sc = [_dot(jnp.concatenate([_stack_heads(x), _stack_heads(y)], axis=0), _stack_heads(w), _NT)
          for x, y, w in zip(kb, q, k)]
    tinv = _neumann_inverse_many([-(x[0:2 * c] * d) for x, d in zip(sc, dec_s)], c)
    qk = [x[2 * c:4 * c] * d for x, d in zip(sc, dec_i)]
    s = [s_scr[bi, pr] for bi, pr in chains]
    kqs = [_dot(jnp.concatenate([x, y], axis=0), s_) for x, y, s_ in zip(kg, qg, s)]
    v_new = [_dot(t, _stack_heads(x - y[0:c])) for t, x, y in zip(tinv, vb, kqs)]
    o_st = [_dot(x, y) for x, y in zip(qk, v_new)]
    ds = [_dot(x, _fold_heads(y, c), _TN) for x, y in zip(kd, v_new)]
    for i, (bi, pr) in enumerate(chains):
        s_scr[bi, pr] = s[i] * d_end[i] + jnp.where(bd, ds[i], 0.0)

    for bi in range(bt):
        z = pre[bi]["z"]
        o = jnp.concatenate([kqs[3 * bi + pr][c:2 * c] + _fold_heads(o_st[3 * bi + pr], c)
                             for pr in range(3)], axis=1)
        o = o * lax.rsqrt(_seg_sum(o * o, gpair) * (1.0 / HEAD) + RMS_EPS) * nw_ref[...]
        y_ref[bi] = (o * (z * _sigmoid(z)))[0:cin]

    @pl.when(ci == nci - 1)
    def _():
        for bi in range(bt):
            for pr in range(3):
                _store_pair_state(sout_ref, bi, pr, s_scr[bi, pr])


def _gdn_call(p_b, conv0, s0, cw, tailp, nw, gmat, bsel, gsel, *, cin, t_valid, bt):
    b, tp, bcols = p_b.shape
    nh = s0.shape[1]
    bw = nh * HEAD
    nc = tp // cin
    const = lambda a: pl.BlockSpec(a.shape, lambda i, j: (0,) * a.ndim)
    kern = functools.partial(_gdn_kernel, bt=bt, cin=cin, t_valid=t_valid)
    return pl.pallas_call(
        kern,
        grid=(b // bt, nc),
        in_specs=[
            pl.BlockSpec((bt, cin, bcols), lambda i, j: (i, j, 0)),
            pl.BlockSpec((bt, CONV_K - 1, 3 * bw), lambda i, j: (i, 0, 0)),
            pl.BlockSpec((bt, nh, HEAD, HEAD), lambda i, j: (i, 0, 0, 0)),
            const(cw), const(tailp), const(nw), const(gmat), const(bsel), const(gsel),
        ],
        out_specs=[
            pl.BlockSpec((bt, cin, bw), lambda i, j: (i, j, 0)),
            pl.BlockSpec((bt, nh, HEAD, HEAD), lambda i, j: (i, 0, 0, 0)),
        ],
        out_shape=[jax.ShapeDtypeStruct((b, tp, bw), F32),
                   jax.ShapeDtypeStruct(s0.shape, F32)],
        scratch_shapes=[pltpu.VMEM((bt, nh // 2, PAIR, PAIR), F32),
                        pltpu.VMEM((bt, CHUNK + 8, 3 * bw), F32)],
        compiler_params=pltpu.CompilerParams(
            dimension_semantics=("parallel", "arbitrary"), vmem_limit_bytes=VMEM_LIMIT),
        name="gdn",
    )(p_b, conv0, s0, cw, tailp, nw, gmat, bsel, gsel)


def _hgrn_kernel(p_ref, s0_ref, lbl_ref, nw_ref, g_ref, y_ref, sout_ref, s_scr,
                 *, bt, cin, c, t_valid, layer):
    ci = pl.program_id(1)
    nci = pl.num_programs(1)
    cw = 2 * PAIR
    rows = _iota((c, 1), 0)
    cols = _iota((1, c), 1)
    sameblk = (rows >> 4) == (cols >> 4)
    tri_tot = jnp.concatenate([sameblk & (rows >= cols), sameblk], axis=0).astype(BF16)
    rows_b = _iota((HBLK, 1), 0)
    bd = _pair_blockdiag_mask()
    gpair = g_ref[...]

    logits = lbl_ref[...]
    ex = jnp.exp(logits - jnp.max(logits, axis=0, keepdims=True))
    gam = ex / jnp.sum(ex, axis=0, keepdims=True)
    lb = jnp.sum(gam[0:layer + 1], axis=0, keepdims=True) - gam[0:1]

    @pl.when(ci == 0)
    def _():
        for bi in range(bt):
            for pr in range(2):
                s_scr[bi, pr] = _load_pair_state(s0_ref, bi, pr).T

    pre = []
    for bi in range(bt):
        p = _pad_rows(p_ref[bi], c)
        qp = p[:, 0:cw]
        f = p[:, cw:2 * cw]
        v = p[:, 2 * cw:3 * cw]
        z = p[:, 3 * cw:4 * cw]
        q = qp * _sigmoid(qp)
        logf = jnp.log(lb + (1.0 - lb) * _sigmoid(f))
        k = (1.0 - lb) * _sigmoid(-f)
        if t_valid < c:
            valid = rows < t_valid
            logf = jnp.where(valid, logf, 0.0)
            k = jnp.where(valid, k, 0.0)
        cums = _dot_left01(tri_tot, logf)
        bl = cums[0:c]
        btot = cums[c:2 * c]
        pre.append(dict(q=q, k=k, v=v, z=z, bl=bl, q_in=q * jnp.exp(bl),
                        k_out=k * jnp.exp(btot - bl), d_blk=jnp.exp(btot)))

    chains = [(bi, pr) for bi in range(bt) for pr in range(2)]
    nblk = c // HBLK

    def blk_of(name, bi, pr, blk):
        return pre[bi][name][HBLK * blk:HBLK * (blk + 1), PAIR * pr:PAIR * (pr + 1)]

    intra = {}
    ds = {}
    for blk in range(nblk):
        for bi, pr in chains:
            qb, kb, vb, bb = (blk_of(n, bi, pr, blk) for n in ("q", "k", "v", "bl"))
            xs = []
            for j in range(HBLK):
                e = jnp.exp(jnp.minimum(bb - bb[j:j + 1], 0.0))
                xs.append(qb * e * kb[j:j + 1])
            att = _dot_sel(jnp.concatenate(xs, axis=0), gpair, pieces=2)
            acc = None
            for j in range(HBLK):
                t = jnp.where(rows_b >= j, att[HBLK * j:HBLK * (j + 1)], 0.0) * vb[j:j + 1]
                acc = t if acc is None else acc + t
            intra[bi, pr, blk] = acc
            ds[bi, pr, blk] = jnp.where(bd, _dot(vb, blk_of("k_out", bi, pr, blk), _TN), 0.0)

    s = {ch: s_scr[ch] for ch in chains}
    outs = {}
    for blk in range(nblk):
        for bi, pr in chains:
            outs[bi, pr, blk] = intra[bi, pr, blk] + _dot(blk_of("q_in", bi, pr, blk), s[bi, pr], _NT)
            s[bi, pr] = s[bi, pr] * blk_of("d_blk", bi, pr, blk)[0:1] + ds[bi, pr, blk]
    for ch in chains:
        s_scr[ch] = s[ch]

    for bi in range(bt):
        z = pre[bi]["z"]
        o = jnp.concatenate(
            [jnp.concatenate([outs[bi, pr, blk] for blk in range(nblk)], axis=0) for pr in range(2)], axis=1)
        o = o * lax.rsqrt(_seg_sum(o * o, gpair) * (1.0 / HEAD) + RMS_EPS) * nw_ref[...]
        y_ref[bi] = (o * _sigmoid(z))[0:cin]

    @pl.when(ci == nci - 1)
    def _():
        for bi in range(bt):
            for pr in range(2):
                _store_pair_state(sout_ref, bi, pr, s_scr[bi, pr].T)


def _hgrn_call(p_c, s0, lbl, nw, gmat, *, cin, c, t_valid, bt, layer):
    b, tp, ccols = p_c.shape
    nh = s0.shape[1]
    cw = nh * HEAD
    nc = tp // cin
    const = lambda a: pl.BlockSpec(a.shape, lambda i, j: (0,) * a.ndim)
    kern = functools.partial(_hgrn_kernel, bt=bt, cin=cin, c=c, t_valid=t_valid, layer=layer)
    return pl.pallas_call(
        kern,
        grid=(b // bt, nc),
        in_specs=[
            pl.BlockSpec((bt, cin, ccols), lambda i, j: (i, j, 0)),
            pl.BlockSpec((bt, nh, HEAD, HEAD), lambda i, j: (i, 0, 0, 0)),
            const(lbl), const(nw), const(gmat),
        ],
        out_specs=[
            pl.BlockSpec((bt, cin, cw), lambda i, j: (i, j, 0)),
            pl.BlockSpec((bt, nh, HEAD, HEAD), lambda i, j: (i, 0, 0, 0)),
        ],
        out_shape=[jax.ShapeDtypeStruct((b, tp, cw), F32),
                   jax.ShapeDtypeStruct(s0.shape, F32)],
        scratch_shapes=[pltpu.VMEM((bt, nh // 2, PAIR, PAIR), F32)],
        compiler_params=pltpu.CompilerParams(
            dimension_semantics=("parallel", "arbitrary"), vmem_limit_bytes=VMEM_LIMIT),
        name="hgrn",
    )(p_c, s0, lbl, nw, gmat)


def _inproj_t_kernel(x_ref, nw_ref, sc_ref, sh_ref, hprev_ref, wta_ref, wtb_ref, wtc_ref,
                     pta_ref, ptb_ref, ptc_ref, rows_ref, hlast_ref, pprev_ref):
    h = _norm_mod(x_ref[...], nw_ref[...], sc_ref[...], sh_ref[...])
    hb = h.astype(BF16)
    nt = lambda w, a: lax.dot_general(w, a, _NT, preferred_element_type=F32)
    pta_ref[...] = nt(wta_ref[...], hb)
    ptb_ref[...] = nt(wtb_ref[...], hb)
    ptc_ref[...] = nt(wtc_ref[...], hb)
    rows_ref[...] = nt(hb, wtb_ref[0:rows_ref.shape[1], :])
    nb = hlast_ref.shape[0]
    hlast_ref[...] = h[h.shape[0] - nb:]
    pprev_ref[...] = nt(wta_ref[...], hprev_ref[...].astype(BF16))


def _inproj_t_call(x, nw, scale, shift, h_prev, wta, wtb, wtc, qkvw):
    rows, d = x.shape
    nb = h_prev.shape[0]
    sds = jax.ShapeDtypeStruct
    return pl.pallas_call(
        _inproj_t_kernel,
        out_shape=[sds((wta.shape[0], rows), F32), sds((wtb.shape[0], rows), F32),
                   sds((wtc.shape[0], rows), F32), sds((rows, qkvw), F32),
                   sds((nb, d), F32), sds((wta.shape[0], nb), F32)],
        compiler_params=pltpu.CompilerParams(vmem_limit_bytes=VMEM_LIMIT),
        name="inproj_t",
    )(x, nw, scale, shift, h_prev, wta, wtb, wtc)


def _col_sum(x):
    return jnp.sum(x, axis=0, keepdims=True)


def _rwkv_t_kernel(r_ref, k_ref, v_ref, tail_ref, rp_ref, kp_ref, vp_ref, tailp_ref,
                   mur_ref, muk_ref, muv_ref, mut_ref, vec_ref, wl_ref, s0_ref,
                   y_ref, sout_ref, v_scr, y_scr):
    nb = s0_ref.shape[-1]
    nt = r_ref.shape[-1] // nb
    w0, a0, k_k, k_a, r_k, ln_w, ln_b = (vec_ref[i, 0] for i in range(7))
    sout_ref[...] = s0_ref[...]

    def shifted(ref, pref, mu, idx, t):
        cur = ref[idx, :, t * nb:(t + 1) * nb]
        prv = pref[idx] if t == 0 else ref[idx, :, (t - 1) * nb:t * nb]
        return cur + mu * (prv - cur)

    for t in range(nt):
        r = shifted(r_ref, rp_ref, mur_ref[0], 0, t)
        k = shifted(k_ref, kp_ref, muk_ref[0], 0, t)
        v = shifted(v_ref, vp_ref, muv_ref[0], 0, t)
        t0 = shifted(tail_ref, tailp_ref, mut_ref[0], 0, t)
        t1 = shifted(tail_ref, tailp_ref, mut_ref[1], 1, t)
        act = jnp.concatenate([jnp.tanh(t0[0:32]), t0[32:64], _sigmoid(t1)], axis=0)
        lo = jnp.dot(wl_ref[0], act.astype(BF16), preferred_element_type=F32)
        w = -_softplus(-(w0 + lo[0:HEAD])) - 0.5
        dec = jnp.exp(-jnp.exp(w))
        a = _sigmoid(a0 + lo[HEAD:2 * HEAD])
        g = lo[2 * HEAD:3 * HEAD]
        kk = k * k_k
        kk = kk * lax.rsqrt(_col_sum(kk * kk) + L2_EPS)
        k = k * (1.0 + (a - 1.0) * k_a)
        a_t = -kk
        b_t = kk * a
        v_scr[...] = v

        def body(g, carry):
            base = pl.multiple_of(g * 8, 8)
            vg = v_scr[pl.ds(base, 8), :]
            ys = []
            for j in range(8):
                sv = sout_ref[0, base + j]
                sa = _col_sum(sv * a_t)
                sv = sv * dec + sa * b_t + vg[j:j + 1] * k
                sout_ref[0, base + j] = sv
                ys.append(_col_sum(sv * r))
            y_scr[pl.ds(base, 8), :] = jnp.concatenate(ys, axis=0)
            return carry

        lax.fori_loop(0, HEAD // 8, body, 0)
        y = y_scr[...]
        mean = _col_sum(y) * (1.0 / HEAD)
        dy = y - mean
        var = _col_sum(dy * dy) * (1.0 / HEAD)
        yn = dy * lax.rsqrt(var + GN_EPS) * ln_w + ln_b
        bonus = _col_sum(r * k * r_k) * v
        y_ref[0, :, t * nb:(t + 1) * nb] = (yn + bonus) * g


def _head_blocks(arr3, first, n=1):
    if n == 1:
        return pl.BlockSpec((1,) + arr3.shape[1:], lambda h: (first + h, 0, 0))
    return pl.BlockSpec((n,) + arr3.shape[1:], lambda h: (first // n, 0, 0))


def _rwkv_t_call(pta, pprev_t, s0, mu, vec, wl_t):
    nh, _, _, nb = s0.shape
    rows = pta.shape[1]
    p3 = pta.reshape(-1, HEAD, rows)
    pp3 = pprev_t.reshape(-1, HEAD, nb)
    mu3 = mu.reshape(-1, HEAD, 1)
    tail0 = 3 * nh
    state = pl.BlockSpec((1, HEAD, HEAD, nb), lambda h: (h, 0, 0, 0))
    return pl.pallas_call(
        _rwkv_t_kernel,
        grid=(nh,),
        in_specs=[
            _head_blocks(p3, 0), _head_blocks(p3, nh), _head_blocks(p3, 2 * nh), _head_blocks(p3, tail0, 2),
            _head_blocks(pp3, 0), _head_blocks(pp3, nh), _head_blocks(pp3, 2 * nh), _head_blocks(pp3, tail0, 2),
            _head_blocks(mu3, 0), _head_blocks(mu3, nh), _head_blocks(mu3, 2 * nh), _head_blocks(mu3, tail0, 2),
            pl.BlockSpec((vec.shape[0], 1, HEAD, 1), lambda h: (0, h, 0, 0)),
            pl.BlockSpec((1,) + wl_t.shape[1:], lambda h: (h, 0, 0)),
            state,
        ],
        out_specs=[pl.BlockSpec((1, HEAD, rows), lambda h: (h, 0, 0)), state],
        out_shape=[jax.ShapeDtypeStruct((nh, HEAD, rows), F32), jax.ShapeDtypeStruct(s0.shape, F32)],
        scratch_shapes=[pltpu.VMEM((HEAD, nb), F32), pltpu.VMEM((HEAD, nb), F32)],
        compiler_params=pltpu.CompilerParams(
            dimension_semantics=("parallel",), vmem_limit_bytes=VMEM_LIMIT),
        name="rwkv_t",
    )(p3, p3, p3, p3, pp3, pp3, pp3, pp3, mu3, mu3, mu3, mu3, vec, wl_t, s0)


def _gdn_t_kernel(q_ref, k_ref, v_ref, z_ref, tail_ref, cq_ref, ck_ref, cv_ref, wq_ref, wk_ref, wv_ref,
                  sc_ref, nw_ref, s0_ref, y_ref, sout_ref, k_scr, q_scr):
    h = pl.program_id(0)
    nh = pl.num_programs(0)
    nb = s0_ref.shape[-1]
    nt = q_ref.shape[-1] // nb
    a_log = sc_ref[0, 0:1, 0:1]
    dt_bias = sc_ref[0, 1:2, 0:1]
    sout_ref[...] = s0_ref[...]

    def conv(x_ref, c_ref, w_ref, t):
        acc = None
        for j in range(CONV_K):
            u = t + j - (CONV_K - 1)
            tap = c_ref[u + CONV_K - 1, 0] if u < 0 else x_ref[0, :, u * nb:(u + 1) * nb]
            term = w_ref[j, 0] * tap
            acc = term if acc is None else acc + term
        return acc * _sigmoid(acc)

    for t in range(nt):
        sl = slice(t * nb, (t + 1) * nb)
        q = conv(q_ref, cq_ref, wq_ref, t)
        k = conv(k_ref, ck_ref, wk_ref, t)
        v = conv(v_ref, cv_ref, wv_ref, t)
        q = q * (lax.rsqrt(_col_sum(q * q) + L2_EPS) * (HEAD ** -0.5))
        k = k * lax.rsqrt(_col_sum(k * k) + L2_EPS)
        tl = tail_ref[0:2 * 8, sl]
        pick = lambda row: _col_sum(jnp.where(_iota((2 * 8, 1), 0) == row, tl, 0.0))
        beta = _sigmoid(pick(h))
        eg = jnp.exp(-jnp.exp(a_log) * _softplus(pick(nh + h) + dt_bias))
        k_scr[...] = k
        q_scr[...] = q

        def body1(g, acc):
            base = pl.multiple_of(g * 8, 8)
            kg = k_scr[pl.ds(base, 8), :]
            for j in range(8):
                acc = acc + sout_ref[0, base + j] * kg[j:j + 1]
            return acc

        sk = lax.fori_loop(0, HEAD // 8, body1, jnp.zeros((HEAD, nb), F32))
        v_new = beta * (v - eg * sk)

        def body2(g, acc):
            base = pl.multiple_of(g * 8, 8)
            kg = k_scr[pl.ds(base, 8), :]
            qg = q_scr[pl.ds(base, 8), :]
            for j in range(8):
                s = eg * sout_ref[0, base + j] + kg[j:j + 1] * v_new
                sout_ref[0, base + j] = s
                acc = acc + s * qg[j:j + 1]
            return acc

        o = lax.fori_loop(0, HEAD // 8, body2, jnp.zeros((HEAD, nb), F32))
        o = o * lax.rsqrt(_col_sum(o * o) * (1.0 / HEAD) + RMS_EPS) * nw_ref[...]
        z = z_ref[0, :, sl]
        y_ref[0, :, sl] = o * (z * _sigmoid(z))


def _gdn_t_call(ptb, conv_t, s0, cw, scal, nw):
    nh, _, _, nb = s0.shape
    rows = ptb.shape[1]
    p3 = ptb.reshape(-1, HEAD, rows)
    c4 = conv_t.reshape(CONV_K - 1, -1, HEAD, nb)
    w4 = cw.reshape(CONV_K, -1, HEAD, 1)
    tap = lambda first: pl.BlockSpec((CONV_K - 1, 1, HEAD, nb), lambda h: (0, first + h, 0, 0))
    wsp = lambda first: pl.BlockSpec((CONV_K, 1, HEAD, 1), lambda h: (0, first + h, 0, 0))
    state = pl.BlockSpec((1, HEAD, HEAD, nb), lambda h: (h, 0, 0, 0))
    tail_blk = 4 * nh * HEAD // PAIR
    return pl.pallas_call(
        _gdn_t_kernel,
        grid=(nh,),
        in_specs=[
            _head_blocks(p3, 0), _head_blocks(p3, nh), _head_blocks(p3, 2 * nh), _head_blocks(p3, 3 * nh),
            pl.BlockSpec((PAIR, rows), lambda h: (tail_blk, 0)),
            tap(0), tap(nh), tap(2 * nh), wsp(0), wsp(nh), wsp(2 * nh),
            pl.BlockSpec((1,) + scal.shape[1:], lambda h: (h, 0, 0)),
            pl.BlockSpec(nw.shape, lambda h: (0, 0)),
            state,
        ],
        out_specs=[pl.BlockSpec((1, HEAD, rows), lambda h: (h, 0, 0)), state],
        out_shape=[jax.ShapeDtypeStruct((nh, HEAD, rows), F32), jax.ShapeDtypeStruct(s0.shape, F32)],
        scratch_shapes=[pltpu.VMEM((HEAD, nb), F32), pltpu.VMEM((HEAD, nb), F32)],
        compiler_params=pltpu.CompilerParams(
            dimension_semantics=("parallel",), vmem_limit_bytes=VMEM_LIMIT),
        name="gdn_t",
    )(p3, p3, p3, p3, ptb, c4, c4, c4, w4, w4, w4, scal, nw, s0)


def _hgrn_t_kernel(q_ref, f_ref, v_ref, z_ref, lbl_ref, nw_ref, s0_ref, y_ref, sout_ref,
                   f_scr, k_scr, q_scr, *, layer):
    nb = s0_ref.shape[-1]
    nt = q_ref.shape[-1] // nb
    logits = lbl_ref[:, 0]
    ex = jnp.exp(logits - jnp.max(logits, axis=0, keepdims=True))
    gam = ex / jnp.sum(ex, axis=0, keepdims=True)
    lb = jnp.sum(gam[0:layer + 1], axis=0) - gam[0]
    sout_ref[...] = s0_ref[...]

    for t in range(nt):
        sl = slice(t * nb, (t + 1) * nb)
        qp = q_ref[0, :, sl]
        f = f_ref[0, :, sl]
        v = v_ref[0, :, sl]
        z = z_ref[0, :, sl]
        q_scr[...] = qp * _sigmoid(qp)
        f_scr[...] = lb + (1.0 - lb) * _sigmoid(f)
        k_scr[...] = (1.0 - lb) * _sigmoid(-f)

        def body(g, acc):
            base = pl.multiple_of(g * 8, 8)
            fg, kg, qg = (ref[pl.ds(base, 8), :] for ref in (f_scr, k_scr, q_scr))
            for j in range(8):
                s = fg[j:j + 1] * sout_ref[0, base + j] + kg[j:j + 1] * v
                sout_ref[0, base + j] = s
                acc = acc + qg[j:j + 1] * s
            return acc

        o = lax.fori_loop(0, HEAD // 8, body, jnp.zeros((HEAD, nb), F32))
        o = o * lax.rsqrt(_col_sum(o * o) * (1.0 / HEAD) + RMS_EPS) * nw_ref[...]
        y_ref[0, :, sl] = o * _sigmoid(z)


def _hgrn_t_call(ptc, s0, lbl4, nw, layer):
    nh, _, _, nb = s0.shape
    rows = ptc.shape[1]
    p3 = ptc.reshape(-1, HEAD, rows)
    state = pl.BlockSpec((1, HEAD, HEAD, nb), lambda h: (h, 0, 0, 0))
    return pl.pallas_call(
        functools.partial(_hgrn_t_kernel, layer=layer),
        grid=(nh,),
        in_specs=[
            _head_blocks(p3, 0), _head_blocks(p3, nh), _head_blocks(p3, 2 * nh), _head_blocks(p3, 3 * nh),
            pl.BlockSpec((lbl4.shape[0], 1, HEAD, 1), lambda h: (0, h, 0, 0)),
            pl.BlockSpec(nw.shape, lambda h: (0, 0)),
            state,
        ],
        out_specs=[pl.BlockSpec((1, HEAD, rows), lambda h: (h, 0, 0)), state],
        out_shape=[jax.ShapeDtypeStruct((nh, HEAD, rows), F32), jax.ShapeDtypeStruct(s0.shape, F32)],
        scratch_shapes=[pltpu.VMEM((HEAD, nb), F32)] * 3,
        compiler_params=pltpu.CompilerParams(
            dimension_semantics=("parallel",), vmem_limit_bytes=VMEM_LIMIT),
        name="hgrn_t",
    )(p3, p3, p3, p3, lbl4, nw, s0)


def _ffn_kernel(x_ref, ya_ref, yb_ref, yc_ref, woa_ref, wob_ref, woc_ref, gm_ref, nw_ref,
                sc_ref, sh_ref, gf_ref, win_ref, wd_ref, fnw_ref, xo_ref, *yo_ref, th, y_transposed):
    ff = wd_ref.shape[0]
    dims = _TN if y_transposed else _NN
    proj = lambda y_ref, w_ref: lax.dot_general(y_ref[...].astype(BF16), w_ref[...], dims,
                                                preferred_element_type=F32)
    mix = proj(ya_ref, woa_ref) + proj(yb_ref, wob_ref) + proj(yc_ref, woc_ref)
    x1 = x_ref[...] + gm_ref[0] * mix
    h = _norm_mod(x1, nw_ref[...], sc_ref[0], sh_ref[0]).astype(BF16)
    acc = None
    for j in range(ff // th):
        gate = jnp.dot(h, win_ref[:, j * th:(j + 1) * th], preferred_element_type=F32)
        up = jnp.dot(h, win_ref[:, ff + j * th:ff + (j + 1) * th], preferred_element_type=F32)
        act = (gate * _sigmoid(gate) * up).astype(BF16)
        t = jnp.dot(act, wd_ref[j * th:(j + 1) * th, :], preferred_element_type=F32)
        acc = t if acc is None else acc + t
    xo = x1 + gf_ref[0] * acc
    xo_ref[...] = xo
    if yo_ref:
        yo_ref[0][...] = xo * lax.rsqrt(jnp.mean(xo * xo, axis=-1, keepdims=True) + RMS_EPS) * fnw_ref[...]


def _ffn_call(x, ya, yb, yc, woa, wob, woc, gate_m, nw, scale_f, shift_f, gate_f,
              w_in, w_down, fnw, *, tm, th, tiles_per_seq, final, y_transposed=False):
    rows, d = x.shape
    row = lambda a: pl.BlockSpec((tm, a.shape[1]), lambda i: (i, 0))
    yspec = (lambda a: pl.BlockSpec((a.shape[0], tm), lambda i: (0, i))) if y_transposed else row
    const = lambda a: pl.BlockSpec(a.shape, lambda i: (0, 0), pipeline_mode=pl.Buffered(1))
    mod = lambda a: _mod_spec(a, tm, tiles_per_seq)
    n_out = 2 if final else 1
    return pl.pallas_call(
        functools.partial(_ffn_kernel, th=th, y_transposed=y_transposed),
        grid=(rows // tm,),
        in_specs=[
            row(x), yspec(ya), yspec(yb), yspec(yc), const(woa), const(wob), const(woc),
            mod(gate_m), const(nw), mod(scale_f), mod(shift_f), mod(gate_f),
            const(w_in), const(w_down), const(fnw),
        ],
        out_specs=[row(x)] * n_out,
        out_shape=[jax.ShapeDtypeStruct((rows, d), F32)] * n_out,
        compiler_params=pltpu.CompilerParams(
            dimension_semantics=("parallel",), vmem_limit_bytes=VMEM_LIMIT),
        name="ffn",
    )(x, ya, yb, yc, woa, wob, woc, gate_m, nw, scale_f, shift_f, gate_f, w_in, w_down, fnw)


def _block_ones(n, blk):
    i = np.arange(n) // blk
    return jnp.asarray(i[:, None] == i[None, :], BF16)


def _head_select(first_lane, nh):
    m = np.zeros((PAIR, nh * HEAD), np.float32)
    for h in range(nh):
        m[first_lane + h, h * HEAD:(h + 1) * HEAD] = 1.0
    return jnp.asarray(m, BF16)


def _lane_place(x, first_lane):
    return jnp.zeros((PAIR,), F32).at[first_lane:first_lane + x.shape[0]].set(x)


def _run_group(x, t_valid, mods, states, wts, *, tm, tm_ffn, th, bt):
    b, tp, d = x.shape
    rows = b * tp
    tm, tm_ffn = min(tm, rows), min(tm_ffn, rows)
    per_token_mod = tp < max(tm, tm_ffn)
    cin = min(tp, CHUNK)
    shift0, rwkv0, conv0, gdn0, hgrn0 = states
    nl = len(wts)
    new = ([], [], [], [], [])
    xr = x.reshape(rows, d)
    y = None
    for l in range(nl):
        w = wts[l]
        m = mods[l]

        def mod(i, tile, m=m):
            if per_token_mod:
                return jnp.repeat(m[:, i], tp, axis=0).reshape(rows // tile, tile, d)
            return m[:, i].reshape(b, 1, d)

        shift_m, scale_m = mod(0, tm), mod(1, tm)
        gate_m, shift_f, scale_f, gate_f = (mod(i, tm_ffn) for i in range(2, 6))
        p_a, p_b, p_c = _inproj_call(xr, w["norm_mix"], scale_m, shift_m, w["wta"], w["wtb"], w["wtc"],
                                     tm, max(tp // tm, 1))
        x_last = xr.reshape(b, tp, d)[:, t_valid - 1]
        h_last, p_prev = _lastrow_call(x_last, w["norm_mix"], m[:, 1], m[:, 0], shift0[l], w["wta"])
        p_a = p_a.reshape(b, tp, -1)
        p_b = p_b.reshape(b, tp, -1)
        p_c = p_c.reshape(b, tp, -1)
        y_a, s_a = _rwkv_call(p_a, p_prev, rwkv0[l], w["mu"], w["rwkv_vec"], w["lora"], w["gpair"],
                              cin=cin, t_valid=t_valid, bt=bt)
        y_b, s_b = _gdn_call(p_b, conv0[l], gdn0[l], w["conv_w"], w["gdn_tail"], w["gdn_norm"],
                             w["gpair"], w["bsel"], w["gsel"], cin=cin, t_valid=t_valid, bt=bt)
        hc = CHUNK if tp >= CHUNK else HBLK
        y_c, s_c = _hgrn_call(p_c, hgrn0[l], w["lb_logits"], w["hgrn_norm"], w["gpair"],
                              cin=min(tp, hc), c=hc, t_valid=t_valid, bt=bt, layer=l)
        qkvw = conv0.shape[-1]
        keep = min(t_valid, CONV_K - 1)
        conv_new = jnp.concatenate([conv0[l], p_b[:, t_valid - keep:t_valid, :qkvw]], axis=1)[:, -(CONV_K - 1):]
        res = _ffn_call(xr, y_a.reshape(rows, -1), y_b.reshape(rows, -1), y_c.reshape(rows, -1),
                        w["woa"], w["wob"], w["woc"], gate_m, w["norm_ffn"], scale_f, shift_f, gate_f,
                        w["w_ffn_in"], w["w_ffn_out"], w["final_norm"],
                        tm=tm_ffn, th=th, tiles_per_seq=max(tp // tm_ffn, 1), final=l == nl - 1)
        xr = res[0]
        for acc, s in zip(new, (h_last, s_a, conv_new, s_b, s_c)):
            acc.append(s)
    y = res[1].reshape(b, tp, d)[:, :t_valid]
    return y, [jnp.stack(acc) for acc in new]


def _run_steps(x, mods, states, wts, *, tm_ffn, th):
    b, t, d = x.shape
    rows = t * b
    tm_ffn = min(tm_ffn, rows)
    shift0, rwkv0, conv0, gdn0, hgrn0 = states
    to_lanes = lambda s: jnp.transpose(s, (0, 2, 3, 4, 1))
    rwkv_t, gdn_t, hgrn_t = to_lanes(rwkv0), to_lanes(gdn0), to_lanes(hgrn0)
    conv_t = jnp.transpose(conv0, (0, 2, 3, 1))
    nl = len(wts)
    new = ([], [], [], [], [])
    xr = jnp.transpose(x, (1, 0, 2)).reshape(rows, d)
    for l in range(nl):
        w = wts[l]
        m = mods[l]
        mod = lambda i, m=m: jnp.tile(m[:, i], (t, 1))
        tiled = lambda a: a.reshape(rows // tm_ffn, tm_ffn, d)
        pta, ptb, ptc, qkv_rows, h_last, pprev_t = _inproj_t_call(
            xr, w["norm_mix"], mod(1), mod(0), shift0[l], w["wta"], w["wtb"], w["wtc"], w["qkvw"])
        y_a, s_a = _rwkv_t_call(pta, pprev_t, rwkv_t[l], w["mu"], w["rwkv_vec_t"], w["lora_t"])
        y_b, s_b = _gdn_t_call(ptb, conv_t[l], gdn_t[l], w["conv_w_t"], w["gdn_scal"], w["gdn_norm_t"])
        y_c, s_c = _hgrn_t_call(ptc, hgrn_t[l], w["lb_logits_t"], w["hgrn_norm_t"], l)
        xp = jnp.concatenate([jnp.transpose(conv0[l], (1, 0, 2)), qkv_rows.reshape(t, b, -1)], axis=0)
        conv_new = jnp.transpose(xp[-(CONV_K - 1):], (1, 0, 2))
        res = _ffn_call(xr, y_a.reshape(-1, rows), y_b.reshape(-1, rows), y_c.reshape(-1, rows),
                        w["woa"], w["wob"], w["woc"], tiled(mod(2)), w["norm_ffn"],
                        tiled(mod(4)), tiled(mod(3)), tiled(mod(5)),
                        w["w_ffn_in"], w["w_ffn_out"], w["final_norm"],
                        tm=tm_ffn, th=th, tiles_per_seq=1, final=l == nl - 1, y_transposed=True)
        xr = res[0]
        for acc, s in zip(new, (h_last, s_a, conv_new, s_b, s_c)):
            acc.append(s)
    y = jnp.transpose(res[1].reshape(t, b, d), (1, 0, 2))
    from_lanes = lambda s: jnp.transpose(jnp.stack(s), (0, 4, 1, 2, 3))
    return y, [jnp.stack(new[0]), from_lanes(new[1]), jnp.stack(new[2]), from_lanes(new[3]), from_lanes(new[4])]


def kernel(x_prompt, x_sample, c_prompt, c_sample, state_rwkv_shift, state_rwkv, state_gdn_conv, state_gdn, state_hgrn, w_ada, b_ada, norm_mix_w, w_in, rwkv_mu, rwkv_w0, rwkv_w2, rwkv_a0, rwkv_a2, rwkv_g2, rwkv_k_k, rwkv_k_a, rwkv_r_k, rwkv_ln_w, rwkv_ln_b, gdn_conv_w, gdn_A_log, gdn_dt_bias, gdn_norm_w, hgrn_lb_logits, hgrn_norm_w, w_out, norm_ffn_w, w_ffn_in, w_ffn_out, final_norm_w):
    nl, d, _ = w_in.shape
    a_heads = state_rwkv.shape[2]
    b_heads = state_gdn.shape[2]
    c_heads = state_hgrn.shape[2]
    aw, bw, cw = a_heads * HEAD, b_heads * HEAD, c_heads * HEAD
    lw, la, lg = rwkv_w2.shape[1], rwkv_a2.shape[1], rwkv_g2.shape[1]
    a_cols = 3 * aw + lw + la + lg
    qkvw = 3 * bw
    b_cols = qkvw + 2 * b_heads + bw
    assert (aw, bw, cw) == (3 * PAIR, 3 * PAIR, 2 * PAIR) and lw + la + lg == PAIR and lw == 32 and la == 32

    bp, tpr, _ = x_prompt.shape
    bs, ts, _ = x_sample.shape

    gpair = _block_ones(PAIR, HEAD)
    bsel = _head_select(0, b_heads)
    gsel = _head_select(b_heads, b_heads)
    wts = []
    for l in range(nl):
        wit = jnp.swapaxes(w_in[l], 0, 1).astype(BF16)
        wbt = wit[a_cols:a_cols + b_cols]
        wtb = jnp.concatenate(
            [wbt[:qkvw], wbt[qkvw + 2 * b_heads:], wbt[qkvw:qkvw + 2 * b_heads],
             jnp.zeros((PAIR - 2 * b_heads, d), BF16)], axis=0)
        lora = jnp.zeros((PAIR, 3 * aw), F32)
        lora = lora.at[0:lw, 0:aw].set(rwkv_w2[l])
        lora = lora.at[lw:lw + la, aw:2 * aw].set(rwkv_a2[l])
        lora = lora.at[lw + la:, 2 * aw:].set(rwkv_g2[l])
        rwkv_vec = jnp.stack([rwkv_w0[l], rwkv_a0[l], rwkv_k_k[l], rwkv_k_a[l], rwkv_r_k[l].reshape(-1),
                              rwkv_ln_w[l], rwkv_ln_b[l], jnp.zeros((aw,), F32)])
        wo = w_out[l].astype(BF16)
        hcols = lambda a: a.reshape(a.shape[0], a_heads, HEAD).transpose(1, 2, 0)
        lora_t = jnp.concatenate([
            jnp.pad(hcols(rwkv_w2[l]), ((0, 0), (0, 0), (0, la + lg))),
            jnp.pad(hcols(rwkv_a2[l]), ((0, 0), (0, 0), (lw, lg))),
            jnp.pad(hcols(rwkv_g2[l]), ((0, 0), (0, 0), (lw + la, 0)))], axis=1)
        steps = dict(
            rwkv_vec_t=rwkv_vec[:7].reshape(7, a_heads, HEAD, 1), lora_t=lora_t.astype(BF16),
            conv_w_t=gdn_conv_w[l],
            gdn_scal=jnp.zeros((b_heads, 8, PAIR), F32)
                .at[:, 0].set(jnp.broadcast_to(gdn_A_log[l][:, None], (b_heads, PAIR)))
                .at[:, 1].set(jnp.broadcast_to(gdn_dt_bias[l][:, None], (b_heads, PAIR))),
            gdn_norm_t=gdn_norm_w[l].reshape(HEAD, 1),
            lb_logits_t=hgrn_lb_logits.reshape(nl, c_heads, HEAD, 1),
            hgrn_norm_t=hgrn_norm_w[l].reshape(HEAD, 1),
        )
        wts.append(dict(
            **steps,
            norm_mix=norm_mix_w[l].reshape(1, d),
            wta=wit[:a_cols], wtb=wtb, wtc=wit[a_cols + b_cols:], qkvw=qkvw,
            mu=rwkv_mu[l].reshape(1, a_cols), rwkv_vec=rwkv_vec, lora=lora.astype(BF16),
            gpair=gpair, bsel=bsel, gsel=gsel,
            conv_w=jnp.concatenate([gdn_conv_w[l], jnp.zeros((8 - CONV_K, qkvw), F32)], axis=0),
            gdn_tail=jnp.zeros((8, PAIR), F32).at[0].set(_lane_place(gdn_A_log[l], b_heads))
                                              .at[1].set(_lane_place(gdn_dt_bias[l], b_heads)),
            gdn_norm=jnp.tile(gdn_norm_w[l], b_heads).reshape(1, bw),
            lb_logits=hgrn_lb_logits,
            hgrn_norm=jnp.tile(hgrn_norm_w[l], c_heads).reshape(1, cw),
            woa=wo[:aw], wob=wo[aw:aw + bw], woc=wo[aw + bw:],
            norm_ffn=norm_ffn_w[l].reshape(1, d),
            w_ffn_in=w_ffn_in[l].astype(BF16), w_ffn_out=w_ffn_out[l].astype(BF16),
            final_norm=final_norm_w.reshape(1, d),
        ))

    mod = _ada_call(jnp.concatenate([c_prompt, c_sample], axis=0), w_ada, b_ada)
    mod = mod.reshape(nl, bp + bs, 6, d)
    mods_p = [mod[l, :bp] for l in range(nl)]
    mods_s = [mod[l, bp:] for l in range(nl)]

    zeros = lambda s: jnp.zeros((nl, bp) + s.shape[2:], s.dtype)
    states_p = tuple(zeros(s) for s in (state_rwkv_shift, state_rwkv, state_gdn_conv, state_gdn, state_hgrn))
    states_s = (state_rwkv_shift, state_rwkv, state_gdn_conv, state_gdn, state_hgrn)

    ff = w_ffn_out.shape[1]
    th = ff // 2
    y_p, new_p = _run_group(x_prompt, tpr, mods_p, states_p, wts, tm=256, tm_ffn=512, th=th, bt=2)
    y_s, new_s = _run_steps(x_sample, mods_s, states_s, wts, tm_ffn=256, th=th)
    return (y_p, y_s, *new_p, *new_s)
```

```python
import functools

import numpy as np
import jax
import jax.numpy as jnp
from jax import lax
from jax.experimental import pallas as pl
from jax.experimental.pallas import tpu as pltpu

F32 = jnp.float32
BF16 = jnp.bfloat16

HEAD = 64
PAIR = 2 * HEAD
CHUNK = 64
HBLK = 16
CONV_K = 4
RMS_EPS = 1e-6
L2_EPS = 1e-6
GN_EPS = 64e-5
VMEM_LIMIT = 56 * 1024 * 1024

TM_INPROJ = 256
TM_FFN = 512
TM_FFN_STEPS = 256
FFN_HIDDEN_TILES = 2
SEQS_PER_STEP = 4

_NN = (((1,), (0,)), ((), ()))
_NT = (((1,), (1,)), ((), ()))
_TN = (((0,), (0,)), ((), ()))


def _dot(a, b, dims=_NN):
    return lax.dot_general(a.astype(BF16), b.astype(BF16), dims, preferred_element_type=F32)


def _dot_sel(x, m, pieces=3):
    ps = _bf16_pieces(x, pieces)
    return lax.dot_general(jnp.concatenate(ps, axis=1), jnp.concatenate([m] * pieces, axis=0), _NN,
                           preferred_element_type=F32)


def _bf16_pieces(x, pieces):
    ps = []
    rem = x
    for i in range(pieces):
        p = rem.astype(BF16)
        ps.append(p)
        if i + 1 < pieces:
            rem = rem - p.astype(F32)
    return ps


def _seg_sum(x, gpair, pieces=2):
    return jnp.concatenate(
        [_dot_sel(x[:, i:i + PAIR], gpair, pieces) for i in range(0, x.shape[1], PAIR)], axis=1)


def _sigmoid(x):
    return jax.nn.sigmoid(x)


def _softplus(x):
    return jnp.maximum(x, 0.0) + jnp.log1p(jnp.exp(-jnp.abs(x)))


def _iota(shape, dim):
    return lax.broadcasted_iota(jnp.int32, shape, dim)


def _stack_heads(x):
    is_a = _iota((1, PAIR), 1) < HEAD
    return jnp.concatenate([jnp.where(is_a, x, 0.0), jnp.where(is_a, 0.0, x)], axis=0)


def _fold_heads(x, c):
    return x[0:c] + x[c:2 * c]


def _pair_blockdiag_mask():
    return (_iota((PAIR, 1), 0) >> 6) == (_iota((1, PAIR), 1) >> 6)


def _load_pair_state(s_ref, bi, pr):
    sa = s_ref[bi, 2 * pr]
    sb = s_ref[bi, 2 * pr + 1]
    z = jnp.zeros((HEAD, HEAD), F32)
    return jnp.concatenate(
        [jnp.concatenate([sa, z], axis=1), jnp.concatenate([z, sb], axis=1)], axis=0)


def _store_pair_state(s_ref, bi, pr, s):
    s_ref[bi, 2 * pr] = s[0:HEAD, 0:HEAD]
    s_ref[bi, 2 * pr + 1] = s[HEAD:PAIR, HEAD:PAIR]


def _split(x):
    hi = x.astype(BF16)
    return hi, (x - hi.astype(F32)).astype(BF16)


def _dot3(a, b):
    return lax.dot_general(jnp.concatenate([a[0], a[1], a[0]], axis=1),
                           jnp.concatenate([b[0], b[0], b[1]], axis=0), _NN, preferred_element_type=F32)


def _dot_left01(m, x, pieces=3):
    return lax.dot_general(jnp.concatenate([m] * pieces, axis=1),
                           jnp.concatenate(_bf16_pieces(x, pieces), axis=0), _NN,
                           preferred_element_type=F32)


def _neumann_inverse_many(ns, c):
    size = ns[0].shape[0]
    eye = (_iota((size, 1), 0) == _iota((1, size), 1)).astype(F32)
    invs = [eye + n for n in ns]
    pws = [_split(n) for n in ns]
    span = 2
    while span < c:
        pws = [_split(_dot3(pw, pw)) for pw in pws]
        invs = [inv + _dot3(_split(inv), pw) for inv, pw in zip(invs, pws)]
        span *= 2
    return invs


def _ada_kernel(c_ref, w_ref, b_ref, o_ref):
    c = c_ref[...]
    o_ref[0] = _dot(c * _sigmoid(c), w_ref[0]) + b_ref[0]


def _ada_call(c_all, w_ada, b_ada):
    nl, d, n6 = w_ada.shape
    rows = c_all.shape[0]
    tn = n6 // 4
    return pl.pallas_call(
        _ada_kernel,
        grid=(nl, n6 // tn),
        in_specs=[
            pl.BlockSpec((rows, d), lambda l, j: (0, 0)),
            pl.BlockSpec((1, d, tn), lambda l, j: (l, 0, j)),
            pl.BlockSpec((1, 1, tn), lambda l, j: (l, 0, j)),
        ],
        out_specs=pl.BlockSpec((1, rows, tn), lambda l, j: (l, 0, j)),
        out_shape=jax.ShapeDtypeStruct((nl, rows, n6), F32),
        compiler_params=pltpu.CompilerParams(
            dimension_semantics=("parallel", "parallel"), vmem_limit_bytes=VMEM_LIMIT),
        name="ada",
    )(c_all, w_ada, b_ada.reshape(nl, 1, n6))


def _norm_mod(x, nw, scale, shift):
    y = x * lax.rsqrt(jnp.mean(x * x, axis=-1, keepdims=True) + RMS_EPS) * nw
    return y * (1.0 + scale) + shift


def _inproj_kernel(x_ref, nw_ref, sc_ref, sh_ref, wa_ref, wb_ref, wc_ref, pa_ref, pb_ref, pc_ref):
    h = _norm_mod(x_ref[...], nw_ref[...], sc_ref[0], sh_ref[0]).astype(BF16)
    pa_ref[...] = lax.dot_general(h, wa_ref[...], _NT, preferred_element_type=F32)
    pb_ref[...] = lax.dot_general(h, wb_ref[...], _NT, preferred_element_type=F32)
    pc_ref[...] = lax.dot_general(h, wc_ref[...], _NT, preferred_element_type=F32)


def _mod_spec(mod, tm, tiles_per_seq):
    if mod.shape[1] == 1:
        return pl.BlockSpec((1, 1, mod.shape[2]), lambda i, *_: (i // tiles_per_seq, 0, 0))
    return pl.BlockSpec((1, tm, mod.shape[2]), lambda i, *_: (i, 0, 0))


def _inproj_call(x, nw, scale, shift, wa, wb, wc, tm, tiles_per_seq):
    rows, d = x.shape
    full = lambda a: pl.BlockSpec(a.shape, lambda i: (0, 0), pipeline_mode=pl.Buffered(1))
    outs = [jax.ShapeDtypeStruct((rows, w.shape[0]), F32) for w in (wa, wb, wc)]
    return pl.pallas_call(
        _inproj_kernel,
        grid=(rows // tm,),
        in_specs=[
            pl.BlockSpec((tm, d), lambda i: (i, 0)),
            full(nw),
            _mod_spec(scale, tm, tiles_per_seq),
            _mod_spec(shift, tm, tiles_per_seq),
            full(wa), full(wb), full(wc),
        ],
        out_specs=[pl.BlockSpec((tm, w.shape[0]), lambda i: (i, 0)) for w in (wa, wb, wc)],
        out_shape=outs,
        compiler_params=pltpu.CompilerParams(
            dimension_semantics=("parallel",), vmem_limit_bytes=VMEM_LIMIT),
        name="inproj",
    )(x, nw, scale, shift, wa, wb, wc)


def _lastrow_kernel(x_ref, nw_ref, sc_ref, sh_ref, hprev_ref, wa_ref, h_ref, pprev_ref):
    h_ref[...] = _norm_mod(x_ref[...], nw_ref[...], sc_ref[...], sh_ref[...])
    pprev_ref[...] = lax.dot_general(hprev_ref[...].astype(BF16), wa_ref[...], _NT,
                                     preferred_element_type=F32)


def _lastrow_call(x_last, nw, scale, shift, h_prev, wa):
    b, d = x_last.shape
    return pl.pallas_call(
        _lastrow_kernel,
        out_shape=[jax.ShapeDtypeStruct((b, d), F32), jax.ShapeDtypeStruct((b, wa.shape[0]), F32)],
        compiler_params=pltpu.CompilerParams(vmem_limit_bytes=VMEM_LIMIT),
        name="lastrow",
    )(x_last, nw, scale, shift, h_prev, wa)


def _rwkv_kernel(p_ref, pprev_ref, s0_ref, mu_ref, vec_ref, wl_ref, g_ref,
                 y_ref, sout_ref, s_scr, prev_scr, *, bt):
    c = CHUNK
    ci = pl.program_id(1)
    nci = pl.num_programs(1)
    aw = 3 * PAIR
    rows = _iota((c, 1), 0)
    lane = _iota((1, PAIR), 1)
    vec = vec_ref[...]
    w0, a0, k_k, k_a, r_k, ln_w, ln_b = (vec[i:i + 1] for i in range(7))
    gpair = g_ref[...]
    tri = (rows >= _iota((1, c), 1)).astype(BF16)
    r2 = _iota((2 * c, 1), 0) & (c - 1)
    c2 = _iota((1, 2 * c), 1) & (c - 1)
    strict = r2 > c2
    incl = r2 >= c2
    bd = _pair_blockdiag_mask()

    @pl.when(ci == 0)
    def _():
        for bi in range(bt):
            for pr in range(3):
                s_scr[bi, pr] = _load_pair_state(s0_ref, bi, pr)
            prev_scr[bi] = pprev_ref[bi]

    pre = []
    for bi in range(bt):
        p = p_ref[bi]
        prev = pltpu.roll(p, 1, axis=0)
        prev = jnp.where(rows == 0, prev_scr[bi], prev)
        prev_scr[bi] = p[c - 1:c]
        xs = p + mu_ref[...] * (prev - p)
        r = xs[:, 0:aw]
        k = xs[:, aw:2 * aw]
        v = xs[:, 2 * aw:3 * aw]
        tail = xs[:, 3 * aw:3 * aw + PAIR]
        act = jnp.where(lane < 32, jnp.tanh(tail), jnp.where(lane < 64, tail, _sigmoid(tail)))
        lo = _dot(act, wl_ref[...])
        w = -_softplus(-(w0 + lo[:, 0:aw])) - 0.5
        ld = -jnp.exp(w)
        a = _sigmoid(a0 + lo[:, aw:2 * aw])
        g = lo[:, 2 * aw:3 * aw]
        kk = k * k_k
        kk = kk * lax.rsqrt(_seg_sum(kk * kk, gpair) + L2_EPS)
        k = k * (1.0 + (a - 1.0) * k_a)
        cum = _dot_left01(tri, ld)
        cum_last = cum[c - 1:c]
        e_neg = jnp.exp(-cum)
        a_t = -kk * jnp.exp(cum - ld)
        b_t = kk * a * e_neg
        k_t = k * e_neg
        r_t = r * jnp.exp(cum)
        e_end = jnp.exp(cum_last - cum)
        pre.append(dict(r=r, k=k, v=v, g=g, a_t=a_t, b_t=b_t, k_t=k_t, r_t=r_t,
                        b_end=kk * a * e_end, k_end=k * e_end, d_end=jnp.exp(cum_last)))

    chains = [(bi, pr) for bi in range(bt) for pr in range(3)]
    pair = lambda name: [pre[bi][name][:, PAIR * pr:PAIR * (pr + 1)] for bi, pr in chains]
    a_t, b_t, k_t, r_t, v = pair("a_t"), pair("b_t"), pair("k_t"), pair("r_t"), pair("v")
    b_end, k_end, d_end = pair("b_end"), pair("k_end"), pair("d_end")
    v_st = [_stack_heads(x) for x in v]
    gs = [_dot(jnp.concatenate([_stack_heads(a), _stack_heads(r)], axis=0),
               jnp.concatenate([_stack_heads(b), _stack_heads(k)], axis=0), _NT)
          for a, r, b, k in zip(a_t, r_t, b_t, k_t)]
    a_ak = [jnp.where(strict, g[0:2 * c, 2 * c:4 * c], 0.0) for g in gs]
    p_rbk = [jnp.concatenate([jnp.where(incl, g[2 * c:4 * c, 0:2 * c], 0.0),
                              jnp.where(incl, g[2 * c:4 * c, 2 * c:4 * c], 0.0)], axis=1) for g in gs]
    tinv = _neumann_inverse_many([jnp.where(strict, g[0:2 * c, 0:2 * c], 0.0) for g in gs], c)
    s = [s_scr[bi, pr] for bi, pr in chains]
    ars = [_dot(jnp.concatenate([a, r], axis=0), s_, _NT) for a, r, s_ in zip(a_t, r_t, s)]
    akv = [_dot(m, x) for m, x in zip(a_ak, v_st)]
    u_st = [_dot(t, _stack_heads(x[0:c]) + y) for t, x, y in zip(tinv, ars, akv)]
    y_st = [_dot(p, jnp.concatenate([u, x], axis=0)) for p, u, x in zip(p_rbk, u_st, v_st)]
    ds = [_dot(jnp.concatenate([_fold_heads(u, c), x], axis=0), jnp.concatenate([b, k], axis=0), _TN)
          for u, x, b, k in zip(u_st, v, b_end, k_end)]
    for i, (bi, pr) in enumerate(chains):
        s_scr[bi, pr] = s[i] * d_end[i] + jnp.where(bd, ds[i], 0.0)

    for bi in range(bt):
        r, k, v_, g = (pre[bi][n] for n in ("r", "k", "v", "g"))
        y = jnp.concatenate([ars[3 * bi + pr][c:2 * c] + _fold_heads(y_st[3 * bi + pr], c)
                             for pr in range(3)], axis=1)
        mean = _seg_sum(y, gpair) * (1.0 / HEAD)
        dy = y - mean
        var = _seg_sum(dy * dy, gpair) * (1.0 / HEAD)
        yn = dy * lax.rsqrt(var + GN_EPS) * ln_w + ln_b
        bonus = _seg_sum(r * k * r_k, gpair) * v_
        y_ref[bi] = (yn + bonus) * g

    @pl.when(ci == nci - 1)
    def _():
        for bi in range(bt):
            for pr in range(3):
                _store_pair_state(sout_ref, bi, pr, s_scr[bi, pr])


def _rwkv_call(p_a, p_prev, s0, mu, vec, wl, gmat, *, bt):
    b, tp, acols = p_a.shape
    nh = s0.shape[1]
    aw = nh * HEAD
    cin = CHUNK
    nc = tp // cin
    const = lambda a: pl.BlockSpec(a.shape, lambda i, j: (0,) * a.ndim)
    kern = functools.partial(_rwkv_kernel, bt=bt)
    return pl.pallas_call(
        kern,
        grid=(b // bt, nc),
        in_specs=[
            pl.BlockSpec((bt, cin, acols), lambda i, j: (i, j, 0)),
            pl.BlockSpec((bt, 1, acols), lambda i, j: (i, 0, 0)),
            pl.BlockSpec((bt, nh, HEAD, HEAD), lambda i, j: (i, 0, 0, 0)),
            const(mu), const(vec), const(wl), const(gmat),
        ],
        out_specs=[
            pl.BlockSpec((bt, cin, aw), lambda i, j: (i, j, 0)),
            pl.BlockSpec((bt, nh, HEAD, HEAD), lambda i, j: (i, 0, 0, 0)),
        ],
        out_shape=[jax.ShapeDtypeStruct((b, tp, aw), F32),
                   jax.ShapeDtypeStruct(s0.shape, F32)],
        scratch_shapes=[pltpu.VMEM((bt, nh // 2, PAIR, PAIR), F32),
                        pltpu.VMEM((bt, 1, acols), F32)],
        compiler_params=pltpu.CompilerParams(
            dimension_semantics=("parallel", "arbitrary"), vmem_limit_bytes=VMEM_LIMIT),
        name="rwkv",
    )(p_a, p_prev.reshape(b, 1, acols), s0, mu, vec, wl, gmat)


def _gdn_kernel(p_ref, conv0_ref, s0_ref, cw_ref, tp_ref, nw_ref, g_ref, bsel_ref, gsel_ref,
                y_ref, sout_ref, s_scr, cbuf, *, bt):
    c = CHUNK
    ci = pl.program_id(1)
    nci = pl.num_programs(1)
    bw = 3 * PAIR
    qkvw = 3 * bw
    rows = _iota((c, 1), 0)
    lane = _iota((1, PAIR), 1)
    nh = 2 * 3
    gpair = g_ref[...]
    tri = (rows >= _iota((1, c), 1)).astype(BF16)
    r2 = _iota((2 * c, 1), 0) & (c - 1)
    c2 = _iota((1, 2 * c), 1) & (c - 1)
    same =(_iota((2 * c, 1), 0) >> 6) == (_iota((1, 2 * c), 1) >> 6)
    strict = same & (r2 > c2)
    incl = same & (r2 >= c2)
    bd = _pair_blockdiag_mask()
    is_a = lane < HEAD
    a_log = tp_ref[0:1]
    dt_bias = tp_ref[1:2]
    is_beta = lane < nh
    is_g = (lane >= nh) & (lane < 2 * nh)

    @pl.when(ci == 0)
    def _():
        for bi in range(bt):
            for pr in range(3):
                s_scr[bi, pr] = _load_pair_state(s0_ref, bi, pr)
            cbuf[bi, 0:8] = jnp.zeros((8, qkvw), F32)
            cbuf[bi, 8 - (CONV_K - 1):8] = conv0_ref[bi]

    pre = []
    for bi in range(bt):
        p = p_ref[bi]
        cbuf[bi, 8:8 + c] = p[:, 0:qkvw]
        conv = None
        for j in range(CONV_K):
            t = cw_ref[j:j + 1] * cbuf[bi, pl.ds(8 - (CONV_K - 1) + j, c), :]
            conv = t if conv is None else conv + t
        cbuf[bi, 0:8] = cbuf[bi, c:c + 8]
        qkv = conv * _sigmoid(conv)
        q = qkv[:, 0:bw]
        k = qkv[:, bw:2 * bw]
        v = qkv[:, 2 * bw:3 * bw]
        z = p[:, qkvw:qkvw + bw]
        tail = p[:, qkvw + bw:qkvw + bw + PAIR]
        q = q * (lax.rsqrt(_seg_sum(q * q, gpair) + L2_EPS) * (HEAD ** -0.5))
        k = k * lax.rsqrt(_seg_sum(k * k, gpair) + L2_EPS)
        beta = jnp.where(is_beta, _sigmoid(tail), 0.0)
        gl = jnp.where(is_g, -jnp.exp(a_log) * _softplus(tail + dt_bias), 0.0)
        gcum = _dot_sel(_dot_left01(tri, gl), gsel_ref[...])
        beta = _dot_sel(beta, bsel_ref[...])
        g_last = gcum[c - 1:c]
        eg = jnp.exp(gcum)
        kb = k * beta
        pre.append(dict(z=z, q=q, k=k, kb=kb, gcum=gcum, kg=kb * eg, qg=q * eg,
                        kd=k * jnp.exp(g_last - gcum), vb=v * beta, d_end=jnp.exp(g_last)))

    chains = [(bi, pr) for bi in range(bt) for pr in range(3)]
    pair = lambda name: [pre[bi][name][:, PAIR * pr:PAIR * (pr + 1)] for bi, pr in chains]
    q, k, kb, gcum, kg, qg = pair("q"), pair("k"), pair("kb"), pair("gcum"), pair("kg"), pair("qg")
    kd, vb, d_end = pair("kd"), pair("vb"), pair("d_end")

    def col(x):
        sw = pltpu.roll(x, HEAD, axis=1)
        return jnp.concatenate([jnp.where(is_a, x, sw), jnp.where(is_a, sw, x)], axis=0)

    gcol = [col(x) for x in gcum]
    diff = [x - x.T for x in gcol]
    dec_s = [jnp.where(strict, jnp.exp(jnp.where(strict, x, 0.0)), 0.0) for x in diff]
    dec_i = [jnp.where(incl, jnp.exp(jnp.where(incl, x, 0.0)), 0.0) for x in diff]
    sc = [_dot(jnp.concatenate([_stack_heads(x), _stack_heads(y)], axis=0), _stack_heads(w), _NT)
          for x, y, w in zip(kb, q, k)]
    tinv = _neumann_inverse_many([-(x[0:2 * c] * d) for x, d in zip(sc, dec_s)], c)
    qk = [x[2 * c:4 * c] * d for x, d in zip(sc, dec_i)]
    s = [s_scr[bi, pr] for bi, pr in chains]
    kqs = [_dot(jnp.concatenate([x, y], axis=0), s_) for x, y, s_ in zip(kg, qg, s)]
    v_new = [_dot(t, _stack_heads(x - y[0:c])) for t, x, y in zip(tinv, vb, kqs)]
    o_st = [_dot(x, y) for x, y in zip(qk, v_new)]
    ds = [_dot(x, _fold_heads(y, c), _TN) for x, y in zip(kd, v_new)]
    for i, (bi, pr) in enumerate(chains):
        s_scr[bi, pr] = s[i] * d_end[i] + jnp.where(bd, ds[i], 0.0)

    for bi in range(bt):
        z = pre[bi]["z"]
        o = jnp.concatenate([kqs[3 * bi + pr][c:2 * c] + _fold_heads(o_st[3 * bi + pr], c)
                             for pr in range(3)], axis=1)
        o = o * lax.rsqrt(_seg_sum(o * o, gpair) * (1.0 / HEAD) + RMS_EPS) * nw_ref[...]
        y_ref[bi] = o * (z * _sigmoid(z))

    @pl.when(ci == nci - 1)
    def _():
        for bi in range(bt):
            for pr in range(3):
                _store_pair_state(sout_ref, bi, pr, s_scr[bi, pr])


def _gdn_call(p_b, conv0, s0, cw, tailp, nw, gmat, bsel, gsel, *, bt):
    b, tp, bcols = p_b.shape
    nh = s0.shape[1]
    bw = nh * HEAD
    cin = CHUNK
    nc = tp // cin
    const = lambda a: pl.BlockSpec(a.shape, lambda i, j: (0,) * a.ndim)
    kern = functools.partial(_gdn_kernel, bt=bt)
    return pl.pallas_call(
        kern,
        grid=(b // bt, nc),
        in_specs=[
            pl.BlockSpec((bt, cin, bcols), lambda i, j: (i, j, 0)),
            pl.BlockSpec((bt, CONV_K - 1, 3 * bw), lambda i, j: (i, 0, 0)),
            pl.BlockSpec((bt, nh, HEAD, HEAD), lambda i, j: (i, 0, 0, 0)),
            const(cw), const(tailp), const(nw), const(gmat), const(bsel), const(gsel),
        ],
        out_specs=[
            pl.BlockSpec((bt, cin, bw), lambda i, j: (i, j, 0)),
            pl.BlockSpec((bt, nh, HEAD, HEAD), lambda i, j: (i, 0, 0, 0)),
        ],
        out_shape=[jax.ShapeDtypeStruct((b, tp, bw), F32),
                   jax.ShapeDtypeStruct(s0.shape, F32)],
        scratch_shapes=[pltpu.VMEM((bt, nh // 2, PAIR, PAIR), F32),
                        pltpu.VMEM((bt, CHUNK + 8, 3 * bw), F32)],
        compiler_params=pltpu.CompilerParams(
            dimension_semantics=("parallel", "arbitrary"), vmem_limit_bytes=VMEM_LIMIT),
        name="gdn",
    )(p_b, conv0, s0, cw, tailp, nw, gmat, bsel, gsel)


def _hgrn_kernel(p_ref, s0_ref, lbl_ref, nw_ref, g_ref, y_ref, sout_ref, s_scr,
                 *, bt, layer):
    c = CHUNK
    ci = pl.program_id(1)
    nci = pl.num_programs(1)
    cw = 2 * PAIR
    rows = _iota((c, 1), 0)
    cols = _iota((1, c), 1)
    sameblk = (rows >> 4) == (cols >> 4)
    tri_tot = jnp.concatenate([sameblk & (rows >= cols), sameblk], axis=0).astype(BF16)
    rows_b = _iota((HBLK, 1), 0)
    bd = _pair_blockdiag_mask()
    gpair = g_ref[...]

    logits = lbl_ref[...]
    ex = jnp.exp(logits - jnp.max(logits, axis=0, keepdims=True))
    gam = ex / jnp.sum(ex, axis=0, keepdims=True)
    lb = jnp.sum(gam[0:layer + 1], axis=0, keepdims=True) - gam[0:1]

    @pl.when(ci == 0)
    def _():
        for bi in range(bt):
            for pr in range(2):
                s_scr[bi, pr] = _load_pair_state(s0_ref, bi, pr).T

    pre = []
    for bi in range(bt):
        p = p_ref[bi]
        qp = p[:, 0:cw]
        f = p[:, cw:2 * cw]
        v = p[:, 2 * cw:3 * cw]
        z = p[:, 3 * cw:4 * cw]
        q = qp * _sigmoid(qp)
        logf = jnp.log(lb + (1.0 - lb) * _sigmoid(f))
        k = (1.0 - lb) * _sigmoid(-f)
        cums = _dot_left01(tri_tot, logf)
        bl = cums[0:c]
        btot = cums[c:2 * c]
        pre.append(dict(q=q, k=k, v=v, z=z, bl=bl, q_in=q * jnp.exp(bl),
                        k_out=k * jnp.exp(btot - bl), d_blk=jnp.exp(btot)))

    chains = [(bi, pr) for bi in range(bt) for pr in range(2)]
    nblk = c // HBLK

    def blk_of(name, bi, pr, blk):
        return pre[bi][name][HBLK * blk:HBLK * (blk + 1), PAIR * pr:PAIR * (pr + 1)]

    intra = {}
    ds = {}
    for blk in range(nblk):
        for bi, pr in chains:
            qb, kb, vb, bb = (blk_of(n, bi, pr, blk) for n in ("q", "k", "v", "bl"))
            xs = []
            for j in range(HBLK):
                e = jnp.exp(jnp.minimum(bb - bb[j:j + 1], 0.0))
                xs.append(qb * e * kb[j:j + 1])
            att = _dot_sel(jnp.concatenate(xs, axis=0), gpair, pieces=2)
            acc = None
            for j in range(HBLK):
                t = jnp.where(rows_b >= j, att[HBLK * j:HBLK * (j + 1)], 0.0) * vb[j:j + 1]
                acc = t if acc is None else acc + t
            intra[bi, pr, blk] = acc
            ds[bi, pr, blk] = jnp.where(bd, _dot(vb, blk_of("k_out", bi, pr, blk), _TN), 0.0)

    s = {ch: s_scr[ch] for ch in chains}
    outs = {}
    for blk in range(nblk):
        for bi, pr in chains:
            outs[bi, pr, blk] = intra[bi, pr, blk] + _dot(blk_of("q_in", bi, pr, blk), s[bi, pr], _NT)
            s[bi, pr] = s[bi, pr] * blk_of("d_blk", bi, pr, blk)[0:1] + ds[bi, pr, blk]
    for ch in chains:
        s_scr[ch] = s[ch]

    for bi in range(bt):
        z = pre[bi]["z"]
        o = jnp.concatenate(
            [jnp.concatenate([outs[bi, pr, blk] for blk in range(nblk)], axis=0) for pr in range(2)], axis=1)
        o = o * lax.rsqrt(_seg_sum(o * o, gpair) * (1.0 / HEAD) + RMS_EPS) * nw_ref[...]
        y_ref[bi] = o * _sigmoid(z)

    @pl.when(ci == nci - 1)
    def _():
        for bi in range(bt):
            for pr in range(2):
                _store_pair_state(sout_ref, bi, pr, s_scr[bi, pr].T)


def _hgrn_call(p_c, s0, lbl, nw, gmat, *, bt, layer):
    b, tp, ccols = p_c.shape
    nh = s0.shape[1]
    cw = nh * HEAD
    cin = CHUNK
    nc = tp // cin
    const = lambda a: pl.BlockSpec(a.shape, lambda i, j: (0,) * a.ndim)
    kern = functools.partial(_hgrn_kernel, bt=bt, layer=layer)
    return pl.pallas_call(
        kern,
        grid=(b // bt, nc),
        in_specs=[
            pl.BlockSpec((bt, cin, ccols), lambda i, j: (i, j, 0)),
            pl.BlockSpec((bt, nh, HEAD, HEAD), lambda i, j: (i, 0, 0, 0)),
            const(lbl), const(nw), const(gmat),
        ],
        out_specs=[
            pl.BlockSpec((bt, cin, cw), lambda i, j: (i, j, 0)),
            pl.BlockSpec((bt, nh, HEAD, HEAD), lambda i, j: (i, 0, 0, 0)),
        ],
        out_shape=[jax.ShapeDtypeStruct((b, tp, cw), F32),
                   jax.ShapeDtypeStruct(s0.shape, F32)],
        scratch_shapes=[pltpu.VMEM((bt, nh // 2, PAIR, PAIR), F32)],
        compiler_params=pltpu.CompilerParams(
            dimension_semantics=("parallel", "arbitrary"), vmem_limit_bytes=VMEM_LIMIT),
        name="hgrn",
    )(p_c, s0, lbl, nw, gmat)


def _inproj_t_kernel(x_ref, nw_ref, sc_ref, sh_ref, hprev_ref, wta_ref, wtb_ref, wtc_ref,
                     pta_ref, ptb_ref, ptc_ref, rows_ref, hlast_ref, pprev_ref):
    h = _norm_mod(x_ref[...], nw_ref[...], sc_ref[...], sh_ref[...])
    hb = h.astype(BF16)
    nt = lambda w, a: lax.dot_general(w, a, _NT, preferred_element_type=F32)
    pta_ref[...] = nt(wta_ref[...], hb)
    ptb_ref[...] = nt(wtb_ref[...], hb)
    ptc_ref[...] = nt(wtc_ref[...], hb)
    rows_ref[...] = nt(hb, wtb_ref[0:rows_ref.shape[1], :])
    nb = hlast_ref.shape[0]
    hlast_ref[...] = h[h.shape[0] - nb:]
    pprev_ref[...] = nt(wta_ref[...], hprev_ref[...].astype(BF16))


def _inproj_t_call(x, nw, scale, shift, h_prev, wta, wtb, wtc, qkvw):
    rows, d = x.shape
    nb = h_prev.shape[0]
    sds = jax.ShapeDtypeStruct
    return pl.pallas_call(
        _inproj_t_kernel,
        out_shape=[sds((wta.shape[0], rows), F32), sds((wtb.shape[0], rows), F32),
                   sds((wtc.shape[0], rows), F32), sds((rows, qkvw), F32),
                   sds((nb, d), F32), sds((wta.shape[0], nb), F32)],
        compiler_params=pltpu.CompilerParams(vmem_limit_bytes=VMEM_LIMIT),
        name="inproj_t",
    )(x, nw, scale, shift, h_prev, wta, wtb, wtc)


def _col_sum(x):
    return jnp.sum(x, axis=0, keepdims=True)


def _rwkv_t_kernel(r_ref, k_ref, v_ref, tail_ref, rp_ref, kp_ref, vp_ref, tailp_ref,
                   mur_ref, muk_ref, muv_ref, mut_ref, vec_ref, wl_ref, s0_ref,
                   y_ref, sout_ref, v_scr, y_scr):
    nb = s0_ref.shape[-1]
    nt = r_ref.shape[-1] // nb
    w0, a0, k_k, k_a, r_k, ln_w, ln_b = (vec_ref[i, 0] for i in range(7))
    sout_ref[...] = s0_ref[...]

    def shifted(ref, pref, mu, idx, t):
        cur = ref[idx, :, t * nb:(t + 1) * nb]
        prv = pref[idx] if t == 0 else ref[idx, :, (t - 1) * nb:t * nb]
        return cur + mu * (prv - cur)

    for t in range(nt):
        r = shifted(r_ref, rp_ref, mur_ref[0], 0, t)
        k = shifted(k_ref, kp_ref, muk_ref[0], 0, t)
        v = shifted(v_ref, vp_ref, muv_ref[0], 0, t)
        t0 = shifted(tail_ref, tailp_ref, mut_ref[0], 0, t)
        t1 = shifted(tail_ref, tailp_ref, mut_ref[1], 1, t)
        act = jnp.concatenate([jnp.tanh(t0[0:32]), t0[32:64], _sigmoid(t1)], axis=0)
        lo = jnp.dot(wl_ref[0], act.astype(BF16), preferred_element_type=F32)
        w = -_softplus(-(w0 + lo[0:HEAD])) - 0.5
        dec = jnp.exp(-jnp.exp(w))
        a = _sigmoid(a0 + lo[HEAD:2 * HEAD])
        g = lo[2 * HEAD:3 * HEAD]
        kk = k * k_k
        kk = kk * lax.rsqrt(_col_sum(kk * kk) + L2_EPS)
        k = k * (1.0 + (a - 1.0) * k_a)
        a_t = -kk
        b_t = kk * a
        v_scr[...] = v

        def body(g, carry):
            base = pl.multiple_of(g * 8, 8)
            vg = v_scr[pl.ds(base, 8), :]
            ys = []
            for j in range(8):
                sv = sout_ref[0, base + j]
                sa = _col_sum(sv * a_t)
                sv = sv * dec + sa * b_t + vg[j:j + 1] * k
                sout_ref[0, base + j] = sv
                ys.append(_col_sum(sv * r))
            y_scr[pl.ds(base, 8), :] = jnp.concatenate(ys, axis=0)
            return carry

        lax.fori_loop(0, HEAD // 8, body, 0)
        y = y_scr[...]
        mean = _col_sum(y) * (1.0 / HEAD)
        dy = y - mean
        var = _col_sum(dy * dy) * (1.0 / HEAD)
        yn = dy * lax.rsqrt(var + GN_EPS) * ln_w + ln_b
        bonus = _col_sum(r * k * r_k) * v
        y_ref[0, :, t * nb:(t + 1) * nb] = (yn + bonus) * g


def _head_blocks(arr3, first, n=1):
    if n == 1:
        return pl.BlockSpec((1,) + arr3.shape[1:], lambda h: (first + h, 0, 0))
    return pl.BlockSpec((n,) + arr3.shape[1:], lambda h: (first // n, 0, 0))


def _rwkv_t_call(pta, pprev_t, s0, mu, vec, wl_t):
    nh, _, _, nb = s0.shape
    rows = pta.shape[1]
    p3 = pta.reshape(-1, HEAD, rows)
    pp3 = pprev_t.reshape(-1, HEAD, nb)
    mu3 = mu.reshape(-1, HEAD, 1)
    tail0 = 3 * nh
    state = pl.BlockSpec((1, HEAD, HEAD, nb), lambda h: (h, 0, 0, 0))
    return pl.pallas_call(
        _rwkv_t_kernel,
        grid=(nh,),
        in_specs=[
            _head_blocks(p3, 0), _head_blocks(p3, nh), _head_blocks(p3, 2 * nh), _head_blocks(p3, tail0, 2),
            _head_blocks(pp3, 0), _head_blocks(pp3, nh), _head_blocks(pp3, 2 * nh), _head_blocks(pp3, tail0, 2),
            _head_blocks(mu3, 0), _head_blocks(mu3, nh), _head_blocks(mu3, 2 * nh), _head_blocks(mu3, tail0, 2),
            pl.BlockSpec((vec.shape[0], 1, HEAD, 1), lambda h: (0, h, 0, 0)),
            pl.BlockSpec((1,) + wl_t.shape[1:], lambda h: (h, 0, 0)),
            state,
        ],
        out_specs=[pl.BlockSpec((1, HEAD, rows), lambda h: (h, 0, 0)), state],
        out_shape=[jax.ShapeDtypeStruct((nh, HEAD, rows), F32), jax.ShapeDtypeStruct(s0.shape, F32)],
        scratch_shapes=[pltpu.VMEM((HEAD, nb), F32), pltpu.VMEM((HEAD, nb), F32)],
        compiler_params=pltpu.CompilerParams(
            dimension_semantics=("parallel",), vmem_limit_bytes=VMEM_LIMIT),
        name="rwkv_t",
    )(p3, p3, p3, p3, pp3, pp3, pp3, pp3, mu3, mu3, mu3, mu3, vec, wl_t, s0)


def _gdn_t_kernel(q_ref, k_ref, v_ref, z_ref, tail_ref, cq_ref, ck_ref, cv_ref, wq_ref, wk_ref, wv_ref,
                  sc_ref, nw_ref, s0_ref, y_ref, sout_ref, k_scr, q_scr):
    h = pl.program_id(0)
    nh = pl.num_programs(0)
    nb = s0_ref.shape[-1]
    nt = q_ref.shape[-1] // nb
    a_log = sc_ref[0, 0:1, 0:1]
    dt_bias = sc_ref[0, 1:2, 0:1]
    sout_ref[...] = s0_ref[...]

    def conv(x_ref, c_ref, w_ref, t):
        acc = None
        for j in range(CONV_K):
            u = t + j - (CONV_K - 1)
            tap = c_ref[u + CONV_K - 1, 0] if u < 0 else x_ref[0, :, u * nb:(u + 1) * nb]
            term = w_ref[j, 0] * tap
            acc = term if acc is None else acc + term
        return acc * _sigmoid(acc)

    for t in range(nt):
        sl = slice(t * nb, (t + 1) * nb)
        q = conv(q_ref, cq_ref, wq_ref, t)
        k = conv(k_ref, ck_ref, wk_ref, t)
        v = conv(v_ref, cv_ref, wv_ref, t)
        q = q * (lax.rsqrt(_col_sum(q * q) + L2_EPS) * (HEAD ** -0.5))
        k = k * lax.rsqrt(_col_sum(k * k) + L2_EPS)
        tl = tail_ref[0:2 * 8, sl]
        pick = lambda row: _col_sum(jnp.where(_iota((2 * 8, 1), 0) == row, tl, 0.0))
        beta = _sigmoid(pick(h))
        eg = jnp.exp(-jnp.exp(a_log) * _softplus(pick(nh + h) + dt_bias))
        k_scr[...] = k
        q_scr[...] = q

        def body1(g, acc):
            base = pl.multiple_of(g * 8, 8)
            kg = k_scr[pl.ds(base, 8), :]
            for j in range(8):
                acc = acc + sout_ref[0, base + j] * kg[j:j + 1]
            return acc

        sk = lax.fori_loop(0, HEAD // 8, body1, jnp.zeros((HEAD, nb), F32))
        v_new = beta * (v - eg * sk)

        def body2(g, acc):
            base = pl.multiple_of(g * 8, 8)
            kg = k_scr[pl.ds(base, 8), :]
            qg = q_scr[pl.ds(base, 8), :]
            for j in range(8):
                s = eg * sout_ref[0, base + j] + kg[j:j + 1] * v_new
                sout_ref[0, base + j] = s
                acc = acc + s * qg[j:j + 1]
            return acc

        o = lax.fori_loop(0, HEAD // 8, body2, jnp.zeros((HEAD, nb), F32))
        o = o * lax.rsqrt(_col_sum(o * o) * (1.0 / HEAD) + RMS_EPS) * nw_ref[...]
        z = z_ref[0, :, sl]
        y_ref[0, :, sl] = o * (z * _sigmoid(z))


def _gdn_t_call(ptb, conv_t, s0, cw, scal, nw):
    nh, _, _, nb = s0.shape
    rows = ptb.shape[1]
    p3 = ptb.reshape(-1, HEAD, rows)
    c4 = conv_t.reshape(CONV_K - 1, -1, HEAD, nb)
    w4 = cw.reshape(CONV_K, -1, HEAD, 1)
    tap = lambda first: pl.BlockSpec((CONV_K - 1, 1, HEAD, nb), lambda h: (0, first + h, 0, 0))
    wsp = lambda first: pl.BlockSpec((CONV_K, 1, HEAD, 1), lambda h: (0, first + h, 0, 0))
    state = pl.BlockSpec((1, HEAD, HEAD, nb), lambda h: (h, 0, 0, 0))
    tail_blk = 4 * nh * HEAD // PAIR
    return pl.pallas_call(
        _gdn_t_kernel,
        grid=(nh,),
        in_specs=[
            _head_blocks(p3, 0), _head_blocks(p3, nh), _head_blocks(p3, 2 * nh), _head_blocks(p3, 3 * nh),
            pl.BlockSpec((PAIR, rows), lambda h: (tail_blk, 0)),
            tap(0), tap(nh), tap(2 * nh), wsp(0), wsp(nh), wsp(2 * nh),
            pl.BlockSpec((1,) + scal.shape[1:], lambda h: (h, 0, 0)),
            pl.BlockSpec(nw.shape, lambda h: (0, 0)),
            state,
        ],
        out_specs=[pl.BlockSpec((1, HEAD, rows), lambda h: (h, 0, 0)), state],
        out_shape=[jax.ShapeDtypeStruct((nh, HEAD, rows), F32), jax.ShapeDtypeStruct(s0.shape, F32)],
        scratch_shapes=[pltpu.VMEM((HEAD, nb), F32), pltpu.VMEM((HEAD, nb), F32)],
        compiler_params=pltpu.CompilerParams(
            dimension_semantics=("parallel",), vmem_limit_bytes=VMEM_LIMIT),
        name="gdn_t",
    )(p3, p3, p3, p3, ptb, c4, c4, c4, w4, w4, w4, scal, nw, s0)


def _hgrn_t_kernel(q_ref, f_ref, v_ref, z_ref, lbl_ref, nw_ref, s0_ref, y_ref, sout_ref,
                   f_scr, k_scr, q_scr, *, layer):
    nb = s0_ref.shape[-1]
    nt = q_ref.shape[-1] // nb
    logits = lbl_ref[:, 0]
    ex = jnp.exp(logits - jnp.max(logits, axis=0, keepdims=True))
    gam = ex / jnp.sum(ex, axis=0, keepdims=True)
    lb = jnp.sum(gam[0:layer + 1], axis=0) - gam[0]
    sout_ref[...] = s0_ref[...]

    for t in range(nt):
        sl = slice(t * nb, (t + 1) * nb)
        qp = q_ref[0, :, sl]
        f = f_ref[0, :, sl]
        v = v_ref[0, :, sl]
        z = z_ref[0, :, sl]
        q_scr[...] = qp * _sigmoid(qp)
        f_scr[...] = lb + (1.0 - lb) * _sigmoid(f)
        k_scr[...] = (1.0 - lb) * _sigmoid(-f)

        def body(g, acc):
            base = pl.multiple_of(g * 8, 8)
            fg, kg, qg = (ref[pl.ds(base, 8), :] for ref in (f_scr, k_scr, q_scr))
            for j in range(8):
                s = fg[j:j + 1] * sout_ref[0, base + j] + kg[j:j + 1] * v
                sout_ref[0, base + j] = s
                acc = acc + qg[j:j + 1] * s
            return acc

        o = lax.fori_loop(0, HEAD // 8, body, jnp.zeros((HEAD, nb), F32))
        o = o * lax.rsqrt(_col_sum(o * o) * (1.0 / HEAD) + RMS_EPS) * nw_ref[...]
        y_ref[0, :, sl] = o * _sigmoid(z)


def _hgrn_t_call(ptc, s0, lbl4, nw, layer):
    nh, _, _, nb = s0.shape
    rows = ptc.shape[1]
    p3 = ptc.reshape(-1, HEAD, rows)
    state = pl.BlockSpec((1, HEAD, HEAD, nb), lambda h: (h, 0, 0, 0))
    return pl.pallas_call(
        functools.partial(_hgrn_t_kernel, layer=layer),
        grid=(nh,),
        in_specs=[
            _head_blocks(p3, 0), _head_blocks(p3, nh), _head_blocks(p3, 2 * nh), _head_blocks(p3, 3 * nh),
            pl.BlockSpec((lbl4.shape[0], 1, HEAD, 1), lambda h: (0, h, 0, 0)),
            pl.BlockSpec(nw.shape, lambda h: (0, 0)),
            state,
        ],
        out_specs=[pl.BlockSpec((1, HEAD, rows), lambda h: (h, 0, 0)), state],
        out_shape=[jax.ShapeDtypeStruct((nh, HEAD, rows), F32), jax.ShapeDtypeStruct(s0.shape, F32)],
        scratch_shapes=[pltpu.VMEM((HEAD, nb), F32)] * 3,
        compiler_params=pltpu.CompilerParams(
            dimension_semantics=("parallel",), vmem_limit_bytes=VMEM_LIMIT),
        name="hgrn_t",
    )(p3, p3, p3, p3, lbl4, nw, s0)


def _ffn_kernel(x_ref, ya_ref, yb_ref, yc_ref, woa_ref, wob_ref, woc_ref, gm_ref, nw_ref,
                sc_ref, sh_ref, gf_ref, win_ref, wd_ref, fnw_ref, xo_ref, *yo_ref, th, y_transposed):
    ff = wd_ref.shape[0]
    dims = _TN if y_transposed else _NN
    proj = lambda y_ref, w_ref: lax.dot_general(y_ref[...].astype(BF16), w_ref[...], dims,
                                                preferred_element_type=F32)
    mix = proj(ya_ref, woa_ref) + proj(yb_ref, wob_ref) + proj(yc_ref, woc_ref)
    x1 = x_ref[...] + gm_ref[0] * mix
    h = _norm_mod(x1, nw_ref[...], sc_ref[0], sh_ref[0]).astype(BF16)
    acc = None
    for j in range(ff // th):
        gate = jnp.dot(h, win_ref[:, j * th:(j + 1) * th], preferred_element_type=F32)
        up = jnp.dot(h, win_ref[:, ff + j * th:ff + (j + 1) * th], preferred_element_type=F32)
        act = (gate * _sigmoid(gate) * up).astype(BF16)
        t = jnp.dot(act, wd_ref[j * th:(j + 1) * th, :], preferred_element_type=F32)
        acc = t if acc is None else acc + t
    xo = x1 + gf_ref[0] * acc
    xo_ref[...] = xo
    if yo_ref:
        yo_ref[0][...] = xo * lax.rsqrt(jnp.mean(xo * xo, axis=-1, keepdims=True) + RMS_EPS) * fnw_ref[...]


def _ffn_call(x, ya, yb, yc, woa, wob, woc, gate_m, nw, scale_f, shift_f, gate_f,
              w_in, w_down, fnw, *, tm, th, tiles_per_seq, final, y_transposed=False):
    rows, d = x.shape
    row = lambda a: pl.BlockSpec((tm, a.shape[1]), lambda i: (i, 0))
    yspec = (lambda a: pl.BlockSpec((a.shape[0], tm), lambda i: (0, i))) if y_transposed else row
    const = lambda a: pl.BlockSpec(a.shape, lambda i: (0, 0), pipeline_mode=pl.Buffered(1))
    mod = lambda a: _mod_spec(a, tm, tiles_per_seq)
    n_out = 2 if final else 1
    return pl.pallas_call(
        functools.partial(_ffn_kernel, th=th, y_transposed=y_transposed),
        grid=(rows // tm,),
        in_specs=[
            row(x), yspec(ya), yspec(yb), yspec(yc), const(woa), const(wob), const(woc),
            mod(gate_m), const(nw), mod(scale_f), mod(shift_f), mod(gate_f),
            const(w_in), const(w_down), const(fnw),
        ],
        out_specs=[row(x)] * n_out,
        out_shape=[jax.ShapeDtypeStruct((rows, d), F32)] * n_out,
        compiler_params=pltpu.CompilerParams(
            dimension_semantics=("parallel",), vmem_limit_bytes=VMEM_LIMIT),
        name="ffn",
    )(x, ya, yb, yc, woa, wob, woc, gate_m, nw, scale_f, shift_f, gate_f, w_in, w_down, fnw)


def _block_ones(n, blk):
    i = np.arange(n) // blk
    return jnp.asarray(i[:, None] == i[None, :], BF16)


def _head_select(first_lane, nh):
    m = np.zeros((PAIR, nh * HEAD), np.float32)
    for h in range(nh):
        m[first_lane + h, h * HEAD:(h + 1) * HEAD] = 1.0
    return jnp.asarray(m, BF16)


def _lane_place(x, first_lane):
    return jnp.zeros((PAIR,), F32).at[first_lane:first_lane + x.shape[0]].set(x)


def _run_chunked(x, mods, states, wts, *, tm, tm_ffn, th, bt):
    b, t, d = x.shape
    rows = b * t
    assert t % CHUNK == 0 and t % tm == 0 and t % tm_ffn == 0 and b % bt == 0 and t >= CONV_K - 1
    shift0, rwkv0, conv0, gdn0, hgrn0 = states
    nl = len(wts)
    new = ([], [], [], [], [])
    xr = x.reshape(rows, d)
    for l in range(nl):
        w = wts[l]
        m = mods[l]
        shift_m, scale_m, gate_m, shift_f, scale_f, gate_f = (m[:, i].reshape(b, 1, d) for i in range(6))
        p_a, p_b, p_c = _inproj_call(xr, w["norm_mix"], scale_m, shift_m, w["wta"], w["wtb"], w["wtc"],
                                     tm, t // tm)
        h_last, p_prev = _lastrow_call(xr.reshape(b, t, d)[:, t - 1], w["norm_mix"], m[:, 1], m[:, 0],
                                       shift0[l], w["wta"])
        p_a = p_a.reshape(b, t, -1)
        p_b = p_b.reshape(b, t, -1)
        p_c = p_c.reshape(b, t, -1)
        y_a, s_a = _rwkv_call(p_a, p_prev, rwkv0[l], w["mu"], w["rwkv_vec"], w["lora"], w["gpair"], bt=bt)
        y_b, s_b = _gdn_call(p_b, conv0[l], gdn0[l], w["conv_w"], w["gdn_tail"], w["gdn_norm"],
                             w["gpair"], w["bsel"], w["gsel"], bt=bt)
        y_c, s_c = _hgrn_call(p_c, hgrn0[l], w["lb_logits"], w["hgrn_norm"], w["gpair"], bt=bt, layer=l)
        conv_new = p_b[:, t - (CONV_K - 1):, :conv0.shape[-1]]
        res = _ffn_call(xr, y_a.reshape(rows, -1), y_b.reshape(rows, -1), y_c.reshape(rows, -1),
                        w["woa"], w["wob"], w["woc"], gate_m, w["norm_ffn"], scale_f, shift_f, gate_f,
                        w["w_ffn_in"], w["w_ffn_out"], w["final_norm"],
                        tm=tm_ffn, th=th, tiles_per_seq=t // tm_ffn, final=l == nl - 1)
        xr = res[0]
        for acc, s in zip(new, (h_last, s_a, conv_new, s_b, s_c)):
            acc.append(s)
    return res[1].reshape(b, t, d), [jnp.stack(acc) for acc in new]


def _run_steps(x, mods, states, wts, *, tm_ffn, th):
    b, t, d = x.shape
    rows = t * b
    tm_ffn = min(tm_ffn, rows)
    shift0, rwkv0, conv0, gdn0, hgrn0 = states
    to_lanes = lambda s: jnp.transpose(s, (0, 2, 3, 4, 1))
    rwkv_t, gdn_t, hgrn_t = to_lanes(rwkv0), to_lanes(gdn0), to_lanes(hgrn0)
    conv_t = jnp.transpose(conv0, (0, 2, 3, 1))
    nl = len(wts)
    new = ([], [], [], [], [])
    xr = jnp.transpose(x, (1, 0, 2)).reshape(rows, d)
    for l in range(nl):
        w = wts[l]
        m = mods[l]
        mod = lambda i, m=m: jnp.tile(m[:, i], (t, 1))
        tiled = lambda a: a.reshape(rows // tm_ffn, tm_ffn, d)
        pta, ptb, ptc, qkv_rows, h_last, pprev_t = _inproj_t_call(
            xr, w["norm_mix"], mod(1), mod(0), shift0[l], w["wta"], w["wtb"], w["wtc"], w["qkvw"])
        y_a, s_a = _rwkv_t_call(pta, pprev_t, rwkv_t[l], w["mu"], w["rwkv_vec_t"], w["lora_t"])
        y_b, s_b = _gdn_t_call(ptb, conv_t[l], gdn_t[l], w["conv_w_t"], w["gdn_scal"], w["gdn_norm_t"])
        y_c, s_c = _hgrn_t_call(ptc, hgrn_t[l], w["lb_logits_t"], w["hgrn_norm_t"], l)
        xp = jnp.concatenate([jnp.transpose(conv0[l], (1, 0, 2)), qkv_rows.reshape(t, b, -1)], axis=0)
        conv_new = jnp.transpose(xp[-(CONV_K - 1):], (1, 0, 2))
        res = _ffn_call(xr, y_a.reshape(-1, rows), y_b.reshape(-1, rows), y_c.reshape(-1, rows),
                        w["woa"], w["wob"], w["woc"], tiled(mod(2)), w["norm_ffn"],
                        tiled(mod(4)), tiled(mod(3)), tiled(mod(5)),
                        w["w_ffn_in"], w["w_ffn_out"], w["final_norm"],
                        tm=tm_ffn, th=th, tiles_per_seq=1, final=l == nl - 1, y_transposed=True)
        xr = res[0]
        for acc, s in zip(new, (h_last, s_a, conv_new, s_b, s_c)):
            acc.append(s)
    y = jnp.transpose(res[1].reshape(t, b, d), (1, 0, 2))
    from_lanes = lambda s: jnp.transpose(jnp.stack(s), (0, 4, 1, 2, 3))
    return y, [jnp.stack(new[0]), from_lanes(new[1]), jnp.stack(new[2]), from_lanes(new[3]), from_lanes(new[4])]


def kernel(x_prompt, x_sample, c_prompt, c_sample, state_rwkv_shift, state_rwkv, state_gdn_conv, state_gdn, state_hgrn, w_ada, b_ada, norm_mix_w, w_in, rwkv_mu, rwkv_w0, rwkv_w2, rwkv_a0, rwkv_a2, rwkv_g2, rwkv_k_k, rwkv_k_a, rwkv_r_k, rwkv_ln_w, rwkv_ln_b, gdn_conv_w, gdn_A_log, gdn_dt_bias, gdn_norm_w, hgrn_lb_logits, hgrn_norm_w, w_out, norm_ffn_w, w_ffn_in, w_ffn_out, final_norm_w):
    nl, d, _ = w_in.shape
    a_heads = state_rwkv.shape[2]
    b_heads = state_gdn.shape[2]
    c_heads = state_hgrn.shape[2]
    aw, bw, cw = a_heads * HEAD, b_heads * HEAD, c_heads * HEAD
    lw, la, lg = rwkv_w2.shape[1], rwkv_a2.shape[1], rwkv_g2.shape[1]
    a_cols = 3 * aw + lw + la + lg
    qkvw = 3 * bw
    b_cols = qkvw + 2 * b_heads + bw
    assert (aw, bw, cw) == (3 * PAIR, 3 * PAIR, 2 * PAIR) and lw + la + lg == PAIR and lw == 32 and la == 32

    bp, tpr, _ = x_prompt.shape
    bs, ts, _ = x_sample.shape

    gpair = _block_ones(PAIR, HEAD)
    bsel = _head_select(0, b_heads)
    gsel = _head_select(b_heads, b_heads)
    wts = []
    for l in range(nl):
        wit = jnp.swapaxes(w_in[l], 0, 1).astype(BF16)
        wbt = wit[a_cols:a_cols + b_cols]
        wtb = jnp.concatenate(
            [wbt[:qkvw], wbt[qkvw + 2 * b_heads:], wbt[qkvw:qkvw + 2 * b_heads],
             jnp.zeros((PAIR - 2 * b_heads, d), BF16)], axis=0)
        lora = jnp.zeros((PAIR, 3 * aw), F32)
        lora = lora.at[0:lw, 0:aw].set(rwkv_w2[l])
        lora = lora.at[lw:lw + la, aw:2 * aw].set(rwkv_a2[l])
        lora = lora.at[lw + la:, 2 * aw:].set(rwkv_g2[l])
        rwkv_vec = jnp.stack([rwkv_w0[l], rwkv_a0[l], rwkv_k_k[l], rwkv_k_a[l], rwkv_r_k[l].reshape(-1),
                              rwkv_ln_w[l], rwkv_ln_b[l], jnp.zeros((aw,), F32)])
        wo = w_out[l].astype(BF16)
        hcols = lambda a: a.reshape(a.shape[0], a_heads, HEAD).transpose(1, 2, 0)
        lora_t = jnp.concatenate([
            jnp.pad(hcols(rwkv_w2[l]), ((0, 0), (0, 0), (0, la + lg))),
            jnp.pad(hcols(rwkv_a2[l]), ((0, 0), (0, 0), (lw, lg))),
            jnp.pad(hcols(rwkv_g2[l]), ((0, 0), (0, 0), (lw + la, 0)))], axis=1)
        steps = dict(
            rwkv_vec_t=rwkv_vec[:7].reshape(7, a_heads, HEAD, 1), lora_t=lora_t.astype(BF16),
            conv_w_t=gdn_conv_w[l],
            gdn_scal=jnp.zeros((b_heads, 8, PAIR), F32)
                .at[:, 0].set(jnp.broadcast_to(gdn_A_log[l][:, None], (b_heads, PAIR)))
                .at[:, 1].set(jnp.broadcast_to(gdn_dt_bias[l][:, None], (b_heads, PAIR))),
            gdn_norm_t=gdn_norm_w[l].reshape(HEAD, 1),
            lb_logits_t=hgrn_lb_logits.reshape(nl, c_heads, HEAD, 1),
            hgrn_norm_t=hgrn_norm_w[l].reshape(HEAD, 1),
        )
        wts.append(dict(
            **steps,
            norm_mix=norm_mix_w[l].reshape(1, d),
            wta=wit[:a_cols], wtb=wtb, wtc=wit[a_cols + b_cols:], qkvw=qkvw,
            mu=rwkv_mu[l].reshape(1, a_cols), rwkv_vec=rwkv_vec, lora=lora.astype(BF16),
            gpair=gpair, bsel=bsel, gsel=gsel,
            conv_w=jnp.concatenate([gdn_conv_w[l], jnp.zeros((8 - CONV_K, qkvw), F32)], axis=0),
            gdn_tail=jnp.zeros((8, PAIR), F32).at[0].set(_lane_place(gdn_A_log[l], b_heads))
                                              .at[1].set(_lane_place(gdn_dt_bias[l], b_heads)),
            gdn_norm=jnp.tile(gdn_norm_w[l], b_heads).reshape(1, bw),
            lb_logits=hgrn_lb_logits,
            hgrn_norm=jnp.tile(hgrn_norm_w[l], c_heads).reshape(1, cw),
            woa=wo[:aw], wob=wo[aw:aw + bw], woc=wo[aw + bw:],
            norm_ffn=norm_ffn_w[l].reshape(1, d),
            w_ffn_in=w_ffn_in[l].astype(BF16), w_ffn_out=w_ffn_out[l].astype(BF16),
            final_norm=final_norm_w.reshape(1, d),
        ))

    mod = _ada_call(jnp.concatenate([c_prompt, c_sample], axis=0), w_ada, b_ada)
    mod = mod.reshape(nl, bp + bs, 6, d)
    mods_p = [mod[l, :bp] for l in range(nl)]
    mods_s = [mod[l, bp:] for l in range(nl)]

    zeros = lambda s: jnp.zeros((nl, bp) + s.shape[2:], s.dtype)
    states_p = tuple(zeros(s) for s in (state_rwkv_shift, state_rwkv, state_gdn_conv, state_gdn, state_hgrn))
    states_s = (state_rwkv_shift, state_rwkv, state_gdn_conv, state_gdn, state_hgrn)

    th = w_ffn_out.shape[1] // FFN_HIDDEN_TILES
    y_p, new_p = _run_chunked(x_prompt, mods_p, states_p, wts, tm=TM_INPROJ, tm_ffn=TM_FFN, th=th, bt=SEQS_PER_STEP)
    y_s, new_s = _run_steps(x_sample, mods_s, states_s, wts, tm_ffn=TM_FFN_STEPS, th=th)
    return (y_p, y_s, *new_p, *new_s)
```

```python
import functools

import numpy as np
import jax
import jax.numpy as jnp
from jax import lax
from jax.experimental import pallas as pl
from jax.experimental.pallas import tpu as pltpu

F32 = jnp.float32
BF16 = jnp.bfloat16

HEAD = 64
PAIR = 2 * HEAD
CHUNK = 64
HBLK = 16
INV_BASE = 16
CONV_K = 4
RMS_EPS = 1e-6
L2_EPS = 1e-6
GN_EPS = 64e-5
VMEM_LIMIT = 56 * 1024 * 1024

TM_INPROJ = 256
TM_FFN = 512
TM_FFN_STEPS = 256
FFN_HIDDEN_TILES = 2
SEQS_PER_STEP = 4

_NN = (((1,), (0,)), ((), ()))
_NT = (((1,), (1,)), ((), ()))
_TN = (((0,), (0,)), ((), ()))


def _dot(a, b, dims=_NN):
    return lax.dot_general(a.astype(BF16), b.astype(BF16), dims, preferred_element_type=F32)


def _dot_sel(x, m, pieces=3):
    ps = _bf16_pieces(x, pieces)
    return lax.dot_general(jnp.concatenate(ps, axis=1), jnp.concatenate([m] * pieces, axis=0), _NN,
                           preferred_element_type=F32)


def _bf16_pieces(x, pieces):
    ps = []
    rem = x
    for i in range(pieces):
        p = rem.astype(BF16)
        ps.append(p)
        if i + 1 < pieces:
            rem = rem - p.astype(F32)
    return ps


def _seg_sum(x, gpair, pieces=2):
    return jnp.concatenate(
        [_dot_sel(x[:, i:i + PAIR], gpair, pieces) for i in range(0, x.shape[1], PAIR)], axis=1)


def _sigmoid(x):
    return jax.nn.sigmoid(x)


def _softplus(x):
    return jnp.maximum(x, 0.0) + jnp.log1p(jnp.exp(-jnp.abs(x)))


def _iota(shape, dim):
    return lax.broadcasted_iota(jnp.int32, shape, dim)


def _stack_heads(x):
    is_a = _iota((1, PAIR), 1) < HEAD
    return jnp.concatenate([jnp.where(is_a, x, 0.0), jnp.where(is_a, 0.0, x)], axis=0)


def _fold_heads(x, c):
    return x[0:c] + x[c:2 * c]


def _pair_blockdiag_mask():
    return (_iota((PAIR, 1), 0) >> 6) == (_iota((1, PAIR), 1) >> 6)


def _load_pair_state(s_ref, bi, pr):
    sa = s_ref[bi, 2 * pr]
    sb = s_ref[bi, 2 * pr + 1]
    z = jnp.zeros((HEAD, HEAD), F32)
    return jnp.concatenate(
        [jnp.concatenate([sa, z], axis=1), jnp.concatenate([z, sb], axis=1)], axis=0)


def _store_pair_state(s_ref, bi, pr, s):
    s_ref[bi, 2 * pr] = s[0:HEAD, 0:HEAD]
    s_ref[bi, 2 * pr + 1] = s[HEAD:PAIR, HEAD:PAIR]


def _dot_left01(m, x, pieces=3):
    return lax.dot_general(jnp.concatenate([m] * pieces, axis=1),
                           jnp.concatenate(_bf16_pieces(x, pieces), axis=0), _NN,
                           preferred_element_type=F32)


def _tri_inverse_many(ns, c):
    size = ns[0].shape[0]
    r = _iota((size, 1), 0)
    cc = _iota((1, size), 1)
    eye = (r == cc).astype(F32)
    sh = INV_BASE.bit_length() - 1
    diag = [jnp.where((r >> sh) == (cc >> sh), n, 0.0) for n in ns]
    invs = [eye + d for d in diag]
    pws = diag
    span = 2
    while span < INV_BASE:
        pws = [_dot(pw, pw) for pw in pws]
        invs = [inv + _dot(inv, pw) for inv, pw in zip(invs, pws)]
        span *= 2
    blk = INV_BASE
    while blk < c:
        sh = blk.bit_length() - 1
        off = ((r >> (sh + 1)) == (cc >> (sh + 1))) & ((r >> sh) != (cc >> sh))
        low = [_dot(inv, jnp.where(off, n, 0.0)) for inv, n in zip(invs, ns)]
        invs = [inv + _dot(t, inv) for inv, t in zip(invs, low)]
        blk *= 2
    return invs


def _ada_kernel(c_ref, w_ref, b_ref, o_ref):
    c = c_ref[...]
    o_ref[0] = _dot(c * _sigmoid(c), w_ref[0]) + b_ref[0]


def _ada_call(c_all, w_ada, b_ada):
    nl, d, n6 = w_ada.shape
    rows = c_all.shape[0]
    tn = n6 // 4
    return pl.pallas_call(
        _ada_kernel,
        grid=(nl, n6 // tn),
        in_specs=[
            pl.BlockSpec((rows, d), lambda l, j: (0, 0)),
            pl.BlockSpec((1, d, tn), lambda l, j: (l, 0, j)),
            pl.BlockSpec((1, 1, tn), lambda l, j: (l, 0, j)),
        ],
        out_specs=pl.BlockSpec((1, rows, tn), lambda l, j: (l, 0, j)),
        out_shape=jax.ShapeDtypeStruct((nl, rows, n6), F32),
        compiler_params=pltpu.CompilerParams(
            dimension_semantics=("parallel", "parallel"), vmem_limit_bytes=VMEM_LIMIT),
        name="ada",
    )(c_all, w_ada, b_ada.reshape(nl, 1, n6))


def _norm_mod(x, nw, scale, shift):
    y = x * lax.rsqrt(jnp.mean(x * x, axis=-1, keepdims=True) + RMS_EPS) * nw
    return y * (1.0 + scale) + shift


def _inproj_kernel(x_ref, nw_ref, sc_ref, sh_ref, wa_ref, wb_ref, wc_ref, pa_ref, pb_ref, pc_ref):
    h = _norm_mod(x_ref[...], nw_ref[...], sc_ref[0], sh_ref[0]).astype(BF16)
    pa_ref[...] = lax.dot_general(h, wa_ref[...], _NT, preferred_element_type=F32)
    pb_ref[...] = lax.dot_general(h, wb_ref[...], _NT, preferred_element_type=F32)
    pc_ref[...] = lax.dot_general(h, wc_ref[...], _NT, preferred_element_type=F32)


def _mod_spec(mod, tm, tiles_per_seq):
    if mod.shape[1] == 1:
        return pl.BlockSpec((1, 1, mod.shape[2]), lambda i, *_: (i // tiles_per_seq, 0, 0))
    return pl.BlockSpec((1, tm, mod.shape[2]), lambda i, *_: (i, 0, 0))


def _inproj_call(x, nw, scale, shift, wa, wb, wc, tm, tiles_per_seq):
    rows, d = x.shape
    full = lambda a: pl.BlockSpec(a.shape, lambda i: (0, 0), pipeline_mode=pl.Buffered(1))
    outs = [jax.ShapeDtypeStruct((rows, w.shape[0]), F32) for w in (wa, wb, wc)]
    return pl.pallas_call(
        _inproj_kernel,
        grid=(rows // tm,),
        in_specs=[
            pl.BlockSpec((tm, d), lambda i: (i, 0)),
            full(nw),
            _mod_spec(scale, tm, tiles_per_seq),
            _mod_spec(shift, tm, tiles_per_seq),
            full(wa), full(wb), full(wc),
        ],
        out_specs=[pl.BlockSpec((tm, w.shape[0]), lambda i: (i, 0)) for w in (wa, wb, wc)],
        out_shape=outs,
        compiler_params=pltpu.CompilerParams(
            dimension_semantics=("parallel",), vmem_limit_bytes=VMEM_LIMIT),
        name="inproj",
    )(x, nw, scale, shift, wa, wb, wc)


def _lastrow_kernel(x_ref, nw_ref, sc_ref, sh_ref, hprev_ref, wa_ref, h_ref, pprev_ref):
    h_ref[...] = _norm_mod(x_ref[...], nw_ref[...], sc_ref[...], sh_ref[...])
    pprev_ref[...] = lax.dot_general(hprev_ref[...].astype(BF16), wa_ref[...], _NT,
                                     preferred_element_type=F32)


def _lastrow_call(x_last, nw, scale, shift, h_prev, wa):
    b, d = x_last.shape
    return pl.pallas_call(
        _lastrow_kernel,
        out_shape=[jax.ShapeDtypeStruct((b, d), F32), jax.ShapeDtypeStruct((b, wa.shape[0]), F32)],
        compiler_params=pltpu.CompilerParams(vmem_limit_bytes=VMEM_LIMIT),
        name="lastrow",
    )(x_last, nw, scale, shift, h_prev, wa)


def _rwkv_kernel(p_ref, pprev_ref, s0_ref, mu_ref, vec_ref, wl_ref, g_ref,
                 y_ref, sout_ref, s_scr, prev_scr, *, bt):
    c = CHUNK
    ci = pl.program_id(1)
    nci = pl.num_programs(1)
    aw = 3 * PAIR
    rows = _iota((c, 1), 0)
    lane = _iota((1, PAIR), 1)
    vec = vec_ref[...]
    w0, a0, k_k, k_a, r_k, ln_w, ln_b = (vec[i:i + 1] for i in range(7))
    gpair = g_ref[...]
    tri = (rows >= _iota((1, c), 1)).astype(BF16)
    r2 = _iota((2 * c, 1), 0) & (c - 1)
    c2 = _iota((1, 2 * c), 1) & (c - 1)
    strict = r2 > c2
    incl = r2 >= c2
    bd = _pair_blockdiag_mask()

    @pl.when(ci == 0)
    def _():
        for bi in range(bt):
            for pr in range(3):
                s_scr[bi, pr] = _load_pair_state(s0_ref, bi, pr)
            prev_scr[bi] = pprev_ref[bi]

    pre = []
    for bi in range(bt):
        p = p_ref[bi]
        prev = pltpu.roll(p, 1, axis=0)
        prev = jnp.where(rows == 0, prev_scr[bi], prev)
        prev_scr[bi] = p[c - 1:c]
        xs = p + mu_ref[...] * (prev - p)
        r = xs[:, 0:aw]
        k = xs[:, aw:2 * aw]
        v = xs[:, 2 * aw:3 * aw]
        tail = xs[:, 3 * aw:3 * aw + PAIR]
        act = jnp.where(lane < 32, jnp.tanh(tail), jnp.where(lane < 64, tail, _sigmoid(tail)))
        lo = _dot(act, wl_ref[...])
        w = -_softplus(-(w0 + lo[:, 0:aw])) - 0.5
        ld = -jnp.exp(w)
        a = _sigmoid(a0 + lo[:, aw:2 * aw])
        g = lo[:, 2 * aw:3 * aw]
        kk = k * k_k
        kk = kk * lax.rsqrt(_seg_sum(kk * kk, gpair) + L2_EPS)
        k = k * (1.0 + (a - 1.0) * k_a)
        cum = _dot_left01(tri, ld)
        cum_last = cum[c - 1:c]
        e_neg = jnp.exp(-cum)
        a_t = -kk * jnp.exp(cum - ld)
        b_t = kk * a * e_neg
        k_t = k * e_neg
        r_t = r * jnp.exp(cum)
        e_end = jnp.exp(cum_last - cum)
        pre.append(dict(r=r, k=k, v=v, g=g, a_t=a_t, b_t=b_t, k_t=k_t, r_t=r_t,
                        b_end=kk * a * e_end, k_end=k * e_end, d_end=jnp.exp(cum_last)))

    chains = [(bi, pr) for bi in range(bt) for pr in range(3)]
    pair = lambda name: [pre[bi][name][:, PAIR * pr:PAIR * (pr + 1)] for bi, pr in chains]
    a_t, b_t, k_t, r_t, v = pair("a_t"), pair("b_t"), pair("k_t"), pair("r_t"), pair("v")
    b_end, k_end, d_end = pair("b_end"), pair("k_end"), pair("d_end")
    v_st = [_stack_heads(x) for x in v]
    gs = [_dot(jnp.concatenate([_stack_heads(a), _stack_heads(r)], axis=0),
               jnp.concatenate([_stack_heads(b), _stack_heads(k)], axis=0), _NT)
          for a, r, b, k in zip(a_t, r_t, b_t, k_t)]
    a_ak = [jnp.where(strict, g[0:2 * c, 2 * c:4 * c], 0.0) for g in gs]
    p_rbk = [jnp.concatenate([jnp.where(incl, g[2 * c:4 * c, 0:2 * c], 0.0),
                              jnp.where(incl, g[2 * c:4 * c, 2 * c:4 * c], 0.0)], axis=1) for g in gs]
    tinv = _tri_inverse_many([jnp.where(strict, g[0:2 * c, 0:2 * c], 0.0) for g in gs], c)
    s = [s_scr[bi, pr] for bi, pr in chains]
    ars = [_dot(jnp.concatenate([a, r], axis=0), s_, _NT) for a, r, s_ in zip(a_t, r_t, s)]
    akv = [_dot(m, x) for m, x in zip(a_ak, v_st)]
    u_st = [_dot(t, _stack_heads(x[0:c]) + y) for t, x, y in zip(tinv, ars, akv)]
    y_st = [_dot(p, jnp.concatenate([u, x], axis=0)) for p, u, x in zip(p_rbk, u_st, v_st)]
    ds = [_dot(jnp.concatenate([_fold_heads(u, c), x], axis=0), jnp.concatenate([b, k], axis=0), _TN)
          for u, x, b, k in zip(u_st, v, b_end, k_end)]
    for i, (bi, pr) in enumerate(chains):
        s_scr[bi, pr] = s[i] * d_end[i] + jnp.where(bd, ds[i], 0.0)

    for bi in range(bt):
        r, k, v_, g = (pre[bi][n] for n in ("r", "k", "v", "g"))
        y = jnp.concatenate([ars[3 * bi + pr][c:2 * c] + _fold_heads(y_st[3 * bi + pr], c)
                             for pr in range(3)], axis=1)
        mean = _seg_sum(y, gpair) * (1.0 / HEAD)
        dy = y - mean
        var = _seg_sum(dy * dy, gpair) * (1.0 / HEAD)
        yn = dy * lax.rsqrt(var + GN_EPS) * ln_w + ln_b
        bonus = _seg_sum(r * k * r_k, gpair) * v_
        y_ref[bi] = (yn + bonus) * g

    @pl.when(ci == nci - 1)
    def _():
        for bi in range(bt):
            for pr in range(3):
                _store_pair_state(sout_ref, bi, pr, s_scr[bi, pr])


def _rwkv_call(p_a, p_prev, s0, mu, vec, wl, gmat, *, bt):
    b, tp, acols = p_a.shape
    nh = s0.shape[1]
    aw = nh * HEAD
    cin = CHUNK
    nc = tp // cin
    const = lambda a: pl.BlockSpec(a.shape, lambda i, j: (0,) * a.ndim)
    kern = functools.partial(_rwkv_kernel, bt=bt)
    return pl.pallas_call(
        kern,
        grid=(b // bt, nc),
        in_specs=[
            pl.BlockSpec((bt, cin, acols), lambda i, j: (i, j, 0)),
            pl.BlockSpec((bt, 1, acols), lambda i, j: (i, 0, 0)),
            pl.BlockSpec((bt, nh, HEAD, HEAD), lambda i, j: (i, 0, 0, 0)),
            const(mu), const(vec), const(wl), const(gmat),
        ],
        out_specs=[
            pl.BlockSpec((bt, cin, aw), lambda i, j: (i, j, 0)),
            pl.BlockSpec((bt, nh, HEAD, HEAD), lambda i, j: (i, 0, 0, 0)),
        ],
        out_shape=[jax.ShapeDtypeStruct((b, tp, aw), F32),
                   jax.ShapeDtypeStruct(s0.shape, F32)],
        scratch_shapes=[pltpu.VMEM((bt, nh // 2, PAIR, PAIR), F32),
                        pltpu.VMEM((bt, 1, acols), F32)],
        compiler_params=pltpu.CompilerParams(
            dimension_semantics=("parallel", "arbitrary"), vmem_limit_bytes=VMEM_LIMIT),
        name="rwkv",
    )(p_a, p_prev.reshape(b, 1, acols), s0, mu, vec, wl, gmat)


def _gdn_kernel(p_ref, conv0_ref, s0_ref, cw_ref, tp_ref, nw_ref, g_ref, bsel_ref, gsel_ref,
                y_ref, sout_ref, s_scr, cbuf, *, bt):
    c = CHUNK
    ci = pl.program_id(1)
    nci = pl.num_programs(1)
    bw = 3 * PAIR
    qkvw = 3 * bw
    rows = _iota((c, 1), 0)
    lane = _iota((1, PAIR), 1)
    nh = 2 * 3
    gpair = g_ref[...]
    tri = (rows >= _iota((1, c), 1)).astype(BF16)
    r2 = _iota((2 * c, 1), 0) & (c - 1)
    c2 = _iota((1, 2 * c), 1) & (c - 1)
    same =(_iota((2 * c, 1), 0) >> 6) == (_iota((1, 2 * c), 1) >> 6)
    strict = same & (r2 > c2)
    incl = same & (r2 >= c2)
    bd = _pair_blockdiag_mask()
    is_a = lane < HEAD
    a_log = tp_ref[0:1]
    dt_bias = tp_ref[1:2]
    is_beta = lane < nh
    is_g = (lane >= nh) & (lane < 2 * nh)

    @pl.when(ci == 0)
    def _():
        for bi in range(bt):
            for pr in range(3):
                s_scr[bi, pr] = _load_pair_state(s0_ref, bi, pr)
            cbuf[bi, 0:8] = jnp.zeros((8, qkvw), F32)
            cbuf[bi, 8 - (CONV_K - 1):8] = conv0_ref[bi]

    pre = []
    for bi in range(bt):
        p = p_ref[bi]
        cbuf[bi, 8:8 + c] = p[:, 0:qkvw]
        conv = None
        for j in range(CONV_K):
            t = cw_ref[j:j + 1] * cbuf[bi, pl.ds(8 - (CONV_K - 1) + j, c), :]
            conv = t if conv is None else conv + t
        cbuf[bi, 0:8] = cbuf[bi, c:c + 8]
        qkv = conv * _sigmoid(conv)
        q = qkv[:, 0:bw]
        k = qkv[:, bw:2 * bw]
        v = qkv[:, 2 * bw:3 * bw]
        z = p[:, qkvw:qkvw + bw]
        tail = p[:, qkvw + bw:qkvw + bw + PAIR]
        q = q * (lax.rsqrt(_seg_sum(q * q, gpair) + L2_EPS) * (HEAD ** -0.5))
        k = k * lax.rsqrt(_seg_sum(k * k, gpair) + L2_EPS)
        beta = jnp.where(is_beta, _sigmoid(tail), 0.0)
        gl = jnp.where(is_g, -jnp.exp(a_log) * _softplus(tail + dt_bias), 0.0)
        gcum = _dot_sel(_dot_left01(tri, gl), gsel_ref[...])
        beta = _dot_sel(beta, bsel_ref[...])
        g_last = gcum[c - 1:c]
        eg = jnp.exp(gcum)
        kb = k * beta
        pre.append(dict(z=z, q=q, k=k, kb=kb, gcum=gcum, kg=kb * eg, qg=q * eg,
                        kd=k * jnp.exp(g_last - gcum), vb=v * beta, d_end=jnp.exp(g_last)))

    chains = [(bi, pr) for bi in range(bt) for pr in range(3)]
    pair = lambda name: [pre[bi][name][:, PAIR * pr:PAIR * (pr + 1)] for bi, pr in chains]
    q, k, kb, gcum, kg, qg = pair("q"), pair("k"), pair("kb"), pair("gcum"), pair("kg"), pair("qg")
    kd, vb, d_end = pair("kd"), pair("vb"), pair("d_end")

    def col(x):
        sw = pltpu.roll(x, HEAD, axis=1)
        return jnp.concatenate([jnp.where(is_a, x, sw), jnp.where(is_a, sw, x)], axis=0)

    gcol = [col(x) for x in gcum]
    diff = [x - x.T for x in gcol]
    dec_s = [jnp.where(strict, jnp.exp(jnp.where(strict, x, 0.0)), 0.0) for x in diff]
    dec_i = [jnp.where(incl, jnp.exp(jnp.where(incl, x, 0.0)), 0.0) for x in diff]
    sc = [_dot(jnp.concatenate([_stack_heads(x), _stack_heads(y)], axis=0), _stack_heads(w), _NT)
          for x, y, w in zip(kb, q, k)]
    tinv = _tri_inverse_many([-(x[0:2 * c] * d) for x, d in zip(sc, dec_s)], c)
    qk = [x[2 * c:4 * c] * d for x, d in zip(sc, dec_i)]
    s = [s_scr[bi, pr] for bi, pr in chains]
    kqs = [_dot(jnp.concatenate([x, y], axis=0), s_) for x, y, s_ in zip(kg, qg, s)]
    v_new = [_dot(t, _stack_heads(x - y[0:c])) for t, x, y in zip(tinv, vb, kqs)]
    o_st = [_dot(x, y) for x, y in zip(qk, v_new)]
    ds = [_dot(x, _fold_heads(y, c), _TN) for x, y in zip(kd, v_new)]
    for i, (bi, pr) in enumerate(chains):
        s_scr[bi, pr] = s[i] * d_end[i] + jnp.where(bd, ds[i], 0.0)

    for bi in range(bt):
        z = pre[bi]["z"]
        o = jnp.concatenate([kqs[3 * bi + pr][c:2 * c] + _fold_heads(o_st[3 * bi + pr], c)
                             for pr in range(3)], axis=1)
        o = o * lax.rsqrt(_seg_sum(o * o, gpair) * (1.0 / HEAD) + RMS_EPS) * nw_ref[...]
        y_ref[bi] = o * (z * _sigmoid(z))

    @pl.when(ci == nci - 1)
    def _():
        for bi in range(bt):
            for pr in range(3):
                _store_pair_state(sout_ref, bi, pr, s_scr[bi, pr])


def _gdn_call(p_b, conv0, s0, cw, tailp, nw, gmat, bsel, gsel, *, bt):
    b, tp, bcols = p_b.shape
    nh = s0.shape[1]
    bw = nh * HEAD
    cin = CHUNK
    nc = tp // cin
    const = lambda a: pl.BlockSpec(a.shape, lambda i, j: (0,) * a.ndim)
    kern = functools.partial(_gdn_kernel, bt=bt)
    return pl.pallas_call(
        kern,
        grid=(b // bt, nc),
        in_specs=[
            pl.BlockSpec((bt, cin, bcols), lambda i, j: (i, j, 0)),
            pl.BlockSpec((bt, CONV_K - 1, 3 * bw), lambda i, j: (i, 0, 0)),
            pl.BlockSpec((bt, nh, HEAD, HEAD), lambda i, j: (i, 0, 0, 0)),
            const(cw), const(tailp), const(nw), const(gmat), const(bsel), const(gsel),
        ],
        out_specs=[
            pl.BlockSpec((bt, cin, bw), lambda i, j: (i, j, 0)),
            pl.BlockSpec((bt, nh, HEAD, HEAD), lambda i, j: (i, 0, 0, 0)),
        ],
        out_shape=[jax.ShapeDtypeStruct((b, tp, bw), F32),
                   jax.ShapeDtypeStruct(s0.shape, F32)],
        scratch_shapes=[pltpu.VMEM((bt, nh // 2, PAIR, PAIR), F32),
                        pltpu.VMEM((bt, CHUNK + 8, 3 * bw), F32)],
        compiler_params=pltpu.CompilerParams(
            dimension_semantics=("parallel", "arbitrary"), vmem_limit_bytes=VMEM_LIMIT),
        name="gdn",
    )(p_b, conv0, s0, cw, tailp, nw, gmat, bsel, gsel)


def _hgrn_kernel(p_ref, s0_ref, lbl_ref, nw_ref, g_ref, y_ref, sout_ref, s_scr,
                 *, bt, layer):
    c = CHUNK
    ci = pl.program_id(1)
    nci = pl.num_programs(1)
    cw = 2 * PAIR
    rows = _iota((c, 1), 0)
    cols = _iota((1, c), 1)
    sameblk = (rows >> 4) == (cols >> 4)
    tri_tot = jnp.concatenate([sameblk & (rows >= cols), sameblk], axis=0).astype(BF16)
    rows_b = _iota((HBLK, 1), 0)
    bd = _pair_blockdiag_mask()
    gpair = g_ref[...]

    logits = lbl_ref[...]
    ex = jnp.exp(logits - jnp.max(logits, axis=0, keepdims=True))
    gam = ex / jnp.sum(ex, axis=0, keepdims=True)
    lb = jnp.sum(gam[0:layer + 1], axis=0, keepdims=True) - gam[0:1]

    @pl.when(ci == 0)
    def _():
        for bi in range(bt):
            for pr in range(2):
                s_scr[bi, pr] = _load_pair_state(s0_ref, bi, pr).T

    pre = []
    for bi in range(bt):
        p = p_ref[bi]
        qp = p[:, 0:cw]
        f = p[:, cw:2 * cw]
        v = p[:, 2 * cw:3 * cw]
        z = p[:, 3 * cw:4 * cw]
        q = qp * _sigmoid(qp)
        logf = jnp.log(lb + (1.0 - lb) * _sigmoid(f))
        k = (1.0 - lb) * _sigmoid(-f)
        cums = _dot_left01(tri_tot, logf)
        bl = cums[0:c]
        btot = cums[c:2 * c]
        pre.append(dict(q=q, k=k, v=v, z=z, bl=bl, q_in=q * jnp.exp(bl),
                        k_out=k * jnp.exp(btot - bl), d_blk=jnp.exp(btot)))

    chains = [(bi, pr) for bi in range(bt) for pr in range(2)]
    nblk = c // HBLK

    def blk_of(name, bi, pr, blk):
        return pre[bi][name][HBLK * blk:HBLK * (blk + 1), PAIR * pr:PAIR * (pr + 1)]

    intra = {}
    ds = {}
    for blk in range(nblk):
        for bi, pr in chains:
            qb, kb, vb, bb = (blk_of(n, bi, pr, blk) for n in ("q", "k", "v", "bl"))
            xs = []
            for j in range(HBLK):
                e = jnp.exp(jnp.minimum(bb - bb[j:j + 1], 0.0))
                xs.append(qb * e * kb[j:j + 1])
            att = _dot_sel(jnp.concatenate(xs, axis=0), gpair, pieces=2)
            acc = None
            for j in range(HBLK):
                t = jnp.where(rows_b >= j, att[HBLK * j:HBLK * (j + 1)], 0.0) * vb[j:j + 1]
                acc = t if acc is None else acc + t
            intra[bi, pr, blk] = acc
            ds[bi, pr, blk] = jnp.where(bd, _dot(vb, blk_of("k_out", bi, pr, blk), _TN), 0.0)

    s = {ch: s_scr[ch] for ch in chains}
    outs = {}
    for blk in range(nblk):
        for bi, pr in chains:
            outs[bi, pr, blk] = intra[bi, pr, blk] + _dot(blk_of("q_in", bi, pr, blk), s[bi, pr], _NT)
            s[bi, pr] = s[bi, pr] * blk_of("d_blk", bi, pr, blk)[0:1] + ds[bi, pr, blk]
    for ch in chains:
        s_scr[ch] = s[ch]

    for bi in range(bt):
        z = pre[bi]["z"]
        o = jnp.concatenate(
            [jnp.concatenate([outs[bi, pr, blk] for blk in range(nblk)], axis=0) for pr in range(2)], axis=1)
        o = o * lax.rsqrt(_seg_sum(o * o, gpair) * (1.0 / HEAD) + RMS_EPS) * nw_ref[...]
        y_ref[bi] = o * _sigmoid(z)

    @pl.when(ci == nci - 1)
    def _():
        for bi in range(bt):
            for pr in range(2):
                _store_pair_state(sout_ref, bi, pr, s_scr[bi, pr].T)


def _hgrn_call(p_c, s0, lbl, nw, gmat, *, bt, layer):
    b, tp, ccols = p_c.shape
    nh = s0.shape[1]
    cw = nh * HEAD
    cin = CHUNK
    nc = tp // cin
    const = lambda a: pl.BlockSpec(a.shape, lambda i, j: (0,) * a.ndim)
    kern = functools.partial(_hgrn_kernel, bt=bt, layer=layer)
    return pl.pallas_call(
        kern,
        grid=(b // bt, nc),
        in_specs=[
            pl.BlockSpec((bt, cin, ccols), lambda i, j: (i, j, 0)),
            pl.BlockSpec((bt, nh, HEAD, HEAD), lambda i, j: (i, 0, 0, 0)),
            const(lbl), const(nw), const(gmat),
        ],
        out_specs=[
            pl.BlockSpec((bt, cin, cw), lambda i, j: (i, j, 0)),
            pl.BlockSpec((bt, nh, HEAD, HEAD), lambda i, j: (i, 0, 0, 0)),
        ],
        out_shape=[jax.ShapeDtypeStruct((b, tp, cw), F32),
                   jax.ShapeDtypeStruct(s0.shape, F32)],
        scratch_shapes=[pltpu.VMEM((bt, nh // 2, PAIR, PAIR), F32)],
        compiler_params=pltpu.CompilerParams(
            dimension_semantics=("parallel", "arbitrary"), vmem_limit_bytes=VMEM_LIMIT),
        name="hgrn",
    )(p_c, s0, lbl, nw, gmat)


def _inproj_t_kernel(x_ref, nw_ref, sc_ref, sh_ref, hprev_ref, wta_ref, wtb_ref, wtc_ref,
                     pta_ref, ptb_ref, ptc_ref, rows_ref, hlast_ref, pprev_ref):
    h = _norm_mod(x_ref[...], nw_ref[...], sc_ref[...], sh_ref[...])
    hb = h.astype(BF16)
    nt = lambda w, a: lax.dot_general(w, a, _NT, preferred_element_type=F32)
    pta_ref[...] = nt(wta_ref[...], hb)
    ptb_ref[...] = nt(wtb_ref[...], hb)
    ptc_ref[...] = nt(wtc_ref[...], hb)
    rows_ref[...] = nt(hb, wtb_ref[0:rows_ref.shape[1], :])
    nb = hlast_ref.shape[0]
    hlast_ref[...] = h[h.shape[0] - nb:]
    pprev_ref[...] = nt(wta_ref[...], hprev_ref[...].astype(BF16))


def _inproj_t_call(x, nw, scale, shift, h_prev, wta, wtb, wtc, qkvw):
    rows, d = x.shape
    nb = h_prev.shape[0]
    sds = jax.ShapeDtypeStruct
    return pl.pallas_call(
        _inproj_t_kernel,
        out_shape=[sds((wta.shape[0], rows), F32), sds((wtb.shape[0], rows), F32),
                   sds((wtc.shape[0], rows), F32), sds((rows, qkvw), F32),
                   sds((nb, d), F32), sds((wta.shape[0], nb), F32)],
        compiler_params=pltpu.CompilerParams(vmem_limit_bytes=VMEM_LIMIT),
        name="inproj_t",
    )(x, nw, scale, shift, h_prev, wta, wtb, wtc)


def _col_sum(x):
    return jnp.sum(x, axis=0, keepdims=True)


def _rwkv_t_kernel(r_ref, k_ref, v_ref, tail_ref, rp_ref, kp_ref, vp_ref, tailp_ref,
                   mur_ref, muk_ref, muv_ref, mut_ref, vec_ref, wl_ref, s0_ref,
                   y_ref, sout_ref, v_scr, y_scr):
    nb = s0_ref.shape[-1]
    nt = r_ref.shape[-1] // nb
    w0, a0, k_k, k_a, r_k, ln_w, ln_b = (vec_ref[i, 0] for i in range(7))
    sout_ref[...] = s0_ref[...]

    def shifted(ref, pref, mu, idx, t):
        cur = ref[idx, :, t * nb:(t + 1) * nb]
        prv = pref[idx] if t == 0 else ref[idx, :, (t - 1) * nb:t * nb]
        return cur + mu * (prv - cur)

    for t in range(nt):
        r = shifted(r_ref, rp_ref, mur_ref[0], 0, t)
        k = shifted(k_ref, kp_ref, muk_ref[0], 0, t)
        v = shifted(v_ref, vp_ref, muv_ref[0], 0, t)
        t0 = shifted(tail_ref, tailp_ref, mut_ref[0], 0, t)
        t1 = shifted(tail_ref, tailp_ref, mut_ref[1], 1, t)
        act = jnp.concatenate([jnp.tanh(t0[0:32]), t0[32:64], _sigmoid(t1)], axis=0)
        lo = jnp.dot(wl_ref[0], act.astype(BF16), preferred_element_type=F32)
        w = -_softplus(-(w0 + lo[0:HEAD])) - 0.5
        dec = jnp.exp(-jnp.exp(w))
        a = _sigmoid(a0 + lo[HEAD:2 * HEAD])
        g = lo[2 * HEAD:3 * HEAD]
        kk = k * k_k
        kk = kk * lax.rsqrt(_col_sum(kk * kk) + L2_EPS)
        k = k * (1.0 + (a - 1.0) * k_a)
        a_t = -kk
        b_t = kk * a
        v_scr[...] = v

        def body(g, carry):
            base = pl.multiple_of(g * 8, 8)
            vg = v_scr[pl.ds(base, 8), :]
            ys = []
            for j in range(8):
                sv = sout_ref[0, base + j]
                sa = _col_sum(sv * a_t)
                sv = sv * dec + sa * b_t + vg[j:j + 1] * k
                sout_ref[0, base + j] = sv
                ys.append(_col_sum(sv * r))
            y_scr[pl.ds(base, 8), :] = jnp.concatenate(ys, axis=0)
            return carry

        lax.fori_loop(0, HEAD // 8, body, 0)
        y = y_scr[...]
        mean = _col_sum(y) * (1.0 / HEAD)
        dy = y - mean
        var = _col_sum(dy * dy) * (1.0 / HEAD)
        yn = dy * lax.rsqrt(var + GN_EPS) * ln_w + ln_b
        bonus = _col_sum(r * k * r_k) * v
        y_ref[0, :, t * nb:(t + 1) * nb] = (yn + bonus) * g


def _head_blocks(arr3, first, n=1):
    if n == 1:
        return pl.BlockSpec((1,) + arr3.shape[1:], lambda h: (first + h, 0, 0))
    return pl.BlockSpec((n,) + arr3.shape[1:], lambda h: (first // n, 0, 0))


def _rwkv_t_call(pta, pprev_t, s0, mu, vec, wl_t):
    nh, _, _, nb = s0.shape
    rows = pta.shape[1]
    p3 = pta.reshape(-1, HEAD, rows)
    pp3 = pprev_t.reshape(-1, HEAD, nb)
    mu3 = mu.reshape(-1, HEAD, 1)
    tail0 = 3 * nh
    state = pl.BlockSpec((1, HEAD, HEAD, nb), lambda h: (h, 0, 0, 0))
    return pl.pallas_call(
        _rwkv_t_kernel,
        grid=(nh,),
        in_specs=[
            _head_blocks(p3, 0), _head_blocks(p3, nh), _head_blocks(p3, 2 * nh), _head_blocks(p3, tail0, 2),
            _head_blocks(pp3, 0), _head_blocks(pp3, nh), _head_blocks(pp3, 2 * nh), _head_blocks(pp3, tail0, 2),
            _head_blocks(mu3, 0), _head_blocks(mu3, nh), _head_blocks(mu3, 2 * nh), _head_blocks(mu3, tail0, 2),
            pl.BlockSpec((vec.shape[0], 1, HEAD, 1), lambda h: (0, h, 0, 0)),
            pl.BlockSpec((1,) + wl_t.shape[1:], lambda h: (h, 0, 0)),
            state,
        ],
        out_specs=[pl.BlockSpec((1, HEAD, rows), lambda h: (h, 0, 0)), state],
        out_shape=[jax.ShapeDtypeStruct((nh, HEAD, rows), F32), jax.ShapeDtypeStruct(s0.shape, F32)],
        scratch_shapes=[pltpu.VMEM((HEAD, nb), F32), pltpu.VMEM((HEAD, nb), F32)],
        compiler_params=pltpu.CompilerParams(
            dimension_semantics=("parallel",), vmem_limit_bytes=VMEM_LIMIT),
        name="rwkv_t",
    )(p3, p3, p3, p3, pp3, pp3, pp3, pp3, mu3, mu3, mu3, mu3, vec, wl_t, s0)


def _gdn_t_kernel(q_ref, k_ref, v_ref, z_ref, tail_ref, cq_ref, ck_ref, cv_ref, wq_ref, wk_ref, wv_ref,
                  sc_ref, nw_ref, s0_ref, y_ref, sout_ref, k_scr, q_scr):
    h = pl.program_id(0)
    nh = pl.num_programs(0)
    nb = s0_ref.shape[-1]
    nt = q_ref.shape[-1] // nb
    a_log = sc_ref[0, 0:1, 0:1]
    dt_bias = sc_ref[0, 1:2, 0:1]
    sout_ref[...] = s0_ref[...]

    def conv(x_ref, c_ref, w_ref, t):
        acc = None
        for j in range(CONV_K):
            u = t + j - (CONV_K - 1)
            tap = c_ref[u + CONV_K - 1, 0] if u < 0 else x_ref[0, :, u * nb:(u + 1) * nb]
            term = w_ref[j, 0] * tap
            acc = term if acc is None else acc + term
        return acc * _sigmoid(acc)

    for t in range(nt):
        sl = slice(t * nb, (t + 1) * nb)
        q = conv(q_ref, cq_ref, wq_ref, t)
        k = conv(k_ref, ck_ref, wk_ref, t)
        v = conv(v_ref, cv_ref, wv_ref, t)
        q = q * (lax.rsqrt(_col_sum(q * q) + L2_EPS) * (HEAD ** -0.5))
        k = k * lax.rsqrt(_col_sum(k * k) + L2_EPS)
        tl = tail_ref[0:2 * 8, sl]
        pick = lambda row: _col_sum(jnp.where(_iota((2 * 8, 1), 0) == row, tl, 0.0))
        beta = _sigmoid(pick(h))
        eg = jnp.exp(-jnp.exp(a_log) * _softplus(pick(nh + h) + dt_bias))
        k_scr[...] = k
        q_scr[...] = q

        def body1(g, acc):
            base = pl.multiple_of(g * 8, 8)
            kg = k_scr[pl.ds(base, 8), :]
            for j in range(8):
                acc = acc + sout_ref[0, base + j] * kg[j:j + 1]
            return acc

        sk = lax.fori_loop(0, HEAD // 8, body1, jnp.zeros((HEAD, nb), F32))
        v_new = beta * (v - eg * sk)

        def body2(g, acc):
            base = pl.multiple_of(g * 8, 8)
            kg = k_scr[pl.ds(base, 8), :]
            qg = q_scr[pl.ds(base, 8), :]
            for j in range(8):
                s = eg * sout_ref[0, base + j] + kg[j:j + 1] * v_new
                sout_ref[0, base + j] = s
                acc = acc + s * qg[j:j + 1]
            return acc

        o = lax.fori_loop(0, HEAD // 8, body2, jnp.zeros((HEAD, nb), F32))
        o = o * lax.rsqrt(_col_sum(o * o) * (1.0 / HEAD) + RMS_EPS) * nw_ref[...]
        z = z_ref[0, :, sl]
        y_ref[0, :, sl] = o * (z * _sigmoid(z))


def _gdn_t_call(ptb, conv_t, s0, cw, scal, nw):
    nh, _, _, nb = s0.shape
    rows = ptb.shape[1]
    p3 = ptb.reshape(-1, HEAD, rows)
    c4 = conv_t.reshape(CONV_K - 1, -1, HEAD, nb)
    w4 = cw.reshape(CONV_K, -1, HEAD, 1)
    tap = lambda first: pl.BlockSpec((CONV_K - 1, 1, HEAD, nb), lambda h: (0, first + h, 0, 0))
    wsp = lambda first: pl.BlockSpec((CONV_K, 1, HEAD, 1), lambda h: (0, first + h, 0, 0))
    state = pl.BlockSpec((1, HEAD, HEAD, nb), lambda h: (h, 0, 0, 0))
    tail_blk = 4 * nh * HEAD // PAIR
    return pl.pallas_call(
        _gdn_t_kernel,
        grid=(nh,),
        in_specs=[
            _head_blocks(p3, 0), _head_blocks(p3, nh), _head_blocks(p3, 2 * nh), _head_blocks(p3, 3 * nh),
            pl.BlockSpec((PAIR, rows), lambda h: (tail_blk, 0)),
            tap(0), tap(nh), tap(2 * nh), wsp(0), wsp(nh), wsp(2 * nh),
            pl.BlockSpec((1,) + scal.shape[1:], lambda h: (h, 0, 0)),
            pl.BlockSpec(nw.shape, lambda h: (0, 0)),
            state,
        ],
        out_specs=[pl.BlockSpec((1, HEAD, rows), lambda h: (h, 0, 0)), state],
        out_shape=[jax.ShapeDtypeStruct((nh, HEAD, rows), F32), jax.ShapeDtypeStruct(s0.shape, F32)],
        scratch_shapes=[pltpu.VMEM((HEAD, nb), F32), pltpu.VMEM((HEAD, nb), F32)],
        compiler_params=pltpu.CompilerParams(
            dimension_semantics=("parallel",), vmem_limit_bytes=VMEM_LIMIT),
        name="gdn_t",
    )(p3, p3, p3, p3, ptb, c4, c4, c4, w4, w4, w4, scal, nw, s0)


def _hgrn_t_kernel(q_ref, f_ref, v_ref, z_ref, lbl_ref, nw_ref, s0_ref, y_ref, sout_ref,
                   f_scr, k_scr, q_scr, *, layer):
    nb = s0_ref.shape[-1]
    nt = q_ref.shape[-1] // nb
    logits = lbl_ref[:, 0]
    ex = jnp.exp(logits - jnp.max(logits, axis=0, keepdims=True))
    gam = ex / jnp.sum(ex, axis=0, keepdims=True)
    lb = jnp.sum(gam[0:layer + 1], axis=0) - gam[0]
    sout_ref[...] = s0_ref[...]

    for t in range(nt):
        sl = slice(t * nb, (t + 1) * nb)
        qp = q_ref[0, :, sl]
        f = f_ref[0, :, sl]
        v = v_ref[0, :, sl]
        z = z_ref[0, :, sl]
        q_scr[...] = qp * _sigmoid(qp)
        f_scr[...] = lb + (1.0 - lb) * _sigmoid(f)
        k_scr[...] = (1.0 - lb) * _sigmoid(-f)

        def body(g, acc):
            base = pl.multiple_of(g * 8, 8)
            fg, kg, qg = (ref[pl.ds(base, 8), :] for ref in (f_scr, k_scr, q_scr))
            for j in range(8):
                s = fg[j:j + 1] * sout_ref[0, base + j] + kg[j:j + 1] * v
                sout_ref[0, base + j] = s
                acc = acc + qg[j:j + 1] * s
            return acc

        o = lax.fori_loop(0, HEAD // 8, body, jnp.zeros((HEAD, nb), F32))
        o = o * lax.rsqrt(_col_sum(o * o) * (1.0 / HEAD) + RMS_EPS) * nw_ref[...]
        y_ref[0, :, sl] = o * _sigmoid(z)


def _hgrn_t_call(ptc, s0, lbl4, nw, layer):
    nh, _, _, nb = s0.shape
    rows = ptc.shape[1]
    p3 = ptc.reshape(-1, HEAD, rows)
    state = pl.BlockSpec((1, HEAD, HEAD, nb), lambda h: (h, 0, 0, 0))
    return pl.pallas_call(
        functools.partial(_hgrn_t_kernel, layer=layer),
        grid=(nh,),
        in_specs=[
            _head_blocks(p3, 0), _head_blocks(p3, nh), _head_blocks(p3, 2 * nh), _head_blocks(p3, 3 * nh),
            pl.BlockSpec((lbl4.shape[0], 1, HEAD, 1), lambda h: (0, h, 0, 0)),
            pl.BlockSpec(nw.shape, lambda h: (0, 0)),
            state,
        ],
        out_specs=[pl.BlockSpec((1, HEAD, rows), lambda h: (h, 0, 0)), state],
        out_shape=[jax.ShapeDtypeStruct((nh, HEAD, rows), F32), jax.ShapeDtypeStruct(s0.shape, F32)],
        scratch_shapes=[pltpu.VMEM((HEAD, nb), F32)] * 3,
        compiler_params=pltpu.CompilerParams(
            dimension_semantics=("parallel",), vmem_limit_bytes=VMEM_LIMIT),
        name="hgrn_t",
    )(p3, p3, p3, p3, lbl4, nw, s0)


def _ffn_kernel(x_ref, ya_ref, yb_ref, yc_ref, woa_ref, wob_ref, woc_ref, gm_ref, nw_ref,
                sc_ref, sh_ref, gf_ref, win_ref, wd_ref, fnw_ref, xo_ref, *yo_ref, th, y_transposed):
    ff = wd_ref.shape[0]
    dims = _TN if y_transposed else _NN
    proj = lambda y_ref, w_ref: lax.dot_general(y_ref[...].astype(BF16), w_ref[...], dims,
                                                preferred_element_type=F32)
    mix = proj(ya_ref, woa_ref) + proj(yb_ref, wob_ref) + proj(yc_ref, woc_ref)
    x1 = x_ref[...] + gm_ref[0] * mix
    h = _norm_mod(x1, nw_ref[...], sc_ref[0], sh_ref[0]).astype(BF16)
    acc = None
    for j in range(ff // th):
        gate = jnp.dot(h, win_ref[:, j * th:(j + 1) * th], preferred_element_type=F32)
        up = jnp.dot(h, win_ref[:, ff + j * th:ff + (j + 1) * th], preferred_element_type=F32)
        act = (gate * _sigmoid(gate) * up).astype(BF16)
        t = jnp.dot(act, wd_ref[j * th:(j + 1) * th, :], preferred_element_type=F32)
        acc = t if acc is None else acc + t
    xo = x1 + gf_ref[0] * acc
    xo_ref[...] = xo
    if yo_ref:
        yo_ref[0][...] = xo * lax.rsqrt(jnp.mean(xo * xo, axis=-1, keepdims=True) + RMS_EPS) * fnw_ref[...]


def _ffn_call(x, ya, yb, yc, woa, wob, woc, gate_m, nw, scale_f, shift_f, gate_f,
              w_in, w_down, fnw, *, tm, th, tiles_per_seq, final, y_transposed=False):
    rows, d = x.shape
    row = lambda a: pl.BlockSpec((tm, a.shape[1]), lambda i: (i, 0))
    yspec = (lambda a: pl.BlockSpec((a.shape[0], tm), lambda i: (0, i))) if y_transposed else row
    const = lambda a: pl.BlockSpec(a.shape, lambda i: (0, 0), pipeline_mode=pl.Buffered(1))
    mod = lambda a: _mod_spec(a, tm, tiles_per_seq)
    n_out = 2 if final else 1
    return pl.pallas_call(
        functools.partial(_ffn_kernel, th=th, y_transposed=y_transposed),
        grid=(rows // tm,),
        in_specs=[
            row(x), yspec(ya), yspec(yb), yspec(yc), const(woa), const(wob), const(woc),
            mod(gate_m), const(nw), mod(scale_f), mod(shift_f), mod(gate_f),
            const(w_in), const(w_down), const(fnw),
        ],
        out_specs=[row(x)] * n_out,
        out_shape=[jax.ShapeDtypeStruct((rows, d), F32)] * n_out,
        compiler_params=pltpu.CompilerParams(
            dimension_semantics=("parallel",), vmem_limit_bytes=VMEM_LIMIT),
        name="ffn",
    )(x, ya, yb, yc, woa, wob, woc, gate_m, nw, scale_f, shift_f, gate_f, w_in, w_down, fnw)


def _block_ones(n, blk):
    i = np.arange(n) // blk
    return jnp.asarray(i[:, None] == i[None, :], BF16)


def _head_select(first_lane, nh):
    m = np.zeros((PAIR, nh * HEAD), np.float32)
    for h in range(nh):
        m[first_lane + h, h * HEAD:(h + 1) * HEAD] = 1.0
    return jnp.asarray(m, BF16)


def _lane_place(x, first_lane):
    return jnp.zeros((PAIR,), F32).at[first_lane:first_lane + x.shape[0]].set(x)


def _run_chunked(x, mods, states, wts, *, tm, tm_ffn, th, bt):
    b, t, d = x.shape
    rows = b * t
    assert t % CHUNK == 0 and t % tm == 0 and t % tm_ffn == 0 and b % bt == 0 and t >= CONV_K - 1
    shift0, rwkv0, conv0, gdn0, hgrn0 = states
    nl = len(wts)
    new = ([], [], [], [], [])
    xr = x.reshape(rows, d)
    for l in range(nl):
        w = wts[l]
        m = mods[l]
        shift_m, scale_m, gate_m, shift_f, scale_f, gate_f = (m[:, i].reshape(b, 1, d) for i in range(6))
        p_a, p_b, p_c = _inproj_call(xr, w["norm_mix"], scale_m, shift_m, w["wta"], w["wtb"], w["wtc"],
                                     tm, t // tm)
        h_last, p_prev = _lastrow_call(xr.reshape(b, t, d)[:, t - 1], w["norm_mix"], m[:, 1], m[:, 0],
                                       shift0[l], w["wta"])
        p_a = p_a.reshape(b, t, -1)
        p_b = p_b.reshape(b, t, -1)
        p_c = p_c.reshape(b, t, -1)
        y_a, s_a = _rwkv_call(p_a, p_prev, rwkv0[l], w["mu"], w["rwkv_vec"], w["lora"], w["gpair"], bt=bt)
        y_b, s_b = _gdn_call(p_b, conv0[l], gdn0[l], w["conv_w"], w["gdn_tail"], w["gdn_norm"],
                             w["gpair"], w["bsel"], w["gsel"], bt=bt)
        y_c, s_c = _hgrn_call(p_c, hgrn0[l], w["lb_logits"], w["hgrn_norm"], w["gpair"], bt=bt, layer=l)
        conv_new = p_b[:, t - (CONV_K - 1):, :conv0.shape[-1]]
        res = _ffn_call(xr, y_a.reshape(rows, -1), y_b.reshape(rows, -1), y_c.reshape(rows, -1),
                        w["woa"], w["wob"], w["woc"], gate_m, w["norm_ffn"], scale_f, shift_f, gate_f,
                        w["w_ffn_in"], w["w_ffn_out"], w["final_norm"],
                        tm=tm_ffn, th=th, tiles_per_seq=t // tm_ffn, final=l == nl - 1)
        xr = res[0]
        for acc, s in zip(new, (h_last, s_a, conv_new, s_b, s_c)):
            acc.append(s)
    return res[1].reshape(b, t, d), [jnp.stack(acc) for acc in new]


def _run_steps(x, mods, states, wts, *, tm_ffn, th):
    b, t, d = x.shape
    rows = t * b
    tm_ffn = min(tm_ffn, rows)
    shift0, rwkv0, conv0, gdn0, hgrn0 = states
    to_lanes = lambda s: jnp.transpose(s, (0, 2, 3, 4, 1))
    rwkv_t, gdn_t, hgrn_t = to_lanes(rwkv0), to_lanes(gdn0), to_lanes(hgrn0)
    conv_t = jnp.transpose(conv0, (0, 2, 3, 1))
    nl = len(wts)
    new = ([], [], [], [], [])
    xr = jnp.transpose(x, (1, 0, 2)).reshape(rows, d)
    for l in range(nl):
        w = wts[l]
        m = mods[l]
        mod = lambda i, m=m: jnp.tile(m[:, i], (t, 1))
        tiled = lambda a: a.reshape(rows // tm_ffn, tm_ffn, d)
        pta, ptb, ptc, qkv_rows, h_last, pprev_t = _inproj_t_call(
            xr, w["norm_mix"], mod(1), mod(0), shift0[l], w["wta"], w["wtb"], w["wtc"], w["qkvw"])
        y_a, s_a = _rwkv_t_call(pta, pprev_t, rwkv_t[l], w["mu"], w["rwkv_vec_t"], w["lora_t"])
        y_b, s_b = _gdn_t_call(ptb, conv_t[l], gdn_t[l], w["conv_w_t"], w["gdn_scal"], w["gdn_norm_t"])
        y_c, s_c = _hgrn_t_call(ptc, hgrn_t[l], w["lb_logits_t"], w["hgrn_norm_t"], l)
        xp = jnp.concatenate([jnp.transpose(conv0[l], (1, 0, 2)), qkv_rows.reshape(t, b, -1)], axis=0)
        conv_new = jnp.transpose(xp[-(CONV_K - 1):], (1, 0, 2))
        res = _ffn_call(xr, y_a.reshape(-1, rows), y_b.reshape(-1, rows), y_c.reshape(-1, rows),
                        w["woa"], w["wob"], w["woc"], tiled(mod(2)), w["norm_ffn"],
                        tiled(mod(4)), tiled(mod(3)), tiled(mod(5)),
                        w["w_ffn_in"], w["w_ffn_out"], w["final_norm"],
                        tm=tm_ffn, th=th, tiles_per_seq=1, final=l == nl - 1, y_transposed=True)
        xr = res[0]
        for acc, s in zip(new, (h_last, s_a, conv_new, s_b, s_c)):
            acc.append(s)
    y = jnp.transpose(res[1].reshape(t, b, d), (1, 0, 2))
    from_lanes = lambda s: jnp.transpose(jnp.stack(s), (0, 4, 1, 2, 3))
    return y, [jnp.stack(new[0]), from_lanes(new[1]), jnp.stack(new[2]), from_lanes(new[3]), from_lanes(new[4])]


def kernel(x_prompt, x_sample, c_prompt, c_sample, state_rwkv_shift, state_rwkv, state_gdn_conv, state_gdn, state_hgrn, w_ada, b_ada, norm_mix_w, w_in, rwkv_mu, rwkv_w0, rwkv_w2, rwkv_a0, rwkv_a2, rwkv_g2, rwkv_k_k, rwkv_k_a, rwkv_r_k, rwkv_ln_w, rwkv_ln_b, gdn_conv_w, gdn_A_log, gdn_dt_bias, gdn_norm_w, hgrn_lb_logits, hgrn_norm_w, w_out, norm_ffn_w, w_ffn_in, w_ffn_out, final_norm_w):
    nl, d, _ = w_in.shape
    a_heads = state_rwkv.shape[2]
    b_heads = state_gdn.shape[2]
    c_heads = state_hgrn.shape[2]
    aw, bw, cw = a_heads * HEAD, b_heads * HEAD, c_heads * HEAD
    lw, la, lg = rwkv_w2.shape[1], rwkv_a2.shape[1], rwkv_g2.shape[1]
    a_cols = 3 * aw + lw + la + lg
    qkvw = 3 * bw
    b_cols = qkvw + 2 * b_heads + bw
    assert (aw, bw, cw) == (3 * PAIR, 3 * PAIR, 2 * PAIR) and lw + la + lg == PAIR and lw == 32 and la == 32

    bp, tpr, _ = x_prompt.shape
    bs, ts, _ = x_sample.shape

    gpair = _block_ones(PAIR, HEAD)
    bsel = _head_select(0, b_heads)
    gsel = _head_select(b_heads, b_heads)
    wts = []
    for l in range(nl):
        wit = jnp.swapaxes(w_in[l], 0, 1).astype(BF16)
        wbt = wit[a_cols:a_cols + b_cols]
        wtb = jnp.concatenate(
            [wbt[:qkvw], wbt[qkvw + 2 * b_heads:], wbt[qkvw:qkvw + 2 * b_heads],
             jnp.zeros((PAIR - 2 * b_heads, d), BF16)], axis=0)
        lora = jnp.zeros((PAIR, 3 * aw), F32)
        lora = lora.at[0:lw, 0:aw].set(rwkv_w2[l])
        lora = lora.at[lw:lw + la, aw:2 * aw].set(rwkv_a2[l])
        lora = lora.at[lw + la:, 2 * aw:].set(rwkv_g2[l])
        rwkv_vec = jnp.stack([rwkv_w0[l], rwkv_a0[l], rwkv_k_k[l], rwkv_k_a[l], rwkv_r_k[l].reshape(-1),
                              rwkv_ln_w[l], rwkv_ln_b[l], jnp.zeros((aw,), F32)])
        wo = w_out[l].astype(BF16)
        hcols = lambda a: a.reshape(a.shape[0], a_heads, HEAD).transpose(1, 2, 0)
        lora_t = jnp.concatenate([
            jnp.pad(hcols(rwkv_w2[l]), ((0, 0), (0, 0), (0, la + lg))),
            jnp.pad(hcols(rwkv_a2[l]), ((0, 0), (0, 0), (lw, lg))),
            jnp.pad(hcols(rwkv_g2[l]), ((0, 0), (0, 0), (lw + la, 0)))], axis=1)
        steps = dict(
            rwkv_vec_t=rwkv_vec[:7].reshape(7, a_heads, HEAD, 1), lora_t=lora_t.astype(BF16),
            conv_w_t=gdn_conv_w[l],
            gdn_scal=jnp.zeros((b_heads, 8, PAIR), F32)
                .at[:, 0].set(jnp.broadcast_to(gdn_A_log[l][:, None], (b_heads, PAIR)))
                .at[:, 1].set(jnp.broadcast_to(gdn_dt_bias[l][:, None], (b_heads, PAIR))),
            gdn_norm_t=gdn_norm_w[l].reshape(HEAD, 1),
            lb_logits_t=hgrn_lb_logits.reshape(nl, c_heads, HEAD, 1),
            hgrn_norm_t=hgrn_norm_w[l].reshape(HEAD, 1),
        )
        wts.append(dict(
            **steps,
            norm_mix=norm_mix_w[l].reshape(1, d),
            wta=wit[:a_cols], wtb=wtb, wtc=wit[a_cols + b_cols:], qkvw=qkvw,
            mu=rwkv_mu[l].reshape(1, a_cols), rwkv_vec=rwkv_vec, lora=lora.astype(BF16),
            gpair=gpair, bsel=bsel, gsel=gsel,
            conv_w=jnp.concatenate([gdn_conv_w[l], jnp.zeros((8 - CONV_K, qkvw), F32)], axis=0),
            gdn_tail=jnp.zeros((8, PAIR), F32).at[0].set(_lane_place(gdn_A_log[l], b_heads))
                                              .at[1].set(_lane_place(gdn_dt_bias[l], b_heads)),
            gdn_norm=jnp.tile(gdn_norm_w[l], b_heads).reshape(1, bw),
            lb_logits=hgrn_lb_logits,
            hgrn_norm=jnp.tile(hgrn_norm_w[l], c_heads).reshape(1, cw),
            woa=wo[:aw], wob=wo[aw:aw + bw], woc=wo[aw + bw:],
            norm_ffn=norm_ffn_w[l].reshape(1, d),
            w_ffn_in=w_ffn_in[l].astype(BF16), w_ffn_out=w_ffn_out[l].astype(BF16),
            final_norm=final_norm_w.reshape(1, d),
        ))

    mod = _ada_call(jnp.concatenate([c_prompt, c_sample], axis=0), w_ada, b_ada)
    mod = mod.reshape(nl, bp + bs, 6, d)
    mods_p = [mod[l, :bp] for l in range(nl)]
    mods_s = [mod[l, bp:] for l in range(nl)]

    zeros = lambda s: jnp.zeros((nl, bp) + s.shape[2:], s.dtype)
    states_p = tuple(zeros(s) for s in (state_rwkv_shift, state_rwkv, state_gdn_conv, state_gdn, state_hgrn))
    states_s = (state_rwkv_shift, state_rwkv, state_gdn_conv, state_gdn, state_hgrn)

    th = w_ffn_out.shape[1] // FFN_HIDDEN_TILES
    y_p, new_p = _run_chunked(x_prompt, mods_p, states_p, wts, tm=TM_INPROJ, tm_ffn=TM_FFN, th=th, bt=SEQS_PER_STEP)
    y_s, new_s = _run_steps(x_sample, mods_s, states_s, wts, tm_ffn=TM_FFN_STEPS, th=th)
    return (y_p, y_s, *new_p, *new_s)
```

```python
import functools

import numpy as np
import jax
import jax.numpy as jnp
from jax import lax
from jax.experimental import pallas as pl
from jax.experimental.pallas import tpu as pltpu

F32 = jnp.float32
BF16 = jnp.bfloat16

HEAD = 64
PAIR = 2 * HEAD
CHUNK = 64
HBLK = 16
INV_BASE = 16
CONV_K = 4
RMS_EPS = 1e-6
L2_EPS = 1e-6
GN_EPS = 64e-5
VMEM_LIMIT = 56 * 1024 * 1024

TM_INPROJ = 512
TM_FFN = 512
TM_FFN_STEPS = 256
FFN_HIDDEN_TILES = 2
SEQS_PER_STEP = 4

_NN = (((1,), (0,)), ((), ()))
_NT = (((1,), (1,)), ((), ()))
_TN = (((0,), (0,)), ((), ()))


def _dot(a, b, dims=_NN):
    return lax.dot_general(a.astype(BF16), b.astype(BF16), dims, preferred_element_type=F32)


def _dot_sel(x, m, pieces=3):
    ps = _bf16_pieces(x, pieces)
    return lax.dot_general(jnp.concatenate(ps, axis=1), jnp.concatenate([m] * pieces, axis=0), _NN,
                           preferred_element_type=F32)


def _bf16_pieces(x, pieces):
    ps = []
    rem = x
    for i in range(pieces):
        p = rem.astype(BF16)
        ps.append(p)
        if i + 1 < pieces:
            rem = rem - p.astype(F32)
    return ps


def _seg_sum(x, gpair, pieces=2):
    return jnp.concatenate(
        [_dot_sel(x[:, i:i + PAIR], gpair, pieces) for i in range(0, x.shape[1], PAIR)], axis=1)


def _sigmoid(x):
    return jax.nn.sigmoid(x)


def _softplus(x):
    return jnp.maximum(x, 0.0) + jnp.log1p(jnp.exp(-jnp.abs(x)))


def _iota(shape, dim):
    return lax.broadcasted_iota(jnp.int32, shape, dim)


def _stack_heads(x):
    is_a = _iota((1, PAIR), 1) < HEAD
    return jnp.concatenate([jnp.where(is_a, x, 0.0), jnp.where(is_a, 0.0, x)], axis=0)


def _fold_heads(x, c):
    return x[0:c] + x[c:2 * c]


def _pair_blockdiag_mask():
    return (_iota((PAIR, 1), 0) >> 6) == (_iota((1, PAIR), 1) >> 6)


def _load_pair_state(s_ref, bi, pr):
    sa = s_ref[bi, 2 * pr]
    sb = s_ref[bi, 2 * pr + 1]
    z = jnp.zeros((HEAD, HEAD), F32)
    return jnp.concatenate(
        [jnp.concatenate([sa, z], axis=1), jnp.concatenate([z, sb], axis=1)], axis=0)


def _store_pair_state(s_ref, bi, pr, s):
    s_ref[bi, 2 * pr] = s[0:HEAD, 0:HEAD]
    s_ref[bi, 2 * pr + 1] = s[HEAD:PAIR, HEAD:PAIR]


def _dot_left01(m, x, pieces=3):
    return lax.dot_general(jnp.concatenate([m] * pieces, axis=1),
                           jnp.concatenate(_bf16_pieces(x, pieces), axis=0), _NN,
                           preferred_element_type=F32)


def _tri_inverse_many(ns, c):
    size = ns[0].shape[0]
    r = _iota((size, 1), 0)
    cc = _iota((1, size), 1)
    eye = (r == cc).astype(F32)
    sh = INV_BASE.bit_length() - 1
    diag = [jnp.where((r >> sh) == (cc >> sh), n, 0.0) for n in ns]
    invs = [eye + d for d in diag]
    pws = [_dot(d, d) for d in diag]
    span = 4
    while span < INV_BASE:
        both = [_dot(jnp.concatenate([inv, pw], axis=0), pw) for inv, pw in zip(invs, pws)]
        invs = [inv + b[0:size] for inv, b in zip(invs, both)]
        pws = [b[size:2 * size] for b in both]
        span *= 2
    invs = [inv + _dot(inv, pw) for inv, pw in zip(invs, pws)]
    blk = INV_BASE
    while blk < c:
        sh = blk.bit_length() - 1
        off = ((r >> (sh + 1)) == (cc >> (sh + 1))) & ((r >> sh) != (cc >> sh))
        low = [_dot(inv, jnp.where(off, n, 0.0)) for inv, n in zip(invs, ns)]
        invs = [inv + _dot(t, inv) for inv, t in zip(invs, low)]
        blk *= 2
    return invs


def _ada_kernel(c_ref, w_ref, b_ref, o_ref):
    c = c_ref[...]
    o_ref[0] = _dot(c * _sigmoid(c), w_ref[0]) + b_ref[0]


def _ada_call(c_all, w_ada, b_ada):
    nl, d, n6 = w_ada.shape
    rows = c_all.shape[0]
    tn = n6 // 4
    return pl.pallas_call(
        _ada_kernel,
        grid=(nl, n6 // tn),
        in_specs=[
            pl.BlockSpec((rows, d), lambda l, j: (0, 0)),
            pl.BlockSpec((1, d, tn), lambda l, j: (l, 0, j)),
            pl.BlockSpec((1, 1, tn), lambda l, j: (l, 0, j)),
        ],
        out_specs=pl.BlockSpec((1, rows, tn), lambda l, j: (l, 0, j)),
        out_shape=jax.ShapeDtypeStruct((nl, rows, n6), F32),
        compiler_params=pltpu.CompilerParams(
            dimension_semantics=("parallel", "parallel"), vmem_limit_bytes=VMEM_LIMIT),
        name="ada",
    )(c_all, w_ada, b_ada.reshape(nl, 1, n6))


def _norm_mod(x, nw, scale, shift):
    y = x * lax.rsqrt(jnp.mean(x * x, axis=-1, keepdims=True) + RMS_EPS) * nw
    return y * (1.0 + scale) + shift


def _inproj_kernel(x_ref, nw_ref, sc_ref, sh_ref, wa_ref, wb_ref, wc_ref, pa_ref, pb_ref, pc_ref):
    h = _norm_mod(x_ref[...], nw_ref[...], sc_ref[0], sh_ref[0]).astype(BF16)
    pa_ref[...] = lax.dot_general(h, wa_ref[...], _NT, preferred_element_type=F32)
    pb_ref[...] = lax.dot_general(h, wb_ref[...], _NT, preferred_element_type=F32)
    pc_ref[...] = lax.dot_general(h, wc_ref[...], _NT, preferred_element_type=F32)


def _mod_spec(mod, tm, tiles_per_seq):
    if mod.shape[1] == 1:
        return pl.BlockSpec((1, 1, mod.shape[2]), lambda i, *_: (i // tiles_per_seq, 0, 0))
    return pl.BlockSpec((1, tm, mod.shape[2]), lambda i, *_: (i, 0, 0))


def _inproj_call(x, nw, scale, shift, wa, wb, wc, tm, tiles_per_seq):
    rows, d = x.shape
    full = lambda a: pl.BlockSpec(a.shape, lambda i: (0, 0), pipeline_mode=pl.Buffered(1))
    outs = [jax.ShapeDtypeStruct((rows, w.shape[0]), F32) for w in (wa, wb, wc)]
    return pl.pallas_call(
        _inproj_kernel,
        grid=(rows // tm,),
        in_specs=[
            pl.BlockSpec((tm, d), lambda i: (i, 0)),
            full(nw),
            _mod_spec(scale, tm, tiles_per_seq),
            _mod_spec(shift, tm, tiles_per_seq),
            full(wa), full(wb), full(wc),
        ],
        out_specs=[pl.BlockSpec((tm, w.shape[0]), lambda i: (i, 0)) for w in (wa, wb, wc)],
        out_shape=outs,
        compiler_params=pltpu.CompilerParams(
            dimension_semantics=("parallel",), vmem_limit_bytes=VMEM_LIMIT),
        name="inproj",
    )(x, nw, scale, shift, wa, wb, wc)


def _lastrow_kernel(x_ref, nw_ref, sc_ref, sh_ref, hprev_ref, wa_ref, h_ref, pprev_ref):
    h_ref[...] = _norm_mod(x_ref[...], nw_ref[...], sc_ref[...], sh_ref[...])
    pprev_ref[...] = lax.dot_general(hprev_ref[...].astype(BF16), wa_ref[...], _NT,
                                     preferred_element_type=F32)


def _lastrow_call(x_last, nw, scale, shift, h_prev, wa):
    b, d = x_last.shape
    return pl.pallas_call(
        _lastrow_kernel,
        out_shape=[jax.ShapeDtypeStruct((b, d), F32), jax.ShapeDtypeStruct((b, wa.shape[0]), F32)],
        compiler_params=pltpu.CompilerParams(vmem_limit_bytes=VMEM_LIMIT),
        name="lastrow",
    )(x_last, nw, scale, shift, h_prev, wa)


def _rwkv_kernel(p_ref, pprev_ref, s0_ref, mu_ref, vec_ref, wl_ref, g_ref,
                 y_ref, sout_ref, s_scr, prev_scr, *, bt):
    c = CHUNK
    ci = pl.program_id(1)
    nci = pl.num_programs(1)
    aw = 3 * PAIR
    rows = _iota((c, 1), 0)
    lane = _iota((1, PAIR), 1)
    vec = vec_ref[...]
    w0, a0, k_k, k_a, r_k, ln_w, ln_b = (vec[i:i + 1] for i in range(7))
    gpair = g_ref[...]
    tri = (rows >= _iota((1, c), 1)).astype(BF16)
    r2 = _iota((2 * c, 1), 0) & (c - 1)
    c2 = _iota((1, 2 * c), 1) & (c - 1)
    strict = r2 > c2
    incl = r2 >= c2
    bd = _pair_blockdiag_mask()

    @pl.when(ci == 0)
    def _():
        for bi in range(bt):
            for pr in range(3):
                s_scr[bi, pr] = _load_pair_state(s0_ref, bi, pr)
            prev_scr[bi] = pprev_ref[bi]

    shifted = []
    for bi in range(bt):
        p = p_ref[bi]
        prev = pltpu.roll(p, 1, axis=0)
        prev = jnp.where(rows == 0, prev_scr[bi], prev)
        prev_scr[bi] = p[c - 1:c]
        shifted.append(p + mu_ref[...] * (prev - p))
    xs = jnp.concatenate(shifted, axis=0)
    r = xs[:, 0:aw]
    k = xs[:, aw:2 * aw]
    v = xs[:, 2 * aw:3 * aw]
    tail = xs[:, 3 * aw:3 * aw + PAIR]
    act = jnp.where(lane < 32, jnp.tanh(tail), jnp.where(lane < 64, tail, _sigmoid(tail)))
    lo = _dot(act, wl_ref[...])
    w = -_softplus(-(w0 + lo[:, 0:aw])) - 0.5
    ld = -jnp.exp(w)
    a = _sigmoid(a0 + lo[:, aw:2 * aw])
    g = lo[:, 2 * aw:3 * aw]
    kk = k * k_k
    kk = kk * lax.rsqrt(_seg_sum(kk * kk, gpair) + L2_EPS)
    k = k * (1.0 + (a - 1.0) * k_a)
    cums = [_dot_left01(tri, ld[c * bi:c * (bi + 1)]) for bi in range(bt)]
    cum = jnp.concatenate(cums, axis=0)
    cum_last = jnp.concatenate([jnp.broadcast_to(x[c - 1:c], x.shape) for x in cums], axis=0)
    e_neg = jnp.exp(-cum)
    e_end = jnp.exp(cum_last - cum)
    pre = dict(a_t=-kk * jnp.exp(cum - ld), b_t=kk * a * e_neg, k_t=k * e_neg, r_t=r * jnp.exp(cum), v=v,
               b_end=kk * a * e_end, k_end=k * e_end)

    chains = [(bi, pr) for bi in range(bt) for pr in range(3)]
    pair = lambda name: [pre[name][c * bi:c * (bi + 1), PAIR * pr:PAIR * (pr + 1)] for bi, pr in chains]
    a_t, b_t, k_t, r_t, v_p = pair("a_t"), pair("b_t"), pair("k_t"), pair("r_t"), pair("v")
    b_end, k_end = pair("b_end"), pair("k_end")
    d_end = [jnp.exp(cums[bi][c - 1:c, PAIR * pr:PAIR * (pr + 1)]) for bi, pr in chains]
    v_st = [_stack_heads(x) for x in v_p]
    gs = [_dot(jnp.concatenate([_stack_heads(a), _stack_heads(r)], axis=0),
               jnp.concatenate([_stack_heads(b), _stack_heads(k)], axis=0), _NT)
          for a, r, b, k in zip(a_t, r_t, b_t, k_t)]
    a_ak = [jnp.where(strict, g[0:2 * c, 2 * c:4 * c], 0.0) for g in gs]
    p_rbk = [jnp.concatenate([jnp.where(incl, g[2 * c:4 * c, 0:2 * c], 0.0),
                              jnp.where(incl, g[2 * c:4 * c, 2 * c:4 * c], 0.0)], axis=1) for g in gs]
    tinv = _tri_inverse_many([jnp.where(strict, g[0:2 * c, 0:2 * c], 0.0) for g in gs], c)
    s = [s_scr[bi, pr] for bi, pr in chains]
    ars = [_dot(jnp.concatenate([a, r], axis=0), s_, _NT) for a, r, s_ in zip(a_t, r_t, s)]
    akv = [_dot(m, x) for m, x in zip(a_ak, v_st)]
    u_st = [_dot(t, _stack_heads(x[0:c]) + y) for t, x, y in zip(tinv, ars, akv)]
    y_st = [_dot(p, jnp.concatenate([u, x], axis=0)) for p, u, x in zip(p_rbk, u_st, v_st)]
    ds = [_dot(jnp.concatenate([_fold_heads(u, c), x], axis=0), jnp.concatenate([b_, k_], axis=0), _TN)
          for u, x, b_, k_ in zip(u_st, v_p, b_end, k_end)]
    for i, (bi, pr) in enumerate(chains):
        s_scr[bi, pr] = s[i] * d_end[i] + jnp.where(bd, ds[i], 0.0)

    y = jnp.concatenate(
        [jnp.concatenate([ars[3 * bi + pr][c:2 * c] + _fold_heads(y_st[3 * bi + pr], c) for pr in range(3)],
                         axis=1) for bi in range(bt)], axis=0)
    mean = _seg_sum(y, gpair) * (1.0 / HEAD)
    dy = y - mean
    var = _seg_sum(dy * dy, gpair) * (1.0 / HEAD)
    yn = dy * lax.rsqrt(var + GN_EPS) * ln_w + ln_b
    bonus = _seg_sum(r * k * r_k, gpair) * v
    y_ref[...] = ((yn + bonus) * g).reshape(bt, c, aw)

    @pl.when(ci == nci - 1)
    def _():
        for bi in range(bt):
            for pr in range(3):
                _store_pair_state(sout_ref, bi, pr, s_scr[bi, pr])


def _rwkv_call(p_a, p_prev, s0, mu, vec, wl, gmat, *, bt):
    b, tp, acols = p_a.shape
    nh = s0.shape[1]
    aw = nh * HEAD
    cin = CHUNK
    nc = tp // cin
    const = lambda a: pl.BlockSpec(a.shape, lambda i, j: (0,) * a.ndim)
    kern = functools.partial(_rwkv_kernel, bt=bt)
    return pl.pallas_call(
        kern,
        grid=(b // bt, nc),
        in_specs=[
            pl.BlockSpec((bt, cin, acols), lambda i, j: (i, j, 0)),
            pl.BlockSpec((bt, 1, acols), lambda i, j: (i, 0, 0)),
            pl.BlockSpec((bt, nh, HEAD, HEAD), lambda i, j: (i, 0, 0, 0)),
            const(mu), const(vec), const(wl), const(gmat),
        ],
        out_specs=[
            pl.BlockSpec((bt, cin, aw), lambda i, j: (i, j, 0)),
            pl.BlockSpec((bt, nh, HEAD, HEAD), lambda i, j: (i, 0, 0, 0)),
        ],
        out_shape=[jax.ShapeDtypeStruct((b, tp, aw), F32),
                   jax.ShapeDtypeStruct(s0.shape, F32)],
        scratch_shapes=[pltpu.VMEM((bt, nh // 2, PAIR, PAIR), F32),
                        pltpu.VMEM((bt, 1, acols), F32)],
        compiler_params=pltpu.CompilerParams(
            dimension_semantics=("parallel", "arbitrary"), vmem_limit_bytes=VMEM_LIMIT),
        name="rwkv",
    )(p_a, p_prev.reshape(b, 1, acols), s0, mu, vec, wl, gmat)


def _gdn_kernel(p_ref, conv0_ref, s0_ref, cw_ref, tp_ref, nw_ref, g_ref, bsel_ref, gsel_ref,
                y_ref, sout_ref, s_scr, cbuf, *, bt):
    c = CHUNK
    ci = pl.program_id(1)
    nci = pl.num_programs(1)
    bw = 3 * PAIR
    qkvw = 3 * bw
    rows = _iota((c, 1), 0)
    lane = _iota((1, PAIR), 1)
    nh = 2 * 3
    gpair = g_ref[...]
    tri = (rows >= _iota((1, c), 1)).astype(BF16)
    r2 = _iota((2 * c, 1), 0) & (c - 1)
    c2 = _iota((1, 2 * c), 1) & (c - 1)
    same =(_iota((2 * c, 1), 0) >> 6) == (_iota((1, 2 * c), 1) >> 6)
    strict = same & (r2 > c2)
    incl = same & (r2 >= c2)
    bd = _pair_blockdiag_mask()
    is_a = lane < HEAD
    a_log = tp_ref[0:1]
    dt_bias = tp_ref[1:2]
    is_beta = lane < nh
    is_g = (lane >= nh) & (lane < 2 * nh)

    @pl.when(ci == 0)
    def _():
        for bi in range(bt):
            for pr in range(3):
                s_scr[bi, pr] = _load_pair_state(s0_ref, bi, pr)
            cbuf[bi, 0:8] = jnp.zeros((8, qkvw), F32)
            cbuf[bi, 8 - (CONV_K - 1):8] = conv0_ref[bi]

    convs = []
    for bi in range(bt):
        cbuf[bi, 8:8 + c] = p_ref[bi, :, 0:qkvw]
        conv = None
        for j in range(CONV_K):
            t = cw_ref[j:j + 1] * cbuf[bi, pl.ds(8 - (CONV_K - 1) + j, c), :]
            conv = t if conv is None else conv + t
        cbuf[bi, 0:8] = cbuf[bi, c:c + 8]
        convs.append(conv)
    conv = jnp.concatenate(convs, axis=0)
    p = p_ref[...].reshape(bt * c, p_ref.shape[2])
    qkv = conv * _sigmoid(conv)
    q = qkv[:, 0:bw]
    k = qkv[:, bw:2 * bw]
    v = qkv[:, 2 * bw:3 * bw]
    z = p[:, qkvw:qkvw + bw]
    tail = p[:, qkvw + bw:qkvw + bw + PAIR]
    q = q * (lax.rsqrt(_seg_sum(q * q, gpair) + L2_EPS) * (HEAD ** -0.5))
    k = k * lax.rsqrt(_seg_sum(k * k, gpair) + L2_EPS)
    beta = jnp.where(is_beta, _sigmoid(tail), 0.0)
    gl = jnp.where(is_g, -jnp.exp(a_log) * _softplus(tail + dt_bias), 0.0)
    gcum128 = jnp.concatenate([_dot_left01(tri, gl[c * bi:c * (bi + 1)]) for bi in range(bt)], axis=0)
    gcum = _dot_sel(gcum128, gsel_ref[...])
    beta = _dot_sel(beta, bsel_ref[...])
    g_last = jnp.concatenate(
        [jnp.broadcast_to(gcum[c * (bi + 1) - 1:c * (bi + 1)], (c, bw)) for bi in range(bt)], axis=0)
    eg = jnp.exp(gcum)
    kb = k * beta
    pre = dict(q=q, k=k, kb=kb, gcum=gcum, kg=kb * eg, qg=q * eg, kd=k * jnp.exp(g_last - gcum),
               vb=v * beta, d_end=jnp.exp(g_last))

    chains = [(bi, pr) for bi in range(bt) for pr in range(3)]
    pair = lambda name: [pre[name][c * bi:c * (bi + 1), PAIR * pr:PAIR * (pr + 1)] for bi, pr in chains]
    q, k, kb, gcum, kg, qg = pair("q"), pair("k"), pair("kb"), pair("gcum"), pair("kg"), pair("qg")
    kd, vb = pair("kd"), pair("vb")
    d_end = [x[0:1] for x in pair("d_end")]

    def col(x):
        sw = pltpu.roll(x, HEAD, axis=1)
        return jnp.concatenate([jnp.where(is_a, x, sw), jnp.where(is_a, sw, x)], axis=0)

    gcol = [col(x) for x in gcum]
    diff = [x - x.T for x in gcol]
    dec_s = [jnp.where(strict, jnp.exp(jnp.where(strict, x, 0.0)), 0.0) for x in diff]
    dec_i = [jnp.where(incl, jnp.exp(jnp.where(incl, x, 0.0)), 0.0) for x in diff]
    sc = [_dot(jnp.concatenate([_stack_heads(x), _stack_heads(y)], axis=0), _stack_heads(w), _NT)
          for x, y, w in zip(kb, q, k)]
    tinv = _tri_inverse_many([-(x[0:2 * c] * d) for x, d in zip(sc, dec_s)], c)
    qk = [x[2 * c:4 * c] * d for x, d in zip(sc, dec_i)]
    s = [s_scr[bi, pr] for bi, pr in chains]
    kqs = [_dot(jnp.concatenate([x, y], axis=0), s_) for x, y, s_ in zip(kg, qg, s)]
    v_new = [_dot(t, _stack_heads(x - y[0:c])) for t, x, y in zip(tinv, vb, kqs)]
    o_st = [_dot(x, y) for x, y in zip(qk, v_new)]
    ds = [_dot(x, _fold_heads(y, c), _TN) for x, y in zip(kd, v_new)]
    for i, (bi, pr) in enumerate(chains):
        s_scr[bi, pr] = s[i] * d_end[i] + jnp.where(bd, ds[i], 0.0)

    o = jnp.concatenate(
        [jnp.concatenate([kqs[3 * bi + pr][c:2 * c] + _fold_heads(o_st[3 * bi + pr], c) for pr in range(3)],
                         axis=1) for bi in range(bt)], axis=0)
    o = o * lax.rsqrt(_seg_sum(o * o, gpair) * (1.0 / HEAD) + RMS_EPS) * nw_ref[...]
    y_ref[...] = (o * (z * _sigmoid(z))).reshape(bt, c, bw)

    @pl.when(ci == nci - 1)
    def _():
        for bi in range(bt):
            for pr in range(3):
                _store_pair_state(sout_ref, bi, pr, s_scr[bi, pr])


def _gdn_call(p_b, conv0, s0, cw, tailp, nw, gmat, bsel, gsel, *, bt):
    b, tp, bcols = p_b.shape
    nh = s0.shape[1]
    bw = nh * HEAD
    cin = CHUNK
    nc = tp // cin
    const = lambda a: pl.BlockSpec(a.shape, lambda i, j: (0,) * a.ndim)
    kern = functools.partial(_gdn_kernel, bt=bt)
    return pl.pallas_call(
        kern,
        grid=(b // bt, nc),
        in_specs=[
            pl.BlockSpec((bt, cin, bcols), lambda i, j: (i, j, 0)),
            pl.BlockSpec((bt, CONV_K - 1, 3 * bw), lambda i, j: (i, 0, 0)),
            pl.BlockSpec((bt, nh, HEAD, HEAD), lambda i, j: (i, 0, 0, 0)),
            const(cw), const(tailp), const(nw), const(gmat), const(bsel), const(gsel),
        ],
        out_specs=[
            pl.BlockSpec((bt, cin, bw), lambda i, j: (i, j, 0)),
            pl.BlockSpec((bt, nh, HEAD, HEAD), lambda i, j: (i, 0, 0, 0)),
        ],
        out_shape=[jax.ShapeDtypeStruct((b, tp, bw), F32),
                   jax.ShapeDtypeStruct(s0.shape, F32)],
        scratch_shapes=[pltpu.VMEM((bt, nh // 2, PAIR, PAIR), F32),
                        pltpu.VMEM((bt, CHUNK + 8, 3 * bw), F32)],
        compiler_params=pltpu.CompilerParams(
            dimension_semantics=("parallel", "arbitrary"), vmem_limit_bytes=VMEM_LIMIT),
        name="gdn",
    )(p_b, conv0, s0, cw, tailp, nw, gmat, bsel, gsel)


def _hgrn_kernel(p_ref, s0_ref, lbl_ref, nw_ref, g_ref, y_ref, sout_ref, s_scr,
                 *, bt, layer):
    c = CHUNK
    ci = pl.program_id(1)
    nci = pl.num_programs(1)
    cw = 2 * PAIR
    rows = _iota((c, 1), 0)
    cols = _iota((1, c), 1)
    sameblk = (rows >> 4) == (cols >> 4)
    tri_tot = jnp.concatenate([sameblk & (rows >= cols), sameblk], axis=0).astype(BF16)
    rows_b = _iota((HBLK, 1), 0)
    bd = _pair_blockdiag_mask()
    gpair = g_ref[...]

    logits = lbl_ref[...]
    ex = jnp.exp(logits - jnp.max(logits, axis=0, keepdims=True))
    gam = ex / jnp.sum(ex, axis=0, keepdims=True)
    lb = jnp.sum(gam[0:layer + 1], axis=0, keepdims=True) - gam[0:1]

    @pl.when(ci == 0)
    def _():
        for bi in range(bt):
            for pr in range(2):
                s_scr[bi, pr] = _load_pair_state(s0_ref, bi, pr).T

    pre = []
    for bi in range(bt):
        p = p_ref[bi]
        qp = p[:, 0:cw]
        f = p[:, cw:2 * cw]
        v = p[:, 2 * cw:3 * cw]
        z = p[:, 3 * cw:4 * cw]
        q = qp * _sigmoid(qp)
        logf = jnp.log(lb + (1.0 - lb) * _sigmoid(f))
        k = (1.0 - lb) * _sigmoid(-f)
        cums = _dot_left01(tri_tot, logf)
        bl = cums[0:c]
        btot = cums[c:2 * c]
        pre.append(dict(q=q, k=k, v=v, z=z, bl=bl, q_in=q * jnp.exp(bl),
                        k_out=k * jnp.exp(btot - bl), d_blk=jnp.exp(btot)))

    chains = [(bi, pr) for bi in range(bt) for pr in range(2)]
    nblk = c // HBLK

    def blk_of(name, bi, pr, blk):
        return pre[bi][name][HBLK * blk:HBLK * (blk + 1), PAIR * pr:PAIR * (pr + 1)]

    intra = {}
    ds = {}
    for blk in range(nblk):
        for bi, pr in chains:
            qb, kb, vb, bb = (blk_of(n, bi, pr, blk) for n in ("q", "k", "v", "bl"))
            xs = []
            for j in range(HBLK):
                e = jnp.exp(jnp.minimum(bb - bb[j:j + 1], 0.0))
                xs.append(qb * e * kb[j:j + 1])
            att = _dot_sel(jnp.concatenate(xs, axis=0), gpair, pieces=1)
            acc = None
            for j in range(HBLK):
                t = jnp.where(rows_b >= j, att[HBLK * j:HBLK * (j + 1)], 0.0) * vb[j:j + 1]
                acc = t if acc is None else acc + t
            intra[bi, pr, blk] = acc
            ds[bi, pr, blk] = jnp.where(bd, _dot(vb, blk_of("k_out", bi, pr, blk), _TN), 0.0)

    s = {ch: s_scr[ch] for ch in chains}
    outs = {}
    for blk in range(nblk):
        for bi, pr in chains:
            outs[bi, pr, blk] = intra[bi, pr, blk] + _dot(blk_of("q_in", bi, pr, blk), s[bi, pr], _NT)
            s[bi, pr] = s[bi, pr] * blk_of("d_blk", bi, pr, blk)[0:1] + ds[bi, pr, blk]
    for ch in chains:
        s_scr[ch] = s[ch]

    o = jnp.concatenate(
        [jnp.concatenate([outs[bi, pr, blk] for bi in range(bt) for blk in range(nblk)], axis=0)
         for pr in range(2)], axis=1)
    z = jnp.concatenate([pre[bi]["z"] for bi in range(bt)], axis=0)
    o = o * lax.rsqrt(_seg_sum(o * o, gpair) * (1.0 / HEAD) + RMS_EPS) * nw_ref[...]
    y_ref[...] = (o * _sigmoid(z)).reshape(bt, c, cw)

    @pl.when(ci == nci - 1)
    def _():
        for bi in range(bt):
            for pr in range(2):
                _store_pair_state(sout_ref, bi, pr, s_scr[bi, pr].T)


def _hgrn_call(p_c, s0, lbl, nw, gmat, *, bt, layer):
    b, tp, ccols = p_c.shape
    nh = s0.shape[1]
    cw = nh * HEAD
    cin = CHUNK
    nc = tp // cin
    const = lambda a: pl.BlockSpec(a.shape, lambda i, j: (0,) * a.ndim)
    kern = functools.partial(_hgrn_kernel, bt=bt, layer=layer)
    return pl.pallas_call(
        kern,
        grid=(b // bt, nc),
        in_specs=[
            pl.BlockSpec((bt, cin, ccols), lambda i, j: (i, j, 0)),
            pl.BlockSpec((bt, nh, HEAD, HEAD), lambda i, j: (i, 0, 0, 0)),
            const(lbl), const(nw), const(gmat),
        ],
        out_specs=[
            pl.BlockSpec((bt, cin, cw), lambda i, j: (i, j, 0)),
            pl.BlockSpec((bt, nh, HEAD, HEAD), lambda i, j: (i, 0, 0, 0)),
        ],
        out_shape=[jax.ShapeDtypeStruct((b, tp, cw), F32),
                   jax.ShapeDtypeStruct(s0.shape, F32)],
        scratch_shapes=[pltpu.VMEM((bt, nh // 2, PAIR, PAIR), F32)],
        compiler_params=pltpu.CompilerParams(
            dimension_semantics=("parallel", "arbitrary"), vmem_limit_bytes=VMEM_LIMIT),
        name="hgrn",
    )(p_c, s0, lbl, nw, gmat)


def _inproj_t_kernel(x_ref, nw_ref, sc_ref, sh_ref, hprev_ref, wta_ref, wtb_ref, wtc_ref,
                     pta_ref, ptb_ref, ptc_ref, rows_ref, hlast_ref, pprev_ref):
    h = _norm_mod(x_ref[...], nw_ref[...], sc_ref[...], sh_ref[...])
    hb = h.astype(BF16)
    nt = lambda w, a: lax.dot_general(w, a, _NT, preferred_element_type=F32)
    pta_ref[...] = nt(wta_ref[...], hb)
    ptb_ref[...] = nt(wtb_ref[...], hb)
    ptc_ref[...] = nt(wtc_ref[...], hb)
    rows_ref[...] = nt(hb, wtb_ref[0:rows_ref.shape[1], :])
    nb = hlast_ref.shape[0]
    hlast_ref[...] = h[h.shape[0] - nb:]
    pprev_ref[...] = nt(wta_ref[...], hprev_ref[...].astype(BF16))


def _inproj_t_call(x, nw, scale, shift, h_prev, wta, wtb, wtc, qkvw):
    rows, d = x.shape
    nb = h_prev.shape[0]
    sds = jax.ShapeDtypeStruct
    return pl.pallas_call(
        _inproj_t_kernel,
        out_shape=[sds((wta.shape[0], rows), F32), sds((wtb.shape[0], rows), F32),
                   sds((wtc.shape[0], rows), F32), sds((rows, qkvw), F32),
                   sds((nb, d), F32), sds((wta.shape[0], nb), F32)],
        compiler_params=pltpu.CompilerParams(vmem_limit_bytes=VMEM_LIMIT),
        name="inproj_t",
    )(x, nw, scale, shift, h_prev, wta, wtb, wtc)


def _col_sum(x):
    return jnp.sum(x, axis=0, keepdims=True)


def _rwkv_t_kernel(r_ref, k_ref, v_ref, tail_ref, rp_ref, kp_ref, vp_ref, tailp_ref,
                   mur_ref, muk_ref, muv_ref, mut_ref, vec_ref, wl_ref, s0_ref,
                   y_ref, sout_ref, v_scr, y_scr):
    nb = s0_ref.shape[-1]
    nt = r_ref.shape[-1] // nb
    w0, a0, k_k, k_a, r_k, ln_w, ln_b = (vec_ref[i, 0] for i in range(7))
    sout_ref[...] = s0_ref[...]

    def shifted(ref, pref, mu, idx, t):
        cur = ref[idx, :, t * nb:(t + 1) * nb]
        prv = pref[idx] if t == 0 else ref[idx, :, (t - 1) * nb:t * nb]
        return cur + mu * (prv - cur)

    for t in range(nt):
        r = shifted(r_ref, rp_ref, mur_ref[0], 0, t)
        k = shifted(k_ref, kp_ref, muk_ref[0], 0, t)
        v = shifted(v_ref, vp_ref, muv_ref[0], 0, t)
        t0 = shifted(tail_ref, tailp_ref, mut_ref[0], 0, t)
        t1 = shifted(tail_ref, tailp_ref, mut_ref[1], 1, t)
        act = jnp.concatenate([jnp.tanh(t0[0:32]), t0[32:64], _sigmoid(t1)], axis=0)
        lo = jnp.dot(wl_ref[0], act.astype(BF16), preferred_element_type=F32)
        w = -_softplus(-(w0 + lo[0:HEAD])) - 0.5
        dec = jnp.exp(-jnp.exp(w))
        a = _sigmoid(a0 + lo[HEAD:2 * HEAD])
        g = lo[2 * HEAD:3 * HEAD]
        kk = k * k_k
        kk = kk * lax.rsqrt(_col_sum(kk * kk) + L2_EPS)
        k = k * (1.0 + (a - 1.0) * k_a)
        a_t = -kk
        b_t = kk * a
        v_scr[...] = v

        def body(g, carry):
            base = pl.multiple_of(g * 8, 8)
            vg = v_scr[pl.ds(base, 8), :]
            ys = []
            for j in range(8):
                sv = sout_ref[0, base + j]
                sa = _col_sum(sv * a_t)
                sv = sv * dec + sa * b_t + vg[j:j + 1] * k
                sout_ref[0, base + j] = sv
                ys.append(_col_sum(sv * r))
            y_scr[pl.ds(base, 8), :] = jnp.concatenate(ys, axis=0)
            return carry

        lax.fori_loop(0, HEAD // 8, body, 0)
        y = y_scr[...]
        mean = _col_sum(y) * (1.0 / HEAD)
        dy = y - mean
        var = _col_sum(dy * dy) * (1.0 / HEAD)
        yn = dy * lax.rsqrt(var + GN_EPS) * ln_w + ln_b
        bonus = _col_sum(r * k * r_k) * v
        y_ref[0, :, t * nb:(t + 1) * nb] = (yn + bonus) * g


def _head_blocks(arr3, first, n=1):
    if n == 1:
        return pl.BlockSpec((1,) + arr3.shape[1:], lambda h: (first + h, 0, 0))
    return pl.BlockSpec((n,) + arr3.shape[1:], lambda h: (first // n, 0, 0))


def _rwkv_t_call(pta, pprev_t, s0, mu, vec, wl_t):
    nh, _, _, nb = s0.shape
    rows = pta.shape[1]
    p3 = pta.reshape(-1, HEAD, rows)
    pp3 = pprev_t.reshape(-1, HEAD, nb)
    mu3 = mu.reshape(-1, HEAD, 1)
    tail0 = 3 * nh
    state = pl.BlockSpec((1, HEAD, HEAD, nb), lambda h: (h, 0, 0, 0))
    return pl.pallas_call(
        _rwkv_t_kernel,
        grid=(nh,),
        in_specs=[
            _head_blocks(p3, 0), _head_blocks(p3, nh), _head_blocks(p3, 2 * nh), _head_blocks(p3, tail0, 2),
            _head_blocks(pp3, 0), _head_blocks(pp3, nh), _head_blocks(pp3, 2 * nh), _head_blocks(pp3, tail0, 2),
            _head_blocks(mu3, 0), _head_blocks(mu3, nh), _head_blocks(mu3, 2 * nh), _head_blocks(mu3, tail0, 2),
            pl.BlockSpec((vec.shape[0], 1, HEAD, 1), lambda h: (0, h, 0, 0)),
            pl.BlockSpec((1,) + wl_t.shape[1:], lambda h: (h, 0, 0)),
            state,
        ],
        out_specs=[pl.BlockSpec((1, HEAD, rows), lambda h: (h, 0, 0)), state],
        out_shape=[jax.ShapeDtypeStruct((nh, HEAD, rows), F32), jax.ShapeDtypeStruct(s0.shape, F32)],
        scratch_shapes=[pltpu.VMEM((HEAD, nb), F32), pltpu.VMEM((HEAD, nb), F32)],
        compiler_params=pltpu.CompilerParams(
            dimension_semantics=("parallel",), vmem_limit_bytes=VMEM_LIMIT),
        name="rwkv_t",
    )(p3, p3, p3, p3, pp3, pp3, pp3, pp3, mu3, mu3, mu3, mu3, vec, wl_t, s0)


def _gdn_t_kernel(q_ref, k_ref, v_ref, z_ref, tail_ref, cq_ref, ck_ref, cv_ref, wq_ref, wk_ref, wv_ref,
                  sc_ref, nw_ref, s0_ref, y_ref, sout_ref, k_scr, q_scr):
    h = pl.program_id(0)
    nh = pl.num_programs(0)
    nb = s0_ref.shape[-1]
    nt = q_ref.shape[-1] // nb
    a_log = sc_ref[0, 0:1, 0:1]
    dt_bias = sc_ref[0, 1:2, 0:1]
    sout_ref[...] = s0_ref[...]

    def conv(x_ref, c_ref, w_ref, t):
        acc = None
        for j in range(CONV_K):
            u = t + j - (CONV_K - 1)
            tap = c_ref[u + CONV_K - 1, 0] if u < 0 else x_ref[0, :, u * nb:(u + 1) * nb]
            term = w_ref[j, 0] * tap
            acc = term if acc is None else acc + term
        return acc * _sigmoid(acc)

    for t in range(nt):
        sl = slice(t * nb, (t + 1) * nb)
        q = conv(q_ref, cq_ref, wq_ref, t)
        k = conv(k_ref, ck_ref, wk_ref, t)
        v = conv(v_ref, cv_ref, wv_ref, t)
        q = q * (lax.rsqrt(_col_sum(q * q) + L2_EPS) * (HEAD ** -0.5))
        k = k * lax.rsqrt(_col_sum(k * k) + L2_EPS)
        tl = tail_ref[0:2 * 8, sl]
        pick = lambda row: _col_sum(jnp.where(_iota((2 * 8, 1), 0) == row, tl, 0.0))
        beta = _sigmoid(pick(h))
        eg = jnp.exp(-jnp.exp(a_log) * _softplus(pick(nh + h) + dt_bias))
        k_scr[...] = k
        q_scr[...] = q

        def body1(g, acc):
            base = pl.multiple_of(g * 8, 8)
            kg = k_scr[pl.ds(base, 8), :]
            for j in range(8):
                acc = acc + sout_ref[0, base + j] * kg[j:j + 1]
            return acc

        sk = lax.fori_loop(0, HEAD // 8, body1, jnp.zeros((HEAD, nb), F32))
        v_new = beta * (v - eg * sk)

        def body2(g, acc):
            base = pl.multiple_of(g * 8, 8)
            kg = k_scr[pl.ds(base, 8), :]
            qg = q_scr[pl.ds(base, 8), :]
            for j in range(8):
                s = eg * sout_ref[0, base + j] + kg[j:j + 1] * v_new
                sout_ref[0, base + j] = s
                acc = acc + s * qg[j:j + 1]
            return acc

        o = lax.fori_loop(0, HEAD // 8, body2, jnp.zeros((HEAD, nb), F32))
        o = o * lax.rsqrt(_col_sum(o * o) * (1.0 / HEAD) + RMS_EPS) * nw_ref[...]
        z = z_ref[0, :, sl]
        y_ref[0, :, sl] = o * (z * _sigmoid(z))


def _gdn_t_call(ptb, conv_t, s0, cw, scal, nw):
    nh, _, _, nb = s0.shape
    rows = ptb.shape[1]
    p3 = ptb.reshape(-1, HEAD, rows)
    c4 = conv_t.reshape(CONV_K - 1, -1, HEAD, nb)
    w4 = cw.reshape(CONV_K, -1, HEAD, 1)
    tap = lambda first: pl.BlockSpec((CONV_K - 1, 1, HEAD, nb), lambda h: (0, first + h, 0, 0))
    wsp = lambda first: pl.BlockSpec((CONV_K, 1, HEAD, 1), lambda h: (0, first + h, 0, 0))
    state = pl.BlockSpec((1, HEAD, HEAD, nb), lambda h: (h, 0, 0, 0))
    tail_blk = 4 * nh * HEAD // PAIR
    return pl.pallas_call(
        _gdn_t_kernel,
        grid=(nh,),
        in_specs=[
            _head_blocks(p3, 0), _head_blocks(p3, nh), _head_blocks(p3, 2 * nh), _head_blocks(p3, 3 * nh),
            pl.BlockSpec((PAIR, rows), lambda h: (tail_blk, 0)),
            tap(0), tap(nh), tap(2 * nh), wsp(0), wsp(nh), wsp(2 * nh),
            pl.BlockSpec((1,) + scal.shape[1:], lambda h: (h, 0, 0)),
            pl.BlockSpec(nw.shape, lambda h: (0, 0)),
            state,
        ],
        out_specs=[pl.BlockSpec((1, HEAD, rows), lambda h: (h, 0, 0)), state],
        out_shape=[jax.ShapeDtypeStruct((nh, HEAD, rows), F32), jax.ShapeDtypeStruct(s0.shape, F32)],
        scratch_shapes=[pltpu.VMEM((HEAD, nb), F32), pltpu.VMEM((HEAD, nb), F32)],
        compiler_params=pltpu.CompilerParams(
            dimension_semantics=("parallel",), vmem_limit_bytes=VMEM_LIMIT),
        name="gdn_t",
    )(p3, p3, p3, p3, ptb, c4, c4, c4, w4, w4, w4, scal, nw, s0)


def _hgrn_t_kernel(q_ref, f_ref, v_ref, z_ref, lbl_ref, nw_ref, s0_ref, y_ref, sout_ref,
                   f_scr, k_scr, q_scr, *, layer):
    nb = s0_ref.shape[-1]
    nt = q_ref.shape[-1] // nb
    logits = lbl_ref[:, 0]
    ex = jnp.exp(logits - jnp.max(logits, axis=0, keepdims=True))
    gam = ex / jnp.sum(ex, axis=0, keepdims=True)
    lb = jnp.sum(gam[0:layer + 1], axis=0) - gam[0]
    sout_ref[...] = s0_ref[...]

    for t in range(nt):
        sl = slice(t * nb, (t + 1) * nb)
        qp = q_ref[0, :, sl]
        f = f_ref[0, :, sl]
        v = v_ref[0, :, sl]
        z = z_ref[0, :, sl]
        q_scr[...] = qp * _sigmoid(qp)
        f_scr[...] = lb + (1.0 - lb) * _sigmoid(f)
        k_scr[...] = (1.0 - lb) * _sigmoid(-f)

        def body(g, acc):
            base = pl.multiple_of(g * 8, 8)
            fg, kg, qg = (ref[pl.ds(base, 8), :] for ref in (f_scr, k_scr, q_scr))
            for j in range(8):
                s = fg[j:j + 1] * sout_ref[0, base + j] + kg[j:j + 1] * v
                sout_ref[0, base + j] = s
                acc = acc + qg[j:j + 1] * s
            return acc

        o = lax.fori_loop(0, HEAD // 8, body, jnp.zeros((HEAD, nb), F32))
        o = o * lax.rsqrt(_col_sum(o * o) * (1.0 / HEAD) + RMS_EPS) * nw_ref[...]
        y_ref[0, :, sl] = o * _sigmoid(z)


def _hgrn_t_call(ptc, s0, lbl4, nw, layer):
    nh, _, _, nb = s0.shape
    rows = ptc.shape[1]
    p3 = ptc.reshape(-1, HEAD, rows)
    state = pl.BlockSpec((1, HEAD, HEAD, nb), lambda h: (h, 0, 0, 0))
    return pl.pallas_call(
        functools.partial(_hgrn_t_kernel, layer=layer),
        grid=(nh,),
        in_specs=[
            _head_blocks(p3, 0), _head_blocks(p3, nh), _head_blocks(p3, 2 * nh), _head_blocks(p3, 3 * nh),
            pl.BlockSpec((lbl4.shape[0], 1, HEAD, 1), lambda h: (0, h, 0, 0)),
            pl.BlockSpec(nw.shape, lambda h: (0, 0)),
            state,
        ],
        out_specs=[pl.BlockSpec((1, HEAD, rows), lambda h: (h, 0, 0)), state],
        out_shape=[jax.ShapeDtypeStruct((nh, HEAD, rows), F32), jax.ShapeDtypeStruct(s0.shape, F32)],
        scratch_shapes=[pltpu.VMEM((HEAD, nb), F32)] * 3,
        compiler_params=pltpu.CompilerParams(
            dimension_semantics=("parallel",), vmem_limit_bytes=VMEM_LIMIT),
        name="hgrn_t",
    )(p3, p3, p3, p3, lbl4, nw, s0)


def _ffn_kernel(x_ref, ya_ref, yb_ref, yc_ref, woa_ref, wob_ref, woc_ref, gm_ref, nw_ref,
                sc_ref, sh_ref, gf_ref, win_ref, wd_ref, fnw_ref, xo_ref, *yo_ref, th, y_transposed):
    ff = wd_ref.shape[0]
    dims = _TN if y_transposed else _NN
    proj = lambda y_ref, w_ref: lax.dot_general(y_ref[...].astype(BF16), w_ref[...], dims,
                                                preferred_element_type=F32)
    mix = proj(ya_ref, woa_ref) + proj(yb_ref, wob_ref) + proj(yc_ref, woc_ref)
    x1 = x_ref[...] + gm_ref[0] * mix
    h = _norm_mod(x1, nw_ref[...], sc_ref[0], sh_ref[0]).astype(BF16)
    acc = None
    for j in range(ff // th):
        gate = jnp.dot(h, win_ref[:, j * th:(j + 1) * th], preferred_element_type=F32)
        up = jnp.dot(h, win_ref[:, ff + j * th:ff + (j + 1) * th], preferred_element_type=F32)
        act = (gate * _sigmoid(gate) * up).astype(BF16)
        t = jnp.dot(act, wd_ref[j * th:(j + 1) * th, :], preferred_element_type=F32)
        acc = t if acc is None else acc + t
    xo = x1 + gf_ref[0] * acc
    xo_ref[...] = xo
    if yo_ref:
        yo_ref[0][...] = xo * lax.rsqrt(jnp.mean(xo * xo, axis=-1, keepdims=True) + RMS_EPS) * fnw_ref[...]


def _ffn_call(x, ya, yb, yc, woa, wob, woc, gate_m, nw, scale_f, shift_f, gate_f,
              w_in, w_down, fnw, *, tm, th, tiles_per_seq, final, y_transposed=False):
    rows, d = x.shape
    row = lambda a: pl.BlockSpec((tm, a.shape[1]), lambda i: (i, 0))
    yspec = (lambda a: pl.BlockSpec((a.shape[0], tm), lambda i: (0, i))) if y_transposed else row
    const = lambda a: pl.BlockSpec(a.shape, lambda i: (0, 0), pipeline_mode=pl.Buffered(1))
    mod = lambda a: _mod_spec(a, tm, tiles_per_seq)
    n_out = 2 if final else 1
    return pl.pallas_call(
        functools.partial(_ffn_kernel, th=th, y_transposed=y_transposed),
        grid=(rows // tm,),
        in_specs=[
            row(x), yspec(ya), yspec(yb), yspec(yc), const(woa), const(wob), const(woc),
            mod(gate_m), const(nw), mod(scale_f), mod(shift_f), mod(gate_f),
            const(w_in), const(w_down), const(fnw),
        ],
        out_specs=[row(x)] * n_out,
        out_shape=[jax.ShapeDtypeStruct((rows, d), F32)] * n_out,
        compiler_params=pltpu.CompilerParams(
            dimension_semantics=("parallel",), vmem_limit_bytes=VMEM_LIMIT),
        name="ffn",
    )(x, ya, yb, yc, woa, wob, woc, gate_m, nw, scale_f, shift_f, gate_f, w_in, w_down, fnw)


def _block_ones(n, blk):
    i = np.arange(n) // blk
    return jnp.asarray(i[:, None] == i[None, :], BF16)


def _head_select(first_lane, nh):
    m = np.zeros((PAIR, nh * HEAD), np.float32)
    for h in range(nh):
        m[first_lane + h, h * HEAD:(h + 1) * HEAD] = 1.0
    return jnp.asarray(m, BF16)


def _lane_place(x, first_lane):
    return jnp.zeros((PAIR,), F32).at[first_lane:first_lane + x.shape[0]].set(x)


def _run_chunked(x, mods, states, wts, *, tm, tm_ffn, th, bt):
    b, t, d = x.shape
    rows = b * t
    assert t % CHUNK == 0 and t % tm == 0 and t % tm_ffn == 0 and b % bt == 0 and t >= CONV_K - 1
    shift0, rwkv0, conv0, gdn0, hgrn0 = states
    nl = len(wts)
    new = ([], [], [], [], [])
    xr = x.reshape(rows, d)
    for l in range(nl):
        w = wts[l]
        m = mods[l]
        shift_m, scale_m, gate_m, shift_f, scale_f, gate_f = (m[:, i].reshape(b, 1, d) for i in range(6))
        p_a, p_b, p_c = _inproj_call(xr, w["norm_mix"], scale_m, shift_m, w["wta"], w["wtb"], w["wtc"],
                                     tm, t // tm)
        h_last, p_prev = _lastrow_call(xr.reshape(b, t, d)[:, t - 1], w["norm_mix"], m[:, 1], m[:, 0],
                                       shift0[l], w["wta"])
        p_a = p_a.reshape(b, t, -1)
        p_b = p_b.reshape(b, t, -1)
        p_c = p_c.reshape(b, t, -1)
        y_a, s_a = _rwkv_call(p_a, p_prev, rwkv0[l], w["mu"], w["rwkv_vec"], w["lora"], w["gpair"], bt=bt)
        y_b, s_b = _gdn_call(p_b, conv0[l], gdn0[l], w["conv_w"], w["gdn_tail"], w["gdn_norm"],
                             w["gpair"], w["bsel"], w["gsel"], bt=bt)
        y_c, s_c = _hgrn_call(p_c, hgrn0[l], w["lb_logits"], w["hgrn_norm"], w["gpair"], bt=bt, layer=l)
        conv_new = p_b[:, t - (CONV_K - 1):, :conv0.shape[-1]]
        res = _ffn_call(xr, y_a.reshape(rows, -1), y_b.reshape(rows, -1), y_c.reshape(rows, -1),
                        w["woa"], w["wob"], w["woc"], gate_m, w["norm_ffn"], scale_f, shift_f, gate_f,
                        w["w_ffn_in"], w["w_ffn_out"], w["final_norm"],
                        tm=tm_ffn, th=th, tiles_per_seq=t // tm_ffn, final=l == nl - 1)
        xr = res[0]
        for acc, s in zip(new, (h_last, s_a, conv_new, s_b, s_c)):
            acc.append(s)
    return res[1].reshape(b, t, d), [jnp.stack(acc) for acc in new]


def _run_steps(x, mods, states, wts, *, tm_ffn, th):
    b, t, d = x.shape
    rows = t * b
    tm_ffn = min(tm_ffn, rows)
    shift0, rwkv0, conv0, gdn0, hgrn0 = states
    to_lanes = lambda s: jnp.transpose(s, (0, 2, 3, 4, 1))
    rwkv_t, gdn_t, hgrn_t = to_lanes(rwkv0), to_lanes(gdn0), to_lanes(hgrn0)
    conv_t = jnp.transpose(conv0, (0, 2, 3, 1))
    nl = len(wts)
    new = ([], [], [], [], [])
    xr = jnp.transpose(x, (1, 0, 2)).reshape(rows, d)
    for l in range(nl):
        w = wts[l]
        m = mods[l]
        mod = lambda i, m=m: jnp.tile(m[:, i], (t, 1))
        tiled = lambda a: a.reshape(rows // tm_ffn, tm_ffn, d)
        pta, ptb, ptc, qkv_rows, h_last, pprev_t = _inproj_t_call(
            xr, w["norm_mix"], mod(1), mod(0), shift0[l], w["wta"], w["wtb"], w["wtc"], w["qkvw"])
        y_a, s_a = _rwkv_t_call(pta, pprev_t, rwkv_t[l], w["mu"], w["rwkv_vec_t"], w["lora_t"])
        y_b, s_b = _gdn_t_call(ptb, conv_t[l], gdn_t[l], w["conv_w_t"], w["gdn_scal"], w["gdn_norm_t"])
        y_c, s_c = _hgrn_t_call(ptc, hgrn_t[l], w["lb_logits_t"], w["hgrn_norm_t"], l)
        xp = jnp.concatenate([jnp.transpose(conv0[l], (1, 0, 2)), qkv_rows.reshape(t, b, -1)], axis=0)
        conv_new = jnp.transpose(xp[-(CONV_K - 1):], (1, 0, 2))
        res = _ffn_call(xr, y_a.reshape(-1, rows), y_b.reshape(-1, rows), y_c.reshape(-1, rows),
                        w["woa"], w["wob"], w["woc"], tiled(mod(2)), w["norm_ffn"],
                        tiled(mod(4)), tiled(mod(3)), tiled(mod(5)),
                        w["w_ffn_in"], w["w_ffn_out"], w["final_norm"],
                        tm=tm_ffn, th=th, tiles_per_seq=1, final=l == nl - 1, y_transposed=True)
        xr = res[0]
        for acc, s in zip(new, (h_last, s_a, conv_new, s_b, s_c)):
            acc.append(s)
    y = jnp.transpose(res[1].reshape(t, b, d), (1, 0, 2))
    from_lanes = lambda s: jnp.transpose(jnp.stack(s), (0, 4, 1, 2, 3))
    return y, [jnp.stack(new[0]), from_lanes(new[1]), jnp.stack(new[2]), from_lanes(new[3]), from_lanes(new[4])]


def kernel(x_prompt, x_sample, c_prompt, c_sample, state_rwkv_shift, state_rwkv, state_gdn_conv, state_gdn, state_hgrn, w_ada, b_ada, norm_mix_w, w_in, rwkv_mu, rwkv_w0, rwkv_w2, rwkv_a0, rwkv_a2, rwkv_g2, rwkv_k_k, rwkv_k_a, rwkv_r_k, rwkv_ln_w, rwkv_ln_b, gdn_conv_w, gdn_A_log, gdn_dt_bias, gdn_norm_w, hgrn_lb_logits, hgrn_norm_w, w_out, norm_ffn_w, w_ffn_in, w_ffn_out, final_norm_w):
    nl, d, _ = w_in.shape
    a_heads = state_rwkv.shape[2]
    b_heads = state_gdn.shape[2]
    c_heads = state_hgrn.shape[2]
    aw, bw, cw = a_heads * HEAD, b_heads * HEAD, c_heads * HEAD
    lw, la, lg = rwkv_w2.shape[1], rwkv_a2.shape[1], rwkv_g2.shape[1]
    a_cols = 3 * aw + lw + la + lg
    qkvw = 3 * bw
    b_cols = qkvw + 2 * b_heads + bw
    assert (aw, bw, cw) == (3 * PAIR, 3 * PAIR, 2 * PAIR) and lw + la + lg == PAIR and lw == 32 and la == 32

    bp, tpr, _ = x_prompt.shape
    bs, ts, _ = x_sample.shape

    gpair = _block_ones(PAIR, HEAD)
    bsel = _head_select(0, b_heads)
    gsel = _head_select(b_heads, b_heads)
    wts = []
    for l in range(nl):
        wit = jnp.swapaxes(w_in[l], 0, 1).astype(BF16)
        wbt = wit[a_cols:a_cols + b_cols]
        wtb = jnp.concatenate(
            [wbt[:qkvw], wbt[qkvw + 2 * b_heads:], wbt[qkvw:qkvw + 2 * b_heads],
             jnp.zeros((PAIR - 2 * b_heads, d), BF16)], axis=0)
        lora = jnp.zeros((PAIR, 3 * aw), F32)
        lora = lora.at[0:lw, 0:aw].set(rwkv_w2[l])
        lora = lora.at[lw:lw + la, aw:2 * aw].set(rwkv_a2[l])
        lora = lora.at[lw + la:, 2 * aw:].set(rwkv_g2[l])
        rwkv_vec = jnp.stack([rwkv_w0[l], rwkv_a0[l], rwkv_k_k[l], rwkv_k_a[l], rwkv_r_k[l].reshape(-1),
                              rwkv_ln_w[l], rwkv_ln_b[l], jnp.zeros((aw,), F32)])
        wo = w_out[l].astype(BF16)
        hcols = lambda a: a.reshape(a.shape[0], a_heads, HEAD).transpose(1, 2, 0)
        lora_t = jnp.concatenate([
            jnp.pad(hcols(rwkv_w2[l]), ((0, 0), (0, 0), (0, la + lg))),
            jnp.pad(hcols(rwkv_a2[l]), ((0, 0), (0, 0), (lw, lg))),
            jnp.pad(hcols(rwkv_g2[l]), ((0, 0), (0, 0), (lw + la, 0)))], axis=1)
        steps = dict(
            rwkv_vec_t=rwkv_vec[:7].reshape(7, a_heads, HEAD, 1), lora_t=lora_t.astype(BF16),
            conv_w_t=gdn_conv_w[l],
            gdn_scal=jnp.zeros((b_heads, 8, PAIR), F32)
                .at[:, 0].set(jnp.broadcast_to(gdn_A_log[l][:, None], (b_heads, PAIR)))
                .at[:, 1].set(jnp.broadcast_to(gdn_dt_bias[l][:, None], (b_heads, PAIR))),
            gdn_norm_t=gdn_norm_w[l].reshape(HEAD, 1),
            lb_logits_t=hgrn_lb_logits.reshape(nl, c_heads, HEAD, 1),
            hgrn_norm_t=hgrn_norm_w[l].reshape(HEAD, 1),
        )
        wts.append(dict(
            **steps,
            norm_mix=norm_mix_w[l].reshape(1, d),
            wta=wit[:a_cols], wtb=wtb, wtc=wit[a_cols + b_cols:], qkvw=qkvw,
            mu=rwkv_mu[l].reshape(1, a_cols), rwkv_vec=rwkv_vec, lora=lora.astype(BF16),
            gpair=gpair, bsel=bsel, gsel=gsel,
            conv_w=jnp.concatenate([gdn_conv_w[l], jnp.zeros((8 - CONV_K, qkvw), F32)], axis=0),
            gdn_tail=jnp.zeros((8, PAIR), F32).at[0].set(_lane_place(gdn_A_log[l], b_heads))
                                              .at[1].set(_lane_place(gdn_dt_bias[l], b_heads)),
            gdn_norm=jnp.tile(gdn_norm_w[l], b_heads).reshape(1, bw),
            lb_logits=hgrn_lb_logits,
            hgrn_norm=jnp.tile(hgrn_norm_w[l], c_heads).reshape(1, cw),
            woa=wo[:aw], wob=wo[aw:aw + bw], woc=wo[aw + bw:],
            norm_ffn=norm_ffn_w[l].reshape(1, d),
            w_ffn_in=w_ffn_in[l].astype(BF16), w_ffn_out=w_ffn_out[l].astype(BF16),
            final_norm=final_norm_w.reshape(1, d),
        ))

    mod = _ada_call(jnp.concatenate([c_prompt, c_sample], axis=0), w_ada, b_ada)
    mod = mod.reshape(nl, bp + bs, 6, d)
    mods_p = [mod[l, :bp] for l in range(nl)]
    mods_s = [mod[l, bp:] for l in range(nl)]

    zeros = lambda s: jnp.zeros((nl, bp) + s.shape[2:], s.dtype)
    states_p = tuple(zeros(s) for s in (state_rwkv_shift, state_rwkv, state_gdn_conv, state_gdn, state_hgrn))
    states_s = (state_rwkv_shift, state_rwkv, state_gdn_conv, state_gdn, state_hgrn)

    th = w_ffn_out.shape[1] // FFN_HIDDEN_TILES
    y_p, new_p = _run_chunked(x_prompt, mods_p, states_p, wts, tm=TM_INPROJ, tm_ffn=TM_FFN, th=th, bt=SEQS_PER_STEP)
    y_s, new_s = _run_steps(x_sample, mods_s, states_s, wts, tm_ffn=TM_FFN_STEPS, th=th)
    return (y_p, y_s, *new_p, *new_s)
```

```python
import functools

import numpy as np
import jax
import jax.numpy as jnp
from jax import lax
from jax.experimental import pallas as pl
from jax.experimental.pallas import tpu as pltpu

F32 = jnp.float32
BF16 = jnp.bfloat16

HEAD = 64
PAIR = 2 * HEAD
CHUNK = 64
HBLK = 16
INV_BASE = 16
CONV_K = 4
RMS_EPS = 1e-6
L2_EPS = 1e-6
GN_EPS = 64e-5
VMEM_LIMIT = 56 * 1024 * 1024

TM_INPROJ = 512
TM_FFN = 512
TM_FFN_STEPS = 256
FFN_HIDDEN_TILES = 2
SEQS_PER_STEP = 8

_NN = (((1,), (0,)), ((), ()))
_NT = (((1,), (1,)), ((), ()))
_TN = (((0,), (0,)), ((), ()))


def _dot(a, b, dims=_NN):
    return lax.dot_general(a.astype(BF16), b.astype(BF16), dims, preferred_element_type=F32)


def _dot_sel(x, m, pieces=3):
    ps = _bf16_pieces(x, pieces)
    return lax.dot_general(jnp.concatenate(ps, axis=1), jnp.concatenate([m] * pieces, axis=0), _NN,
                           preferred_element_type=F32)


def _bf16_pieces(x, pieces):
    ps = []
    rem = x
    for i in range(pieces):
        p = rem.astype(BF16)
        ps.append(p)
        if i + 1 < pieces:
            rem = rem - p.astype(F32)
    return ps


def _seg_sum(x, gpair, pieces=2):
    return jnp.concatenate(
        [_dot_sel(x[:, i:i + PAIR], gpair, pieces) for i in range(0, x.shape[1], PAIR)], axis=1)


def _sigmoid(x):
    return jax.nn.sigmoid(x)


def _softplus(x):
    return jnp.maximum(x, 0.0) + jnp.log1p(jnp.exp(-jnp.abs(x)))


def _iota(shape, dim):
    return lax.broadcasted_iota(jnp.int32, shape, dim)


def _stack_heads(x):
    is_a = _iota((1, PAIR), 1) < HEAD
    return jnp.concatenate([jnp.where(is_a, x, 0.0), jnp.where(is_a, 0.0, x)], axis=0)


def _fold_heads(x, c):
    return x[0:c] + x[c:2 * c]


def _pair_blockdiag_mask():
    return (_iota((PAIR, 1), 0) >> 6) == (_iota((1, PAIR), 1) >> 6)


def _load_pair_state(s_ref, bi, pr):
    sa = s_ref[bi, 2 * pr]
    sb = s_ref[bi, 2 * pr + 1]
    z = jnp.zeros((HEAD, HEAD), F32)
    return jnp.concatenate(
        [jnp.concatenate([sa, z], axis=1), jnp.concatenate([z, sb], axis=1)], axis=0)


def _store_pair_state(s_ref, bi, pr, s):
    s_ref[bi, 2 * pr] = s[0:HEAD, 0:HEAD]
    s_ref[bi, 2 * pr + 1] = s[HEAD:PAIR, HEAD:PAIR]


def _dot_left01(m, x, pieces=3):
    return lax.dot_general(jnp.concatenate([m] * pieces, axis=1),
                           jnp.concatenate(_bf16_pieces(x, pieces), axis=0), _NN,
                           preferred_element_type=F32)


def _tri_inverse_many(ns, c):
    size = ns[0].shape[0]
    r = _iota((size, 1), 0)
    cc = _iota((1, size), 1)
    eye = (r == cc).astype(F32)
    sh = INV_BASE.bit_length() - 1
    diag = [jnp.where((r >> sh) == (cc >> sh), n, 0.0) for n in ns]
    invs = [eye + d for d in diag]
    pws = [_dot(d, d) for d in diag]
    span = 4
    while span < INV_BASE:
        both = [_dot(jnp.concatenate([inv, pw], axis=0), pw) for inv, pw in zip(invs, pws)]
        invs = [inv + b[0:size] for inv, b in zip(invs, both)]
        pws = [b[size:2 * size] for b in both]
        span *= 2
    invs = [inv + _dot(inv, pw) for inv, pw in zip(invs, pws)]
    blk = INV_BASE
    while blk < c:
        sh = blk.bit_length() - 1
        off = ((r >> (sh + 1)) == (cc >> (sh + 1))) & ((r >> sh) != (cc >> sh))
        low = [_dot(inv, jnp.where(off, n, 0.0)) for inv, n in zip(invs, ns)]
        invs = [inv + _dot(t, inv) for inv, t in zip(invs, low)]
        blk *= 2
    return invs


def _ada_kernel(c_ref, w_ref, b_ref, o_ref):
    c = c_ref[...]
    o_ref[0] = _dot(c * _sigmoid(c), w_ref[0]) + b_ref[0]


def _ada_call(c_all, w_ada, b_ada):
    nl, d, n6 = w_ada.shape
    rows = c_all.shape[0]
    tn = n6 // 4
    return pl.pallas_call(
        _ada_kernel,
        grid=(nl, n6 // tn),
        in_specs=[
            pl.BlockSpec((rows, d), lambda l, j: (0, 0)),
            pl.BlockSpec((1, d, tn), lambda l, j: (l, 0, j)),
            pl.BlockSpec((1, 1, tn), lambda l, j: (l, 0, j)),
        ],
        out_specs=pl.BlockSpec((1, rows, tn), lambda l, j: (l, 0, j)),
        out_shape=jax.ShapeDtypeStruct((nl, rows, n6), F32),
        compiler_params=pltpu.CompilerParams(
            dimension_semantics=("parallel", "parallel"), vmem_limit_bytes=VMEM_LIMIT),
        name="ada",
    )(c_all, w_ada, b_ada.reshape(nl, 1, n6))


def _norm_mod(x, nw, scale, shift):
    y = x * lax.rsqrt(jnp.mean(x * x, axis=-1, keepdims=True) + RMS_EPS) * nw
    return y * (1.0 + scale) + shift


def _inproj_kernel(x_ref, nw_ref, sc_ref, sh_ref, wa_ref, wb_ref, wc_ref, pa_ref, pb_ref, pc_ref):
    h = _norm_mod(x_ref[...], nw_ref[...], sc_ref[0], sh_ref[0]).astype(BF16)
    pa_ref[...] = lax.dot_general(h, wa_ref[...], _NT, preferred_element_type=F32)
    pb_ref[...] = lax.dot_general(h, wb_ref[...], _NT, preferred_element_type=F32)
    pc_ref[...] = lax.dot_general(h, wc_ref[...], _NT, preferred_element_type=F32)


def _mod_spec(mod, tm, tiles_per_seq):
    if mod.shape[1] == 1:
        return pl.BlockSpec((1, 1, mod.shape[2]), lambda i, *_: (i // tiles_per_seq, 0, 0))
    return pl.BlockSpec((1, tm, mod.shape[2]), lambda i, *_: (i, 0, 0))


def _inproj_call(x, nw, scale, shift, wa, wb, wc, tm, tiles_per_seq):
    rows, d = x.shape
    full = lambda a: pl.BlockSpec(a.shape, lambda i: (0, 0), pipeline_mode=pl.Buffered(1))
    outs = [jax.ShapeDtypeStruct((rows, w.shape[0]), F32) for w in (wa, wb, wc)]
    return pl.pallas_call(
        _inproj_kernel,
        grid=(rows // tm,),
        in_specs=[
            pl.BlockSpec((tm, d), lambda i: (i, 0)),
            full(nw),
            _mod_spec(scale, tm, tiles_per_seq),
            _mod_spec(shift, tm, tiles_per_seq),
            full(wa), full(wb), full(wc),
        ],
        out_specs=[pl.BlockSpec((tm, w.shape[0]), lambda i: (i, 0)) for w in (wa, wb, wc)],
        out_shape=outs,
        compiler_params=pltpu.CompilerParams(
            dimension_semantics=("parallel",), vmem_limit_bytes=VMEM_LIMIT),
        name="inproj",
    )(x, nw, scale, shift, wa, wb, wc)


def _lastrow_kernel(x_ref, nw_ref, sc_ref, sh_ref, hprev_ref, wa_ref, h_ref, pprev_ref):
    h_ref[...] = _norm_mod(x_ref[...], nw_ref[...], sc_ref[...], sh_ref[...])
    pprev_ref[...] = lax.dot_general(hprev_ref[...].astype(BF16), wa_ref[...], _NT,
                                     preferred_element_type=F32)


def _lastrow_call(x_last, nw, scale, shift, h_prev, wa):
    b, d = x_last.shape
    return pl.pallas_call(
        _lastrow_kernel,
        out_shape=[jax.ShapeDtypeStruct((b, d), F32), jax.ShapeDtypeStruct((b, wa.shape[0]), F32)],
        compiler_params=pltpu.CompilerParams(vmem_limit_bytes=VMEM_LIMIT),
        name="lastrow",
    )(x_last, nw, scale, shift, h_prev, wa)


def _rwkv_kernel(p_ref, pprev_ref, s0_ref, mu_ref, vec_ref, wl_ref, g_ref,
                 y_ref, sout_ref, s_scr, prev_scr, *, bt):
    c = CHUNK
    ci = pl.program_id(1)
    nci = pl.num_programs(1)
    aw = 3 * PAIR
    rows = _iota((c, 1), 0)
    lane = _iota((1, PAIR), 1)
    vec = vec_ref[...]
    w0, a0, k_k, k_a, r_k, ln_w, ln_b = (vec[i:i + 1] for i in range(7))
    gpair = g_ref[...]
    tri = (rows >= _iota((1, c), 1)).astype(BF16)
    r2 = _iota((2 * c, 1), 0) & (c - 1)
    c2 = _iota((1, 2 * c), 1) & (c - 1)
    bd = _pair_blockdiag_mask()
    strict = bd & (r2 > c2)
    incl = bd & (r2 >= c2)

    @pl.when(ci == 0)
    def _():
        for bi in range(bt):
            for pr in range(3):
                s_scr[bi, pr] = _load_pair_state(s0_ref, bi, pr)
            prev_scr[bi] = pprev_ref[bi]

    shifted = []
    for bi in range(bt):
        p = p_ref[bi]
        prev = pltpu.roll(p, 1, axis=0)
        prev = jnp.where(rows == 0, prev_scr[bi], prev)
        prev_scr[bi] = p[c - 1:c]
        shifted.append(p + mu_ref[...] * (prev - p))
    xs = jnp.concatenate(shifted, axis=0)
    r = xs[:, 0:aw]
    k = xs[:, aw:2 * aw]
    v = xs[:, 2 * aw:3 * aw]
    tail = xs[:, 3 * aw:3 * aw + PAIR]
    act = jnp.where(lane < 32, jnp.tanh(tail), jnp.where(lane < 64, tail, _sigmoid(tail)))
    lo = _dot(act, wl_ref[...])
    w = -_softplus(-(w0 + lo[:, 0:aw])) - 0.5
    ld = -jnp.exp(w)
    a = _sigmoid(a0 + lo[:, aw:2 * aw])
    g = lo[:, 2 * aw:3 * aw]
    kk = k * k_k
    kk = kk * lax.rsqrt(_seg_sum(kk * kk, gpair) + L2_EPS)
    k = k * (1.0 + (a - 1.0) * k_a)
    cums = [_dot_left01(tri, ld[c * bi:c * (bi + 1)]) for bi in range(bt)]
    cum = jnp.concatenate(cums, axis=0)
    cum_last = jnp.concatenate([jnp.broadcast_to(x[c - 1:c], x.shape) for x in cums], axis=0)
    e_neg = jnp.exp(-cum)
    e_end = jnp.exp(cum_last - cum)
    pre = dict(a_t=-kk * jnp.exp(cum - ld), b_t=kk * a * e_neg, k_t=k * e_neg, r_t=r * jnp.exp(cum), v=v,
               b_end=kk * a * e_end, k_end=k * e_end)

    chains = [(bi, pr) for bi in range(bt) for pr in range(3)]
    pair = lambda name: [pre[name][c * bi:c * (bi + 1), PAIR * pr:PAIR * (pr + 1)] for bi, pr in chains]
    a_t, b_t, k_t, r_t, v_p = pair("a_t"), pair("b_t"), pair("k_t"), pair("r_t"), pair("v")
    b_end, k_end = pair("b_end"), pair("k_end")
    d_end = [jnp.exp(cums[bi][c - 1:c, PAIR * pr:PAIR * (pr + 1)]) for bi, pr in chains]
    v_st = [_stack_heads(x) for x in v_p]
    gs = [_dot(jnp.concatenate([_stack_heads(a), _stack_heads(r)], axis=0),
               jnp.concatenate([b, b, k, k], axis=0), _NT)
          for a, r, b, k in zip(a_t, r_t, b_t, k_t)]
    a_ak = [jnp.where(strict, g[0:2 * c, 2 * c:4 * c], 0.0) for g in gs]
    p_rbk = [jnp.concatenate([jnp.where(incl, g[2 * c:4 * c, 0:2 * c], 0.0),
                              jnp.where(incl, g[2 * c:4 * c, 2 * c:4 * c], 0.0)], axis=1) for g in gs]
    tinv = _tri_inverse_many([jnp.where(strict, g[0:2 * c, 0:2 * c], 0.0) for g in gs], c)
    s = [s_scr[bi, pr] for bi, pr in chains]
    ars = [_dot(jnp.concatenate([a, r], axis=0), s_, _NT) for a, r, s_ in zip(a_t, r_t, s)]
    akv = [_dot(m, x) for m, x in zip(a_ak, v_st)]
    u_st = [_dot(t, _stack_heads(x[0:c]) + y) for t, x, y in zip(tinv, ars, akv)]
    y_st = [_dot(p, jnp.concatenate([u, x], axis=0)) for p, u, x in zip(p_rbk, u_st, v_st)]
    ds = [_dot(jnp.concatenate([_fold_heads(u, c), x], axis=0), jnp.concatenate([b_, k_], axis=0), _TN)
          for u, x, b_, k_ in zip(u_st, v_p, b_end, k_end)]
    for i, (bi, pr) in enumerate(chains):
        s_scr[bi, pr] = s[i] * d_end[i] + jnp.where(bd, ds[i], 0.0)

    y = jnp.concatenate(
        [jnp.concatenate([ars[3 * bi + pr][c:2 * c] + _fold_heads(y_st[3 * bi + pr], c) for pr in range(3)],
                         axis=1) for bi in range(bt)], axis=0)
    mean = _seg_sum(y, gpair) * (1.0 / HEAD)
    dy = y - mean
    var = _seg_sum(dy * dy, gpair) * (1.0 / HEAD)
    yn = dy * lax.rsqrt(var + GN_EPS) * ln_w + ln_b
    bonus = _seg_sum(r * k * r_k, gpair) * v
    y_ref[...] = ((yn + bonus) * g).reshape(bt, c, aw)

    @pl.when(ci == nci - 1)
    def _():
        for bi in range(bt):
            for pr in range(3):
                _store_pair_state(sout_ref, bi, pr, s_scr[bi, pr])


def _rwkv_call(p_a, p_prev, s0, mu, vec, wl, gmat, *, bt):
    b, tp, acols = p_a.shape
    nh = s0.shape[1]
    aw = nh * HEAD
    cin = CHUNK
    nc = tp // cin
    const = lambda a: pl.BlockSpec(a.shape, lambda i, j: (0,) * a.ndim)
    kern = functools.partial(_rwkv_kernel, bt=bt)
    return pl.pallas_call(
        kern,
        grid=(b // bt, nc),
        in_specs=[
            pl.BlockSpec((bt, cin, acols), lambda i, j: (i, j, 0)),
            pl.BlockSpec((bt, 1, acols), lambda i, j: (i, 0, 0)),
            pl.BlockSpec((bt, nh, HEAD, HEAD), lambda i, j: (i, 0, 0, 0)),
            const(mu), const(vec), const(wl), const(gmat),
        ],
        out_specs=[
            pl.BlockSpec((bt, cin, aw), lambda i, j: (i, j, 0)),
            pl.BlockSpec((bt, nh, HEAD, HEAD), lambda i, j: (i, 0, 0, 0)),
        ],
        out_shape=[jax.ShapeDtypeStruct((b, tp, aw), F32),
                   jax.ShapeDtypeStruct(s0.shape, F32)],
        scratch_shapes=[pltpu.VMEM((bt, nh // 2, PAIR, PAIR), F32),
                        pltpu.VMEM((bt, 1, acols), F32)],
        compiler_params=pltpu.CompilerParams(
            dimension_semantics=("parallel", "arbitrary"), vmem_limit_bytes=VMEM_LIMIT),
        name="rwkv",
    )(p_a, p_prev.reshape(b, 1, acols), s0, mu, vec, wl, gmat)


def _gdn_kernel(p_ref, conv0_ref, s0_ref, cw_ref, tp_ref, nw_ref, g_ref, bsel_ref, gsel_ref,
                y_ref, sout_ref, s_scr, cbuf, *, bt):
    c = CHUNK
    ci = pl.program_id(1)
    nci = pl.num_programs(1)
    bw = 3 * PAIR
    qkvw = 3 * bw
    rows = _iota((c, 1), 0)
    lane = _iota((1, PAIR), 1)
    nh = 2 * 3
    gpair = g_ref[...]
    tri = (rows >= _iota((1, c), 1)).astype(BF16)
    r2 = _iota((2 * c, 1), 0) & (c - 1)
    c2 = _iota((1, 2 * c), 1) & (c - 1)
    same =(_iota((2 * c, 1), 0) >> 6) == (_iota((1, 2 * c), 1) >> 6)
    strict = same & (r2 > c2)
    incl = same & (r2 >= c2)
    bd = _pair_blockdiag_mask()
    is_a = lane < HEAD
    a_log = tp_ref[0:1]
    dt_bias = tp_ref[1:2]
    is_beta = lane < nh
    is_g = (lane >= nh) & (lane < 2 * nh)

    @pl.when(ci == 0)
    def _():
        for bi in range(bt):
            for pr in range(3):
                s_scr[bi, pr] = _load_pair_state(s0_ref, bi, pr)
            cbuf[bi, 0:8] = jnp.zeros((8, qkvw), F32)
            cbuf[bi, 8 - (CONV_K - 1):8] = conv0_ref[bi]

    convs = []
    for bi in range(bt):
        cbuf[bi, 8:8 + c] = p_ref[bi, :, 0:qkvw]
        conv = None
        for j in range(CONV_K):
            t = cw_ref[j:j + 1] * cbuf[bi, pl.ds(8 - (CONV_K - 1) + j, c), :]
            conv = t if conv is None else conv + t
        cbuf[bi, 0:8] = cbuf[bi, c:c + 8]
        convs.append(conv)
    conv = jnp.concatenate(convs, axis=0)
    p = p_ref[...].reshape(bt * c, p_ref.shape[2])
    qkv = conv * _sigmoid(conv)
    q = qkv[:, 0:bw]
    k = qkv[:, bw:2 * bw]
    v = qkv[:, 2 * bw:3 * bw]
    z = p[:, qkvw:qkvw + bw]
    tail = p[:, qkvw + bw:qkvw + bw + PAIR]
    q = q * (lax.rsqrt(_seg_sum(q * q, gpair) + L2_EPS) * (HEAD ** -0.5))
    k = k * lax.rsqrt(_seg_sum(k * k, gpair) + L2_EPS)
    beta = jnp.where(is_beta, _sigmoid(tail), 0.0)
    gl = jnp.where(is_g, -jnp.exp(a_log) * _softplus(tail + dt_bias), 0.0)
    gcum128 = jnp.concatenate([_dot_left01(tri, gl[c * bi:c * (bi + 1)]) for bi in range(bt)], axis=0)
    gcum = _dot_sel(gcum128, gsel_ref[...])
    beta = _dot_sel(beta, bsel_ref[...])
    g_last = jnp.concatenate(
        [jnp.broadcast_to(gcum[c * (bi + 1) - 1:c * (bi + 1)], (c, bw)) for bi in range(bt)], axis=0)
    eg = jnp.exp(gcum)
    kb = k * beta
    pre = dict(q=q, k=k, kb=kb, gcum=gcum, kg=kb * eg, qg=q * eg, kd=k * jnp.exp(g_last - gcum),
               vb=v * beta, d_end=jnp.exp(g_last))

    chains = [(bi, pr) for bi in range(bt) for pr in range(3)]
    pair = lambda name: [pre[name][c * bi:c * (bi + 1), PAIR * pr:PAIR * (pr + 1)] for bi, pr in chains]
    q, k, kb, gcum, kg, qg = pair("q"), pair("k"), pair("kb"), pair("gcum"), pair("kg"), pair("qg")
    kd, vb = pair("kd"), pair("vb")
    d_end = [x[0:1] for x in pair("d_end")]

    def col(x):
        sw = pltpu.roll(x, HEAD, axis=1)
        return jnp.concatenate([jnp.where(is_a, x, sw), jnp.where(is_a, sw, x)], axis=0)

    gcol = [col(x) for x in gcum]
    diff = [x - x.T for x in gcol]
    dec = [jnp.exp(jnp.where(incl, x, 0.0)) for x in diff]
    dec_s = [jnp.where(strict, x, 0.0) for x in dec]
    dec_i = [jnp.where(incl, x, 0.0) for x in dec]
    sc = [_dot(jnp.concatenate([_stack_heads(x), _stack_heads(y)], axis=0),
               jnp.concatenate([w, w], axis=0), _NT)
          for x, y, w in zip(kb, q, k)]
    tinv = _tri_inverse_many([-(x[0:2 * c] * d) for x, d in zip(sc, dec_s)], c)
    qk = [x[2 * c:4 * c] * d for x, d in zip(sc, dec_i)]
    s = [s_scr[bi, pr] for bi, pr in chains]
    kqs = [_dot(jnp.concatenate([x, y], axis=0), s_) for x, y, s_ in zip(kg, qg, s)]
    v_new = [_dot(t, _stack_heads(x - y[0:c])) for t, x, y in zip(tinv, vb, kqs)]
    o_st = [_dot(x, y) for x, y in zip(qk, v_new)]
    ds = [_dot(x, _fold_heads(y, c), _TN) for x, y in zip(kd, v_new)]
    for i, (bi, pr) in enumerate(chains):
        s_scr[bi, pr] = s[i] * d_end[i] + jnp.where(bd, ds[i], 0.0)

    o = jnp.concatenate(
        [jnp.concatenate([kqs[3 * bi + pr][c:2 * c] + _fold_heads(o_st[3 * bi + pr], c) for pr in range(3)],
                         axis=1) for bi in range(bt)], axis=0)
    o = o * lax.rsqrt(_seg_sum(o * o, gpair) * (1.0 / HEAD) + RMS_EPS) * nw_ref[...]
    y_ref[...] = (o * (z * _sigmoid(z))).reshape(bt, c, bw)

    @pl.when(ci == nci - 1)
    def _():
        for bi in range(bt):
            for pr in range(3):
                _store_pair_state(sout_ref, bi, pr, s_scr[bi, pr])


def _gdn_call(p_b, conv0, s0, cw, tailp, nw, gmat, bsel, gsel, *, bt):
    b, tp, bcols = p_b.shape
    nh = s0.shape[1]
    bw = nh * HEAD
    cin = CHUNK
    nc = tp // cin
    const = lambda a: pl.BlockSpec(a.shape, lambda i, j: (0,) * a.ndim)
    kern = functools.partial(_gdn_kernel, bt=bt)
    return pl.pallas_call(
        kern,
        grid=(b // bt, nc),
        in_specs=[
            pl.BlockSpec((bt, cin, bcols), lambda i, j: (i, j, 0)),
            pl.BlockSpec((bt, CONV_K - 1, 3 * bw), lambda i, j: (i, 0, 0)),
            pl.BlockSpec((bt, nh, HEAD, HEAD), lambda i, j: (i, 0, 0, 0)),
            const(cw), const(tailp), const(nw), const(gmat), const(bsel), const(gsel),
        ],
        out_specs=[
            pl.BlockSpec((bt, cin, bw), lambda i, j: (i, j, 0)),
            pl.BlockSpec((bt, nh, HEAD, HEAD), lambda i, j: (i, 0, 0, 0)),
        ],
        out_shape=[jax.ShapeDtypeStruct((b, tp, bw), F32),
                   jax.ShapeDtypeStruct(s0.shape, F32)],
        scratch_shapes=[pltpu.VMEM((bt, nh // 2, PAIR, PAIR), F32),
                        pltpu.VMEM((bt, CHUNK + 8, 3 * bw), F32)],
        compiler_params=pltpu.CompilerParams(
            dimension_semantics=("parallel", "arbitrary"), vmem_limit_bytes=VMEM_LIMIT),
        name="gdn",
    )(p_b, conv0, s0, cw, tailp, nw, gmat, bsel, gsel)


def _hgrn_kernel(p_ref, s0_ref, lbl_ref, nw_ref, g_ref, y_ref, sout_ref, s_scr,
                 *, bt, layer):
    c = CHUNK
    ci = pl.program_id(1)
    nci = pl.num_programs(1)
    cw = 2 * PAIR
    rows = _iota((c, 1), 0)
    cols = _iota((1, c), 1)
    sameblk = (rows >> 4) == (cols >> 4)
    tri_tot = jnp.concatenate([sameblk & (rows >= cols), sameblk], axis=0).astype(BF16)
    rows_b = _iota((HBLK, 1), 0)
    bd = _pair_blockdiag_mask()
    gpair = g_ref[...]

    logits = lbl_ref[...]
    ex = jnp.exp(logits - jnp.max(logits, axis=0, keepdims=True))
    gam = ex / jnp.sum(ex, axis=0, keepdims=True)
    lb = jnp.sum(gam[0:layer + 1], axis=0, keepdims=True) - gam[0:1]

    @pl.when(ci == 0)
    def _():
        for bi in range(bt):
            for pr in range(2):
                s_scr[bi, pr] = _load_pair_state(s0_ref, bi, pr).T

    pre = []
    for bi in range(bt):
        p = p_ref[bi]
        qp = p[:, 0:cw]
        f = p[:, cw:2 * cw]
        v = p[:, 2 * cw:3 * cw]
        z = p[:, 3 * cw:4 * cw]
        q = qp * _sigmoid(qp)
        logf = jnp.log(lb + (1.0 - lb) * _sigmoid(f))
        k = (1.0 - lb) * _sigmoid(-f)
        cums = _dot_left01(tri_tot, logf)
        bl = cums[0:c]
        btot = cums[c:2 * c]
        pre.append(dict(q=q, k=k, v=v, z=z, bl=bl, q_in=q * jnp.exp(bl),
                        k_out=k * jnp.exp(btot - bl), d_blk=jnp.exp(btot)))

    chains = [(bi, pr) for bi in range(bt) for pr in range(2)]
    nblk = c // HBLK

    def blk_of(name, bi, pr, blk):
        return pre[bi][name][HBLK * blk:HBLK * (blk + 1), PAIR * pr:PAIR * (pr + 1)]

    intra = {}
    ds = {}
    for blk in range(nblk):
        for bi, pr in chains:
            qb, kb, vb, bb = (blk_of(n, bi, pr, blk) for n in ("q", "k", "v", "bl"))
            xs = []
            for j in range(HBLK):
                e = jnp.exp(jnp.minimum(bb - bb[j:j + 1], 0.0))
                xs.append(qb * e * kb[j:j + 1])
            att = _dot_sel(jnp.concatenate(xs, axis=0), gpair, pieces=1)
            acc = None
            for j in range(HBLK):
                t = jnp.where(rows_b >= j, att[HBLK * j:HBLK * (j + 1)], 0.0) * vb[j:j + 1]
                acc = t if acc is None else acc + t
            intra[bi, pr, blk] = acc
            ds[bi, pr, blk] = jnp.where(bd, _dot(vb, blk_of("k_out", bi, pr, blk), _TN), 0.0)

    s = {ch: s_scr[ch] for ch in chains}
    outs = {}
    for blk in range(nblk):
        for bi, pr in chains:
            outs[bi, pr, blk] = intra[bi, pr, blk] + _dot(blk_of("q_in", bi, pr, blk), s[bi, pr], _NT)
            s[bi, pr] = s[bi, pr] * blk_of("d_blk", bi, pr, blk)[0:1] + ds[bi, pr, blk]
    for ch in chains:
        s_scr[ch] = s[ch]

    o = jnp.concatenate(
        [jnp.concatenate([outs[bi, pr, blk] for bi in range(bt) for blk in range(nblk)], axis=0)
         for pr in range(2)], axis=1)
    z = jnp.concatenate([pre[bi]["z"] for bi in range(bt)], axis=0)
    o = o * lax.rsqrt(_seg_sum(o * o, gpair) * (1.0 / HEAD) + RMS_EPS) * nw_ref[...]
    y_ref[...] = (o * _sigmoid(z)).reshape(bt, c, cw)

    @pl.when(ci == nci - 1)
    def _():
        for bi in range(bt):
            for pr in range(2):
                _store_pair_state(sout_ref, bi, pr, s_scr[bi, pr].T)


def _hgrn_call(p_c, s0, lbl, nw, gmat, *, bt, layer):
    b, tp, ccols = p_c.shape
    nh = s0.shape[1]
    cw = nh * HEAD
    cin = CHUNK
    nc = tp // cin
    const = lambda a: pl.BlockSpec(a.shape, lambda i, j: (0,) * a.ndim)
    kern = functools.partial(_hgrn_kernel, bt=bt, layer=layer)
    return pl.pallas_call(
        kern,
        grid=(b // bt, nc),
        in_specs=[
            pl.BlockSpec((bt, cin, ccols), lambda i, j: (i, j, 0)),
            pl.BlockSpec((bt, nh, HEAD, HEAD), lambda i, j: (i, 0, 0, 0)),
            const(lbl), const(nw), const(gmat),
        ],
        out_specs=[
            pl.BlockSpec((bt, cin, cw), lambda i, j: (i, j, 0)),
            pl.BlockSpec((bt, nh, HEAD, HEAD), lambda i, j: (i, 0, 0, 0)),
        ],
        out_shape=[jax.ShapeDtypeStruct((b, tp, cw), F32),
                   jax.ShapeDtypeStruct(s0.shape, F32)],
        scratch_shapes=[pltpu.VMEM((bt, nh // 2, PAIR, PAIR), F32)],
        compiler_params=pltpu.CompilerParams(
            dimension_semantics=("parallel", "arbitrary"), vmem_limit_bytes=VMEM_LIMIT),
        name="hgrn",
    )(p_c, s0, lbl, nw, gmat)


def _inproj_t_kernel(x_ref, nw_ref, sc_ref, sh_ref, hprev_ref, wta_ref, wtb_ref, wtc_ref,
                     pta_ref, ptb_ref, ptc_ref, rows_ref, hlast_ref, pprev_ref):
    h = _norm_mod(x_ref[...], nw_ref[...], sc_ref[...], sh_ref[...])
    hb = h.astype(BF16)
    nt = lambda w, a: lax.dot_general(w, a, _NT, preferred_element_type=F32)
    pta_ref[...] = nt(wta_ref[...], hb)
    ptb_ref[...] = nt(wtb_ref[...], hb)
    ptc_ref[...] = nt(wtc_ref[...], hb)
    rows_ref[...] = nt(hb, wtb_ref[0:rows_ref.shape[1], :])
    nb = hlast_ref.shape[0]
    hlast_ref[...] = h[h.shape[0] - nb:]
    pprev_ref[...] = nt(wta_ref[...], hprev_ref[...].astype(BF16))


def _inproj_t_call(x, nw, scale, shift, h_prev, wta, wtb, wtc, qkvw):
    rows, d = x.shape
    nb = h_prev.shape[0]
    sds = jax.ShapeDtypeStruct
    return pl.pallas_call(
        _inproj_t_kernel,
        out_shape=[sds((wta.shape[0], rows), F32), sds((wtb.shape[0], rows), F32),
                   sds((wtc.shape[0], rows), F32), sds((rows, qkvw), F32),
                   sds((nb, d), F32), sds((wta.shape[0], nb), F32)],
        compiler_params=pltpu.CompilerParams(vmem_limit_bytes=VMEM_LIMIT),
        name="inproj_t",
    )(x, nw, scale, shift, h_prev, wta, wtb, wtc)


def _col_sum(x):
    return jnp.sum(x, axis=0, keepdims=True)


def _rwkv_t_kernel(r_ref, k_ref, v_ref, tail_ref, rp_ref, kp_ref, vp_ref, tailp_ref,
                   mur_ref, muk_ref, muv_ref, mut_ref, vec_ref, wl_ref, s0_ref,
                   y_ref, sout_ref, v_scr, y_scr):
    nb = s0_ref.shape[-1]
    nt = r_ref.shape[-1] // nb
    w0, a0, k_k, k_a, r_k, ln_w, ln_b = (vec_ref[i, 0] for i in range(7))
    sout_ref[...] = s0_ref[...]

    def shifted(ref, pref, mu, idx, t):
        cur = ref[idx, :, t * nb:(t + 1) * nb]
        prv = pref[idx] if t == 0 else ref[idx, :, (t - 1) * nb:t * nb]
        return cur + mu * (prv - cur)

    for t in range(nt):
        r = shifted(r_ref, rp_ref, mur_ref[0], 0, t)
        k = shifted(k_ref, kp_ref, muk_ref[0], 0, t)
        v = shifted(v_ref, vp_ref, muv_ref[0], 0, t)
        t0 = shifted(tail_ref, tailp_ref, mut_ref[0], 0, t)
        t1 = shifted(tail_ref, tailp_ref, mut_ref[1], 1, t)
        act = jnp.concatenate([jnp.tanh(t0[0:32]), t0[32:64], _sigmoid(t1)], axis=0)
        lo = jnp.dot(wl_ref[0], act.astype(BF16), preferred_element_type=F32)
        w = -_softplus(-(w0 + lo[0:HEAD])) - 0.5
        dec = jnp.exp(-jnp.exp(w))
        a = _sigmoid(a0 + lo[HEAD:2 * HEAD])
        g = lo[2 * HEAD:3 * HEAD]
        kk = k * k_k
        kk = kk * lax.rsqrt(_col_sum(kk * kk) + L2_EPS)
        k = k * (1.0 + (a - 1.0) * k_a)
        a_t = -kk
        b_t = kk * a
        v_scr[...] = v

        def body(g, carry):
            base = pl.multiple_of(g * 8, 8)
            vg = v_scr[pl.ds(base, 8), :]
            ys = []
            for j in range(8):
                sv = sout_ref[0, base + j]
                sa = _col_sum(sv * a_t)
                sv = sv * dec + sa * b_t + vg[j:j + 1] * k
                sout_ref[0, base + j] = sv
                ys.append(_col_sum(sv * r))
            y_scr[pl.ds(base, 8), :] = jnp.concatenate(ys, axis=0)
            return carry

        lax.fori_loop(0, HEAD // 8, body, 0)
        y = y_scr[...]
        mean = _col_sum(y) * (1.0 / HEAD)
        dy = y - mean
        var = _col_sum(dy * dy) * (1.0 / HEAD)
        yn = dy * lax.rsqrt(var + GN_EPS) * ln_w + ln_b
        bonus = _col_sum(r * k * r_k) * v
        y_ref[0, :, t * nb:(t + 1) * nb] = (yn + bonus) * g


def _head_blocks(arr3, first, n=1):
    if n == 1:
        return pl.BlockSpec((1,) + arr3.shape[1:], lambda h: (first + h, 0, 0))
    return pl.BlockSpec((n,) + arr3.shape[1:], lambda h: (first // n, 0, 0))


def _rwkv_t_call(pta, pprev_t, s0, mu, vec, wl_t):
    nh, _, _, nb = s0.shape
    rows = pta.shape[1]
    p3 = pta.reshape(-1, HEAD, rows)
    pp3 = pprev_t.reshape(-1, HEAD, nb)
    mu3 = mu.reshape(-1, HEAD, 1)
    tail0 = 3 * nh
    state = pl.BlockSpec((1, HEAD, HEAD, nb), lambda h: (h, 0, 0, 0))
    return pl.pallas_call(
        _rwkv_t_kernel,
        grid=(nh,),
        in_specs=[
            _head_blocks(p3, 0), _head_blocks(p3, nh), _head_blocks(p3, 2 * nh), _head_blocks(p3, tail0, 2),
            _head_blocks(pp3, 0), _head_blocks(pp3, nh), _head_blocks(pp3, 2 * nh), _head_blocks(pp3, tail0, 2),
            _head_blocks(mu3, 0), _head_blocks(mu3, nh), _head_blocks(mu3, 2 * nh), _head_blocks(mu3, tail0, 2),
            pl.BlockSpec((vec.shape[0], 1, HEAD, 1), lambda h: (0, h, 0, 0)),
            pl.BlockSpec((1,) + wl_t.shape[1:], lambda h: (h, 0, 0)),
            state,
        ],
        out_specs=[pl.BlockSpec((1, HEAD, rows), lambda h: (h, 0, 0)), state],
        out_shape=[jax.ShapeDtypeStruct((nh, HEAD, rows), F32), jax.ShapeDtypeStruct(s0.shape, F32)],
        scratch_shapes=[pltpu.VMEM((HEAD, nb), F32), pltpu.VMEM((HEAD, nb), F32)],
        compiler_params=pltpu.CompilerParams(
            dimension_semantics=("parallel",), vmem_limit_bytes=VMEM_LIMIT),
        name="rwkv_t",
    )(p3, p3, p3, p3, pp3, pp3, pp3, pp3, mu3, mu3, mu3, mu3, vec, wl_t, s0)


def _gdn_t_kernel(q_ref, k_ref, v_ref, z_ref, tail_ref, cq_ref, ck_ref, cv_ref, wq_ref, wk_ref, wv_ref,
                  sc_ref, nw_ref, s0_ref, y_ref, sout_ref, k_scr, q_scr):
    h = pl.program_id(0)
    nh = pl.num_programs(0)
    nb = s0_ref.shape[-1]
    nt = q_ref.shape[-1] // nb
    a_log = sc_ref[0, 0:1, 0:1]
    dt_bias = sc_ref[0, 1:2, 0:1]
    sout_ref[...] = s0_ref[...]

    def conv(x_ref, c_ref, w_ref, t):
        acc = None
        for j in range(CONV_K):
            u = t + j - (CONV_K - 1)
            tap = c_ref[u + CONV_K - 1, 0] if u < 0 else x_ref[0, :, u * nb:(u + 1) * nb]
            term = w_ref[j, 0] * tap
            acc = term if acc is None else acc + term
        return acc * _sigmoid(acc)

    for t in range(nt):
        sl = slice(t * nb, (t + 1) * nb)
        q = conv(q_ref, cq_ref, wq_ref, t)
        k = conv(k_ref, ck_ref, wk_ref, t)
        v = conv(v_ref, cv_ref, wv_ref, t)
        q = q * (lax.rsqrt(_col_sum(q * q) + L2_EPS) * (HEAD ** -0.5))
        k = k * lax.rsqrt(_col_sum(k * k) + L2_EPS)
        tl = tail_ref[0:2 * 8, sl]
        pick = lambda row: _col_sum(jnp.where(_iota((2 * 8, 1), 0) == row, tl, 0.0))
        beta = _sigmoid(pick(h))
        eg = jnp.exp(-jnp.exp(a_log) * _softplus(pick(nh + h) + dt_bias))
        k_scr[...] = k
        q_scr[...] = q

        def body1(g, acc):
            base = pl.multiple_of(g * 8, 8)
            kg = k_scr[pl.ds(base, 8), :]
            for j in range(8):
                acc = acc + sout_ref[0, base + j] * kg[j:j + 1]
            return acc

        sk = lax.fori_loop(0, HEAD // 8, body1, jnp.zeros((HEAD, nb), F32))
        v_new = beta * (v - eg * sk)

        def body2(g, acc):
            base = pl.multiple_of(g * 8, 8)
            kg = k_scr[pl.ds(base, 8), :]
            qg = q_scr[pl.ds(base, 8), :]
            for j in range(8):
                s = eg * sout_ref[0, base + j] + kg[j:j + 1] * v_new
                sout_ref[0, base + j] = s
                acc = acc + s * qg[j:j + 1]
            return acc

        o = lax.fori_loop(0, HEAD // 8, body2, jnp.zeros((HEAD, nb), F32))
        o = o * lax.rsqrt(_col_sum(o * o) * (1.0 / HEAD) + RMS_EPS) * nw_ref[...]
        z = z_ref[0, :, sl]
        y_ref[0, :, sl] = o * (z * _sigmoid(z))


def _gdn_t_call(ptb, conv_t, s0, cw, scal, nw):
    nh, _, _, nb = s0.shape
    rows = ptb.shape[1]
    p3 = ptb.reshape(-1, HEAD, rows)
    c4 = conv_t.reshape(CONV_K - 1, -1, HEAD, nb)
    w4 = cw.reshape(CONV_K, -1, HEAD, 1)
    tap = lambda first: pl.BlockSpec((CONV_K - 1, 1, HEAD, nb), lambda h: (0, first + h, 0, 0))
    wsp = lambda first: pl.BlockSpec((CONV_K, 1, HEAD, 1), lambda h: (0, first + h, 0, 0))
    state = pl.BlockSpec((1, HEAD, HEAD, nb), lambda h: (h, 0, 0, 0))
    tail_blk = 4 * nh * HEAD // PAIR
    return pl.pallas_call(
        _gdn_t_kernel,
        grid=(nh,),
        in_specs=[
            _head_blocks(p3, 0), _head_blocks(p3, nh), _head_blocks(p3, 2 * nh), _head_blocks(p3, 3 * nh),
            pl.BlockSpec((PAIR, rows), lambda h: (tail_blk, 0)),
            tap(0), tap(nh), tap(2 * nh), wsp(0), wsp(nh), wsp(2 * nh),
            pl.BlockSpec((1,) + scal.shape[1:], lambda h: (h, 0, 0)),
            pl.BlockSpec(nw.shape, lambda h: (0, 0)),
            state,
        ],
        out_specs=[pl.BlockSpec((1, HEAD, rows), lambda h: (h, 0, 0)), state],
        out_shape=[jax.ShapeDtypeStruct((nh, HEAD, rows), F32), jax.ShapeDtypeStruct(s0.shape, F32)],
        scratch_shapes=[pltpu.VMEM((HEAD, nb), F32), pltpu.VMEM((HEAD, nb), F32)],
        compiler_params=pltpu.CompilerParams(
            dimension_semantics=("parallel",), vmem_limit_bytes=VMEM_LIMIT),
        name="gdn_t",
    )(p3, p3, p3, p3, ptb, c4, c4, c4, w4, w4, w4, scal, nw, s0)


def _hgrn_t_kernel(q_ref, f_ref, v_ref, z_ref, lbl_ref, nw_ref, s0_ref, y_ref, sout_ref,
                   f_scr, k_scr, q_scr, *, layer):
    nb = s0_ref.shape[-1]
    nt = q_ref.shape[-1] // nb
    logits = lbl_ref[:, 0]
    ex = jnp.exp(logits - jnp.max(logits, axis=0, keepdims=True))
    gam = ex / jnp.sum(ex, axis=0, keepdims=True)
    lb = jnp.sum(gam[0:layer + 1], axis=0) - gam[0]
    sout_ref[...] = s0_ref[...]

    for t in range(nt):
        sl = slice(t * nb, (t + 1) * nb)
        qp = q_ref[0, :, sl]
        f = f_ref[0, :, sl]
        v = v_ref[0, :, sl]
        z = z_ref[0, :, sl]
        q_scr[...] = qp * _sigmoid(qp)
        f_scr[...] = lb + (1.0 - lb) * _sigmoid(f)
        k_scr[...] = (1.0 - lb) * _sigmoid(-f)

        def body(g, acc):
            base = pl.multiple_of(g * 8, 8)
            fg, kg, qg = (ref[pl.ds(base, 8), :] for ref in (f_scr, k_scr, q_scr))
            for j in range(8):
                s = fg[j:j + 1] * sout_ref[0, base + j] + kg[j:j + 1] * v
                sout_ref[0, base + j] = s
                acc = acc + qg[j:j + 1] * s
            return acc

        o = lax.fori_loop(0, HEAD // 8, body, jnp.zeros((HEAD, nb), F32))
        o = o * lax.rsqrt(_col_sum(o * o) * (1.0 / HEAD) + RMS_EPS) * nw_ref[...]
        y_ref[0, :, sl] = o * _sigmoid(z)


def _hgrn_t_call(ptc, s0, lbl4, nw, layer):
    nh, _, _, nb = s0.shape
    rows = ptc.shape[1]
    p3 = ptc.reshape(-1, HEAD, rows)
    state = pl.BlockSpec((1, HEAD, HEAD, nb), lambda h: (h, 0, 0, 0))
    return pl.pallas_call(
        functools.partial(_hgrn_t_kernel, layer=layer),
        grid=(nh,),
        in_specs=[
            _head_blocks(p3, 0), _head_blocks(p3, nh), _head_blocks(p3, 2 * nh), _head_blocks(p3, 3 * nh),
            pl.BlockSpec((lbl4.shape[0], 1, HEAD, 1), lambda h: (0, h, 0, 0)),
            pl.BlockSpec(nw.shape, lambda h: (0, 0)),
            state,
        ],
        out_specs=[pl.BlockSpec((1, HEAD, rows), lambda h: (h, 0, 0)), state],
        out_shape=[jax.ShapeDtypeStruct((nh, HEAD, rows), F32), jax.ShapeDtypeStruct(s0.shape, F32)],
        scratch_shapes=[pltpu.VMEM((HEAD, nb), F32)] * 3,
        compiler_params=pltpu.CompilerParams(
            dimension_semantics=("parallel",), vmem_limit_bytes=VMEM_LIMIT),
        name="hgrn_t",
    )(p3, p3, p3, p3, lbl4, nw, s0)


def _ffn_kernel(x_ref, ya_ref, yb_ref, yc_ref, woa_ref, wob_ref, woc_ref, gm_ref, nw_ref,
                sc_ref, sh_ref, gf_ref, win_ref, wd_ref, fnw_ref, xo_ref, *yo_ref, th, y_transposed):
    ff = wd_ref.shape[0]
    dims = _TN if y_transposed else _NN
    proj = lambda y_ref, w_ref: lax.dot_general(y_ref[...].astype(BF16), w_ref[...], dims,
                                                preferred_element_type=F32)
    mix = proj(ya_ref, woa_ref) + proj(yb_ref, wob_ref) + proj(yc_ref, woc_ref)
    x1 = x_ref[...] + gm_ref[0] * mix
    h = _norm_mod(x1, nw_ref[...], sc_ref[0], sh_ref[0]).astype(BF16)
    acc = None
    for j in range(ff // th):
        gate = jnp.dot(h, win_ref[:, j * th:(j + 1) * th], preferred_element_type=F32)
        up = jnp.dot(h, win_ref[:, ff + j * th:ff + (j + 1) * th], preferred_element_type=F32)
        act = (gate * _sigmoid(gate) * up).astype(BF16)
        t = jnp.dot(act, wd_ref[j * th:(j + 1) * th, :], preferred_element_type=F32)
        acc = t if acc is None else acc + t
    xo = x1 + gf_ref[0] * acc
    xo_ref[...] = xo
    if yo_ref:
        yo_ref[0][...] = xo * lax.rsqrt(jnp.mean(xo * xo, axis=-1, keepdims=True) + RMS_EPS) * fnw_ref[...]


def _ffn_call(x, ya, yb, yc, woa, wob, woc, gate_m, nw, scale_f, shift_f, gate_f,
              w_in, w_down, fnw, *, tm, th, tiles_per_seq, final, y_transposed=False):
    rows, d = x.shape
    row = lambda a: pl.BlockSpec((tm, a.shape[1]), lambda i: (i, 0))
    yspec = (lambda a: pl.BlockSpec((a.shape[0], tm), lambda i: (0, i))) if y_transposed else row
    const = lambda a: pl.BlockSpec(a.shape, lambda i: (0, 0), pipeline_mode=pl.Buffered(1))
    mod = lambda a: _mod_spec(a, tm, tiles_per_seq)
    n_out = 2 if final else 1
    return pl.pallas_call(
        functools.partial(_ffn_kernel, th=th, y_transposed=y_transposed),
        grid=(rows // tm,),
        in_specs=[
            row(x), yspec(ya), yspec(yb), yspec(yc), const(woa), const(wob), const(woc),
            mod(gate_m), const(nw), mod(scale_f), mod(shift_f), mod(gate_f),
            const(w_in), const(w_down), const(fnw),
        ],
        out_specs=[row(x)] * n_out,
        out_shape=[jax.ShapeDtypeStruct((rows, d), F32)] * n_out,
        compiler_params=pltpu.CompilerParams(
            dimension_semantics=("parallel",), vmem_limit_bytes=VMEM_LIMIT),
        name="ffn",
    )(x, ya, yb, yc, woa, wob, woc, gate_m, nw, scale_f, shift_f, gate_f, w_in, w_down, fnw)


def _block_ones(n, blk):
    i = np.arange(n) // blk
    return jnp.asarray(i[:, None] == i[None, :], BF16)


def _head_select(first_lane, nh):
    m = np.zeros((PAIR, nh * HEAD), np.float32)
    for h in range(nh):
        m[first_lane + h, h * HEAD:(h + 1) * HEAD] = 1.0
    return jnp.asarray(m, BF16)


def _lane_place(x, first_lane):
    return jnp.zeros((PAIR,), F32).at[first_lane:first_lane + x.shape[0]].set(x)


def _run_chunked(x, mods, states, wts, *, tm, tm_ffn, th, bt):
    b, t, d = x.shape
    rows = b * t
    assert t % CHUNK == 0 and t % tm == 0 and t % tm_ffn == 0 and b % bt == 0 and t >= CONV_K - 1
    shift0, rwkv0, conv0, gdn0, hgrn0 = states
    nl = len(wts)
    new = ([], [], [], [], [])
    xr = x.reshape(rows, d)
    for l in range(nl):
        w = wts[l]
        m = mods[l]
        shift_m, scale_m, gate_m, shift_f, scale_f, gate_f = (m[:, i].reshape(b, 1, d) for i in range(6))
        p_a, p_b, p_c = _inproj_call(xr, w["norm_mix"], scale_m, shift_m, w["wta"], w["wtb"], w["wtc"],
                                     tm, t // tm)
        h_last, p_prev = _lastrow_call(xr.reshape(b, t, d)[:, t - 1], w["norm_mix"], m[:, 1], m[:, 0],
                                       shift0[l], w["wta"])
        p_a = p_a.reshape(b, t, -1)
        p_b = p_b.reshape(b, t, -1)
        p_c = p_c.reshape(b, t, -1)
        y_a, s_a = _rwkv_call(p_a, p_prev, rwkv0[l], w["mu"], w["rwkv_vec"], w["lora"], w["gpair"], bt=bt)
        y_b, s_b = _gdn_call(p_b, conv0[l], gdn0[l], w["conv_w"], w["gdn_tail"], w["gdn_norm"],
                             w["gpair"], w["bsel"], w["gsel"], bt=bt)
        y_c, s_c = _hgrn_call(p_c, hgrn0[l], w["lb_logits"], w["hgrn_norm"], w["gpair"], bt=bt, layer=l)
        conv_new = p_b[:, t - (CONV_K - 1):, :conv0.shape[-1]]
        res = _ffn_call(xr, y_a.reshape(rows, -1), y_b.reshape(rows, -1), y_c.reshape(rows, -1),
                        w["woa"], w["wob"], w["woc"], gate_m, w["norm_ffn"], scale_f, shift_f, gate_f,
                        w["w_ffn_in"], w["w_ffn_out"], w["final_norm"],
                        tm=tm_ffn, th=th, tiles_per_seq=t // tm_ffn, final=l == nl - 1)
        xr = res[0]
        for acc, s in zip(new, (h_last, s_a, conv_new, s_b, s_c)):
            acc.append(s)
    return res[1].reshape(b, t, d), [jnp.stack(acc) for acc in new]


def _run_steps(x, mods, states, wts, *, tm_ffn, th):
    b, t, d = x.shape
    rows = t * b
    tm_ffn = min(tm_ffn, rows)
    shift0, rwkv0, conv0, gdn0, hgrn0 = states
    to_lanes = lambda s: jnp.transpose(s, (0, 2, 3, 4, 1))
    rwkv_t, gdn_t, hgrn_t = to_lanes(rwkv0), to_lanes(gdn0), to_lanes(hgrn0)
    conv_t = jnp.transpose(conv0, (0, 2, 3, 1))
    nl = len(wts)
    new = ([], [], [], [], [])
    xr = jnp.transpose(x, (1, 0, 2)).reshape(rows, d)
    for l in range(nl):
        w = wts[l]
        m = mods[l]
        mod = lambda i, m=m: jnp.tile(m[:, i], (t, 1))
        tiled = lambda a: a.reshape(rows // tm_ffn, tm_ffn, d)
        pta, ptb, ptc, qkv_rows, h_last, pprev_t = _inproj_t_call(
            xr, w["norm_mix"], mod(1), mod(0), shift0[l], w["wta"], w["wtb"], w["wtc"], w["qkvw"])
        y_a, s_a = _rwkv_t_call(pta, pprev_t, rwkv_t[l], w["mu"], w["rwkv_vec_t"], w["lora_t"])
        y_b, s_b = _gdn_t_call(ptb, conv_t[l], gdn_t[l], w["conv_w_t"], w["gdn_scal"], w["gdn_norm_t"])
        y_c, s_c = _hgrn_t_call(ptc, hgrn_t[l], w["lb_logits_t"], w["hgrn_norm_t"], l)
        xp = jnp.concatenate([jnp.transpose(conv0[l], (1, 0, 2)), qkv_rows.reshape(t, b, -1)], axis=0)
        conv_new = jnp.transpose(xp[-(CONV_K - 1):], (1, 0, 2))
        res = _ffn_call(xr, y_a.reshape(-1, rows), y_b.reshape(-1, rows), y_c.reshape(-1, rows),
                        w["woa"], w["wob"], w["woc"], tiled(mod(2)), w["norm_ffn"],
                        tiled(mod(4)), tiled(mod(3)), tiled(mod(5)),
                        w["w_ffn_in"], w["w_ffn_out"], w["final_norm"],
                        tm=tm_ffn, th=th, tiles_per_seq=1, final=l == nl - 1, y_transposed=True)
        xr = res[0]
        for acc, s in zip(new, (h_last, s_a, conv_new, s_b, s_c)):
            acc.append(s)
    y = jnp.transpose(res[1].reshape(t, b, d), (1, 0, 2))
    from_lanes = lambda s: jnp.transpose(jnp.stack(s), (0, 4, 1, 2, 3))
    return y, [jnp.stack(new[0]), from_lanes(new[1]), jnp.stack(new[2]), from_lanes(new[3]), from_lanes(new[4])]


def kernel(x_prompt, x_sample, c_prompt, c_sample, state_rwkv_shift, state_rwkv, state_gdn_conv, state_gdn, state_hgrn, w_ada, b_ada, norm_mix_w, w_in, rwkv_mu, rwkv_w0, rwkv_w2, rwkv_a0, rwkv_a2, rwkv_g2, rwkv_k_k, rwkv_k_a, rwkv_r_k, rwkv_ln_w, rwkv_ln_b, gdn_conv_w, gdn_A_log, gdn_dt_bias, gdn_norm_w, hgrn_lb_logits, hgrn_norm_w, w_out, norm_ffn_w, w_ffn_in, w_ffn_out, final_norm_w):
    nl, d, _ = w_in.shape
    a_heads = state_rwkv.shape[2]
    b_heads = state_gdn.shape[2]
    c_heads = state_hgrn.shape[2]
    aw, bw, cw = a_heads * HEAD, b_heads * HEAD, c_heads * HEAD
    lw, la, lg = rwkv_w2.shape[1], rwkv_a2.shape[1], rwkv_g2.shape[1]
    a_cols = 3 * aw + lw + la + lg
    qkvw = 3 * bw
    b_cols = qkvw + 2 * b_heads + bw
    assert (aw, bw, cw) == (3 * PAIR, 3 * PAIR, 2 * PAIR) and lw + la + lg == PAIR and lw == 32 and la == 32

    bp, tpr, _ = x_prompt.shape
    bs, ts, _ = x_sample.shape

    gpair = _block_ones(PAIR, HEAD)
    bsel = _head_select(0, b_heads)
    gsel = _head_select(b_heads, b_heads)
    wts = []
    for l in range(nl):
        wit = jnp.swapaxes(w_in[l], 0, 1).astype(BF16)
        wbt = wit[a_cols:a_cols + b_cols]
        wtb = jnp.concatenate(
            [wbt[:qkvw], wbt[qkvw + 2 * b_heads:], wbt[qkvw:qkvw + 2 * b_heads],
             jnp.zeros((PAIR - 2 * b_heads, d), BF16)], axis=0)
        lora = jnp.zeros((PAIR, 3 * aw), F32)
        lora = lora.at[0:lw, 0:aw].set(rwkv_w2[l])
        lora = lora.at[lw:lw + la, aw:2 * aw].set(rwkv_a2[l])
        lora = lora.at[lw + la:, 2 * aw:].set(rwkv_g2[l])
        rwkv_vec = jnp.stack([rwkv_w0[l], rwkv_a0[l], rwkv_k_k[l], rwkv_k_a[l], rwkv_r_k[l].reshape(-1),
                              rwkv_ln_w[l], rwkv_ln_b[l], jnp.zeros((aw,), F32)])
        wo = w_out[l].astype(BF16)
        hcols = lambda a: a.reshape(a.shape[0], a_heads, HEAD).transpose(1, 2, 0)
        lora_t = jnp.concatenate([
            jnp.pad(hcols(rwkv_w2[l]), ((0, 0), (0, 0), (0, la + lg))),
            jnp.pad(hcols(rwkv_a2[l]), ((0, 0), (0, 0), (lw, lg))),
            jnp.pad(hcols(rwkv_g2[l]), ((0, 0), (0, 0), (lw + la, 0)))], axis=1)
        steps = dict(
            rwkv_vec_t=rwkv_vec[:7].reshape(7, a_heads, HEAD, 1), lora_t=lora_t.astype(BF16),
            conv_w_t=gdn_conv_w[l],
            gdn_scal=jnp.zeros((b_heads, 8, PAIR), F32)
                .at[:, 0].set(jnp.broadcast_to(gdn_A_log[l][:, None], (b_heads, PAIR)))
                .at[:, 1].set(jnp.broadcast_to(gdn_dt_bias[l][:, None], (b_heads, PAIR))),
            gdn_norm_t=gdn_norm_w[l].reshape(HEAD, 1),
            lb_logits_t=hgrn_lb_logits.reshape(nl, c_heads, HEAD, 1),
            hgrn_norm_t=hgrn_norm_w[l].reshape(HEAD, 1),
        )
        wts.append(dict(
            **steps,
            norm_mix=norm_mix_w[l].reshape(1, d),
            wta=wit[:a_cols], wtb=wtb, wtc=wit[a_cols + b_cols:], qkvw=qkvw,
            mu=rwkv_mu[l].reshape(1, a_cols), rwkv_vec=rwkv_vec, lora=lora.astype(BF16),
            gpair=gpair, bsel=bsel, gsel=gsel,
            conv_w=jnp.concatenate([gdn_conv_w[l], jnp.zeros((8 - CONV_K, qkvw), F32)], axis=0),
            gdn_tail=jnp.zeros((8, PAIR), F32).at[0].set(_lane_place(gdn_A_log[l], b_heads))
                                              .at[1].set(_lane_place(gdn_dt_bias[l], b_heads)),
            gdn_norm=jnp.tile(gdn_norm_w[l], b_heads).reshape(1, bw),
            lb_logits=hgrn_lb_logits,
            hgrn_norm=jnp.tile(hgrn_norm_w[l], c_heads).reshape(1, cw),
            woa=wo[:aw], wob=wo[aw:aw + bw], woc=wo[aw + bw:],
            norm_ffn=norm_ffn_w[l].reshape(1, d),
            w_ffn_in=w_ffn_in[l].astype(BF16), w_ffn_out=w_ffn_out[l].astype(BF16),
            final_norm=final_norm_w.reshape(1, d),
        ))

    mod = _ada_call(jnp.concatenate([c_prompt, c_sample], axis=0), w_ada, b_ada)
    mod = mod.reshape(nl, bp + bs, 6, d)
    mods_p = [mod[l, :bp] for l in range(nl)]
    mods_s = [mod[l, bp:] for l in range(nl)]

    zeros = lambda s: jnp.zeros((nl, bp) + s.shape[2:], s.dtype)
    states_p = tuple(zeros(s) for s in (state_rwkv_shift, state_rwkv, state_gdn_conv, state_gdn, state_hgrn))
    states_s = (state_rwkv_shift, state_rwkv, state_gdn_conv, state_gdn, state_hgrn)

    th = w_ffn_out.shape[1] // FFN_HIDDEN_TILES
    y_p, new_p = _run_chunked(x_prompt, mods_p, states_p, wts, tm=TM_INPROJ, tm_ffn=TM_FFN, th=th, bt=SEQS_PER_STEP)
    y_s, new_s = _run_steps(x_sample, mods_s, states_s, wts, tm_ffn=TM_FFN_STEPS, th=th)
    return (y_p, y_s, *new_p, *new_s)
```

```python
import functools

import numpy as np
import jax
import jax.numpy as jnp
from jax import lax
from jax.experimental import pallas as pl
from jax.experimental.pallas import tpu as pltpu

F32 = jnp.float32
BF16 = jnp.bfloat16

HEAD = 64
PAIR = 2 * HEAD
CHUNK = 64
HBLK = 16
INV_BASE = 16
CONV_K = 4
RMS_EPS = 1e-6
L2_EPS = 1e-6
GN_EPS = 64e-5
VMEM_LIMIT = 56 * 1024 * 1024

TM_INPROJ = 512
TM_FFN = 512
TM_FFN_STEPS = 256
FFN_HIDDEN_TILES = 2
SEQS_PER_STEP = 8

_NN = (((1,), (0,)), ((), ()))
_NT = (((1,), (1,)), ((), ()))
_TN = (((0,), (0,)), ((), ()))


def _dot(a, b, dims=_NN):
    return lax.dot_general(a.astype(BF16), b.astype(BF16), dims, preferred_element_type=F32)


def _dot_sel(x, m, pieces=3):
    ps = _bf16_pieces(x, pieces)
    return lax.dot_general(jnp.concatenate(ps, axis=1), jnp.concatenate([m] * pieces, axis=0), _NN,
                           preferred_element_type=F32)


def _bf16_pieces(x, pieces):
    ps = []
    rem = x
    for i in range(pieces):
        p = rem.astype(BF16)
        ps.append(p)
        if i + 1 < pieces:
            rem = rem - p.astype(F32)
    return ps


def _seg_sum(x, gpair, pieces=2):
    return jnp.concatenate(
        [_dot_sel(x[:, i:i + PAIR], gpair, pieces) for i in range(0, x.shape[1], PAIR)], axis=1)


def _sigmoid(x):
    return jax.nn.sigmoid(x)


def _softplus(x):
    return jnp.maximum(x, 0.0) + jnp.log1p(jnp.exp(-jnp.abs(x)))


def _iota(shape, dim):
    return lax.broadcasted_iota(jnp.int32, shape, dim)


def _stack_heads(x):
    is_a = _iota((1, PAIR), 1) < HEAD
    return jnp.concatenate([jnp.where(is_a, x, 0.0), jnp.where(is_a, 0.0, x)], axis=0)


def _fold_heads(x, c):
    return x[0:c] + x[c:2 * c]


def _pair_blockdiag_mask():
    return (_iota((PAIR, 1), 0) >> 6) == (_iota((1, PAIR), 1) >> 6)


def _load_pair_state(s_ref, bi, pr):
    sa = s_ref[bi, 2 * pr]
    sb = s_ref[bi, 2 * pr + 1]
    z = jnp.zeros((HEAD, HEAD), F32)
    return jnp.concatenate(
        [jnp.concatenate([sa, z], axis=1), jnp.concatenate([z, sb], axis=1)], axis=0)


def _store_pair_state(s_ref, bi, pr, s):
    s_ref[bi, 2 * pr] = s[0:HEAD, 0:HEAD]
    s_ref[bi, 2 * pr + 1] = s[HEAD:PAIR, HEAD:PAIR]


def _dot_left01(m, x, pieces=3):
    return lax.dot_general(jnp.concatenate([m] * pieces, axis=1),
                           jnp.concatenate(_bf16_pieces(x, pieces), axis=0), _NN,
                           preferred_element_type=F32)


def _tri_inverse_many(ns, c):
    size = ns[0].shape[0]
    r = _iota((size, 1), 0)
    cc = _iota((1, size), 1)
    eye = (r == cc).astype(F32)
    sh = INV_BASE.bit_length() - 1
    diag = [jnp.where((r >> sh) == (cc >> sh), n, 0.0) for n in ns]
    invs = [eye + d for d in diag]
    pws = [_dot(d, d) for d in diag]
    span = 4
    while span < INV_BASE:
        both = [_dot(jnp.concatenate([inv, pw], axis=0), pw) for inv, pw in zip(invs, pws)]
        invs = [inv + b[0:size] for inv, b in zip(invs, both)]
        pws = [b[size:2 * size] for b in both]
        span *= 2
    invs = [inv + _dot(inv, pw) for inv, pw in zip(invs, pws)]
    blk = INV_BASE
    while blk < c:
        sh = blk.bit_length() - 1
        off = ((r >> (sh + 1)) == (cc >> (sh + 1))) & ((r >> sh) != (cc >> sh))
        low = [_dot(inv, jnp.where(off, n, 0.0)) for inv, n in zip(invs, ns)]
        invs = [inv + _dot(t, inv) for inv, t in zip(invs, low)]
        blk *= 2
    return invs


def _ada_kernel(c_ref, w_ref, b_ref, o_ref):
    c = c_ref[...]
    o_ref[0] = _dot(c * _sigmoid(c), w_ref[0]) + b_ref[0]


def _ada_call(c_all, w_ada, b_ada):
    nl, d, n6 = w_ada.shape
    rows = c_all.shape[0]
    tn = n6 // 4
    return pl.pallas_call(
        _ada_kernel,
        grid=(nl, n6 // tn),
        in_specs=[
            pl.BlockSpec((rows, d), lambda l, j: (0, 0)),
            pl.BlockSpec((1, d, tn), lambda l, j: (l, 0, j)),
            pl.BlockSpec((1, 1, tn), lambda l, j: (l, 0, j)),
        ],
        out_specs=pl.BlockSpec((1, rows, tn), lambda l, j: (l, 0, j)),
        out_shape=jax.ShapeDtypeStruct((nl, rows, n6), F32),
        compiler_params=pltpu.CompilerParams(
            dimension_semantics=("parallel", "parallel"), vmem_limit_bytes=VMEM_LIMIT),
        name="ada",
    )(c_all, w_ada, b_ada.reshape(nl, 1, n6))


def _norm_mod(x, nw, scale, shift):
    y = x * lax.rsqrt(jnp.mean(x * x, axis=-1, keepdims=True) + RMS_EPS) * nw
    return y * (1.0 + scale) + shift


def _inproj_kernel(x_ref, nw_ref, sc_ref, sh_ref, w_ref, pa_ref, pb_ref, pc_ref, *, splits):
    h = _norm_mod(x_ref[...], nw_ref[...], sc_ref[0], sh_ref[0]).astype(BF16)
    s0, s1 = splits
    pa_ref[...] = lax.dot_general(h, w_ref[0:s0, :], _NT, preferred_element_type=F32)
    pb_ref[...] = lax.dot_general(h, w_ref[s0:s1, :], _NT, preferred_element_type=F32)
    pc_ref[...] = lax.dot_general(h, w_ref[s1:, :], _NT, preferred_element_type=F32)


def _mod_spec(mod, tm, tiles_per_seq):
    if mod.shape[1] == 1:
        return pl.BlockSpec((1, 1, mod.shape[2]), lambda i, *_: (i // tiles_per_seq, 0, 0))
    return pl.BlockSpec((1, tm, mod.shape[2]), lambda i, *_: (i, 0, 0))


def _split_widths(wt, splits):
    return splits[0], splits[1] - splits[0], wt.shape[0] - splits[1]


def _inproj_call(x, nw, scale, shift, wt, splits, tm, tiles_per_seq):
    rows, d = x.shape
    full = lambda a: pl.BlockSpec(a.shape, lambda i: (0, 0), pipeline_mode=pl.Buffered(1))
    widths = _split_widths(wt, splits)
    return pl.pallas_call(
        functools.partial(_inproj_kernel, splits=splits),
        grid=(rows // tm,),
        in_specs=[
            pl.BlockSpec((tm, d), lambda i: (i, 0)),
            full(nw),
            _mod_spec(scale, tm, tiles_per_seq),
            _mod_spec(shift, tm, tiles_per_seq),
            full(wt),
        ],
        out_specs=[pl.BlockSpec((tm, n), lambda i: (i, 0)) for n in widths],
        out_shape=[jax.ShapeDtypeStruct((rows, n), F32) for n in widths],
        compiler_params=pltpu.CompilerParams(
            dimension_semantics=("parallel",), vmem_limit_bytes=VMEM_LIMIT),
        name="inproj",
    )(x, nw, scale, shift, wt)


def _lastrow_kernel(x_ref, nw_ref, sc_ref, sh_ref, hprev_ref, w_ref, h_ref, pprev_ref):
    h_ref[...] = _norm_mod(x_ref[...], nw_ref[...], sc_ref[...], sh_ref[...])
    pprev_ref[...] = lax.dot_general(hprev_ref[...].astype(BF16), w_ref[...], _NT,
                                     preferred_element_type=F32)


def _lastrow_call(x_last, nw, scale, shift, h_prev, wt, a_cols):
    b, d = x_last.shape
    full = lambda a: pl.BlockSpec(a.shape, lambda i: (0, 0))
    return pl.pallas_call(
        _lastrow_kernel,
        grid=(1,),
        in_specs=[full(x_last), full(nw), full(scale), full(shift), full(h_prev),
                  pl.BlockSpec((a_cols, d), lambda i: (0, 0))],
        out_specs=[pl.BlockSpec((b, d), lambda i: (0, 0)), pl.BlockSpec((b, a_cols), lambda i: (0, 0))],
        out_shape=[jax.ShapeDtypeStruct((b, d), F32), jax.ShapeDtypeStruct((b, a_cols), F32)],
        compiler_params=pltpu.CompilerParams(vmem_limit_bytes=VMEM_LIMIT),
        name="lastrow",
    )(x_last, nw, scale, shift, h_prev, wt)


def _rwkv_kernel(p_ref, pprev_ref, s0_ref, mu_ref, vec_ref, wl_ref, g_ref,
                 y_ref, sout_ref, s_scr, prev_scr, *, bt):
    c = CHUNK
    ci = pl.program_id(1)
    nci = pl.num_programs(1)
    aw = 3 * PAIR
    rows = _iota((c, 1), 0)
    lane = _iota((1, PAIR), 1)
    vec = vec_ref[...]
    w0, a0, k_k, k_a, r_k, ln_w, ln_b = (vec[i:i + 1] for i in range(7))
    gpair = g_ref[...]
    tri = (rows >= _iota((1, c), 1)).astype(BF16)
    r2 = _iota((2 * c, 1), 0) & (c - 1)
    c2 = _iota((1, 2 * c), 1) & (c - 1)
    bd = _pair_blockdiag_mask()
    strict = bd & (r2 > c2)
    incl = bd & (r2 >= c2)

    @pl.when(ci == 0)
    def _():
        for bi in range(bt):
            for pr in range(3):
                s_scr[bi, pr] = _load_pair_state(s0_ref, bi, pr)
            prev_scr[bi] = pprev_ref[bi]

    shifted = []
    for bi in range(bt):
        p = p_ref[bi]
        prev = pltpu.roll(p, 1, axis=0)
        prev = jnp.where(rows == 0, prev_scr[bi], prev)
        prev_scr[bi] = p[c - 1:c]
        shifted.append(p + mu_ref[...] * (prev - p))
    xs = jnp.concatenate(shifted, axis=0)
    r = xs[:, 0:aw]
    k = xs[:, aw:2 * aw]
    v = xs[:, 2 * aw:3 * aw]
    tail = xs[:, 3 * aw:3 * aw + PAIR]
    act = jnp.where(lane < 32, jnp.tanh(tail), jnp.where(lane < 64, tail, _sigmoid(tail)))
    lo = _dot(act, wl_ref[...])
    w = -_softplus(-(w0 + lo[:, 0:aw])) - 0.5
    ld = -jnp.exp(w)
    a = _sigmoid(a0 + lo[:, aw:2 * aw])
    g = lo[:, 2 * aw:3 * aw]
    kk = k * k_k
    kk = kk * lax.rsqrt(_seg_sum(kk * kk, gpair) + L2_EPS)
    k = k * (1.0 + (a - 1.0) * k_a)
    cums = [_dot_left01(tri, ld[c * bi:c * (bi + 1)]) for bi in range(bt)]
    cum = jnp.concatenate(cums, axis=0)
    cum_last = jnp.concatenate([jnp.broadcast_to(x[c - 1:c], x.shape) for x in cums], axis=0)
    e_neg = jnp.exp(-cum)
    e_end = jnp.exp(cum_last - cum)
    pre = dict(a_t=-kk * jnp.exp(cum - ld), b_t=kk * a * e_neg, k_t=k * e_neg, r_t=r * jnp.exp(cum), v=v,
               b_end=kk * a * e_end, k_end=k * e_end)

    chains = [(bi, pr) for bi in range(bt) for pr in range(3)]
    pair = lambda name: [pre[name][c * bi:c * (bi + 1), PAIR * pr:PAIR * (pr + 1)] for bi, pr in chains]
    a_t, b_t, k_t, r_t, v_p = pair("a_t"), pair("b_t"), pair("k_t"), pair("r_t"), pair("v")
    b_end, k_end = pair("b_end"), pair("k_end")
    d_end = [jnp.exp(cums[bi][c - 1:c, PAIR * pr:PAIR * (pr + 1)]) for bi, pr in chains]
    v_st = [_stack_heads(x) for x in v_p]
    gs = [_dot(jnp.concatenate([_stack_heads(a), _stack_heads(r)], axis=0),
               jnp.concatenate([b, b, k, k], axis=0), _NT)
          for a, r, b, k in zip(a_t, r_t, b_t, k_t)]
    a_ak = [jnp.where(strict, g[0:2 * c, 2 * c:4 * c], 0.0) for g in gs]
    p_rbk = [jnp.concatenate([jnp.where(incl, g[2 * c:4 * c, 0:2 * c], 0.0),
                              jnp.where(incl, g[2 * c:4 * c, 2 * c:4 * c], 0.0)], axis=1) for g in gs]
    tinv = _tri_inverse_many([jnp.where(strict, g[0:2 * c, 0:2 * c], 0.0) for g in gs], c)
    s = [s_scr[bi, pr] for bi, pr in chains]
    ars = [_dot(jnp.concatenate([a, r], axis=0), s_, _NT) for a, r, s_ in zip(a_t, r_t, s)]
    akv = [_dot(m, x) for m, x in zip(a_ak, v_st)]
    u_st = [_dot(t, _stack_heads(x[0:c]) + y) for t, x, y in zip(tinv, ars, akv)]
    y_st = [_dot(p, jnp.concatenate([u, x], axis=0)) for p, u, x in zip(p_rbk, u_st, v_st)]
    ds = [_dot(jnp.concatenate([_fold_heads(u, c), x], axis=0), jnp.concatenate([b_, k_], axis=0), _TN)
          for u, x, b_, k_ in zip(u_st, v_p, b_end, k_end)]
    for i, (bi, pr) in enumerate(chains):
        s_scr[bi, pr] = s[i] * d_end[i] + jnp.where(bd, ds[i], 0.0)

    y = jnp.concatenate(
        [jnp.concatenate([ars[3 * bi + pr][c:2 * c] + _fold_heads(y_st[3 * bi + pr], c) for pr in range(3)],
                         axis=1) for bi in range(bt)], axis=0)
    mean = _seg_sum(y, gpair) * (1.0 / HEAD)
    dy = y - mean
    var = _seg_sum(dy * dy, gpair) * (1.0 / HEAD)
    yn = dy * lax.rsqrt(var + GN_EPS) * ln_w + ln_b
    bonus = _seg_sum(r * k * r_k, gpair) * v
    y_ref[...] = ((yn + bonus) * g).reshape(bt, c, aw)

    @pl.when(ci == nci - 1)
    def _():
        for bi in range(bt):
            for pr in range(3):
                _store_pair_state(sout_ref, bi, pr, s_scr[bi, pr])


def _rwkv_call(p_a, p_prev, s0, mu, vec, wl, gmat, *, bt):
    b, tp, acols = p_a.shape
    nh = s0.shape[1]
    aw = nh * HEAD
    cin = CHUNK
    nc = tp // cin
    const = lambda a: pl.BlockSpec(a.shape, lambda i, j: (0,) * a.ndim)
    kern = functools.partial(_rwkv_kernel, bt=bt)
    return pl.pallas_call(
        kern,
        grid=(b // bt, nc),
        in_specs=[
            pl.BlockSpec((bt, cin, acols), lambda i, j: (i, j, 0)),
            pl.BlockSpec((bt, 1, acols), lambda i, j: (i, 0, 0)),
            pl.BlockSpec((bt, nh, HEAD, HEAD), lambda i, j: (i, 0, 0, 0)),
            const(mu), const(vec), const(wl), const(gmat),
        ],
        out_specs=[
            pl.BlockSpec((bt, cin, aw), lambda i, j: (i, j, 0)),
            pl.BlockSpec((bt, nh, HEAD, HEAD), lambda i, j: (i, 0, 0, 0)),
        ],
        out_shape=[jax.ShapeDtypeStruct((b, tp, aw), F32),
                   jax.ShapeDtypeStruct(s0.shape, F32)],
        scratch_shapes=[pltpu.VMEM((bt, nh // 2, PAIR, PAIR), F32),
                        pltpu.VMEM((bt, 1, acols), F32)],
        compiler_params=pltpu.CompilerParams(
            dimension_semantics=("parallel", "arbitrary"), vmem_limit_bytes=VMEM_LIMIT),
        name="rwkv",
    )(p_a, p_prev.reshape(b, 1, acols), s0, mu, vec, wl, gmat)


def _gdn_kernel(p_ref, conv0_ref, s0_ref, cw_ref, tp_ref, nw_ref, g_ref, bsel_ref, gsel_ref,
                y_ref, sout_ref, s_scr, cbuf, *, bt):
    c = CHUNK
    ci = pl.program_id(1)
    nci = pl.num_programs(1)
    bw = 3 * PAIR
    qkvw = 3 * bw
    rows = _iota((c, 1), 0)
    lane = _iota((1, PAIR), 1)
    nh = 2 * 3
    gpair = g_ref[...]
    tri = (rows >= _iota((1, c), 1)).astype(BF16)
    r2 = _iota((2 * c, 1), 0) & (c - 1)
    c2 = _iota((1, 2 * c), 1) & (c - 1)
    same =(_iota((2 * c, 1), 0) >> 6) == (_iota((1, 2 * c), 1) >> 6)
    strict = same & (r2 > c2)
    incl = same & (r2 >= c2)
    bd = _pair_blockdiag_mask()
    is_a = lane < HEAD
    a_log = tp_ref[0:1]
    dt_bias = tp_ref[1:2]
    is_beta = lane < nh
    is_g = (lane >= nh) & (lane < 2 * nh)

    @pl.when(ci == 0)
    def _():
        for bi in range(bt):
            for pr in range(3):
                s_scr[bi, pr] = _load_pair_state(s0_ref, bi, pr)
            cbuf[bi, 0:8] = jnp.zeros((8, qkvw), F32)
            cbuf[bi, 8 - (CONV_K - 1):8] = conv0_ref[bi]

    convs = []
    for bi in range(bt):
        cbuf[bi, 8:8 + c] = p_ref[bi, :, 0:qkvw]
        conv = None
        for j in range(CONV_K):
            t = cw_ref[j:j + 1] * cbuf[bi, pl.ds(8 - (CONV_K - 1) + j, c), :]
            conv = t if conv is None else conv + t
        cbuf[bi, 0:8] = cbuf[bi, c:c + 8]
        convs.append(conv)
    conv = jnp.concatenate(convs, axis=0)
    p = p_ref[...].reshape(bt * c, p_ref.shape[2])
    qkv = conv * _sigmoid(conv)
    q = qkv[:, 0:bw]
    k = qkv[:, bw:2 * bw]
    v = qkv[:, 2 * bw:3 * bw]
    z = p[:, qkvw:qkvw + bw]
    tail = p[:, qkvw + bw:qkvw + bw + PAIR]
    q = q * (lax.rsqrt(_seg_sum(q * q, gpair) + L2_EPS) * (HEAD ** -0.5))
    k = k * lax.rsqrt(_seg_sum(k * k, gpair) + L2_EPS)
    beta = jnp.where(is_beta, _sigmoid(tail), 0.0)
    gl = jnp.where(is_g, -jnp.exp(a_log) * _softplus(tail + dt_bias), 0.0)
    gcum128 = jnp.concatenate([_dot_left01(tri, gl[c * bi:c * (bi + 1)]) for bi in range(bt)], axis=0)
    gcum = _dot_sel(gcum128, gsel_ref[...])
    beta = _dot_sel(beta, bsel_ref[...])
    g_last = jnp.concatenate(
        [jnp.broadcast_to(gcum[c * (bi + 1) - 1:c * (bi + 1)], (c, bw)) for bi in range(bt)], axis=0)
    eg = jnp.exp(gcum)
    kb = k * beta
    pre = dict(q=q, k=k, kb=kb, gcum=gcum, kg=kb * eg, qg=q * eg, kd=k * jnp.exp(g_last - gcum),
               vb=v * beta, d_end=jnp.exp(g_last))

    chains = [(bi, pr) for bi in range(bt) for pr in range(3)]
    pair = lambda name: [pre[name][c * bi:c * (bi + 1), PAIR * pr:PAIR * (pr + 1)] for bi, pr in chains]
    q, k, kb, gcum, kg, qg = pair("q"), pair("k"), pair("kb"), pair("gcum"), pair("kg"), pair("qg")
    kd, vb = pair("kd"), pair("vb")
    d_end = [x[0:1] for x in pair("d_end")]

    def col(x):
        sw = pltpu.roll(x, HEAD, axis=1)
        return jnp.concatenate([jnp.where(is_a, x, sw), jnp.where(is_a, sw, x)], axis=0)

    gcol = [col(x) for x in gcum]
    diff = [x - x.T for x in gcol]
    dec = [jnp.exp(jnp.where(incl, x, 0.0)) for x in diff]
    dec_s = [jnp.where(strict, x, 0.0) for x in dec]
    dec_i = [jnp.where(incl, x, 0.0) for x in dec]
    sc = [_dot(jnp.concatenate([_stack_heads(x), _stack_heads(y)], axis=0),
               jnp.concatenate([w, w], axis=0), _NT)
          for x, y, w in zip(kb, q, k)]
    tinv = _tri_inverse_many([-(x[0:2 * c] * d) for x, d in zip(sc, dec_s)], c)
    qk = [x[2 * c:4 * c] * d for x, d in zip(sc, dec_i)]
    s = [s_scr[bi, pr] for bi, pr in chains]
    kqs = [_dot(jnp.concatenate([x, y], axis=0), s_) for x, y, s_ in zip(kg, qg, s)]
    v_new = [_dot(t, _stack_heads(x - y[0:c])) for t, x, y in zip(tinv, vb, kqs)]
    o_st = [_dot(x, y) for x, y in zip(qk, v_new)]
    ds = [_dot(x, _fold_heads(y, c), _TN) for x, y in zip(kd, v_new)]
    for i, (bi, pr) in enumerate(chains):
        s_scr[bi, pr] = s[i] * d_end[i] + jnp.where(bd, ds[i], 0.0)

    o = jnp.concatenate(
        [jnp.concatenate([kqs[3 * bi + pr][c:2 * c] + _fold_heads(o_st[3 * bi + pr], c) for pr in range(3)],
                         axis=1) for bi in range(bt)], axis=0)
    o = o * lax.rsqrt(_seg_sum(o * o, gpair) * (1.0 / HEAD) + RMS_EPS) * nw_ref[...]
    y_ref[...] = (o * (z * _sigmoid(z))).reshape(bt, c, bw)

    @pl.when(ci == nci - 1)
    def _():
        for bi in range(bt):
            for pr in range(3):
                _store_pair_state(sout_ref, bi, pr, s_scr[bi, pr])


def _gdn_call(p_b, conv0, s0, cw, tailp, nw, gmat, bsel, gsel, *, bt):
    b, tp, bcols = p_b.shape
    nh = s0.shape[1]
    bw = nh * HEAD
    cin = CHUNK
    nc = tp // cin
    const = lambda a: pl.BlockSpec(a.shape, lambda i, j: (0,) * a.ndim)
    kern = functools.partial(_gdn_kernel, bt=bt)
    return pl.pallas_call(
        kern,
        grid=(b // bt, nc),
        in_specs=[
            pl.BlockSpec((bt, cin, bcols), lambda i, j: (i, j, 0)),
            pl.BlockSpec((bt, CONV_K - 1, 3 * bw), lambda i, j: (i, 0, 0)),
            pl.BlockSpec((bt, nh, HEAD, HEAD), lambda i, j: (i, 0, 0, 0)),
            const(cw), const(tailp), const(nw), const(gmat), const(bsel), const(gsel),
        ],
        out_specs=[
            pl.BlockSpec((bt, cin, bw), lambda i, j: (i, j, 0)),
            pl.BlockSpec((bt, nh, HEAD, HEAD), lambda i, j: (i, 0, 0, 0)),
        ],
        out_shape=[jax.ShapeDtypeStruct((b, tp, bw), F32),
                   jax.ShapeDtypeStruct(s0.shape, F32)],
        scratch_shapes=[pltpu.VMEM((bt, nh // 2, PAIR, PAIR), F32),
                        pltpu.VMEM((bt, CHUNK + 8, 3 * bw), F32)],
        compiler_params=pltpu.CompilerParams(
            dimension_semantics=("parallel", "arbitrary"), vmem_limit_bytes=VMEM_LIMIT),
        name="gdn",
    )(p_b, conv0, s0, cw, tailp, nw, gmat, bsel, gsel)


def _hgrn_kernel(p_ref, s0_ref, lbl_ref, nw_ref, g_ref, y_ref, sout_ref, s_scr,
                 *, bt, layer):
    c = CHUNK
    ci = pl.program_id(1)
    nci = pl.num_programs(1)
    cw = 2 * PAIR
    rows = _iota((c, 1), 0)
    cols = _iota((1, c), 1)
    sameblk = (rows >> 4) == (cols >> 4)
    tri_tot = jnp.concatenate([sameblk & (rows >= cols), sameblk], axis=0).astype(BF16)
    rows_b = _iota((HBLK, 1), 0)
    bd = _pair_blockdiag_mask()
    gpair = g_ref[...]

    logits = lbl_ref[...]
    ex = jnp.exp(logits - jnp.max(logits, axis=0, keepdims=True))
    gam = ex / jnp.sum(ex, axis=0, keepdims=True)
    lb = jnp.sum(gam[0:layer + 1], axis=0, keepdims=True) - gam[0:1]

    @pl.when(ci == 0)
    def _():
        for bi in range(bt):
            for pr in range(2):
                s_scr[bi, pr] = _load_pair_state(s0_ref, bi, pr).T

    pre = []
    for bi in range(bt):
        p = p_ref[bi]
        qp = p[:, 0:cw]
        f = p[:, cw:2 * cw]
        v = p[:, 2 * cw:3 * cw]
        z = p[:, 3 * cw:4 * cw]
        q = qp * _sigmoid(qp)
        logf = jnp.log(lb + (1.0 - lb) * _sigmoid(f))
        k = (1.0 - lb) * _sigmoid(-f)
        cums = _dot_left01(tri_tot, logf)
        bl = cums[0:c]
        btot = cums[c:2 * c]
        pre.append(dict(q=q, k=k, v=v, z=z, bl=bl, q_in=q * jnp.exp(bl),
                        k_out=k * jnp.exp(btot - bl), d_blk=jnp.exp(btot)))

    chains = [(bi, pr) for bi in range(bt) for pr in range(2)]
    nblk = c // HBLK

    def blk_of(name, bi, pr, blk):
        return pre[bi][name][HBLK * blk:HBLK * (blk + 1), PAIR * pr:PAIR * (pr + 1)]

    intra = {}
    ds = {}
    for blk in range(nblk):
        for bi, pr in chains:
            qb, kb, vb, bb = (blk_of(n, bi, pr, blk) for n in ("q", "k", "v", "bl"))
            xs = []
            for j in range(HBLK):
                e = jnp.exp(jnp.minimum(bb - bb[j:j + 1], 0.0))
                xs.append(qb * e * kb[j:j + 1])
            att = _dot_sel(jnp.concatenate(xs, axis=0), gpair, pieces=1)
            acc = None
            for j in range(HBLK):
                t = jnp.where(rows_b >= j, att[HBLK * j:HBLK * (j + 1)], 0.0) * vb[j:j + 1]
                acc = t if acc is None else acc + t
            intra[bi, pr, blk] = acc
            ds[bi, pr, blk] = jnp.where(bd, _dot(vb, blk_of("k_out", bi, pr, blk), _TN), 0.0)

    s = {ch: s_scr[ch] for ch in chains}
    outs = {}
    for blk in range(nblk):
        for bi, pr in chains:
            outs[bi, pr, blk] = intra[bi, pr, blk] + _dot(blk_of("q_in", bi, pr, blk), s[bi, pr], _NT)
            s[bi, pr] = s[bi, pr] * blk_of("d_blk", bi, pr, blk)[0:1] + ds[bi, pr, blk]
    for ch in chains:
        s_scr[ch] = s[ch]

    o = jnp.concatenate(
        [jnp.concatenate([outs[bi, pr, blk] for bi in range(bt) for blk in range(nblk)], axis=0)
         for pr in range(2)], axis=1)
    z = jnp.concatenate([pre[bi]["z"] for bi in range(bt)], axis=0)
    o = o * lax.rsqrt(_seg_sum(o * o, gpair) * (1.0 / HEAD) + RMS_EPS) * nw_ref[...]
    y_ref[...] = (o * _sigmoid(z)).reshape(bt, c, cw)

    @pl.when(ci == nci - 1)
    def _():
        for bi in range(bt):
            for pr in range(2):
                _store_pair_state(sout_ref, bi, pr, s_scr[bi, pr].T)


def _hgrn_call(p_c, s0, lbl, nw, gmat, *, bt, layer):
    b, tp, ccols = p_c.shape
    nh = s0.shape[1]
    cw = nh * HEAD
    cin = CHUNK
    nc = tp // cin
    const = lambda a: pl.BlockSpec(a.shape, lambda i, j: (0,) * a.ndim)
    kern = functools.partial(_hgrn_kernel, bt=bt, layer=layer)
    return pl.pallas_call(
        kern,
        grid=(b // bt, nc),
        in_specs=[
            pl.BlockSpec((bt, cin, ccols), lambda i, j: (i, j, 0)),
            pl.BlockSpec((bt, nh, HEAD, HEAD), lambda i, j: (i, 0, 0, 0)),
            const(lbl), const(nw), const(gmat),
        ],
        out_specs=[
            pl.BlockSpec((bt, cin, cw), lambda i, j: (i, j, 0)),
            pl.BlockSpec((bt, nh, HEAD, HEAD), lambda i, j: (i, 0, 0, 0)),
        ],
        out_shape=[jax.ShapeDtypeStruct((b, tp, cw), F32),
                   jax.ShapeDtypeStruct(s0.shape, F32)],
        scratch_shapes=[pltpu.VMEM((bt, nh // 2, PAIR, PAIR), F32)],
        compiler_params=pltpu.CompilerParams(
            dimension_semantics=("parallel", "arbitrary"), vmem_limit_bytes=VMEM_LIMIT),
        name="hgrn",
    )(p_c, s0, lbl, nw, gmat)


def _inproj_t_kernel(x_ref, nw_ref, sc_ref, sh_ref, hprev_ref, w_ref,
                     pta_ref, ptb_ref, ptc_ref, rows_ref, hlast_ref, pprev_ref, *, splits):
    h = _norm_mod(x_ref[...], nw_ref[...], sc_ref[...], sh_ref[...])
    hb = h.astype(BF16)
    s0, s1 = splits
    nt = lambda w, a: lax.dot_general(w, a, _NT, preferred_element_type=F32)
    pta_ref[...] = nt(w_ref[0:s0, :], hb)
    ptb_ref[...] = nt(w_ref[s0:s1, :], hb)
    ptc_ref[...] = nt(w_ref[s1:, :], hb)
    rows_ref[...] = nt(hb, w_ref[s0:s0 + rows_ref.shape[1], :])
    nb = hlast_ref.shape[0]
    hlast_ref[...] = h[h.shape[0] - nb:]
    pprev_ref[...] = nt(w_ref[0:s0, :], hprev_ref[...].astype(BF16))


def _inproj_t_call(x, nw, scale, shift, h_prev, wt, splits, qkvw):
    rows, d = x.shape
    nb = h_prev.shape[0]
    wa, wb, wc = _split_widths(wt, splits)
    sds = jax.ShapeDtypeStruct
    return pl.pallas_call(
        functools.partial(_inproj_t_kernel, splits=splits),
        out_shape=[sds((wa, rows), F32), sds((wb, rows), F32), sds((wc, rows), F32), sds((rows, qkvw), F32),
                   sds((nb, d), F32), sds((wa, nb), F32)],
        compiler_params=pltpu.CompilerParams(vmem_limit_bytes=VMEM_LIMIT),
        name="inproj_t",
    )(x, nw, scale, shift, h_prev, wt)


def _col_sum(x):
    return jnp.sum(x, axis=0, keepdims=True)


def _rwkv_t_kernel(r_ref, k_ref, v_ref, tail_ref, rp_ref, kp_ref, vp_ref, tailp_ref,
                   mur_ref, muk_ref, muv_ref, mut_ref, vec_ref, wl_ref, s0_ref,
                   y_ref, sout_ref, v_scr, y_scr):
    nb = s0_ref.shape[-1]
    nt = r_ref.shape[-1] // nb
    w0, a0, k_k, k_a, r_k, ln_w, ln_b = (vec_ref[i, 0] for i in range(7))
    sout_ref[...] = s0_ref[...]

    def shifted(ref, pref, mu, idx, t):
        cur = ref[idx, :, t * nb:(t + 1) * nb]
        prv = pref[idx] if t == 0 else ref[idx, :, (t - 1) * nb:t * nb]
        return cur + mu * (prv - cur)

    for t in range(nt):
        r = shifted(r_ref, rp_ref, mur_ref[0], 0, t)
        k = shifted(k_ref, kp_ref, muk_ref[0], 0, t)
        v = shifted(v_ref, vp_ref, muv_ref[0], 0, t)
        t0 = shifted(tail_ref, tailp_ref, mut_ref[0], 0, t)
        t1 = shifted(tail_ref, tailp_ref, mut_ref[1], 1, t)
        act = jnp.concatenate([jnp.tanh(t0[0:32]), t0[32:64], _sigmoid(t1)], axis=0)
        lo = jnp.dot(wl_ref[0], act.astype(BF16), preferred_element_type=F32)
        w = -_softplus(-(w0 + lo[0:HEAD])) - 0.5
        dec = jnp.exp(-jnp.exp(w))
        a = _sigmoid(a0 + lo[HEAD:2 * HEAD])
        g = lo[2 * HEAD:3 * HEAD]
        kk = k * k_k
        kk = kk * lax.rsqrt(_col_sum(kk * kk) + L2_EPS)
        k = k * (1.0 + (a - 1.0) * k_a)
        a_t = -kk
        b_t = kk * a
        v_scr[...] = v

        def body(g, carry):
            base = pl.multiple_of(g * 8, 8)
            vg = v_scr[pl.ds(base, 8), :]
            ys = []
            for j in range(8):
                sv = sout_ref[0, base + j]
                sa = _col_sum(sv * a_t)
                sv = sv * dec + sa * b_t + vg[j:j + 1] * k
                sout_ref[0, base + j] = sv
                ys.append(_col_sum(sv * r))
            y_scr[pl.ds(base, 8), :] = jnp.concatenate(ys, axis=0)
            return carry

        lax.fori_loop(0, HEAD // 8, body, 0)
        y = y_scr[...]
        mean = _col_sum(y) * (1.0 / HEAD)
        dy = y - mean
        var = _col_sum(dy * dy) * (1.0 / HEAD)
        yn = dy * lax.rsqrt(var + GN_EPS) * ln_w + ln_b
        bonus = _col_sum(r * k * r_k) * v
        y_ref[0, :, t * nb:(t + 1) * nb] = (yn + bonus) * g


def _head_blocks(arr3, first, n=1):
    if n == 1:
        return pl.BlockSpec((1,) + arr3.shape[1:], lambda h: (first + h, 0, 0))
    return pl.BlockSpec((n,) + arr3.shape[1:], lambda h: (first // n, 0, 0))


def _rwkv_t_call(pta, pprev_t, s0, mu, vec, wl_t):
    nh, _, _, nb = s0.shape
    rows = pta.shape[1]
    p3 = pta.reshape(-1, HEAD, rows)
    pp3 = pprev_t.reshape(-1, HEAD, nb)
    mu3 = mu.reshape(-1, HEAD, 1)
    tail0 = 3 * nh
    state = pl.BlockSpec((1, HEAD, HEAD, nb), lambda h: (h, 0, 0, 0))
    return pl.pallas_call(
        _rwkv_t_kernel,
        grid=(nh,),
        in_specs=[
            _head_blocks(p3, 0), _head_blocks(p3, nh), _head_blocks(p3, 2 * nh), _head_blocks(p3, tail0, 2),
            _head_blocks(pp3, 0), _head_blocks(pp3, nh), _head_blocks(pp3, 2 * nh), _head_blocks(pp3, tail0, 2),
            _head_blocks(mu3, 0), _head_blocks(mu3, nh), _head_blocks(mu3, 2 * nh), _head_blocks(mu3, tail0, 2),
            pl.BlockSpec((vec.shape[0], 1, HEAD, 1), lambda h: (0, h, 0, 0)),
            pl.BlockSpec((1,) + wl_t.shape[1:], lambda h: (h, 0, 0)),
            state,
        ],
        out_specs=[pl.BlockSpec((1, HEAD, rows), lambda h: (h, 0, 0)), state],
        out_shape=[jax.ShapeDtypeStruct((nh, HEAD, rows), F32), jax.ShapeDtypeStruct(s0.shape, F32)],
        scratch_shapes=[pltpu.VMEM((HEAD, nb), F32), pltpu.VMEM((HEAD, nb), F32)],
        compiler_params=pltpu.CompilerParams(
            dimension_semantics=("parallel",), vmem_limit_bytes=VMEM_LIMIT),
        name="rwkv_t",
    )(p3, p3, p3, p3, pp3, pp3, pp3, pp3, mu3, mu3, mu3, mu3, vec, wl_t, s0)


def _gdn_t_kernel(q_ref, k_ref, v_ref, z_ref, tail_ref, cq_ref, ck_ref, cv_ref, wq_ref, wk_ref, wv_ref,
                  sc_ref, nw_ref, s0_ref, y_ref, sout_ref, k_scr, q_scr):
    h = pl.program_id(0)
    nh = pl.num_programs(0)
    nb = s0_ref.shape[-1]
    nt = q_ref.shape[-1] // nb
    a_log = sc_ref[0, 0:1, 0:1]
    dt_bias = sc_ref[0, 1:2, 0:1]
    sout_ref[...] = s0_ref[...]

    def conv(x_ref, c_ref, w_ref, t):
        acc = None
        for j in range(CONV_K):
            u = t + j - (CONV_K - 1)
            tap = c_ref[u + CONV_K - 1, 0] if u < 0 else x_ref[0, :, u * nb:(u + 1) * nb]
            term = w_ref[j, 0] * tap
            acc = term if acc is None else acc + term
        return acc * _sigmoid(acc)

    for t in range(nt):
        sl = slice(t * nb, (t + 1) * nb)
        q = conv(q_ref, cq_ref, wq_ref, t)
        k = conv(k_ref, ck_ref, wk_ref, t)
        v = conv(v_ref, cv_ref, wv_ref, t)
        q = q * (lax.rsqrt(_col_sum(q * q) + L2_EPS) * (HEAD ** -0.5))
        k = k * lax.rsqrt(_col_sum(k * k) + L2_EPS)
        tl = tail_ref[0:2 * 8, sl]
        pick = lambda row: _col_sum(jnp.where(_iota((2 * 8, 1), 0) == row, tl, 0.0))
        beta = _sigmoid(pick(h))
        eg = jnp.exp(-jnp.exp(a_log) * _softplus(pick(nh + h) + dt_bias))
        k_scr[...] = k
        q_scr[...] = q

        def body1(g, acc):
            base = pl.multiple_of(g * 8, 8)
            kg = k_scr[pl.ds(base, 8), :]
            for j in range(8):
                acc = acc + sout_ref[0, base + j] * kg[j:j + 1]
            return acc

        sk = lax.fori_loop(0, HEAD // 8, body1, jnp.zeros((HEAD, nb), F32))
        v_new = beta * (v - eg * sk)

        def body2(g, acc):
            base = pl.multiple_of(g * 8, 8)
            kg = k_scr[pl.ds(base, 8), :]
            qg = q_scr[pl.ds(base, 8), :]
            for j in range(8):
                s = eg * sout_ref[0, base + j] + kg[j:j + 1] * v_new
                sout_ref[0, base + j] = s
                acc = acc + s * qg[j:j + 1]
            return acc

        o = lax.fori_loop(0, HEAD // 8, body2, jnp.zeros((HEAD, nb), F32))
        o = o * lax.rsqrt(_col_sum(o * o) * (1.0 / HEAD) + RMS_EPS) * nw_ref[...]
        z = z_ref[0, :, sl]
        y_ref[0, :, sl] = o * (z * _sigmoid(z))


def _gdn_t_call(ptb, conv_t, s0, cw, scal, nw):
    nh, _, _, nb = s0.shape
    rows = ptb.shape[1]
    p3 = ptb.reshape(-1, HEAD, rows)
    c4 = conv_t.reshape(CONV_K - 1, -1, HEAD, nb)
    w4 = cw.reshape(CONV_K, -1, HEAD, 1)
    tap = lambda first: pl.BlockSpec((CONV_K - 1, 1, HEAD, nb), lambda h: (0, first + h, 0, 0))
    wsp = lambda first: pl.BlockSpec((CONV_K, 1, HEAD, 1), lambda h: (0, first + h, 0, 0))
    state = pl.BlockSpec((1, HEAD, HEAD, nb), lambda h: (h, 0, 0, 0))
    tail_blk = 4 * nh * HEAD // PAIR
    return pl.pallas_call(
        _gdn_t_kernel,
        grid=(nh,),
        in_specs=[
            _head_blocks(p3, 0), _head_blocks(p3, nh), _head_blocks(p3, 2 * nh), _head_blocks(p3, 3 * nh),
            pl.BlockSpec((PAIR, rows), lambda h: (tail_blk, 0)),
            tap(0), tap(nh), tap(2 * nh), wsp(0), wsp(nh), wsp(2 * nh),
            pl.BlockSpec((1,) + scal.shape[1:], lambda h: (h, 0, 0)),
            pl.BlockSpec(nw.shape, lambda h: (0, 0)),
            state,
        ],
        out_specs=[pl.BlockSpec((1, HEAD, rows), lambda h: (h, 0, 0)), state],
        out_shape=[jax.ShapeDtypeStruct((nh, HEAD, rows), F32), jax.ShapeDtypeStruct(s0.shape, F32)],
        scratch_shapes=[pltpu.VMEM((HEAD, nb), F32), pltpu.VMEM((HEAD, nb), F32)],
        compiler_params=pltpu.CompilerParams(
            dimension_semantics=("parallel",), vmem_limit_bytes=VMEM_LIMIT),
        name="gdn_t",
    )(p3, p3, p3, p3, ptb, c4, c4, c4, w4, w4, w4, scal, nw, s0)


def _hgrn_t_kernel(q_ref, f_ref, v_ref, z_ref, lbl_ref, nw_ref, s0_ref, y_ref, sout_ref,
                   f_scr, k_scr, q_scr, *, layer):
    nb = s0_ref.shape[-1]
    nt = q_ref.shape[-1] // nb
    logits = lbl_ref[:, 0]
    ex = jnp.exp(logits - jnp.max(logits, axis=0, keepdims=True))
    gam = ex / jnp.sum(ex, axis=0, keepdims=True)
    lb = jnp.sum(gam[0:layer + 1], axis=0) - gam[0]
    sout_ref[...] = s0_ref[...]

    for t in range(nt):
        sl = slice(t * nb, (t + 1) * nb)
        qp = q_ref[0, :, sl]
        f = f_ref[0, :, sl]
        v = v_ref[0, :, sl]
        z = z_ref[0, :, sl]
        q_scr[...] = qp * _sigmoid(qp)
        f_scr[...] = lb + (1.0 - lb) * _sigmoid(f)
        k_scr[...] = (1.0 - lb) * _sigmoid(-f)

        def body(g, acc):
            base = pl.multiple_of(g * 8, 8)
            fg, kg, qg = (ref[pl.ds(base, 8), :] for ref in (f_scr, k_scr, q_scr))
            for j in range(8):
                s = fg[j:j + 1] * sout_ref[0, base + j] + kg[j:j + 1] * v
                sout_ref[0, base + j] = s
                acc = acc + qg[j:j + 1] * s
            return acc

        o = lax.fori_loop(0, HEAD // 8, body, jnp.zeros((HEAD, nb), F32))
        o = o * lax.rsqrt(_col_sum(o * o) * (1.0 / HEAD) + RMS_EPS) * nw_ref[...]
        y_ref[0, :, sl] = o * _sigmoid(z)


def _hgrn_t_call(ptc, s0, lbl4, nw, layer):
    nh, _, _, nb = s0.shape
    rows = ptc.shape[1]
    p3 = ptc.reshape(-1, HEAD, rows)
    state = pl.BlockSpec((1, HEAD, HEAD, nb), lambda h: (h, 0, 0, 0))
    return pl.pallas_call(
        functools.partial(_hgrn_t_kernel, layer=layer),
        grid=(nh,),
        in_specs=[
            _head_blocks(p3, 0), _head_blocks(p3, nh), _head_blocks(p3, 2 * nh), _head_blocks(p3, 3 * nh),
            pl.BlockSpec((lbl4.shape[0], 1, HEAD, 1), lambda h: (0, h, 0, 0)),
            pl.BlockSpec(nw.shape, lambda h: (0, 0)),
            state,
        ],
        out_specs=[pl.BlockSpec((1, HEAD, rows), lambda h: (h, 0, 0)), state],
        out_shape=[jax.ShapeDtypeStruct((nh, HEAD, rows), F32), jax.ShapeDtypeStruct(s0.shape, F32)],
        scratch_shapes=[pltpu.VMEM((HEAD, nb), F32)] * 3,
        compiler_params=pltpu.CompilerParams(
            dimension_semantics=("parallel",), vmem_limit_bytes=VMEM_LIMIT),
        name="hgrn_t",
    )(p3, p3, p3, p3, lbl4, nw, s0)


def _ffn_kernel(x_ref, ya_ref, yb_ref, yc_ref, woa_ref, wob_ref, woc_ref, gm_ref, nw_ref,
                sc_ref, sh_ref, gf_ref, win_ref, wd_ref, fnw_ref, xo_ref, *yo_ref, th, y_transposed):
    ff = wd_ref.shape[0]
    dims = _TN if y_transposed else _NN
    proj = lambda y_ref, w_ref: lax.dot_general(y_ref[...].astype(BF16), w_ref[...], dims,
                                                preferred_element_type=F32)
    mix = proj(ya_ref, woa_ref) + proj(yb_ref, wob_ref) + proj(yc_ref, woc_ref)
    x1 = x_ref[...] + gm_ref[0] * mix
    h = _norm_mod(x1, nw_ref[...], sc_ref[0], sh_ref[0]).astype(BF16)
    acc = None
    for j in range(ff // th):
        gate = jnp.dot(h, win_ref[:, j * th:(j + 1) * th], preferred_element_type=F32)
        up = jnp.dot(h, win_ref[:, ff + j * th:ff + (j + 1) * th], preferred_element_type=F32)
        act = (gate * _sigmoid(gate) * up).astype(BF16)
        t = jnp.dot(act, wd_ref[j * th:(j + 1) * th, :], preferred_element_type=F32)
        acc = t if acc is None else acc + t
    xo = x1 + gf_ref[0] * acc
    xo_ref[...] = xo
    if yo_ref:
        yo_ref[0][...] = xo * lax.rsqrt(jnp.mean(xo * xo, axis=-1, keepdims=True) + RMS_EPS) * fnw_ref[...]


def _ffn_call(x, ya, yb, yc, woa, wob, woc, gate_m, nw, scale_f, shift_f, gate_f,
              w_in, w_down, fnw, *, tm, th, tiles_per_seq, final, y_transposed=False):
    rows, d = x.shape
    row = lambda a: pl.BlockSpec((tm, a.shape[1]), lambda i: (i, 0))
    yspec = (lambda a: pl.BlockSpec((a.shape[0], tm), lambda i: (0, i))) if y_transposed else row
    const = lambda a: pl.BlockSpec(a.shape, lambda i: (0, 0), pipeline_mode=pl.Buffered(1))
    mod = lambda a: _mod_spec(a, tm, tiles_per_seq)
    n_out = 2 if final else 1
    return pl.pallas_call(
        functools.partial(_ffn_kernel, th=th, y_transposed=y_transposed),
        grid=(rows // tm,),
        in_specs=[
            row(x), yspec(ya), yspec(yb), yspec(yc), const(woa), const(wob), const(woc),
            mod(gate_m), const(nw), mod(scale_f), mod(shift_f), mod(gate_f),
            const(w_in), const(w_down), const(fnw),
        ],
        out_specs=[row(x)] * n_out,
        out_shape=[jax.ShapeDtypeStruct((rows, d), F32)] * n_out,
        compiler_params=pltpu.CompilerParams(
            dimension_semantics=("parallel",), vmem_limit_bytes=VMEM_LIMIT),
        name="ffn",
    )(x, ya, yb, yc, woa, wob, woc, gate_m, nw, scale_f, shift_f, gate_f, w_in, w_down, fnw)


def _block_ones(n, blk):
    i = np.arange(n) // blk
    return jnp.asarray(i[:, None] == i[None, :], BF16)


def _head_select(first_lane, nh):
    m = np.zeros((PAIR, nh * HEAD), np.float32)
    for h in range(nh):
        m[first_lane + h, h * HEAD:(h + 1) * HEAD] = 1.0
    return jnp.asarray(m, BF16)


def _lane_place(x, first_lane):
    return jnp.zeros((PAIR,), F32).at[first_lane:first_lane + x.shape[0]].set(x)


def _run_chunked(x, mods, states, wts, *, tm, tm_ffn, th, bt):
    b, t, d = x.shape
    rows = b * t
    assert t % CHUNK == 0 and t % tm == 0 and t % tm_ffn == 0 and b % bt == 0 and t >= CONV_K - 1
    shift0, rwkv0, conv0, gdn0, hgrn0 = states
    nl = len(wts)
    new = ([], [], [], [], [])
    xr = x.reshape(rows, d)
    for l in range(nl):
        w = wts[l]
        m = mods[l]
        shift_m, scale_m, gate_m, shift_f, scale_f, gate_f = (m[:, i].reshape(b, 1, d) for i in range(6))
        p_a, p_b, p_c = _inproj_call(xr, w["norm_mix"], scale_m, shift_m, w["wt"], w["splits"], tm, t // tm)
        h_last, p_prev = _lastrow_call(xr.reshape(b, t, d)[:, t - 1], w["norm_mix"], m[:, 1], m[:, 0],
                                       shift0[l], w["wt"], w["splits"][0])
        p_a = p_a.reshape(b, t, -1)
        p_b = p_b.reshape(b, t, -1)
        p_c = p_c.reshape(b, t, -1)
        y_a, s_a = _rwkv_call(p_a, p_prev, rwkv0[l], w["mu"], w["rwkv_vec"], w["lora"], w["gpair"], bt=bt)
        y_b, s_b = _gdn_call(p_b, conv0[l], gdn0[l], w["conv_w"], w["gdn_tail"], w["gdn_norm"],
                             w["gpair"], w["bsel"], w["gsel"], bt=bt)
        y_c, s_c = _hgrn_call(p_c, hgrn0[l], w["lb_logits"], w["hgrn_norm"], w["gpair"], bt=bt, layer=l)
        conv_new = p_b[:, t - (CONV_K - 1):, :conv0.shape[-1]]
        res = _ffn_call(xr, y_a.reshape(rows, -1), y_b.reshape(rows, -1), y_c.reshape(rows, -1),
                        w["woa"], w["wob"], w["woc"], gate_m, w["norm_ffn"], scale_f, shift_f, gate_f,
                        w["w_ffn_in"], w["w_ffn_out"], w["final_norm"],
                        tm=tm_ffn, th=th, tiles_per_seq=t // tm_ffn, final=l == nl - 1)
        xr = res[0]
        for acc, s in zip(new, (h_last, s_a, conv_new, s_b, s_c)):
            acc.append(s)
    return res[1].reshape(b, t, d), [jnp.stack(acc) for acc in new]


def _run_steps(x, mods, states, wts, *, tm_ffn, th):
    b, t, d = x.shape
    rows = t * b
    tm_ffn = min(tm_ffn, rows)
    shift0, rwkv0, conv0, gdn0, hgrn0 = states
    to_lanes = lambda s: jnp.transpose(s, (0, 2, 3, 4, 1))
    rwkv_t, gdn_t, hgrn_t = to_lanes(rwkv0), to_lanes(gdn0), to_lanes(hgrn0)
    conv_t = jnp.transpose(conv0, (0, 2, 3, 1))
    nl = len(wts)
    new = ([], [], [], [], [])
    xr = jnp.transpose(x, (1, 0, 2)).reshape(rows, d)
    for l in range(nl):
        w = wts[l]
        m = mods[l]
        mod = lambda i, m=m: jnp.tile(m[:, i], (t, 1))
        tiled = lambda a: a.reshape(rows // tm_ffn, tm_ffn, d)
        pta, ptb, ptc, qkv_rows, h_last, pprev_t = _inproj_t_call(
            xr, w["norm_mix"], mod(1), mod(0), shift0[l], w["wt"], w["splits"], w["qkvw"])
        y_a, s_a = _rwkv_t_call(pta, pprev_t, rwkv_t[l], w["mu"], w["rwkv_vec_t"], w["lora_t"])
        y_b, s_b = _gdn_t_call(ptb, conv_t[l], gdn_t[l], w["conv_w_t"], w["gdn_scal"], w["gdn_norm_t"])
        y_c, s_c = _hgrn_t_call(ptc, hgrn_t[l], w["lb_logits_t"], w["hgrn_norm_t"], l)
        xp = jnp.concatenate([jnp.transpose(conv0[l], (1, 0, 2)), qkv_rows.reshape(t, b, -1)], axis=0)
        conv_new = jnp.transpose(xp[-(CONV_K - 1):], (1, 0, 2))
        res = _ffn_call(xr, y_a.reshape(-1, rows), y_b.reshape(-1, rows), y_c.reshape(-1, rows),
                        w["woa"], w["wob"], w["woc"], tiled(mod(2)), w["norm_ffn"],
                        tiled(mod(4)), tiled(mod(3)), tiled(mod(5)),
                        w["w_ffn_in"], w["w_ffn_out"], w["final_norm"],
                        tm=tm_ffn, th=th, tiles_per_seq=1, final=l == nl - 1, y_transposed=True)
        xr = res[0]
        for acc, s in zip(new, (h_last, s_a, conv_new, s_b, s_c)):
            acc.append(s)
    y = jnp.transpose(res[1].reshape(t, b, d), (1, 0, 2))
    from_lanes = lambda s: jnp.transpose(jnp.stack(s), (0, 4, 1, 2, 3))
    return y, [jnp.stack(new[0]), from_lanes(new[1]), jnp.stack(new[2]), from_lanes(new[3]), from_lanes(new[4])]


def kernel(x_prompt, x_sample, c_prompt, c_sample, state_rwkv_shift, state_rwkv, state_gdn_conv, state_gdn, state_hgrn, w_ada, b_ada, norm_mix_w, w_in, rwkv_mu, rwkv_w0, rwkv_w2, rwkv_a0, rwkv_a2, rwkv_g2, rwkv_k_k, rwkv_k_a, rwkv_r_k, rwkv_ln_w, rwkv_ln_b, gdn_conv_w, gdn_A_log, gdn_dt_bias, gdn_norm_w, hgrn_lb_logits, hgrn_norm_w, w_out, norm_ffn_w, w_ffn_in, w_ffn_out, final_norm_w):
    nl, d, _ = w_in.shape
    a_heads = state_rwkv.shape[2]
    b_heads = state_gdn.shape[2]
    c_heads = state_hgrn.shape[2]
    aw, bw, cw = a_heads * HEAD, b_heads * HEAD, c_heads * HEAD
    lw, la, lg = rwkv_w2.shape[1], rwkv_a2.shape[1], rwkv_g2.shape[1]
    a_cols = 3 * aw + lw + la + lg
    qkvw = 3 * bw
    b_cols = qkvw + 2 * b_heads + bw
    assert (aw, bw, cw) == (3 * PAIR, 3 * PAIR, 2 * PAIR) and lw + la + lg == PAIR and lw == 32 and la == 32

    bp, tpr, _ = x_prompt.shape
    bs, ts, _ = x_sample.shape

    gpair = _block_ones(PAIR, HEAD)
    bsel = _head_select(0, b_heads)
    gsel = _head_select(b_heads, b_heads)
    wts = []
    for l in range(nl):
        wit = jnp.swapaxes(w_in[l], 0, 1)
        g0 = a_cols + qkvw
        wt = jnp.concatenate(
            [wit[:g0], wit[g0 + 2 * b_heads:a_cols + b_cols], wit[g0:g0 + 2 * b_heads],
             jnp.zeros((PAIR - 2 * b_heads, d), F32), wit[a_cols + b_cols:]], axis=0).astype(BF16)
        splits = (a_cols, a_cols + qkvw + bw + PAIR)
        lora = jnp.zeros((PAIR, 3 * aw), F32)
        lora = lora.at[0:lw, 0:aw].set(rwkv_w2[l])
        lora = lora.at[lw:lw + la, aw:2 * aw].set(rwkv_a2[l])
        lora = lora.at[lw + la:, 2 * aw:].set(rwkv_g2[l])
        rwkv_vec = jnp.stack([rwkv_w0[l], rwkv_a0[l], rwkv_k_k[l], rwkv_k_a[l], rwkv_r_k[l].reshape(-1),
                              rwkv_ln_w[l], rwkv_ln_b[l], jnp.zeros((aw,), F32)])
        wo = w_out[l].astype(BF16)
        hcols = lambda a: a.reshape(a.shape[0], a_heads, HEAD).transpose(1, 2, 0)
        lora_t = jnp.concatenate([
            jnp.pad(hcols(rwkv_w2[l]), ((0, 0), (0, 0), (0, la + lg))),
            jnp.pad(hcols(rwkv_a2[l]), ((0, 0), (0, 0), (lw, lg))),
            jnp.pad(hcols(rwkv_g2[l]), ((0, 0), (0, 0), (lw + la, 0)))], axis=1)
        steps = dict(
            rwkv_vec_t=rwkv_vec[:7].reshape(7, a_heads, HEAD, 1), lora_t=lora_t.astype(BF16),
            conv_w_t=gdn_conv_w[l],
            gdn_scal=jnp.zeros((b_heads, 8, PAIR), F32)
                .at[:, 0].set(jnp.broadcast_to(gdn_A_log[l][:, None], (b_heads, PAIR)))
                .at[:, 1].set(jnp.broadcast_to(gdn_dt_bias[l][:, None], (b_heads, PAIR))),
            gdn_norm_t=gdn_norm_w[l].reshape(HEAD, 1),
            lb_logits_t=hgrn_lb_logits.reshape(nl, c_heads, HEAD, 1),
            hgrn_norm_t=hgrn_norm_w[l].reshape(HEAD, 1),
        )
        wts.append(dict(
            **steps,
            norm_mix=norm_mix_w[l].reshape(1, d),
            wt=wt, splits=splits, qkvw=qkvw,
            mu=rwkv_mu[l].reshape(1, a_cols), rwkv_vec=rwkv_vec, lora=lora.astype(BF16),
            gpair=gpair, bsel=bsel, gsel=gsel,
            conv_w=jnp.concatenate([gdn_conv_w[l], jnp.zeros((8 - CONV_K, qkvw), F32)], axis=0),
            gdn_tail=jnp.zeros((8, PAIR), F32).at[0].set(_lane_place(gdn_A_log[l], b_heads))
                                              .at[1].set(_lane_place(gdn_dt_bias[l], b_heads)),
            gdn_norm=jnp.tile(gdn_norm_w[l], b_heads).reshape(1, bw),
            lb_logits=hgrn_lb_logits,
            hgrn_norm=jnp.tile(hgrn_norm_w[l], c_heads).reshape(1, cw),
            woa=wo[:aw], wob=wo[aw:aw + bw], woc=wo[aw + bw:],
            norm_ffn=norm_ffn_w[l].reshape(1, d),
            w_ffn_in=w_ffn_in[l].astype(BF16), w_ffn_out=w_ffn_out[l].astype(BF16),
            final_norm=final_norm_w.reshape(1, d),
        ))

    mod = _ada_call(jnp.concatenate([c_prompt, c_sample], axis=0), w_ada, b_ada)
    mod = mod.reshape(nl, bp + bs, 6, d)
    mods_p = [mod[l, :bp] for l in range(nl)]
    mods_s = [mod[l, bp:] for l in range(nl)]

    zeros = lambda s: jnp.zeros((nl, bp) + s.shape[2:], s.dtype)
    states_p = tuple(zeros(s) for s in (state_rwkv_shift, state_rwkv, state_gdn_conv, state_gdn, state_hgrn))
    states_s = (state_rwkv_shift, state_rwkv, state_gdn_conv, state_gdn, state_hgrn)

    th = w_ffn_out.shape[1] // FFN_HIDDEN_TILES
    y_p, new_p = _run_chunked(x_prompt, mods_p, states_p, wts, tm=TM_INPROJ, tm_ffn=TM_FFN, th=th, bt=SEQS_PER_STEP)
    y_s, new_s = _run_steps(x_sample, mods_s, states_s, wts, tm_ffn=TM_FFN_STEPS, th=th)
    return (y_p, y_s, *new_p, *new_s)
```

```python
import functools

import numpy as np
import jax
import jax.numpy as jnp
from jax import lax
from jax.experimental import pallas as pl
from jax.experimental.pallas import tpu as pltpu

F32 = jnp.float32
BF16 = jnp.bfloat16

HEAD = 64
PAIR = 2 * HEAD
CHUNK = 64
HBLK = 16
INV_BASE = 16
CONV_K = 4
RMS_EPS = 1e-6
L2_EPS = 1e-6
GN_EPS = 64e-5
VMEM_LIMIT = 56 * 1024 * 1024

TM_INPROJ = 512
TM_FFN = 512
TM_FFN_STEPS = 256
FFN_HIDDEN_TILES = 2
SEQS_PER_STEP = 8

_NN = (((1,), (0,)), ((), ()))
_NT = (((1,), (1,)), ((), ()))
_TN = (((0,), (0,)), ((), ()))


def _dot(a, b, dims=_NN):
    return lax.dot_general(a.astype(BF16), b.astype(BF16), dims, preferred_element_type=F32)


def _dot_sel(x, m, pieces=3):
    ps = _bf16_pieces(x, pieces)
    return lax.dot_general(jnp.concatenate(ps, axis=1), jnp.concatenate([m] * pieces, axis=0), _NN,
                           preferred_element_type=F32)


def _bf16_pieces(x, pieces):
    ps = []
    rem = x
    for i in range(pieces):
        p = rem.astype(BF16)
        ps.append(p)
        if i + 1 < pieces:
            rem = rem - p.astype(F32)
    return ps


def _seg_sum(x, gpair, pieces=2):
    return jnp.concatenate(
        [_dot_sel(x[:, i:i + PAIR], gpair, pieces) for i in range(0, x.shape[1], PAIR)], axis=1)


def _sigmoid(x):
    return jax.nn.sigmoid(x)


def _softplus(x):
    return jnp.maximum(x, 0.0) + jnp.log1p(jnp.exp(-jnp.abs(x)))


def _iota(shape, dim):
    return lax.broadcasted_iota(jnp.int32, shape, dim)


def _stack_heads(x):
    is_a = _iota((1, PAIR), 1) < HEAD
    return jnp.concatenate([jnp.where(is_a, x, 0.0), jnp.where(is_a, 0.0, x)], axis=0)


def _fold_heads(x, c):
    return x[0:c] + x[c:2 * c]


def _pair_blockdiag_mask():
    return (_iota((PAIR, 1), 0) >> 6) == (_iota((1, PAIR), 1) >> 6)


def _load_pair_state(s_ref, bi, pr):
    sa = s_ref[bi, 2 * pr]
    sb = s_ref[bi, 2 * pr + 1]
    z = jnp.zeros((HEAD, HEAD), F32)
    return jnp.concatenate(
        [jnp.concatenate([sa, z], axis=1), jnp.concatenate([z, sb], axis=1)], axis=0)


def _store_pair_state(s_ref, bi, pr, s):
    s_ref[bi, 2 * pr] = s[0:HEAD, 0:HEAD]
    s_ref[bi, 2 * pr + 1] = s[HEAD:PAIR, HEAD:PAIR]


def _dot_left01(m, x, pieces=3):
    return lax.dot_general(jnp.concatenate([m] * pieces, axis=1),
                           jnp.concatenate(_bf16_pieces(x, pieces), axis=0), _NN,
                           preferred_element_type=F32)


def _tri_inverse_many(ns, c):
    size = ns[0].shape[0]
    r = _iota((size, 1), 0)
    cc = _iota((1, size), 1)
    eye = (r == cc).astype(F32)
    sh = INV_BASE.bit_length() - 1
    diag = [jnp.where((r >> sh) == (cc >> sh), n, 0.0) for n in ns]
    invs = [eye + d for d in diag]
    pws = [_dot(d, d) for d in diag]
    span = 4
    while span < INV_BASE:
        both = [_dot(jnp.concatenate([inv, pw], axis=0), pw) for inv, pw in zip(invs, pws)]
        invs = [inv + b[0:size] for inv, b in zip(invs, both)]
        pws = [b[size:2 * size] for b in both]
        span *= 2
    invs = [inv + _dot(inv, pw) for inv, pw in zip(invs, pws)]
    blk = INV_BASE
    while blk < c:
        sh = blk.bit_length() - 1
        off = ((r >> (sh + 1)) == (cc >> (sh + 1))) & ((r >> sh) != (cc >> sh))
        low = [_dot(inv, jnp.where(off, n, 0.0)) for inv, n in zip(invs, ns)]
        invs = [inv + _dot(t, inv) for inv, t in zip(invs, low)]
        blk *= 2
    return invs


def _ada_kernel(c_ref, w_ref, b_ref, o_ref):
    c = c_ref[...]
    o_ref[0] = _dot(c * _sigmoid(c), w_ref[0]) + b_ref[0]


def _ada_call(c_all, w_ada, b_ada):
    nl, d, n6 = w_ada.shape
    rows = c_all.shape[0]
    tn = n6 // 4
    return pl.pallas_call(
        _ada_kernel,
        grid=(nl, n6 // tn),
        in_specs=[
            pl.BlockSpec((rows, d), lambda l, j: (0, 0)),
            pl.BlockSpec((1, d, tn), lambda l, j: (l, 0, j)),
            pl.BlockSpec((1, 1, tn), lambda l, j: (l, 0, j)),
        ],
        out_specs=pl.BlockSpec((1, rows, tn), lambda l, j: (l, 0, j)),
        out_shape=jax.ShapeDtypeStruct((nl, rows, n6), F32),
        compiler_params=pltpu.CompilerParams(
            dimension_semantics=("parallel", "parallel"), vmem_limit_bytes=VMEM_LIMIT),
        name="ada",
    )(c_all, w_ada, b_ada.reshape(nl, 1, n6))


def _norm_mod(x, nw, scale, shift):
    y = x * lax.rsqrt(jnp.mean(x * x, axis=-1, keepdims=True) + RMS_EPS) * nw
    return y * (1.0 + scale) + shift


def _inproj_kernel(x_ref, nw_ref, sc_ref, sh_ref, wa_ref, wb_ref, wc_ref, pa_ref, pb_ref, pc_ref):
    h = _norm_mod(x_ref[...], nw_ref[...], sc_ref[0], sh_ref[0]).astype(BF16)
    pa_ref[...] = lax.dot_general(h, wa_ref[...], _NT, preferred_element_type=F32)
    pb_ref[...] = lax.dot_general(h, wb_ref[...], _NT, preferred_element_type=F32)
    pc_ref[...] = lax.dot_general(h, wc_ref[...], _NT, preferred_element_type=F32)


def _mod_spec(mod, tm, tiles_per_seq):
    if mod.shape[1] == 1:
        return pl.BlockSpec((1, 1, mod.shape[2]), lambda i, *_: (i // tiles_per_seq, 0, 0))
    return pl.BlockSpec((1, tm, mod.shape[2]), lambda i, *_: (i, 0, 0))


def _inproj_call(x, nw, scale, shift, wa, wb, wc, tm, tiles_per_seq):
    rows, d = x.shape
    full = lambda a: pl.BlockSpec(a.shape, lambda i: (0, 0), pipeline_mode=pl.Buffered(1))
    outs = [jax.ShapeDtypeStruct((rows, w.shape[0]), F32) for w in (wa, wb, wc)]
    return pl.pallas_call(
        _inproj_kernel,
        grid=(rows // tm,),
        in_specs=[
            pl.BlockSpec((tm, d), lambda i: (i, 0)),
            full(nw),
            _mod_spec(scale, tm, tiles_per_seq),
            _mod_spec(shift, tm, tiles_per_seq),
            full(wa), full(wb), full(wc),
        ],
        out_specs=[pl.BlockSpec((tm, w.shape[0]), lambda i: (i, 0)) for w in (wa, wb, wc)],
        out_shape=outs,
        compiler_params=pltpu.CompilerParams(
            dimension_semantics=("parallel",), vmem_limit_bytes=VMEM_LIMIT),
        name="inproj",
    )(x, nw, scale, shift, wa, wb, wc)


def _lastrow_kernel(x_ref, nw_ref, sc_ref, sh_ref, hprev_ref, wa_ref, h_ref, pprev_ref):
    h_ref[...] = _norm_mod(x_ref[...], nw_ref[...], sc_ref[...], sh_ref[...])
    pprev_ref[...] = lax.dot_general(hprev_ref[...].astype(BF16), wa_ref[...], _NT,
                                     preferred_element_type=F32)


def _lastrow_call(x_last, nw, scale, shift, h_prev, wa):
    b, d = x_last.shape
    return pl.pallas_call(
        _lastrow_kernel,
        out_shape=[jax.ShapeDtypeStruct((b, d), F32), jax.ShapeDtypeStruct((b, wa.shape[0]), F32)],
        compiler_params=pltpu.CompilerParams(vmem_limit_bytes=VMEM_LIMIT),
        name="lastrow",
    )(x_last, nw, scale, shift, h_prev, wa)


def _rwkv_kernel(p_ref, pprev_ref, s0_ref, mu_ref, vec_ref, wl_ref, g_ref,
                 y_ref, sout_ref, s_scr, prev_scr, *, bt):
    c = CHUNK
    ci = pl.program_id(1)
    nci = pl.num_programs(1)
    aw = 3 * PAIR
    rows = _iota((c, 1), 0)
    lane = _iota((1, PAIR), 1)
    vec = vec_ref[...]
    w0, a0, k_k, k_a, r_k, ln_w, ln_b = (vec[i:i + 1] for i in range(7))
    gpair = g_ref[...]
    tri = (rows >= _iota((1, c), 1)).astype(BF16)
    r2 = _iota((2 * c, 1), 0) & (c - 1)
    c2 = _iota((1, 2 * c), 1) & (c - 1)
    bd = _pair_blockdiag_mask()
    strict = bd & (r2 > c2)
    incl = bd & (r2 >= c2)

    @pl.when(ci == 0)
    def _():
        for bi in range(bt):
            for pr in range(3):
                s_scr[bi, pr] = _load_pair_state(s0_ref, bi, pr)
            prev_scr[bi] = pprev_ref[bi]

    shifted = []
    for bi in range(bt):
        p = p_ref[bi]
        prev = pltpu.roll(p, 1, axis=0)
        prev = jnp.where(rows == 0, prev_scr[bi], prev)
        prev_scr[bi] = p[c - 1:c]
        shifted.append(p + mu_ref[...] * (prev - p))
    xs = jnp.concatenate(shifted, axis=0)
    r = xs[:, 0:aw]
    k = xs[:, aw:2 * aw]
    v = xs[:, 2 * aw:3 * aw]
    tail = xs[:, 3 * aw:3 * aw + PAIR]
    act = jnp.where(lane < 32, jnp.tanh(tail), jnp.where(lane < 64, tail, _sigmoid(tail)))
    lo = _dot(act, wl_ref[...])
    w = -_softplus(-(w0 + lo[:, 0:aw])) - 0.5
    ld = -jnp.exp(w)
    a = _sigmoid(a0 + lo[:, aw:2 * aw])
    g = lo[:, 2 * aw:3 * aw]
    kk = k * k_k
    kk = kk * lax.rsqrt(_seg_sum(kk * kk, gpair) + L2_EPS)
    k = k * (1.0 + (a - 1.0) * k_a)
    cums = [_dot_left01(tri, ld[c * bi:c * (bi + 1)]) for bi in range(bt)]
    cum = jnp.concatenate(cums, axis=0)
    cum_last = jnp.concatenate([jnp.broadcast_to(x[c - 1:c], x.shape) for x in cums], axis=0)
    e_neg = jnp.exp(-cum)
    e_end = jnp.exp(cum_last - cum)
    pre = dict(a_t=-kk * jnp.exp(cum - ld), b_t=kk * a * e_neg, k_t=k * e_neg, r_t=r * jnp.exp(cum), v=v,
               b_end=kk * a * e_end, k_end=k * e_end)

    chains = [(bi, pr) for bi in range(bt) for pr in range(3)]
    pair = lambda name: [pre[name][c * bi:c * (bi + 1), PAIR * pr:PAIR * (pr + 1)] for bi, pr in chains]
    a_t, b_t, k_t, r_t, v_p = pair("a_t"), pair("b_t"), pair("k_t"), pair("r_t"), pair("v")
    b_end, k_end = pair("b_end"), pair("k_end")
    d_end = [jnp.exp(cums[bi][c - 1:c, PAIR * pr:PAIR * (pr + 1)]) for bi, pr in chains]
    v_st = [_stack_heads(x) for x in v_p]
    gs = [_dot(jnp.concatenate([_stack_heads(a), _stack_heads(r)], axis=0),
               jnp.concatenate([b, b, k, k], axis=0), _NT)
          for a, r, b, k in zip(a_t, r_t, b_t, k_t)]
    a_ak = [jnp.where(strict, g[0:2 * c, 2 * c:4 * c], 0.0) for g in gs]
    p_rbk = [jnp.concatenate([jnp.where(incl, g[2 * c:4 * c, 0:2 * c], 0.0),
                              jnp.where(incl, g[2 * c:4 * c, 2 * c:4 * c], 0.0)], axis=1) for g in gs]
    tinv = _tri_inverse_many([jnp.where(strict, g[0:2 * c, 0:2 * c], 0.0) for g in gs], c)
    s = [s_scr[bi, pr] for bi, pr in chains]
    ars = [_dot(jnp.concatenate([a, r], axis=0), s_, _NT) for a, r, s_ in zip(a_t, r_t, s)]
    akv = [_dot(m, x) for m, x in zip(a_ak, v_st)]
    u_st = [_dot(t, _stack_heads(x[0:c]) + y) for t, x, y in zip(tinv, ars, akv)]
    y_st = [_dot(p, jnp.concatenate([u, x], axis=0)) for p, u, x in zip(p_rbk, u_st, v_st)]
    ds = [_dot(jnp.concatenate([_fold_heads(u, c), x], axis=0), jnp.concatenate([b_, k_], axis=0), _TN)
          for u, x, b_, k_ in zip(u_st, v_p, b_end, k_end)]
    for i, (bi, pr) in enumerate(chains):
        s_scr[bi, pr] = s[i] * d_end[i] + jnp.where(bd, ds[i], 0.0)

    y = jnp.concatenate(
        [jnp.concatenate([ars[3 * bi + pr][c:2 * c] + _fold_heads(y_st[3 * bi + pr], c) for pr in range(3)],
                         axis=1) for bi in range(bt)], axis=0)
    mean = _seg_sum(y, gpair) * (1.0 / HEAD)
    dy = y - mean
    var = _seg_sum(dy * dy, gpair) * (1.0 / HEAD)
    yn = dy * lax.rsqrt(var + GN_EPS) * ln_w + ln_b
    bonus = _seg_sum(r * k * r_k, gpair) * v
    y_ref[...] = ((yn + bonus) * g).reshape(bt, c, aw)

    @pl.when(ci == nci - 1)
    def _():
        for bi in range(bt):
            for pr in range(3):
                _store_pair_state(sout_ref, bi, pr, s_scr[bi, pr])


def _rwkv_call(p_a, p_prev, s0, mu, vec, wl, gmat, *, bt):
    b, tp, acols = p_a.shape
    nh = s0.shape[1]
    aw = nh * HEAD
    cin = CHUNK
    nc = tp // cin
    const = lambda a: pl.BlockSpec(a.shape, lambda i, j: (0,) * a.ndim)
    kern = functools.partial(_rwkv_kernel, bt=bt)
    return pl.pallas_call(
        kern,
        grid=(b // bt, nc),
        in_specs=[
            pl.BlockSpec((bt, cin, acols), lambda i, j: (i, j, 0)),
            pl.BlockSpec((bt, 1, acols), lambda i, j: (i, 0, 0)),
            pl.BlockSpec((bt, nh, HEAD, HEAD), lambda i, j: (i, 0, 0, 0)),
            const(mu), const(vec), const(wl), const(gmat),
        ],
        out_specs=[
            pl.BlockSpec((bt, cin, aw), lambda i, j: (i, j, 0)),
            pl.BlockSpec((bt, nh, HEAD, HEAD), lambda i, j: (i, 0, 0, 0)),
        ],
        out_shape=[jax.ShapeDtypeStruct((b, tp, aw), F32),
                   jax.ShapeDtypeStruct(s0.shape, F32)],
        scratch_shapes=[pltpu.VMEM((bt, nh // 2, PAIR, PAIR), F32),
                        pltpu.VMEM((bt, 1, acols), F32)],
        compiler_params=pltpu.CompilerParams(
            dimension_semantics=("parallel", "arbitrary"), vmem_limit_bytes=VMEM_LIMIT),
        name="rwkv",
    )(p_a, p_prev.reshape(b, 1, acols), s0, mu, vec, wl, gmat)


def _gdn_kernel(p_ref, conv0_ref, s0_ref, cw_ref, tp_ref, nw_ref, g_ref, bsel_ref, gsel_ref,
                y_ref, sout_ref, s_scr, cbuf, *, bt):
    c = CHUNK
    ci = pl.program_id(1)
    nci = pl.num_programs(1)
    bw = 3 * PAIR
    qkvw = 3 * bw
    rows = _iota((c, 1), 0)
    lane = _iota((1, PAIR), 1)
    nh = 2 * 3
    gpair = g_ref[...]
    tri = (rows >= _iota((1, c), 1)).astype(BF16)
    r2 = _iota((2 * c, 1), 0) & (c - 1)
    c2 = _iota((1, 2 * c), 1) & (c - 1)
    same =(_iota((2 * c, 1), 0) >> 6) == (_iota((1, 2 * c), 1) >> 6)
    strict = same & (r2 > c2)
    incl = same & (r2 >= c2)
    bd = _pair_blockdiag_mask()
    is_a = lane < HEAD
    a_log = tp_ref[0:1]
    dt_bias = tp_ref[1:2]
    is_beta = lane < nh
    is_g = (lane >= nh) & (lane < 2 * nh)

    @pl.when(ci == 0)
    def _():
        for bi in range(bt):
            for pr in range(3):
                s_scr[bi, pr] = _load_pair_state(s0_ref, bi, pr)
            cbuf[bi, 0:8] = jnp.zeros((8, qkvw), F32)
            cbuf[bi, 8 - (CONV_K - 1):8] = conv0_ref[bi]

    convs = []
    for bi in range(bt):
        cbuf[bi, 8:8 + c] = p_ref[bi, :, 0:qkvw]
        conv = None
        for j in range(CONV_K):
            t = cw_ref[j:j + 1] * cbuf[bi, pl.ds(8 - (CONV_K - 1) + j, c), :]
            conv = t if conv is None else conv + t
        cbuf[bi, 0:8] = cbuf[bi, c:c + 8]
        convs.append(conv)
    conv = jnp.concatenate(convs, axis=0)
    p = p_ref[...].reshape(bt * c, p_ref.shape[2])
    qkv = conv * _sigmoid(conv)
    q = qkv[:, 0:bw]
    k = qkv[:, bw:2 * bw]
    v = qkv[:, 2 * bw:3 * bw]
    z = p[:, qkvw:qkvw + bw]
    tail = p[:, qkvw + bw:qkvw + bw + PAIR]
    q = q * (lax.rsqrt(_seg_sum(q * q, gpair) + L2_EPS) * (HEAD ** -0.5))
    k = k * lax.rsqrt(_seg_sum(k * k, gpair) + L2_EPS)
    beta = jnp.where(is_beta, _sigmoid(tail), 0.0)
    gl = jnp.where(is_g, -jnp.exp(a_log) * _softplus(tail + dt_bias), 0.0)
    gcum128 = jnp.concatenate([_dot_left01(tri, gl[c * bi:c * (bi + 1)]) for bi in range(bt)], axis=0)
    gcum = _dot_sel(gcum128, gsel_ref[...])
    beta = _dot_sel(beta, bsel_ref[...])
    g_last = jnp.concatenate(
        [jnp.broadcast_to(gcum[c * (bi + 1) - 1:c * (bi + 1)], (c, bw)) for bi in range(bt)], axis=0)
    eg = jnp.exp(gcum)
    kb = k * beta
    pre = dict(q=q, k=k, kb=kb, gcum=gcum, kg=kb * eg, qg=q * eg, kd=k * jnp.exp(g_last - gcum),
               vb=v * beta, d_end=jnp.exp(g_last))

    chains = [(bi, pr) for bi in range(bt) for pr in range(3)]
    pair = lambda name: [pre[name][c * bi:c * (bi + 1), PAIR * pr:PAIR * (pr + 1)] for bi, pr in chains]
    q, k, kb, gcum, kg, qg = pair("q"), pair("k"), pair("kb"), pair("gcum"), pair("kg"), pair("qg")
    kd, vb = pair("kd"), pair("vb")
    d_end = [x[0:1] for x in pair("d_end")]

    def col(x):
        sw = pltpu.roll(x, HEAD, axis=1)
        return jnp.concatenate([jnp.where(is_a, x, sw), jnp.where(is_a, sw, x)], axis=0)

    gcol = [col(x) for x in gcum]
    diff = [x - x.T for x in gcol]
    dec = [jnp.exp(jnp.where(incl, x, 0.0)) for x in diff]
    dec_s = [jnp.where(strict, x, 0.0) for x in dec]
    dec_i = [jnp.where(incl, x, 0.0) for x in dec]
    sc = [_dot(jnp.concatenate([_stack_heads(x), _stack_heads(y)], axis=0),
               jnp.concatenate([w, w], axis=0), _NT)
          for x, y, w in zip(kb, q, k)]
    tinv = _tri_inverse_many([-(x[0:2 * c] * d) for x, d in zip(sc, dec_s)], c)
    qk = [x[2 * c:4 * c] * d for x, d in zip(sc, dec_i)]
    s = [s_scr[bi, pr] for bi, pr in chains]
    kqs = [_dot(jnp.concatenate([x, y], axis=0), s_) for x, y, s_ in zip(kg, qg, s)]
    v_new = [_dot(t, _stack_heads(x - y[0:c])) for t, x, y in zip(tinv, vb, kqs)]
    o_st = [_dot(x, y) for x, y in zip(qk, v_new)]
    ds = [_dot(x, _fold_heads(y, c), _TN) for x, y in zip(kd, v_new)]
    for i, (bi, pr) in enumerate(chains):
        s_scr[bi, pr] = s[i] * d_end[i] + jnp.where(bd, ds[i], 0.0)

    o = jnp.concatenate(
        [jnp.concatenate([kqs[3 * bi + pr][c:2 * c] + _fold_heads(o_st[3 * bi + pr], c) for pr in range(3)],
                         axis=1) for bi in range(bt)], axis=0)
    o = o * lax.rsqrt(_seg_sum(o * o, gpair) * (1.0 / HEAD) + RMS_EPS) * nw_ref[...]
    y_ref[...] = (o * (z * _sigmoid(z))).reshape(bt, c, bw)

    @pl.when(ci == nci - 1)
    def _():
        for bi in range(bt):
            for pr in range(3):
                _store_pair_state(sout_ref, bi, pr, s_scr[bi, pr])


def _gdn_call(p_b, conv0, s0, cw, tailp, nw, gmat, bsel, gsel, *, bt):
    b, tp, bcols = p_b.shape
    nh = s0.shape[1]
    bw = nh * HEAD
    cin = CHUNK
    nc = tp // cin
    const = lambda a: pl.BlockSpec(a.shape, lambda i, j: (0,) * a.ndim)
    kern = functools.partial(_gdn_kernel, bt=bt)
    return pl.pallas_call(
        kern,
        grid=(b // bt, nc),
        in_specs=[
            pl.BlockSpec((bt, cin, bcols), lambda i, j: (i, j, 0)),
            pl.BlockSpec((bt, CONV_K - 1, 3 * bw), lambda i, j: (i, 0, 0)),
            pl.BlockSpec((bt, nh, HEAD, HEAD), lambda i, j: (i, 0, 0, 0)),
            const(cw), const(tailp), const(nw), const(gmat), const(bsel), const(gsel),
        ],
        out_specs=[
            pl.BlockSpec((bt, cin, bw), lambda i, j: (i, j, 0)),
            pl.BlockSpec((bt, nh, HEAD, HEAD), lambda i, j: (i, 0, 0, 0)),
        ],
        out_shape=[jax.ShapeDtypeStruct((b, tp, bw), F32),
                   jax.ShapeDtypeStruct(s0.shape, F32)],
        scratch_shapes=[pltpu.VMEM((bt, nh // 2, PAIR, PAIR), F32),
                        pltpu.VMEM((bt, CHUNK + 8, 3 * bw), F32)],
        compiler_params=pltpu.CompilerParams(
            dimension_semantics=("parallel", "arbitrary"), vmem_limit_bytes=VMEM_LIMIT),
        name="gdn",
    )(p_b, conv0, s0, cw, tailp, nw, gmat, bsel, gsel)


def _hgrn_kernel(p_ref, s0_ref, lbl_ref, nw_ref, g_ref, y_ref, sout_ref, s_scr,
                 *, bt, layer):
    c = CHUNK
    ci = pl.program_id(1)
    nci = pl.num_programs(1)
    cw = 2 * PAIR
    rows = _iota((c, 1), 0)
    cols = _iota((1, c), 1)
    sameblk = (rows >> 4) == (cols >> 4)
    tri_tot = jnp.concatenate([sameblk & (rows >= cols), sameblk], axis=0).astype(BF16)
    rows_b = _iota((HBLK, 1), 0)
    bd = _pair_blockdiag_mask()
    gpair = g_ref[...]

    logits = lbl_ref[...]
    ex = jnp.exp(logits - jnp.max(logits, axis=0, keepdims=True))
    gam = ex / jnp.sum(ex, axis=0, keepdims=True)
    lb = jnp.sum(gam[0:layer + 1], axis=0, keepdims=True) - gam[0:1]

    @pl.when(ci == 0)
    def _():
        for bi in range(bt):
            for pr in range(2):
                s_scr[bi, pr] = _load_pair_state(s0_ref, bi, pr).T

    pre = []
    for bi in range(bt):
        p = p_ref[bi]
        qp = p[:, 0:cw]
        f = p[:, cw:2 * cw]
        v = p[:, 2 * cw:3 * cw]
        z = p[:, 3 * cw:4 * cw]
        q = qp * _sigmoid(qp)
        logf = jnp.log(lb + (1.0 - lb) * _sigmoid(f))
        k = (1.0 - lb) * _sigmoid(-f)
        cums = _dot_left01(tri_tot, logf)
        bl = cums[0:c]
        btot = cums[c:2 * c]
        pre.append(dict(q=q, k=k, v=v, z=z, bl=bl, q_in=q * jnp.exp(bl),
                        k_out=k * jnp.exp(btot - bl), d_blk=jnp.exp(btot)))

    chains = [(bi, pr) for bi in range(bt) for pr in range(2)]
    nblk = c // HBLK

    def blk_of(name, bi, pr, blk):
        return pre[bi][name][HBLK * blk:HBLK * (blk + 1), PAIR * pr:PAIR * (pr + 1)]

    intra = {}
    ds = {}
    for blk in range(nblk):
        for bi, pr in chains:
            qb, kb, vb, bb = (blk_of(n, bi, pr, blk) for n in ("q", "k", "v", "bl"))
            xs = []
            for j in range(HBLK):
                e = jnp.exp(jnp.minimum(bb - bb[j:j + 1], 0.0))
                xs.append(qb * e * kb[j:j + 1])
            att = _dot_sel(jnp.concatenate(xs, axis=0), gpair, pieces=1)
            acc = None
            for j in range(HBLK):
                t = jnp.where(rows_b >= j, att[HBLK * j:HBLK * (j + 1)], 0.0) * vb[j:j + 1]
                acc = t if acc is None else acc + t
            intra[bi, pr, blk] = acc
            ds[bi, pr, blk] = jnp.where(bd, _dot(vb, blk_of("k_out", bi, pr, blk), _TN), 0.0)

    s = {ch: s_scr[ch] for ch in chains}
    outs = {}
    for blk in range(nblk):
        for bi, pr in chains:
            outs[bi, pr, blk] = intra[bi, pr, blk] + _dot(blk_of("q_in", bi, pr, blk), s[bi, pr], _NT)
            s[bi, pr] = s[bi, pr] * blk_of("d_blk", bi, pr, blk)[0:1] + ds[bi, pr, blk]
    for ch in chains:
        s_scr[ch] = s[ch]

    o = jnp.concatenate(
        [jnp.concatenate([outs[bi, pr, blk] for bi in range(bt) for blk in range(nblk)], axis=0)
         for pr in range(2)], axis=1)
    z = jnp.concatenate([pre[bi]["z"] for bi in range(bt)], axis=0)
    o = o * lax.rsqrt(_seg_sum(o * o, gpair) * (1.0 / HEAD) + RMS_EPS) * nw_ref[...]
    y_ref[...] = (o * _sigmoid(z)).reshape(bt, c, cw)

    @pl.when(ci == nci - 1)
    def _():
        for bi in range(bt):
            for pr in range(2):
                _store_pair_state(sout_ref, bi, pr, s_scr[bi, pr].T)


def _hgrn_call(p_c, s0, lbl, nw, gmat, *, bt, layer):
    b, tp, ccols = p_c.shape
    nh = s0.shape[1]
    cw = nh * HEAD
    cin = CHUNK
    nc = tp // cin
    const = lambda a: pl.BlockSpec(a.shape, lambda i, j: (0,) * a.ndim)
    kern = functools.partial(_hgrn_kernel, bt=bt, layer=layer)
    return pl.pallas_call(
        kern,
        grid=(b // bt, nc),
        in_specs=[
            pl.BlockSpec((bt, cin, ccols), lambda i, j: (i, j, 0)),
            pl.BlockSpec((bt, nh, HEAD, HEAD), lambda i, j: (i, 0, 0, 0)),
            const(lbl), const(nw), const(gmat),
        ],
        out_specs=[
            pl.BlockSpec((bt, cin, cw), lambda i, j: (i, j, 0)),
            pl.BlockSpec((bt, nh, HEAD, HEAD), lambda i, j: (i, 0, 0, 0)),
        ],
        out_shape=[jax.ShapeDtypeStruct((b, tp, cw), F32),
                   jax.ShapeDtypeStruct(s0.shape, F32)],
        scratch_shapes=[pltpu.VMEM((bt, nh // 2, PAIR, PAIR), F32)],
        compiler_params=pltpu.CompilerParams(
            dimension_semantics=("parallel", "arbitrary"), vmem_limit_bytes=VMEM_LIMIT),
        name="hgrn",
    )(p_c, s0, lbl, nw, gmat)


def _inproj_t_kernel(x_ref, nw_ref, sc_ref, sh_ref, hprev_ref, wta_ref, wtb_ref, wtc_ref,
                     pta_ref, ptb_ref, ptc_ref, rows_ref, hlast_ref, pprev_ref):
    h = _norm_mod(x_ref[...], nw_ref[...], sc_ref[...], sh_ref[...])
    hb = h.astype(BF16)
    nt = lambda w, a: lax.dot_general(w, a, _NT, preferred_element_type=F32)
    pta_ref[...] = nt(wta_ref[...], hb)
    ptb_ref[...] = nt(wtb_ref[...], hb)
    ptc_ref[...] = nt(wtc_ref[...], hb)
    rows_ref[...] = nt(hb, wtb_ref[0:rows_ref.shape[1], :])
    nb = hlast_ref.shape[0]
    hlast_ref[...] = h[h.shape[0] - nb:]
    pprev_ref[...] = nt(wta_ref[...], hprev_ref[...].astype(BF16))


def _inproj_t_call(x, nw, scale, shift, h_prev, wta, wtb, wtc, qkvw):
    rows, d = x.shape
    nb = h_prev.shape[0]
    sds = jax.ShapeDtypeStruct
    return pl.pallas_call(
        _inproj_t_kernel,
        out_shape=[sds((wta.shape[0], rows), F32), sds((wtb.shape[0], rows), F32),
                   sds((wtc.shape[0], rows), F32), sds((rows, qkvw), F32),
                   sds((nb, d), F32), sds((wta.shape[0], nb), F32)],
        compiler_params=pltpu.CompilerParams(vmem_limit_bytes=VMEM_LIMIT),
        name="inproj_t",
    )(x, nw, scale, shift, h_prev, wta, wtb, wtc)


def _col_sum(x):
    return jnp.sum(x, axis=0, keepdims=True)


def _rwkv_t_kernel(r_ref, k_ref, v_ref, tail_ref, rp_ref, kp_ref, vp_ref, tailp_ref,
                   mur_ref, muk_ref, muv_ref, mut_ref, vec_ref, wl_ref, s0_ref,
                   y_ref, sout_ref, v_scr, y_scr):
    nb = s0_ref.shape[-1]
    nt = r_ref.shape[-1] // nb
    w0, a0, k_k, k_a, r_k, ln_w, ln_b = (vec_ref[i, 0] for i in range(7))
    sout_ref[...] = s0_ref[...]

    def shifted(ref, pref, mu, idx, t):
        cur = ref[idx, :, t * nb:(t + 1) * nb]
        prv = pref[idx] if t == 0 else ref[idx, :, (t - 1) * nb:t * nb]
        return cur + mu * (prv - cur)

    for t in range(nt):
        r = shifted(r_ref, rp_ref, mur_ref[0], 0, t)
        k = shifted(k_ref, kp_ref, muk_ref[0], 0, t)
        v = shifted(v_ref, vp_ref, muv_ref[0], 0, t)
        t0 = shifted(tail_ref, tailp_ref, mut_ref[0], 0, t)
        t1 = shifted(tail_ref, tailp_ref, mut_ref[1], 1, t)
        act = jnp.concatenate([jnp.tanh(t0[0:32]), t0[32:64], _sigmoid(t1)], axis=0)
        lo = jnp.dot(wl_ref[0], act.astype(BF16), preferred_element_type=F32)
        w = -_softplus(-(w0 + lo[0:HEAD])) - 0.5
        dec = jnp.exp(-jnp.exp(w))
        a = _sigmoid(a0 + lo[HEAD:2 * HEAD])
        g = lo[2 * HEAD:3 * HEAD]
        kk = k * k_k
        kk = kk * lax.rsqrt(_col_sum(kk * kk) + L2_EPS)
        k = k * (1.0 + (a - 1.0) * k_a)
        a_t = -kk
        b_t = kk * a
        v_scr[...] = v

        def body(g, carry):
            base = pl.multiple_of(g * 8, 8)
            vg = v_scr[pl.ds(base, 8), :]
            ys = []
            for j in range(8):
                sv = sout_ref[0, base + j]
                sa = _col_sum(sv * a_t)
                sv = sv * dec + sa * b_t + vg[j:j + 1] * k
                sout_ref[0, base + j] = sv
                ys.append(_col_sum(sv * r))
            y_scr[pl.ds(base, 8), :] = jnp.concatenate(ys, axis=0)
            return carry

        lax.fori_loop(0, HEAD // 8, body, 0)
        y = y_scr[...]
        mean = _col_sum(y) * (1.0 / HEAD)
        dy = y - mean
        var = _col_sum(dy * dy) * (1.0 / HEAD)
        yn = dy * lax.rsqrt(var + GN_EPS) * ln_w + ln_b
        bonus = _col_sum(r * k * r_k) * v
        y_ref[0, :, t * nb:(t + 1) * nb] = (yn + bonus) * g


def _head_blocks(arr3, first, n=1):
    if n == 1:
        return pl.BlockSpec((1,) + arr3.shape[1:], lambda h: (first + h, 0, 0))
    return pl.BlockSpec((n,) + arr3.shape[1:], lambda h: (first // n, 0, 0))


def _layer_state_specs(s_all, layer):
    nh, nb = s_all.shape[1], s_all.shape[-1]
    spec_in = pl.BlockSpec((None, 1, HEAD, HEAD, nb), lambda h: (layer, h, 0, 0, 0))
    spec_out = pl.BlockSpec((1, HEAD, HEAD, nb), lambda h: (h, 0, 0, 0))
    return spec_in, spec_out, jax.ShapeDtypeStruct((nh, HEAD, HEAD, nb), F32)


def _rwkv_t_call(pta, pprev_t, s_all, layer, mu, vec, wl_t):
    nh, nb = s_all.shape[1], s_all.shape[-1]
    rows = pta.shape[1]
    p3 = pta.reshape(-1, HEAD, rows)
    pp3 = pprev_t.reshape(-1, HEAD, nb)
    mu3 = mu.reshape(-1, HEAD, 1)
    tail0 = 3 * nh
    state, state_out, state_shape = _layer_state_specs(s_all, layer)
    return pl.pallas_call(
        _rwkv_t_kernel,
        grid=(nh,),
        in_specs=[
            _head_blocks(p3, 0), _head_blocks(p3, nh), _head_blocks(p3, 2 * nh), _head_blocks(p3, tail0, 2),
            _head_blocks(pp3, 0), _head_blocks(pp3, nh), _head_blocks(pp3, 2 * nh), _head_blocks(pp3, tail0, 2),
            _head_blocks(mu3, 0), _head_blocks(mu3, nh), _head_blocks(mu3, 2 * nh), _head_blocks(mu3, tail0, 2),
            pl.BlockSpec((vec.shape[0], 1, HEAD, 1), lambda h: (0, h, 0, 0)),
            pl.BlockSpec((1,) + wl_t.shape[1:], lambda h: (h, 0, 0)),
            state,
        ],
        out_specs=[pl.BlockSpec((1, HEAD, rows), lambda h: (h, 0, 0)), state_out],
        out_shape=[jax.ShapeDtypeStruct((nh, HEAD, rows), F32), state_shape],
        scratch_shapes=[pltpu.VMEM((HEAD, nb), F32), pltpu.VMEM((HEAD, nb), F32)],
        compiler_params=pltpu.CompilerParams(
            dimension_semantics=("parallel",), vmem_limit_bytes=VMEM_LIMIT),
        name="rwkv_t",
    )(p3, p3, p3, p3, pp3, pp3, pp3, pp3, mu3, mu3, mu3, mu3, vec, wl_t, s_all)


def _gdn_t_kernel(q_ref, k_ref, v_ref, z_ref, tail_ref, cq_ref, ck_ref, cv_ref, wq_ref, wk_ref, wv_ref,
                  sc_ref, nw_ref, s0_ref, y_ref, sout_ref, k_scr, q_scr):
    h = pl.program_id(0)
    nh = pl.num_programs(0)
    nb = s0_ref.shape[-1]
    nt = q_ref.shape[-1] // nb
    a_log = sc_ref[0, 0:1, 0:1]
    dt_bias = sc_ref[0, 1:2, 0:1]
    sout_ref[...] = s0_ref[...]

    def conv(x_ref, c_ref, w_ref, t):
        acc = None
        for j in range(CONV_K):
            u = t + j - (CONV_K - 1)
            tap = c_ref[u + CONV_K - 1, 0] if u < 0 else x_ref[0, :, u * nb:(u + 1) * nb]
            term = w_ref[j, 0] * tap
            acc = term if acc is None else acc + term
        return acc * _sigmoid(acc)

    for t in range(nt):
        sl = slice(t * nb, (t + 1) * nb)
        q = conv(q_ref, cq_ref, wq_ref, t)
        k = conv(k_ref, ck_ref, wk_ref, t)
        v = conv(v_ref, cv_ref, wv_ref, t)
        q = q * (lax.rsqrt(_col_sum(q * q) + L2_EPS) * (HEAD ** -0.5))
        k = k * lax.rsqrt(_col_sum(k * k) + L2_EPS)
        tl = tail_ref[0:2 * 8, sl]
        pick = lambda row: _col_sum(jnp.where(_iota((2 * 8, 1), 0) == row, tl, 0.0))
        beta = _sigmoid(pick(h))
        eg = jnp.exp(-jnp.exp(a_log) * _softplus(pick(nh + h) + dt_bias))
        k_scr[...] = k
        q_scr[...] = q

        def body1(g, acc):
            base = pl.multiple_of(g * 8, 8)
            kg = k_scr[pl.ds(base, 8), :]
            for j in range(8):
                acc = acc + sout_ref[0, base + j] * kg[j:j + 1]
            return acc

        sk = lax.fori_loop(0, HEAD // 8, body1, jnp.zeros((HEAD, nb), F32))
        v_new = beta * (v - eg * sk)

        def body2(g, acc):
            base = pl.multiple_of(g * 8, 8)
            kg = k_scr[pl.ds(base, 8), :]
            qg = q_scr[pl.ds(base, 8), :]
            for j in range(8):
                s = eg * sout_ref[0, base + j] + kg[j:j + 1] * v_new
                sout_ref[0, base + j] = s
                acc = acc + s * qg[j:j + 1]
            return acc

        o = lax.fori_loop(0, HEAD // 8, body2, jnp.zeros((HEAD, nb), F32))
        o = o * lax.rsqrt(_col_sum(o * o) * (1.0 / HEAD) + RMS_EPS) * nw_ref[...]
        z = z_ref[0, :, sl]
        y_ref[0, :, sl] = o * (z * _sigmoid(z))


def _gdn_t_call(ptb, conv_t, s_all, layer, cw, scal, nw):
    nh, nb = s_all.shape[1], s_all.shape[-1]
    rows = ptb.shape[1]
    p3 = ptb.reshape(-1, HEAD, rows)
    c4 = conv_t.reshape(CONV_K - 1, -1, HEAD, nb)
    w4 = cw.reshape(CONV_K, -1, HEAD, 1)
    tap = lambda first: pl.BlockSpec((CONV_K - 1, 1, HEAD, nb), lambda h: (0, first + h, 0, 0))
    wsp = lambda first: pl.BlockSpec((CONV_K, 1, HEAD, 1), lambda h: (0, first + h, 0, 0))
    state, state_out, state_shape = _layer_state_specs(s_all, layer)
    tail_blk = 4 * nh * HEAD // PAIR
    return pl.pallas_call(
        _gdn_t_kernel,
        grid=(nh,),
        in_specs=[
            _head_blocks(p3, 0), _head_blocks(p3, nh), _head_blocks(p3, 2 * nh), _head_blocks(p3, 3 * nh),
            pl.BlockSpec((PAIR, rows), lambda h: (tail_blk, 0)),
            tap(0), tap(nh), tap(2 * nh), wsp(0), wsp(nh), wsp(2 * nh),
            pl.BlockSpec((1,) + scal.shape[1:], lambda h: (h, 0, 0)),
            pl.BlockSpec(nw.shape, lambda h: (0, 0)),
            state,
        ],
        out_specs=[pl.BlockSpec((1, HEAD, rows), lambda h: (h, 0, 0)), state_out],
        out_shape=[jax.ShapeDtypeStruct((nh, HEAD, rows), F32), state_shape],
        scratch_shapes=[pltpu.VMEM((HEAD, nb), F32), pltpu.VMEM((HEAD, nb), F32)],
        compiler_params=pltpu.CompilerParams(
            dimension_semantics=("parallel",), vmem_limit_bytes=VMEM_LIMIT),
        name="gdn_t",
    )(p3, p3, p3, p3, ptb, c4, c4, c4, w4, w4, w4, scal, nw, s_all)


def _hgrn_t_kernel(q_ref, f_ref, v_ref, z_ref, lbl_ref, nw_ref, s0_ref, y_ref, sout_ref,
                   f_scr, k_scr, q_scr, *, layer):
    nb = s0_ref.shape[-1]
    nt = q_ref.shape[-1] // nb
    logits = lbl_ref[:, 0]
    ex = jnp.exp(logits - jnp.max(logits, axis=0, keepdims=True))
    gam = ex / jnp.sum(ex, axis=0, keepdims=True)
    lb = jnp.sum(gam[0:layer + 1], axis=0) - gam[0]
    sout_ref[...] = s0_ref[...]

    for t in range(nt):
        sl = slice(t * nb, (t + 1) * nb)
        qp = q_ref[0, :, sl]
        f = f_ref[0, :, sl]
        v = v_ref[0, :, sl]
        z = z_ref[0, :, sl]
        q_scr[...] = qp * _sigmoid(qp)
        f_scr[...] = lb + (1.0 - lb) * _sigmoid(f)
        k_scr[...] = (1.0 - lb) * _sigmoid(-f)

        def body(g, acc):
            base = pl.multiple_of(g * 8, 8)
            fg, kg, qg = (ref[pl.ds(base, 8), :] for ref in (f_scr, k_scr, q_scr))
            for j in range(8):
                s = fg[j:j + 1] * sout_ref[0, base + j] + kg[j:j + 1] * v
                sout_ref[0, base + j] = s
                acc = acc + qg[j:j + 1] * s
            return acc

        o = lax.fori_loop(0, HEAD // 8, body, jnp.zeros((HEAD, nb), F32))
        o = o * lax.rsqrt(_col_sum(o * o) * (1.0 / HEAD) + RMS_EPS) * nw_ref[...]
        y_ref[0, :, sl] = o * _sigmoid(z)


def _hgrn_t_call(ptc, s_all, lbl4, nw, layer):
    nh, nb = s_all.shape[1], s_all.shape[-1]
    rows = ptc.shape[1]
    p3 = ptc.reshape(-1, HEAD, rows)
    state, state_out, state_shape = _layer_state_specs(s_all, layer)
    return pl.pallas_call(
        functools.partial(_hgrn_t_kernel, layer=layer),
        grid=(nh,),
        in_specs=[
            _head_blocks(p3, 0), _head_blocks(p3, nh), _head_blocks(p3, 2 * nh), _head_blocks(p3, 3 * nh),
            pl.BlockSpec((lbl4.shape[0], 1, HEAD, 1), lambda h: (0, h, 0, 0)),
            pl.BlockSpec(nw.shape, lambda h: (0, 0)),
            state,
        ],
        out_specs=[pl.BlockSpec((1, HEAD, rows), lambda h: (h, 0, 0)), state_out],
        out_shape=[jax.ShapeDtypeStruct((nh, HEAD, rows), F32), state_shape],
        scratch_shapes=[pltpu.VMEM((HEAD, nb), F32)] * 3,
        compiler_params=pltpu.CompilerParams(
            dimension_semantics=("parallel",), vmem_limit_bytes=VMEM_LIMIT),
        name="hgrn_t",
    )(p3, p3, p3, p3, lbl4, nw, s_all)


def _ffn_kernel(x_ref, ya_ref, yb_ref, yc_ref, woa_ref, wob_ref, woc_ref, gm_ref, nw_ref,
                sc_ref, sh_ref, gf_ref, win_ref, wd_ref, fnw_ref, xo_ref, *yo_ref, th, y_transposed):
    ff = wd_ref.shape[0]
    dims = _TN if y_transposed else _NN
    proj = lambda y_ref, w_ref: lax.dot_general(y_ref[...].astype(BF16), w_ref[...], dims,
                                                preferred_element_type=F32)
    mix = proj(ya_ref, woa_ref) + proj(yb_ref, wob_ref) + proj(yc_ref, woc_ref)
    x1 = x_ref[...] + gm_ref[0] * mix
    h = _norm_mod(x1, nw_ref[...], sc_ref[0], sh_ref[0]).astype(BF16)
    acc = None
    for j in range(ff // th):
        gate = jnp.dot(h, win_ref[:, j * th:(j + 1) * th], preferred_element_type=F32)
        up = jnp.dot(h, win_ref[:, ff + j * th:ff + (j + 1) * th], preferred_element_type=F32)
        act = (gate * _sigmoid(gate) * up).astype(BF16)
        t = jnp.dot(act, wd_ref[j * th:(j + 1) * th, :], preferred_element_type=F32)
        acc = t if acc is None else acc + t
    xo = x1 + gf_ref[0] * acc
    xo_ref[...] = xo
    if yo_ref:
        yo_ref[0][...] = xo * lax.rsqrt(jnp.mean(xo * xo, axis=-1, keepdims=True) + RMS_EPS) * fnw_ref[...]


def _ffn_call(x, ya, yb, yc, w_out, gate_m, nw, scale_f, shift_f, gate_f,
              w_in, w_down, fnw, *, layer, tm, th, tiles_per_seq, final, y_transposed=False):
    rows, d = x.shape
    row = lambda a: pl.BlockSpec((tm, a.shape[1]), lambda i: (i, 0))
    yspec = (lambda a: pl.BlockSpec((a.shape[0], tm), lambda i: (0, i))) if y_transposed else row
    const = lambda a: pl.BlockSpec(a.shape, lambda i: (0, 0), pipeline_mode=pl.Buffered(1))
    of_layer = lambda a: pl.BlockSpec((None,) + a.shape[1:], lambda i: (layer, 0, 0),
                                      pipeline_mode=pl.Buffered(1))
    widths = [y.shape[0] if y_transposed else y.shape[1] for y in (ya, yb, yc)]
    starts = [0, widths[0], widths[0] + widths[1]]
    assert all(s % n == 0 for s, n in zip(starts, widths))
    wo_rows = lambda s, n: pl.BlockSpec((None, n, d), lambda i: (layer, s // n, 0), pipeline_mode=pl.Buffered(1))
    mod = lambda a: _mod_spec(a, tm, tiles_per_seq)
    n_out = 2 if final else 1
    return pl.pallas_call(
        functools.partial(_ffn_kernel, th=th, y_transposed=y_transposed),
        grid=(rows // tm,),
        in_specs=[
            row(x), yspec(ya), yspec(yb), yspec(yc), *[wo_rows(s, n) for s, n in zip(starts, widths)],
            mod(gate_m), const(nw), mod(scale_f), mod(shift_f), mod(gate_f),
            of_layer(w_in), of_layer(w_down), const(fnw),
        ],
        out_specs=[row(x)] * n_out,
        out_shape=[jax.ShapeDtypeStruct((rows, d), F32)] * n_out,
        compiler_params=pltpu.CompilerParams(
            dimension_semantics=("parallel",), vmem_limit_bytes=VMEM_LIMIT),
        name="ffn",
    )(x, ya, yb, yc, w_out, w_out, w_out, gate_m, nw, scale_f, shift_f, gate_f, w_in, w_down, fnw)


def _block_ones(n, blk):
    i = np.arange(n) // blk
    return jnp.asarray(i[:, None] == i[None, :], BF16)


def _head_select(first_lane, nh):
    m = np.zeros((PAIR, nh * HEAD), np.float32)
    for h in range(nh):
        m[first_lane + h, h * HEAD:(h + 1) * HEAD] = 1.0
    return jnp.asarray(m, BF16)


def _lane_place(x, first_lane):
    return jnp.zeros((PAIR,), F32).at[first_lane:first_lane + x.shape[0]].set(x)


def _run_chunked(x, mods, states, wts, *, tm, tm_ffn, th, bt):
    b, t, d = x.shape
    rows = b * t
    assert t % CHUNK == 0 and t % tm == 0 and t % tm_ffn == 0 and b % bt == 0 and t >= CONV_K - 1
    shift0, rwkv0, conv0, gdn0, hgrn0 = states
    nl = len(wts)
    new = ([], [], [], [], [])
    xr = x.reshape(rows, d)
    for l in range(nl):
        w = wts[l]
        m = mods[l]
        shift_m, scale_m, gate_m, shift_f, scale_f, gate_f = (m[:, i].reshape(b, 1, d) for i in range(6))
        p_a, p_b, p_c = _inproj_call(xr, w["norm_mix"], scale_m, shift_m, w["wta"], w["wtb"], w["wtc"],
                                     tm, t // tm)
        h_last, p_prev = _lastrow_call(xr.reshape(b, t, d)[:, t - 1], w["norm_mix"], m[:, 1], m[:, 0],
                                       shift0[l], w["wta"])
        p_a = p_a.reshape(b, t, -1)
        p_b = p_b.reshape(b, t, -1)
        p_c = p_c.reshape(b, t, -1)
        y_a, s_a = _rwkv_call(p_a, p_prev, rwkv0[l], w["mu"], w["rwkv_vec"], w["lora"], w["gpair"], bt=bt)
        y_b, s_b = _gdn_call(p_b, conv0[l], gdn0[l], w["conv_w"], w["gdn_tail"], w["gdn_norm"],
                             w["gpair"], w["bsel"], w["gsel"], bt=bt)
        y_c, s_c = _hgrn_call(p_c, hgrn0[l], w["lb_logits"], w["hgrn_norm"], w["gpair"], bt=bt, layer=l)
        conv_new = p_b[:, t - (CONV_K - 1):, :conv0.shape[-1]]
        res = _ffn_call(xr, y_a.reshape(rows, -1), y_b.reshape(rows, -1), y_c.reshape(rows, -1),
                        w["w_out"], gate_m, w["norm_ffn"], scale_f, shift_f, gate_f,
                        w["w_ffn_in"], w["w_ffn_out"], w["final_norm"],
                        layer=l, tm=tm_ffn, th=th, tiles_per_seq=t // tm_ffn, final=l == nl - 1)
        xr = res[0]
        for acc, s in zip(new, (h_last, s_a, conv_new, s_b, s_c)):
            acc.append(s)
    return res[1].reshape(b, t, d), [jnp.stack(acc) for acc in new]


def _run_steps(x, mods, states, wts, *, tm_ffn, th):
    b, t, d = x.shape
    rows = t * b
    tm_ffn = min(tm_ffn, rows)
    shift0, rwkv0, conv0, gdn0, hgrn0 = states
    to_lanes = lambda s: jnp.transpose(s, (0, 2, 3, 4, 1))
    rwkv_t, gdn_t, hgrn_t = to_lanes(rwkv0), to_lanes(gdn0), to_lanes(hgrn0)
    conv_t = jnp.transpose(conv0, (0, 2, 3, 1))
    nl = len(wts)
    new = ([], [], [], [], [])
    xr = jnp.transpose(x, (1, 0, 2)).reshape(rows, d)
    for l in range(nl):
        w = wts[l]
        m = mods[l]
        mod = lambda i, m=m: jnp.tile(m[:, i], (t, 1))
        tiled = lambda a: a.reshape(rows // tm_ffn, tm_ffn, d)
        pta, ptb, ptc, qkv_rows, h_last, pprev_t = _inproj_t_call(
            xr, w["norm_mix"], mod(1), mod(0), shift0[l], w["wta"], w["wtb"], w["wtc"], w["qkvw"])
        y_a, s_a = _rwkv_t_call(pta, pprev_t, rwkv_t, l, w["mu"], w["rwkv_vec_t"], w["lora_t"])
        y_b, s_b = _gdn_t_call(ptb, conv_t[l], gdn_t, l, w["conv_w_t"], w["gdn_scal"], w["gdn_norm_t"])
        y_c, s_c = _hgrn_t_call(ptc, hgrn_t, w["lb_logits_t"], w["hgrn_norm_t"], l)
        xp = jnp.concatenate([jnp.transpose(conv0[l], (1, 0, 2)), qkv_rows.reshape(t, b, -1)], axis=0)
        conv_new = jnp.transpose(xp[-(CONV_K - 1):], (1, 0, 2))
        res = _ffn_call(xr, y_a.reshape(-1, rows), y_b.reshape(-1, rows), y_c.reshape(-1, rows),
                        w["w_out"], tiled(mod(2)), w["norm_ffn"],
                        tiled(mod(4)), tiled(mod(3)), tiled(mod(5)),
                        w["w_ffn_in"], w["w_ffn_out"], w["final_norm"],
                        layer=l, tm=tm_ffn, th=th, tiles_per_seq=1, final=l == nl - 1, y_transposed=True)
        xr = res[0]
        for acc, s in zip(new, (h_last, s_a, conv_new, s_b, s_c)):
            acc.append(s)
    y = jnp.transpose(res[1].reshape(t, b, d), (1, 0, 2))
    from_lanes = lambda s: jnp.transpose(jnp.stack(s), (0, 4, 1, 2, 3))
    return y, [jnp.stack(new[0]), from_lanes(new[1]), jnp.stack(new[2]), from_lanes(new[3]), from_lanes(new[4])]


def kernel(x_prompt, x_sample, c_prompt, c_sample, state_rwkv_shift, state_rwkv, state_gdn_conv, state_gdn, state_hgrn, w_ada, b_ada, norm_mix_w, w_in, rwkv_mu, rwkv_w0, rwkv_w2, rwkv_a0, rwkv_a2, rwkv_g2, rwkv_k_k, rwkv_k_a, rwkv_r_k, rwkv_ln_w, rwkv_ln_b, gdn_conv_w, gdn_A_log, gdn_dt_bias, gdn_norm_w, hgrn_lb_logits, hgrn_norm_w, w_out, norm_ffn_w, w_ffn_in, w_ffn_out, final_norm_w):
    nl, d, _ = w_in.shape
    a_heads = state_rwkv.shape[2]
    b_heads = state_gdn.shape[2]
    c_heads = state_hgrn.shape[2]
    aw, bw, cw = a_heads * HEAD, b_heads * HEAD, c_heads * HEAD
    lw, la, lg = rwkv_w2.shape[1], rwkv_a2.shape[1], rwkv_g2.shape[1]
    a_cols = 3 * aw + lw + la + lg
    qkvw = 3 * bw
    b_cols = qkvw + 2 * b_heads + bw
    assert (aw, bw, cw) == (3 * PAIR, 3 * PAIR, 2 * PAIR) and lw + la + lg == PAIR and lw == 32 and la == 32

    bp, tpr, _ = x_prompt.shape
    bs, ts, _ = x_sample.shape

    gpair = _block_ones(PAIR, HEAD)
    bsel = _head_select(0, b_heads)
    gsel = _head_select(b_heads, b_heads)
    w_out_bf, w_ffn_in_bf, w_ffn_out_bf = (a.astype(BF16) for a in (w_out, w_ffn_in, w_ffn_out))
    wts = []
    for l in range(nl):
        wit = jnp.swapaxes(w_in[l], 0, 1).astype(BF16)
        wbt = wit[a_cols:a_cols + b_cols]
        wtb = jnp.concatenate(
            [wbt[:qkvw], wbt[qkvw + 2 * b_heads:], wbt[qkvw:qkvw + 2 * b_heads],
             jnp.zeros((PAIR - 2 * b_heads, d), BF16)], axis=0)
        lora = jnp.zeros((PAIR, 3 * aw), F32)
        lora = lora.at[0:lw, 0:aw].set(rwkv_w2[l])
        lora = lora.at[lw:lw + la, aw:2 * aw].set(rwkv_a2[l])
        lora = lora.at[lw + la:, 2 * aw:].set(rwkv_g2[l])
        rwkv_vec = jnp.stack([rwkv_w0[l], rwkv_a0[l], rwkv_k_k[l], rwkv_k_a[l], rwkv_r_k[l].reshape(-1),
                              rwkv_ln_w[l], rwkv_ln_b[l], jnp.zeros((aw,), F32)])
        hcols = lambda a: a.reshape(a.shape[0], a_heads, HEAD).transpose(1, 2, 0)
        lora_t = jnp.concatenate([
            jnp.pad(hcols(rwkv_w2[l]), ((0, 0), (0, 0), (0, la + lg))),
            jnp.pad(hcols(rwkv_a2[l]), ((0, 0), (0, 0), (lw, lg))),
            jnp.pad(hcols(rwkv_g2[l]), ((0, 0), (0, 0), (lw + la, 0)))], axis=1)
        steps = dict(
            rwkv_vec_t=rwkv_vec[:7].reshape(7, a_heads, HEAD, 1), lora_t=lora_t.astype(BF16),
            conv_w_t=gdn_conv_w[l],
            gdn_scal=jnp.zeros((b_heads, 8, PAIR), F32)
                .at[:, 0].set(jnp.broadcast_to(gdn_A_log[l][:, None], (b_heads, PAIR)))
                .at[:, 1].set(jnp.broadcast_to(gdn_dt_bias[l][:, None], (b_heads, PAIR))),
            gdn_norm_t=gdn_norm_w[l].reshape(HEAD, 1),
            lb_logits_t=hgrn_lb_logits.reshape(nl, c_heads, HEAD, 1),
            hgrn_norm_t=hgrn_norm_w[l].reshape(HEAD, 1),
        )
        wts.append(dict(
            **steps,
            norm_mix=norm_mix_w[l].reshape(1, d),
            wta=wit[:a_cols], wtb=wtb, wtc=wit[a_cols + b_cols:], qkvw=qkvw,
            mu=rwkv_mu[l].reshape(1, a_cols), rwkv_vec=rwkv_vec, lora=lora.astype(BF16),
            gpair=gpair, bsel=bsel, gsel=gsel,
            conv_w=jnp.concatenate([gdn_conv_w[l], jnp.zeros((8 - CONV_K, qkvw), F32)], axis=0),
            gdn_tail=jnp.zeros((8, PAIR), F32).at[0].set(_lane_place(gdn_A_log[l], b_heads))
                                              .at[1].set(_lane_place(gdn_dt_bias[l], b_heads)),
            gdn_norm=jnp.tile(gdn_norm_w[l], b_heads).reshape(1, bw),
            lb_logits=hgrn_lb_logits,
            hgrn_norm=jnp.tile(hgrn_norm_w[l], c_heads).reshape(1, cw),
            w_out=w_out_bf, norm_ffn=norm_ffn_w[l].reshape(1, d),
            w_ffn_in=w_ffn_in_bf, w_ffn_out=w_ffn_out_bf,
            final_norm=final_norm_w.reshape(1, d),
        ))

    mod = _ada_call(jnp.concatenate([c_prompt, c_sample], axis=0), w_ada, b_ada)
    mod = mod.reshape(nl, bp + bs, 6, d)
    mods_p = [mod[l, :bp] for l in range(nl)]
    mods_s = [mod[l, bp:] for l in range(nl)]

    zeros = lambda s: jnp.zeros((nl, bp) + s.shape[2:], s.dtype)
    states_p = tuple(zeros(s) for s in (state_rwkv_shift, state_rwkv, state_gdn_conv, state_gdn, state_hgrn))
    states_s = (state_rwkv_shift, state_rwkv, state_gdn_conv, state_gdn, state_hgrn)

    th = w_ffn_out.shape[1] // FFN_HIDDEN_TILES
    y_p, new_p = _run_chunked(x_prompt, mods_p, states_p, wts, tm=TM_INPROJ, tm_ffn=TM_FFN, th=th, bt=SEQS_PER_STEP)
    y_s, new_s = _run_steps(x_sample, mods_s, states_s, wts, tm_ffn=TM_FFN_STEPS, th=th)
    return (y_p, y_s, *new_p, *new_s)
```

```python
import functools

import numpy as np
import jax
import jax.numpy as jnp
from jax import lax
from jax.experimental import pallas as pl
from jax.experimental.pallas import tpu as pltpu

F32 = jnp.float32
BF16 = jnp.bfloat16

HEAD = 64
PAIR = 2 * HEAD
CHUNK = 64
HBLK = 16
INV_BASE = 16
CONV_K = 4
RMS_EPS = 1e-6
L2_EPS = 1e-6
GN_EPS = 64e-5
VMEM_LIMIT = 56 * 1024 * 1024

TM_INPROJ = 512
TM_FFN = 512
TM_FFN_STEPS = 256
FFN_HIDDEN_TILES = 2
SEQS_PER_STEP = 8

_NN = (((1,), (0,)), ((), ()))
_NT = (((1,), (1,)), ((), ()))
_TN = (((0,), (0,)), ((), ()))


def _dot(a, b, dims=_NN):
    return lax.dot_general(a.astype(BF16), b.astype(BF16), dims, preferred_element_type=F32)


def _dot_sel(x, m, pieces=3):
    ps = _bf16_pieces(x, pieces)
    return lax.dot_general(jnp.concatenate(ps, axis=1), jnp.concatenate([m] * pieces, axis=0), _NN,
                           preferred_element_type=F32)


def _bf16_pieces(x, pieces):
    ps = []
    rem = x
    for i in range(pieces):
        p = rem.astype(BF16)
        ps.append(p)
        if i + 1 < pieces:
            rem = rem - p.astype(F32)
    return ps


def _seg_sum(x, gpair, pieces=2):
    return jnp.concatenate(
        [_dot_sel(x[:, i:i + PAIR], gpair, pieces) for i in range(0, x.shape[1], PAIR)], axis=1)


def _sigmoid(x):
    return jax.nn.sigmoid(x)


def _softplus(x):
    return jnp.maximum(x, 0.0) + jnp.log1p(jnp.exp(-jnp.abs(x)))


def _iota(shape, dim):
    return lax.broadcasted_iota(jnp.int32, shape, dim)


def _stack_heads(x):
    is_a = _iota((1, PAIR), 1) < HEAD
    return jnp.concatenate([jnp.where(is_a, x, 0.0), jnp.where(is_a, 0.0, x)], axis=0)


def _fold_heads(x, c):
    return x[0:c] + x[c:2 * c]


def _pair_blockdiag_mask():
    return (_iota((PAIR, 1), 0) >> 6) == (_iota((1, PAIR), 1) >> 6)


def _load_pair_state(s_ref, bi, pr):
    sa = s_ref[bi, 2 * pr]
    sb = s_ref[bi, 2 * pr + 1]
    z = jnp.zeros((HEAD, HEAD), F32)
    return jnp.concatenate(
        [jnp.concatenate([sa, z], axis=1), jnp.concatenate([z, sb], axis=1)], axis=0)


def _store_pair_state(s_ref, bi, pr, s):
    s_ref[bi, 2 * pr] = s[0:HEAD, 0:HEAD]
    s_ref[bi, 2 * pr + 1] = s[HEAD:PAIR, HEAD:PAIR]


def _dot_left01(m, x, pieces=3):
    return lax.dot_general(jnp.concatenate([m] * pieces, axis=1),
                           jnp.concatenate(_bf16_pieces(x, pieces), axis=0), _NN,
                           preferred_element_type=F32)


def _tri_inverse_many(ns, c):
    size = ns[0].shape[0]
    r = _iota((size, 1), 0)
    cc = _iota((1, size), 1)
    eye = (r == cc).astype(F32)
    sh = INV_BASE.bit_length() - 1
    diag = [jnp.where((r >> sh) == (cc >> sh), n, 0.0) for n in ns]
    invs = [eye + d for d in diag]
    pws = [_dot(d, d) for d in diag]
    span = 4
    while span < INV_BASE:
        both = [_dot(jnp.concatenate([inv, pw], axis=0), pw) for inv, pw in zip(invs, pws)]
        invs = [inv + b[0:size] for inv, b in zip(invs, both)]
        pws = [b[size:2 * size] for b in both]
        span *= 2
    invs = [inv + _dot(inv, pw) for inv, pw in zip(invs, pws)]
    blk = INV_BASE
    while blk < c:
        sh = blk.bit_length() - 1
        off = ((r >> (sh + 1)) == (cc >> (sh + 1))) & ((r >> sh) != (cc >> sh))
        low = [_dot(inv, jnp.where(off, n, 0.0)) for inv, n in zip(invs, ns)]
        invs = [inv + _dot(t, inv) for inv, t in zip(invs, low)]
        blk *= 2
    return invs


def _ada_kernel(c_ref, w_ref, b_ref, o_ref):
    c = c_ref[...]
    o_ref[0] = _dot(c * _sigmoid(c), w_ref[0]) + b_ref[0]


def _ada_call(c_all, w_ada, b_ada):
    nl, d, n6 = w_ada.shape
    rows = c_all.shape[0]
    tn = n6 // 4
    return pl.pallas_call(
        _ada_kernel,
        grid=(nl, n6 // tn),
        in_specs=[
            pl.BlockSpec((rows, d), lambda l, j: (0, 0)),
            pl.BlockSpec((1, d, tn), lambda l, j: (l, 0, j)),
            pl.BlockSpec((1, 1, tn), lambda l, j: (l, 0, j)),
        ],
        out_specs=pl.BlockSpec((1, rows, tn), lambda l, j: (l, 0, j)),
        out_shape=jax.ShapeDtypeStruct((nl, rows, n6), F32),
        compiler_params=pltpu.CompilerParams(
            dimension_semantics=("parallel", "parallel"), vmem_limit_bytes=VMEM_LIMIT),
        name="ada",
    )(c_all, w_ada, b_ada.reshape(nl, 1, n6))


def _norm_mod(x, nw, scale, shift):
    y = x * lax.rsqrt(jnp.mean(x * x, axis=-1, keepdims=True) + RMS_EPS) * nw
    return y * (1.0 + scale) + shift


def _inproj_kernel(x_ref, nw_ref, sc_ref, sh_ref, wa_ref, wb_ref, wc_ref, pa_ref, pb_ref, pc_ref):
    h = _norm_mod(x_ref[...], nw_ref[...], sc_ref[0], sh_ref[0]).astype(BF16)
    pa_ref[...] = lax.dot_general(h, wa_ref[...], _NT, preferred_element_type=F32)
    pb_ref[...] = lax.dot_general(h, wb_ref[...], _NT, preferred_element_type=F32)
    pc_ref[...] = lax.dot_general(h, wc_ref[...], _NT, preferred_element_type=F32)


def _mod_spec(mod, tm, tiles_per_seq):
    if mod.shape[1] == 1:
        return pl.BlockSpec((1, 1, mod.shape[2]), lambda i, *_: (i // tiles_per_seq, 0, 0))
    return pl.BlockSpec((1, tm, mod.shape[2]), lambda i, *_: (i, 0, 0))


def _inproj_call(x, nw, scale, shift, wa, wb, wc, tm, tiles_per_seq):
    rows, d = x.shape
    full = lambda a: pl.BlockSpec(a.shape, lambda i: (0, 0), pipeline_mode=pl.Buffered(1))
    outs = [jax.ShapeDtypeStruct((rows, w.shape[0]), F32) for w in (wa, wb, wc)]
    return pl.pallas_call(
        _inproj_kernel,
        grid=(rows // tm,),
        in_specs=[
            pl.BlockSpec((tm, d), lambda i: (i, 0)),
            full(nw),
            _mod_spec(scale, tm, tiles_per_seq),
            _mod_spec(shift, tm, tiles_per_seq),
            full(wa), full(wb), full(wc),
        ],
        out_specs=[pl.BlockSpec((tm, w.shape[0]), lambda i: (i, 0)) for w in (wa, wb, wc)],
        out_shape=outs,
        compiler_params=pltpu.CompilerParams(
            dimension_semantics=("parallel",), vmem_limit_bytes=VMEM_LIMIT),
        name="inproj",
    )(x, nw, scale, shift, wa, wb, wc)


def _lastrow_kernel(x_ref, nw_ref, sc_ref, sh_ref, hprev_ref, wa_ref, h_ref, pprev_ref):
    h_ref[...] = _norm_mod(x_ref[...], nw_ref[...], sc_ref[...], sh_ref[...])
    pprev_ref[...] = lax.dot_general(hprev_ref[...].astype(BF16), wa_ref[...], _NT,
                                     preferred_element_type=F32)


def _lastrow_call(x_last, nw, scale, shift, h_prev, wa):
    b, d = x_last.shape
    return pl.pallas_call(
        _lastrow_kernel,
        out_shape=[jax.ShapeDtypeStruct((b, d), F32), jax.ShapeDtypeStruct((b, wa.shape[0]), F32)],
        compiler_params=pltpu.CompilerParams(vmem_limit_bytes=VMEM_LIMIT),
        name="lastrow",
    )(x_last, nw, scale, shift, h_prev, wa)


def _rwkv_kernel(p_ref, pprev_ref, s0_ref, mu_ref, vec_ref, wl_ref, g_ref,
                 y_ref, sout_ref, s_scr, prev_scr, *, bt):
    c = CHUNK
    ci = pl.program_id(1)
    nci = pl.num_programs(1)
    aw = 3 * PAIR
    rows = _iota((c, 1), 0)
    lane = _iota((1, PAIR), 1)
    vec = vec_ref[...]
    w0, a0, k_k, k_a, r_k, ln_w, ln_b = (vec[i:i + 1] for i in range(7))
    gpair = g_ref[...]
    tri = (rows >= _iota((1, c), 1)).astype(BF16)
    r2 = _iota((2 * c, 1), 0) & (c - 1)
    c2 = _iota((1, 2 * c), 1) & (c - 1)
    bd = _pair_blockdiag_mask()
    strict = bd & (r2 > c2)
    incl = bd & (r2 >= c2)

    @pl.when(ci == 0)
    def _():
        for bi in range(bt):
            for pr in range(3):
                s_scr[bi, pr] = _load_pair_state(s0_ref, bi, pr)
            prev_scr[bi] = pprev_ref[bi]

    shifted = []
    for bi in range(bt):
        p = p_ref[bi]
        prev = pltpu.roll(p, 1, axis=0)
        prev = jnp.where(rows == 0, prev_scr[bi], prev)
        prev_scr[bi] = p[c - 1:c]
        shifted.append(p + mu_ref[...] * (prev - p))
    xs = jnp.concatenate(shifted, axis=0)
    r = xs[:, 0:aw]
    k = xs[:, aw:2 * aw]
    v = xs[:, 2 * aw:3 * aw]
    tail = xs[:, 3 * aw:3 * aw + PAIR]
    act = jnp.where(lane < 32, jnp.tanh(tail), jnp.where(lane < 64, tail, _sigmoid(tail)))
    lo = _dot(act, wl_ref[...])
    w = -_softplus(-(w0 + lo[:, 0:aw])) - 0.5
    ld = -jnp.exp(w)
    a = _sigmoid(a0 + lo[:, aw:2 * aw])
    g = lo[:, 2 * aw:3 * aw]
    kk = k * k_k
    kk = kk * lax.rsqrt(_seg_sum(kk * kk, gpair) + L2_EPS)
    k = k * (1.0 + (a - 1.0) * k_a)
    cums = [_dot_left01(tri, ld[c * bi:c * (bi + 1)]) for bi in range(bt)]
    cum = jnp.concatenate(cums, axis=0)
    cum_last = jnp.concatenate([jnp.broadcast_to(x[c - 1:c], x.shape) for x in cums], axis=0)
    e_neg = jnp.exp(-cum)
    e_end = jnp.exp(cum_last - cum)
    pre = dict(a_t=-kk * jnp.exp(cum - ld), b_t=kk * a * e_neg, k_t=k * e_neg, r_t=r * jnp.exp(cum), v=v,
               b_end=kk * a * e_end, k_end=k * e_end)

    chains = [(bi, pr) for bi in range(bt) for pr in range(3)]
    pair = lambda name: [pre[name][c * bi:c * (bi + 1), PAIR * pr:PAIR * (pr + 1)] for bi, pr in chains]
    a_t, b_t, k_t, r_t, v_p = pair("a_t"), pair("b_t"), pair("k_t"), pair("r_t"), pair("v")
    b_end, k_end = pair("b_end"), pair("k_end")
    d_end = [jnp.exp(cums[bi][c - 1:c, PAIR * pr:PAIR * (pr + 1)]) for bi, pr in chains]
    v_st = [_stack_heads(x) for x in v_p]
    gs = [_dot(jnp.concatenate([_stack_heads(a), _stack_heads(r)], axis=0),
               jnp.concatenate([b, b, k, k], axis=0), _NT)
          for a, r, b, k in zip(a_t, r_t, b_t, k_t)]
    a_ak = [jnp.where(strict, g[0:2 * c, 2 * c:4 * c], 0.0) for g in gs]
    p_rbk = [jnp.concatenate([jnp.where(incl, g[2 * c:4 * c, 0:2 * c], 0.0),
                              jnp.where(incl, g[2 * c:4 * c, 2 * c:4 * c], 0.0)], axis=1) for g in gs]
    tinv = _tri_inverse_many([jnp.where(strict, g[0:2 * c, 0:2 * c], 0.0) for g in gs], c)
    s = [s_scr[bi, pr] for bi, pr in chains]
    ars = [_dot(jnp.concatenate([a, r], axis=0), s_, _NT) for a, r, s_ in zip(a_t, r_t, s)]
    akv = [_dot(m, x) for m, x in zip(a_ak, v_st)]
    u_st = [_dot(t, _stack_heads(x[0:c]) + y) for t, x, y in zip(tinv, ars, akv)]
    y_st = [_dot(p, jnp.concatenate([u, x], axis=0)) for p, u, x in zip(p_rbk, u_st, v_st)]
    ds = [_dot(jnp.concatenate([_fold_heads(u, c), x], axis=0), jnp.concatenate([b_, k_], axis=0), _TN)
          for u, x, b_, k_ in zip(u_st, v_p, b_end, k_end)]
    for i, (bi, pr) in enumerate(chains):
        s_scr[bi, pr] = s[i] * d_end[i] + jnp.where(bd, ds[i], 0.0)

    y = jnp.concatenate(
        [jnp.concatenate([ars[3 * bi + pr][c:2 * c] + _fold_heads(y_st[3 * bi + pr], c) for pr in range(3)],
                         axis=1) for bi in range(bt)], axis=0)
    mean = _seg_sum(y, gpair) * (1.0 / HEAD)
    dy = y - mean
    var = _seg_sum(dy * dy, gpair) * (1.0 / HEAD)
    yn = dy * lax.rsqrt(var + GN_EPS) * ln_w + ln_b
    bonus = _seg_sum(r * k * r_k, gpair) * v
    y_ref[...] = ((yn + bonus) * g).reshape(bt, c, aw)

    @pl.when(ci == nci - 1)
    def _():
        for bi in range(bt):
            for pr in range(3):
                _store_pair_state(sout_ref, bi, pr, s_scr[bi, pr])


def _rwkv_call(p_a, p_prev, s0, mu, vec, wl, gmat, *, bt):
    b, tp, acols = p_a.shape
    nh = s0.shape[1]
    aw = nh * HEAD
    cin = CHUNK
    nc = tp // cin
    const = lambda a: pl.BlockSpec(a.shape, lambda i, j: (0,) * a.ndim)
    kern = functools.partial(_rwkv_kernel, bt=bt)
    return pl.pallas_call(
        kern,
        grid=(b // bt, nc),
        in_specs=[
            pl.BlockSpec((bt, cin, acols), lambda i, j: (i, j, 0)),
            pl.BlockSpec((bt, 1, acols), lambda i, j: (i, 0, 0)),
            pl.BlockSpec((bt, nh, HEAD, HEAD), lambda i, j: (i, 0, 0, 0)),
            const(mu), const(vec), const(wl), const(gmat),
        ],
        out_specs=[
            pl.BlockSpec((bt, cin, aw), lambda i, j: (i, j, 0)),
            pl.BlockSpec((bt, nh, HEAD, HEAD), lambda i, j: (i, 0, 0, 0)),
        ],
        out_shape=[jax.ShapeDtypeStruct((b, tp, aw), F32),
                   jax.ShapeDtypeStruct(s0.shape, F32)],
        scratch_shapes=[pltpu.VMEM((bt, nh // 2, PAIR, PAIR), F32),
                        pltpu.VMEM((bt, 1, acols), F32)],
        compiler_params=pltpu.CompilerParams(
            dimension_semantics=("parallel", "arbitrary"), vmem_limit_bytes=VMEM_LIMIT),
        name="rwkv",
    )(p_a, p_prev.reshape(b, 1, acols), s0, mu, vec, wl, gmat)


def _gdn_kernel(p_ref, conv0_ref, s0_ref, cw_ref, tp_ref, nw_ref, g_ref, bsel_ref, gsel_ref,
                y_ref, sout_ref, s_scr, cbuf, *, bt):
    c = CHUNK
    ci = pl.program_id(1)
    nci = pl.num_programs(1)
    bw = 3 * PAIR
    qkvw = 3 * bw
    rows = _iota((c, 1), 0)
    lane = _iota((1, PAIR), 1)
    nh = 2 * 3
    gpair = g_ref[...]
    tri = (rows >= _iota((1, c), 1)).astype(BF16)
    r2 = _iota((2 * c, 1), 0) & (c - 1)
    c2 = _iota((1, 2 * c), 1) & (c - 1)
    same =(_iota((2 * c, 1), 0) >> 6) == (_iota((1, 2 * c), 1) >> 6)
    strict = same & (r2 > c2)
    incl = same & (r2 >= c2)
    bd = _pair_blockdiag_mask()
    is_a = lane < HEAD
    a_log = tp_ref[0:1]
    dt_bias = tp_ref[1:2]
    is_beta = lane < nh
    is_g = (lane >= nh) & (lane < 2 * nh)

    @pl.when(ci == 0)
    def _():
        for bi in range(bt):
            for pr in range(3):
                s_scr[bi, pr] = _load_pair_state(s0_ref, bi, pr)
            cbuf[bi, 0:8] = jnp.zeros((8, qkvw), F32)
            cbuf[bi, 8 - (CONV_K - 1):8] = conv0_ref[bi]

    convs = []
    for bi in range(bt):
        cbuf[bi, 8:8 + c] = p_ref[bi, :, 0:qkvw]
        conv = None
        for j in range(CONV_K):
            t = cw_ref[j:j + 1] * cbuf[bi, pl.ds(8 - (CONV_K - 1) + j, c), :]
            conv = t if conv is None else conv + t
        cbuf[bi, 0:8] = cbuf[bi, c:c + 8]
        convs.append(conv)
    conv = jnp.concatenate(convs, axis=0)
    p = p_ref[...].reshape(bt * c, p_ref.shape[2])
    qkv = conv * _sigmoid(conv)
    q = qkv[:, 0:bw]
    k = qkv[:, bw:2 * bw]
    v = qkv[:, 2 * bw:3 * bw]
    z = p[:, qkvw:qkvw + bw]
    tail = p[:, qkvw + bw:qkvw + bw + PAIR]
    q = q * (lax.rsqrt(_seg_sum(q * q, gpair) + L2_EPS) * (HEAD ** -0.5))
    k = k * lax.rsqrt(_seg_sum(k * k, gpair) + L2_EPS)
    beta = jnp.where(is_beta, _sigmoid(tail), 0.0)
    gl = jnp.where(is_g, -jnp.exp(a_log) * _softplus(tail + dt_bias), 0.0)
    gcum128 = jnp.concatenate([_dot_left01(tri, gl[c * bi:c * (bi + 1)]) for bi in range(bt)], axis=0)
    gcum = _dot_sel(gcum128, gsel_ref[...])
    beta = _dot_sel(beta, bsel_ref[...])
    g_last = jnp.concatenate(
        [jnp.broadcast_to(gcum[c * (bi + 1) - 1:c * (bi + 1)], (c, bw)) for bi in range(bt)], axis=0)
    eg = jnp.exp(gcum)
    kb = k * beta
    pre = dict(q=q, k=k, kb=kb, gcum=gcum, kg=kb * eg, qg=q * eg, kd=k * jnp.exp(g_last - gcum),
               vb=v * beta, d_end=jnp.exp(g_last))

    chains = [(bi, pr) for bi in range(bt) for pr in range(3)]
    pair = lambda name: [pre[name][c * bi:c * (bi + 1), PAIR * pr:PAIR * (pr + 1)] for bi, pr in chains]
    q, k, kb, gcum, kg, qg = pair("q"), pair("k"), pair("kb"), pair("gcum"), pair("kg"), pair("qg")
    kd, vb = pair("kd"), pair("vb")
    d_end = [x[0:1] for x in pair("d_end")]

    def col(x):
        sw = pltpu.roll(x, HEAD, axis=1)
        return jnp.concatenate([jnp.where(is_a, x, sw), jnp.where(is_a, sw, x)], axis=0)

    gcol = [col(x) for x in gcum]
    diff = [x - x.T for x in gcol]
    dec = [jnp.exp(jnp.where(incl, x, 0.0)) for x in diff]
    dec_s = [jnp.where(strict, x, 0.0) for x in dec]
    dec_i = [jnp.where(incl, x, 0.0) for x in dec]
    sc = [_dot(jnp.concatenate([_stack_heads(x), _stack_heads(y)], axis=0),
               jnp.concatenate([w, w], axis=0), _NT)
          for x, y, w in zip(kb, q, k)]
    tinv = _tri_inverse_many([-(x[0:2 * c] * d) for x, d in zip(sc, dec_s)], c)
    qk = [x[2 * c:4 * c] * d for x, d in zip(sc, dec_i)]
    s = [s_scr[bi, pr] for bi, pr in chains]
    kqs = [_dot(jnp.concatenate([x, y], axis=0), s_) for x, y, s_ in zip(kg, qg, s)]
    v_new = [_dot(t, _stack_heads(x - y[0:c])) for t, x, y in zip(tinv, vb, kqs)]
    o_st = [_dot(x, y) for x, y in zip(qk, v_new)]
    ds = [_dot(x, _fold_heads(y, c), _TN) for x, y in zip(kd, v_new)]
    for i, (bi, pr) in enumerate(chains):
        s_scr[bi, pr] = s[i] * d_end[i] + jnp.where(bd, ds[i], 0.0)

    o = jnp.concatenate(
        [jnp.concatenate([kqs[3 * bi + pr][c:2 * c] + _fold_heads(o_st[3 * bi + pr], c) for pr in range(3)],
                         axis=1) for bi in range(bt)], axis=0)
    o = o * lax.rsqrt(_seg_sum(o * o, gpair) * (1.0 / HEAD) + RMS_EPS) * nw_ref[...]
    y_ref[...] = (o * (z * _sigmoid(z))).reshape(bt, c, bw)

    @pl.when(ci == nci - 1)
    def _():
        for bi in range(bt):
            for pr in range(3):
                _store_pair_state(sout_ref, bi, pr, s_scr[bi, pr])


def _gdn_call(p_b, conv0, s0, cw, tailp, nw, gmat, bsel, gsel, *, bt):
    b, tp, bcols = p_b.shape
    nh = s0.shape[1]
    bw = nh * HEAD
    cin = CHUNK
    nc = tp // cin
    const = lambda a: pl.BlockSpec(a.shape, lambda i, j: (0,) * a.ndim)
    kern = functools.partial(_gdn_kernel, bt=bt)
    return pl.pallas_call(
        kern,
        grid=(b // bt, nc),
        in_specs=[
            pl.BlockSpec((bt, cin, bcols), lambda i, j: (i, j, 0)),
            pl.BlockSpec((bt, CONV_K - 1, 3 * bw), lambda i, j: (i, 0, 0)),
            pl.BlockSpec((bt, nh, HEAD, HEAD), lambda i, j: (i, 0, 0, 0)),
            const(cw), const(tailp), const(nw), const(gmat), const(bsel), const(gsel),
        ],
        out_specs=[
            pl.BlockSpec((bt, cin, bw), lambda i, j: (i, j, 0)),
            pl.BlockSpec((bt, nh, HEAD, HEAD), lambda i, j: (i, 0, 0, 0)),
        ],
        out_shape=[jax.ShapeDtypeStruct((b, tp, bw), F32),
                   jax.ShapeDtypeStruct(s0.shape, F32)],
        scratch_shapes=[pltpu.VMEM((bt, nh // 2, PAIR, PAIR), F32),
                        pltpu.VMEM((bt, CHUNK + 8, 3 * bw), F32)],
        compiler_params=pltpu.CompilerParams(
            dimension_semantics=("parallel", "arbitrary"), vmem_limit_bytes=VMEM_LIMIT),
        name="gdn",
    )(p_b, conv0, s0, cw, tailp, nw, gmat, bsel, gsel)


def _hgrn_kernel(p_ref, s0_ref, lbl_ref, nw_ref, g_ref, y_ref, sout_ref, s_scr,
                 *, bt, layer):
    c = CHUNK
    ci = pl.program_id(1)
    nci = pl.num_programs(1)
    cw = 2 * PAIR
    rows = _iota((c, 1), 0)
    cols = _iota((1, c), 1)
    sameblk = (rows >> 4) == (cols >> 4)
    tri_tot = jnp.concatenate([sameblk & (rows >= cols), sameblk], axis=0).astype(BF16)
    rows_b = _iota((HBLK, 1), 0)
    bd = _pair_blockdiag_mask()
    gpair = g_ref[...]

    logits = lbl_ref[...]
    ex = jnp.exp(logits - jnp.max(logits, axis=0, keepdims=True))
    gam = ex / jnp.sum(ex, axis=0, keepdims=True)
    lb = jnp.sum(gam[0:layer + 1], axis=0, keepdims=True) - gam[0:1]

    @pl.when(ci == 0)
    def _():
        for bi in range(bt):
            for pr in range(2):
                s_scr[bi, pr] = _load_pair_state(s0_ref, bi, pr).T

    pre = []
    for bi in range(bt):
        p = p_ref[bi]
        qp = p[:, 0:cw]
        f = p[:, cw:2 * cw]
        v = p[:, 2 * cw:3 * cw]
        z = p[:, 3 * cw:4 * cw]
        q = qp * _sigmoid(qp)
        logf = jnp.log(lb + (1.0 - lb) * _sigmoid(f))
        k = (1.0 - lb) * _sigmoid(-f)
        cums = _dot_left01(tri_tot, logf)
        bl = cums[0:c]
        btot = cums[c:2 * c]
        pre.append(dict(q=q, k=k, v=v, z=z, bl=bl, q_in=q * jnp.exp(bl),
                        k_out=k * jnp.exp(btot - bl), d_blk=jnp.exp(btot)))

    chains = [(bi, pr) for bi in range(bt) for pr in range(2)]
    nblk = c // HBLK

    def blk_of(name, bi, pr, blk):
        return pre[bi][name][HBLK * blk:HBLK * (blk + 1), PAIR * pr:PAIR * (pr + 1)]

    intra = {}
    ds = {}
    for blk in range(nblk):
        for bi, pr in chains:
            qb, kb, vb, bb = (blk_of(n, bi, pr, blk) for n in ("q", "k", "v", "bl"))
            first = [0 if j < 8 else 8 for j in range(HBLK)]
            xs = []
            for j, lo in enumerate(first):
                e = jnp.exp(jnp.minimum(bb[lo:] - bb[j:j + 1], 0.0))
                xs.append(qb[lo:] * e * kb[j:j + 1])
            att = _dot_sel(jnp.concatenate(xs, axis=0), gpair, pieces=1)
            acc = [None, None]
            off = 0
            for j, lo in enumerate(first):
                n = HBLK - lo
                t = jnp.where(rows_b[lo:] >= j, att[off:off + n], 0.0) * vb[j:j + 1]
                off += n
                for half, part in enumerate([t[0:8], t[8:16]] if lo == 0 else [None, t]):
                    if part is not None:
                        acc[half] = part if acc[half] is None else acc[half] + part
            intra[bi, pr, blk] = jnp.concatenate(acc, axis=0)
            ds[bi, pr, blk] = jnp.where(bd, _dot(vb, blk_of("k_out", bi, pr, blk), _TN), 0.0)

    s = {ch: s_scr[ch] for ch in chains}
    outs = {}
    for blk in range(nblk):
        for bi, pr in chains:
            outs[bi, pr, blk] = intra[bi, pr, blk] + _dot(blk_of("q_in", bi, pr, blk), s[bi, pr], _NT)
            s[bi, pr] = s[bi, pr] * blk_of("d_blk", bi, pr, blk)[0:1] + ds[bi, pr, blk]
    for ch in chains:
        s_scr[ch] = s[ch]

    o = jnp.concatenate(
        [jnp.concatenate([outs[bi, pr, blk] for bi in range(bt) for blk in range(nblk)], axis=0)
         for pr in range(2)], axis=1)
    z = jnp.concatenate([pre[bi]["z"] for bi in range(bt)], axis=0)
    o = o * lax.rsqrt(_seg_sum(o * o, gpair) * (1.0 / HEAD) + RMS_EPS) * nw_ref[...]
    y_ref[...] = (o * _sigmoid(z)).reshape(bt, c, cw)

    @pl.when(ci == nci - 1)
    def _():
        for bi in range(bt):
            for pr in range(2):
                _store_pair_state(sout_ref, bi, pr, s_scr[bi, pr].T)


def _hgrn_call(p_c, s0, lbl, nw, gmat, *, bt, layer):
    b, tp, ccols = p_c.shape
    nh = s0.shape[1]
    cw = nh * HEAD
    cin = CHUNK
    nc = tp // cin
    const = lambda a: pl.BlockSpec(a.shape, lambda i, j: (0,) * a.ndim)
    kern = functools.partial(_hgrn_kernel, bt=bt, layer=layer)
    return pl.pallas_call(
        kern,
        grid=(b // bt, nc),
        in_specs=[
            pl.BlockSpec((bt, cin, ccols), lambda i, j: (i, j, 0)),
            pl.BlockSpec((bt, nh, HEAD, HEAD), lambda i, j: (i, 0, 0, 0)),
            const(lbl), const(nw), const(gmat),
        ],
        out_specs=[
            pl.BlockSpec((bt, cin, cw), lambda i, j: (i, j, 0)),
            pl.BlockSpec((bt, nh, HEAD, HEAD), lambda i, j: (i, 0, 0, 0)),
        ],
        out_shape=[jax.ShapeDtypeStruct((b, tp, cw), F32),
                   jax.ShapeDtypeStruct(s0.shape, F32)],
        scratch_shapes=[pltpu.VMEM((bt, nh // 2, PAIR, PAIR), F32)],
        compiler_params=pltpu.CompilerParams(
            dimension_semantics=("parallel", "arbitrary"), vmem_limit_bytes=VMEM_LIMIT),
        name="hgrn",
    )(p_c, s0, lbl, nw, gmat)


def _inproj_t_kernel(x_ref, nw_ref, sc_ref, sh_ref, hprev_ref, wta_ref, wtb_ref, wtc_ref,
                     pta_ref, ptb_ref, ptc_ref, rows_ref, hlast_ref, pprev_ref):
    h = _norm_mod(x_ref[...], nw_ref[...], sc_ref[...], sh_ref[...])
    hb = h.astype(BF16)
    nt = lambda w, a: lax.dot_general(w, a, _NT, preferred_element_type=F32)
    pta_ref[...] = nt(wta_ref[...], hb)
    ptb_ref[...] = nt(wtb_ref[...], hb)
    ptc_ref[...] = nt(wtc_ref[...], hb)
    rows_ref[...] = nt(hb, wtb_ref[0:rows_ref.shape[1], :])
    nb = hlast_ref.shape[0]
    hlast_ref[...] = h[h.shape[0] - nb:]
    pprev_ref[...] = nt(wta_ref[...], hprev_ref[...].astype(BF16))


def _inproj_t_call(x, nw, scale, shift, h_prev, wta, wtb, wtc, qkvw):
    rows, d = x.shape
    nb = h_prev.shape[0]
    sds = jax.ShapeDtypeStruct
    return pl.pallas_call(
        _inproj_t_kernel,
        out_shape=[sds((wta.shape[0], rows), F32), sds((wtb.shape[0], rows), F32),
                   sds((wtc.shape[0], rows), F32), sds((rows, qkvw), F32),
                   sds((nb, d), F32), sds((wta.shape[0], nb), F32)],
        compiler_params=pltpu.CompilerParams(vmem_limit_bytes=VMEM_LIMIT),
        name="inproj_t",
    )(x, nw, scale, shift, h_prev, wta, wtb, wtc)


def _col_sum(x):
    return jnp.sum(x, axis=0, keepdims=True)


def _rwkv_t_kernel(r_ref, k_ref, v_ref, tail_ref, rp_ref, kp_ref, vp_ref, tailp_ref,
                   mur_ref, muk_ref, muv_ref, mut_ref, vec_ref, wl_ref, s0_ref,
                   y_ref, sout_ref, v_scr, y_scr):
    nb = s0_ref.shape[-1]
    nt = r_ref.shape[-1] // nb
    w0, a0, k_k, k_a, r_k, ln_w, ln_b = (vec_ref[i, 0] for i in range(7))
    sout_ref[...] = s0_ref[...]

    def shifted(ref, pref, mu, idx, t):
        cur = ref[idx, :, t * nb:(t + 1) * nb]
        prv = pref[idx] if t == 0 else ref[idx, :, (t - 1) * nb:t * nb]
        return cur + mu * (prv - cur)

    for t in range(nt):
        r = shifted(r_ref, rp_ref, mur_ref[0], 0, t)
        k = shifted(k_ref, kp_ref, muk_ref[0], 0, t)
        v = shifted(v_ref, vp_ref, muv_ref[0], 0, t)
        t0 = shifted(tail_ref, tailp_ref, mut_ref[0], 0, t)
        t1 = shifted(tail_ref, tailp_ref, mut_ref[1], 1, t)
        act = jnp.concatenate([jnp.tanh(t0[0:32]), t0[32:64], _sigmoid(t1)], axis=0)
        lo = jnp.dot(wl_ref[0], act.astype(BF16), preferred_element_type=F32)
        w = -_softplus(-(w0 + lo[0:HEAD])) - 0.5
        dec = jnp.exp(-jnp.exp(w))
        a = _sigmoid(a0 + lo[HEAD:2 * HEAD])
        g = lo[2 * HEAD:3 * HEAD]
        kk = k * k_k
        kk = kk * lax.rsqrt(_col_sum(kk * kk) + L2_EPS)
        k = k * (1.0 + (a - 1.0) * k_a)
        a_t = -kk
        b_t = kk * a
        v_scr[...] = v

        def body(g, carry):
            base = pl.multiple_of(g * 8, 8)
            vg = v_scr[pl.ds(base, 8), :]
            ys = []
            for j in range(8):
                sv = sout_ref[0, base + j]
                sa = _col_sum(sv * a_t)
                sv = sv * dec + sa * b_t + vg[j:j + 1] * k
                sout_ref[0, base + j] = sv
                ys.append(_col_sum(sv * r))
            y_scr[pl.ds(base, 8), :] = jnp.concatenate(ys, axis=0)
            return carry

        lax.fori_loop(0, HEAD // 8, body, 0)
        y = y_scr[...]
        mean = _col_sum(y) * (1.0 / HEAD)
        dy = y - mean
        var = _col_sum(dy * dy) * (1.0 / HEAD)
        yn = dy * lax.rsqrt(var + GN_EPS) * ln_w + ln_b
        bonus = _col_sum(r * k * r_k) * v
        y_ref[0, :, t * nb:(t + 1) * nb] = (yn + bonus) * g


def _head_blocks(arr3, first, n=1):
    if n == 1:
        return pl.BlockSpec((1,) + arr3.shape[1:], lambda h: (first + h, 0, 0))
    return pl.BlockSpec((n,) + arr3.shape[1:], lambda h: (first // n, 0, 0))


def _layer_state_specs(s_all, layer):
    nh, nb = s_all.shape[1], s_all.shape[-1]
    spec_in = pl.BlockSpec((None, 1, HEAD, HEAD, nb), lambda h: (layer, h, 0, 0, 0))
    spec_out = pl.BlockSpec((1, HEAD, HEAD, nb), lambda h: (h, 0, 0, 0))
    return spec_in, spec_out, jax.ShapeDtypeStruct((nh, HEAD, HEAD, nb), F32)


def _rwkv_t_call(pta, pprev_t, s_all, layer, mu, vec, wl_t):
    nh, nb = s_all.shape[1], s_all.shape[-1]
    rows = pta.shape[1]
    p3 = pta.reshape(-1, HEAD, rows)
    pp3 = pprev_t.reshape(-1, HEAD, nb)
    mu3 = mu.reshape(-1, HEAD, 1)
    tail0 = 3 * nh
    state, state_out, state_shape = _layer_state_specs(s_all, layer)
    return pl.pallas_call(
        _rwkv_t_kernel,
        grid=(nh,),
        in_specs=[
            _head_blocks(p3, 0), _head_blocks(p3, nh), _head_blocks(p3, 2 * nh), _head_blocks(p3, tail0, 2),
            _head_blocks(pp3, 0), _head_blocks(pp3, nh), _head_blocks(pp3, 2 * nh), _head_blocks(pp3, tail0, 2),
            _head_blocks(mu3, 0), _head_blocks(mu3, nh), _head_blocks(mu3, 2 * nh), _head_blocks(mu3, tail0, 2),
            pl.BlockSpec((vec.shape[0], 1, HEAD, 1), lambda h: (0, h, 0, 0)),
            pl.BlockSpec((1,) + wl_t.shape[1:], lambda h: (h, 0, 0)),
            state,
        ],
        out_specs=[pl.BlockSpec((1, HEAD, rows), lambda h: (h, 0, 0)), state_out],
        out_shape=[jax.ShapeDtypeStruct((nh, HEAD, rows), F32), state_shape],
        scratch_shapes=[pltpu.VMEM((HEAD, nb), F32), pltpu.VMEM((HEAD, nb), F32)],
        compiler_params=pltpu.CompilerParams(
            dimension_semantics=("parallel",), vmem_limit_bytes=VMEM_LIMIT),
        name="rwkv_t",
    )(p3, p3, p3, p3, pp3, pp3, pp3, pp3, mu3, mu3, mu3, mu3, vec, wl_t, s_all)


def _gdn_t_kernel(q_ref, k_ref, v_ref, z_ref, tail_ref, cq_ref, ck_ref, cv_ref, wq_ref, wk_ref, wv_ref,
                  sc_ref, nw_ref, s0_ref, y_ref, sout_ref, k_scr, q_scr):
    h = pl.program_id(0)
    nh = pl.num_programs(0)
    nb = s0_ref.shape[-1]
    nt = q_ref.shape[-1] // nb
    a_log = sc_ref[0, 0:1, 0:1]
    dt_bias = sc_ref[0, 1:2, 0:1]
    sout_ref[...] = s0_ref[...]

    def conv(x_ref, c_ref, w_ref, t):
        acc = None
        for j in range(CONV_K):
            u = t + j - (CONV_K - 1)
            tap = c_ref[u + CONV_K - 1, 0] if u < 0 else x_ref[0, :, u * nb:(u + 1) * nb]
            term = w_ref[j, 0] * tap
            acc = term if acc is None else acc + term
        return acc * _sigmoid(acc)

    for t in range(nt):
        sl = slice(t * nb, (t + 1) * nb)
        q = conv(q_ref, cq_ref, wq_ref, t)
        k = conv(k_ref, ck_ref, wk_ref, t)
        v = conv(v_ref, cv_ref, wv_ref, t)
        q = q * (lax.rsqrt(_col_sum(q * q) + L2_EPS) * (HEAD ** -0.5))
        k = k * lax.rsqrt(_col_sum(k * k) + L2_EPS)
        tl = tail_ref[0:2 * 8, sl]
        pick = lambda row: _col_sum(jnp.where(_iota((2 * 8, 1), 0) == row, tl, 0.0))
        beta = _sigmoid(pick(h))
        eg = jnp.exp(-jnp.exp(a_log) * _softplus(pick(nh + h) + dt_bias))
        k_scr[...] = k
        q_scr[...] = q

        def body1(g, acc):
            base = pl.multiple_of(g * 8, 8)
            kg = k_scr[pl.ds(base, 8), :]
            for j in range(8):
                acc = acc + sout_ref[0, base + j] * kg[j:j + 1]
            return acc

        sk = lax.fori_loop(0, HEAD // 8, body1, jnp.zeros((HEAD, nb), F32))
        v_new = beta * (v - eg * sk)

        def body2(g, acc):
            base = pl.multiple_of(g * 8, 8)
            kg = k_scr[pl.ds(base, 8), :]
            qg = q_scr[pl.ds(base, 8), :]
            for j in range(8):
                s = eg * sout_ref[0, base + j] + kg[j:j + 1] * v_new
                sout_ref[0, base + j] = s
                acc = acc + s * qg[j:j + 1]
            return acc

        o = lax.fori_loop(0, HEAD // 8, body2, jnp.zeros((HEAD, nb), F32))
        o = o * lax.rsqrt(_col_sum(o * o) * (1.0 / HEAD) + RMS_EPS) * nw_ref[...]
        z = z_ref[0, :, sl]
        y_ref[0, :, sl] = o * (z * _sigmoid(z))


def _gdn_t_call(ptb, conv_t, s_all, layer, cw, scal, nw):
    nh, nb = s_all.shape[1], s_all.shape[-1]
    rows = ptb.shape[1]
    p3 = ptb.reshape(-1, HEAD, rows)
    c4 = conv_t.reshape(CONV_K - 1, -1, HEAD, nb)
    w4 = cw.reshape(CONV_K, -1, HEAD, 1)
    tap = lambda first: pl.BlockSpec((CONV_K - 1, 1, HEAD, nb), lambda h: (0, first + h, 0, 0))
    wsp = lambda first: pl.BlockSpec((CONV_K, 1, HEAD, 1), lambda h: (0, first + h, 0, 0))
    state, state_out, state_shape = _layer_state_specs(s_all, layer)
    tail_blk = 4 * nh * HEAD // PAIR
    return pl.pallas_call(
        _gdn_t_kernel,
        grid=(nh,),
        in_specs=[
            _head_blocks(p3, 0), _head_blocks(p3, nh), _head_blocks(p3, 2 * nh), _head_blocks(p3, 3 * nh),
            pl.BlockSpec((PAIR, rows), lambda h: (tail_blk, 0)),
            tap(0), tap(nh), tap(2 * nh), wsp(0), wsp(nh), wsp(2 * nh),
            pl.BlockSpec((1,) + scal.shape[1:], lambda h: (h, 0, 0)),
            pl.BlockSpec(nw.shape, lambda h: (0, 0)),
            state,
        ],
        out_specs=[pl.BlockSpec((1, HEAD, rows), lambda h: (h, 0, 0)), state_out],
        out_shape=[jax.ShapeDtypeStruct((nh, HEAD, rows), F32), state_shape],
        scratch_shapes=[pltpu.VMEM((HEAD, nb), F32), pltpu.VMEM((HEAD, nb), F32)],
        compiler_params=pltpu.CompilerParams(
            dimension_semantics=("parallel",), vmem_limit_bytes=VMEM_LIMIT),
        name="gdn_t",
    )(p3, p3, p3, p3, ptb, c4, c4, c4, w4, w4, w4, scal, nw, s_all)


def _hgrn_t_kernel(q_ref, f_ref, v_ref, z_ref, lbl_ref, nw_ref, s0_ref, y_ref, sout_ref,
                   f_scr, k_scr, q_scr, *, layer):
    nb = s0_ref.shape[-1]
    nt = q_ref.shape[-1] // nb
    logits = lbl_ref[:, 0]
    ex = jnp.exp(logits - jnp.max(logits, axis=0, keepdims=True))
    gam = ex / jnp.sum(ex, axis=0, keepdims=True)
    lb = jnp.sum(gam[0:layer + 1], axis=0) - gam[0]
    sout_ref[...] = s0_ref[...]

    for t in range(nt):
        sl = slice(t * nb, (t + 1) * nb)
        qp = q_ref[0, :, sl]
        f = f_ref[0, :, sl]
        v = v_ref[0, :, sl]
        z = z_ref[0, :, sl]
        q_scr[...] = qp * _sigmoid(qp)
        f_scr[...] = lb + (1.0 - lb) * _sigmoid(f)
        k_scr[...] = (1.0 - lb) * _sigmoid(-f)

        def body(g, acc):
            base = pl.multiple_of(g * 8, 8)
            fg, kg, qg = (ref[pl.ds(base, 8), :] for ref in (f_scr, k_scr, q_scr))
            for j in range(8):
                s = fg[j:j + 1] * sout_ref[0, base + j] + kg[j:j + 1] * v
                sout_ref[0, base + j] = s
                acc = acc + qg[j:j + 1] * s
            return acc

        o = lax.fori_loop(0, HEAD // 8, body, jnp.zeros((HEAD, nb), F32))
        o = o * lax.rsqrt(_col_sum(o * o) * (1.0 / HEAD) + RMS_EPS) * nw_ref[...]
        y_ref[0, :, sl] = o * _sigmoid(z)


def _hgrn_t_call(ptc, s_all, lbl4, nw, layer):
    nh, nb = s_all.shape[1], s_all.shape[-1]
    rows = ptc.shape[1]
    p3 = ptc.reshape(-1, HEAD, rows)
    state, state_out, state_shape = _layer_state_specs(s_all, layer)
    return pl.pallas_call(
        functools.partial(_hgrn_t_kernel, layer=layer),
        grid=(nh,),
        in_specs=[
            _head_blocks(p3, 0), _head_blocks(p3, nh), _head_blocks(p3, 2 * nh), _head_blocks(p3, 3 * nh),
            pl.BlockSpec((lbl4.shape[0], 1, HEAD, 1), lambda h: (0, h, 0, 0)),
            pl.BlockSpec(nw.shape, lambda h: (0, 0)),
            state,
        ],
        out_specs=[pl.BlockSpec((1, HEAD, rows), lambda h: (h, 0, 0)), state_out],
        out_shape=[jax.ShapeDtypeStruct((nh, HEAD, rows), F32), state_shape],
        scratch_shapes=[pltpu.VMEM((HEAD, nb), F32)] * 3,
        compiler_params=pltpu.CompilerParams(
            dimension_semantics=("parallel",), vmem_limit_bytes=VMEM_LIMIT),
        name="hgrn_t",
    )(p3, p3, p3, p3, lbl4, nw, s_all)


def _ffn_kernel(x_ref, ya_ref, yb_ref, yc_ref, woa_ref, wob_ref, woc_ref, gm_ref, nw_ref,
                sc_ref, sh_ref, gf_ref, win_ref, wd_ref, fnw_ref, xo_ref, *yo_ref, th, y_transposed):
    ff = wd_ref.shape[0]
    dims = _TN if y_transposed else _NN
    proj = lambda y_ref, w_ref: lax.dot_general(y_ref[...].astype(BF16), w_ref[...], dims,
                                                preferred_element_type=F32)
    mix = proj(ya_ref, woa_ref) + proj(yb_ref, wob_ref) + proj(yc_ref, woc_ref)
    x1 = x_ref[...] + gm_ref[0] * mix
    h = _norm_mod(x1, nw_ref[...], sc_ref[0], sh_ref[0]).astype(BF16)
    acc = None
    for j in range(ff // th):
        gate = jnp.dot(h, win_ref[:, j * th:(j + 1) * th], preferred_element_type=F32)
        up = jnp.dot(h, win_ref[:, ff + j * th:ff + (j + 1) * th], preferred_element_type=F32)
        act = (gate * _sigmoid(gate) * up).astype(BF16)
        t = jnp.dot(act, wd_ref[j * th:(j + 1) * th, :], preferred_element_type=F32)
        acc = t if acc is None else acc + t
    xo = x1 + gf_ref[0] * acc
    xo_ref[...] = xo
    if yo_ref:
        yo_ref[0][...] = xo * lax.rsqrt(jnp.mean(xo * xo, axis=-1, keepdims=True) + RMS_EPS) * fnw_ref[...]


def _ffn_call(x, ya, yb, yc, w_out, gate_m, nw, scale_f, shift_f, gate_f,
              w_in, w_down, fnw, *, layer, tm, th, tiles_per_seq, final, y_transposed=False):
    rows, d = x.shape
    row = lambda a: pl.BlockSpec((tm, a.shape[1]), lambda i: (i, 0))
    yspec = (lambda a: pl.BlockSpec((a.shape[0], tm), lambda i: (0, i))) if y_transposed else row
    const = lambda a: pl.BlockSpec(a.shape, lambda i: (0, 0), pipeline_mode=pl.Buffered(1))
    of_layer = lambda a: pl.BlockSpec((None,) + a.shape[1:], lambda i: (layer, 0, 0),
                                      pipeline_mode=pl.Buffered(1))
    widths = [y.shape[0] if y_transposed else y.shape[1] for y in (ya, yb, yc)]
    starts = [0, widths[0], widths[0] + widths[1]]
    assert all(s % n == 0 for s, n in zip(starts, widths))
    wo_rows = lambda s, n: pl.BlockSpec((None, n, d), lambda i: (layer, s // n, 0), pipeline_mode=pl.Buffered(1))
    mod = lambda a: _mod_spec(a, tm, tiles_per_seq)
    n_out = 2 if final else 1
    return pl.pallas_call(
        functools.partial(_ffn_kernel, th=th, y_transposed=y_transposed),
        grid=(rows // tm,),
        in_specs=[
            row(x), yspec(ya), yspec(yb), yspec(yc), *[wo_rows(s, n) for s, n in zip(starts, widths)],
            mod(gate_m), const(nw), mod(scale_f), mod(shift_f), mod(gate_f),
            of_layer(w_in), of_layer(w_down), const(fnw),
        ],
        out_specs=[row(x)] * n_out,
        out_shape=[jax.ShapeDtypeStruct((rows, d), F32)] * n_out,
        compiler_params=pltpu.CompilerParams(
            dimension_semantics=("parallel",), vmem_limit_bytes=VMEM_LIMIT),
        name="ffn",
    )(x, ya, yb, yc, w_out, w_out, w_out, gate_m, nw, scale_f, shift_f, gate_f, w_in, w_down, fnw)


def _block_ones(n, blk):
    i = np.arange(n) // blk
    return jnp.asarray(i[:, None] == i[None, :], BF16)


def _head_select(first_lane, nh):
    m = np.zeros((PAIR, nh * HEAD), np.float32)
    for h in range(nh):
        m[first_lane + h, h * HEAD:(h + 1) * HEAD] = 1.0
    return jnp.asarray(m, BF16)


def _lane_place(x, first_lane):
    return jnp.zeros((PAIR,), F32).at[first_lane:first_lane + x.shape[0]].set(x)


def _run_chunked(x, mods, states, wts, *, tm, tm_ffn, th, bt):
    b, t, d = x.shape
    rows = b * t
    assert t % CHUNK == 0 and t % tm == 0 and t % tm_ffn == 0 and b % bt == 0 and t >= CONV_K - 1
    shift0, rwkv0, conv0, gdn0, hgrn0 = states
    nl = len(wts)
    new = ([], [], [], [], [])
    xr = x.reshape(rows, d)
    for l in range(nl):
        w = wts[l]
        m = mods[l]
        shift_m, scale_m, gate_m, shift_f, scale_f, gate_f = (m[:, i].reshape(b, 1, d) for i in range(6))
        p_a, p_b, p_c = _inproj_call(xr, w["norm_mix"], scale_m, shift_m, w["wta"], w["wtb"], w["wtc"],
                                     tm, t // tm)
        h_last, p_prev = _lastrow_call(xr.reshape(b, t, d)[:, t - 1], w["norm_mix"], m[:, 1], m[:, 0],
                                       shift0[l], w["wta"])
        p_a = p_a.reshape(b, t, -1)
        p_b = p_b.reshape(b, t, -1)
        p_c = p_c.reshape(b, t, -1)
        y_a, s_a = _rwkv_call(p_a, p_prev, rwkv0[l], w["mu"], w["rwkv_vec"], w["lora"], w["gpair"], bt=bt)
        y_b, s_b = _gdn_call(p_b, conv0[l], gdn0[l], w["conv_w"], w["gdn_tail"], w["gdn_norm"],
                             w["gpair"], w["bsel"], w["gsel"], bt=bt)
        y_c, s_c = _hgrn_call(p_c, hgrn0[l], w["lb_logits"], w["hgrn_norm"], w["gpair"], bt=bt, layer=l)
        conv_new = p_b[:, t - (CONV_K - 1):, :conv0.shape[-1]]
        res = _ffn_call(xr, y_a.reshape(rows, -1), y_b.reshape(rows, -1), y_c.reshape(rows, -1),
                        w["w_out"], gate_m, w["norm_ffn"], scale_f, shift_f, gate_f,
                        w["w_ffn_in"], w["w_ffn_out"], w["final_norm"],
                        layer=l, tm=tm_ffn, th=th, tiles_per_seq=t // tm_ffn, final=l == nl - 1)
        xr = res[0]
        for acc, s in zip(new, (h_last, s_a, conv_new, s_b, s_c)):
            acc.append(s)
    return res[1].reshape(b, t, d), [jnp.stack(acc) for acc in new]


def _run_steps(x, mods, states, wts, *, tm_ffn, th):
    b, t, d = x.shape
    rows = t * b
    tm_ffn = min(tm_ffn, rows)
    shift0, rwkv0, conv0, gdn0, hgrn0 = states
    to_lanes = lambda s: jnp.transpose(s, (0, 2, 3, 4, 1))
    rwkv_t, gdn_t, hgrn_t = to_lanes(rwkv0), to_lanes(gdn0), to_lanes(hgrn0)
    conv_t = jnp.transpose(conv0, (0, 2, 3, 1))
    nl = len(wts)
    new = ([], [], [], [], [])
    xr = jnp.transpose(x, (1, 0, 2)).reshape(rows, d)
    for l in range(nl):
        w = wts[l]
        m = mods[l]
        mod = lambda i, m=m: jnp.tile(m[:, i], (t, 1))
        tiled = lambda a: a.reshape(rows // tm_ffn, tm_ffn, d)
        pta, ptb, ptc, qkv_rows, h_last, pprev_t = _inproj_t_call(
            xr, w["norm_mix"], mod(1), mod(0), shift0[l], w["wta"], w["wtb"], w["wtc"], w["qkvw"])
        y_a, s_a = _rwkv_t_call(pta, pprev_t, rwkv_t, l, w["mu"], w["rwkv_vec_t"], w["lora_t"])
        y_b, s_b = _gdn_t_call(ptb, conv_t[l], gdn_t, l, w["conv_w_t"], w["gdn_scal"], w["gdn_norm_t"])
        y_c, s_c = _hgrn_t_call(ptc, hgrn_t, w["lb_logits_t"], w["hgrn_norm_t"], l)
        xp = jnp.concatenate([jnp.transpose(conv0[l], (1, 0, 2)), qkv_rows.reshape(t, b, -1)], axis=0)
        conv_new = jnp.transpose(xp[-(CONV_K - 1):], (1, 0, 2))
        res = _ffn_call(xr, y_a.reshape(-1, rows), y_b.reshape(-1, rows), y_c.reshape(-1, rows),
                        w["w_out"], tiled(mod(2)), w["norm_ffn"],
                        tiled(mod(4)), tiled(mod(3)), tiled(mod(5)),
                        w["w_ffn_in"], w["w_ffn_out"], w["final_norm"],
                        layer=l, tm=tm_ffn, th=th, tiles_per_seq=1, final=l == nl - 1, y_transposed=True)
        xr = res[0]
        for acc, s in zip(new, (h_last, s_a, conv_new, s_b, s_c)):
            acc.append(s)
    y = jnp.transpose(res[1].reshape(t, b, d), (1, 0, 2))
    from_lanes = lambda s: jnp.transpose(jnp.stack(s), (0, 4, 1, 2, 3))
    return y, [jnp.stack(new[0]), from_lanes(new[1]), jnp.stack(new[2]), from_lanes(new[3]), from_lanes(new[4])]


def kernel(x_prompt, x_sample, c_prompt, c_sample, state_rwkv_shift, state_rwkv, state_gdn_conv, state_gdn, state_hgrn, w_ada, b_ada, norm_mix_w, w_in, rwkv_mu, rwkv_w0, rwkv_w2, rwkv_a0, rwkv_a2, rwkv_g2, rwkv_k_k, rwkv_k_a, rwkv_r_k, rwkv_ln_w, rwkv_ln_b, gdn_conv_w, gdn_A_log, gdn_dt_bias, gdn_norm_w, hgrn_lb_logits, hgrn_norm_w, w_out, norm_ffn_w, w_ffn_in, w_ffn_out, final_norm_w):
    nl, d, _ = w_in.shape
    a_heads = state_rwkv.shape[2]
    b_heads = state_gdn.shape[2]
    c_heads = state_hgrn.shape[2]
    aw, bw, cw = a_heads * HEAD, b_heads * HEAD, c_heads * HEAD
    lw, la, lg = rwkv_w2.shape[1], rwkv_a2.shape[1], rwkv_g2.shape[1]
    a_cols = 3 * aw + lw + la + lg
    qkvw = 3 * bw
    b_cols = qkvw + 2 * b_heads + bw
    assert (aw, bw, cw) == (3 * PAIR, 3 * PAIR, 2 * PAIR) and lw + la + lg == PAIR and lw == 32 and la == 32

    bp, tpr, _ = x_prompt.shape
    bs, ts, _ = x_sample.shape

    gpair = _block_ones(PAIR, HEAD)
    bsel = _head_select(0, b_heads)
    gsel = _head_select(b_heads, b_heads)
    w_out_bf, w_ffn_in_bf, w_ffn_out_bf = (a.astype(BF16) for a in (w_out, w_ffn_in, w_ffn_out))
    wts = []
    for l in range(nl):
        wit = jnp.swapaxes(w_in[l], 0, 1).astype(BF16)
        wbt = wit[a_cols:a_cols + b_cols]
        wtb = jnp.concatenate(
            [wbt[:qkvw], wbt[qkvw + 2 * b_heads:], wbt[qkvw:qkvw + 2 * b_heads],
             jnp.zeros((PAIR - 2 * b_heads, d), BF16)], axis=0)
        lora = jnp.zeros((PAIR, 3 * aw), F32)
        lora = lora.at[0:lw, 0:aw].set(rwkv_w2[l])
        lora = lora.at[lw:lw + la, aw:2 * aw].set(rwkv_a2[l])
        lora = lora.at[lw + la:, 2 * aw:].set(rwkv_g2[l])
        rwkv_vec = jnp.stack([rwkv_w0[l], rwkv_a0[l], rwkv_k_k[l], rwkv_k_a[l], rwkv_r_k[l].reshape(-1),
                              rwkv_ln_w[l], rwkv_ln_b[l], jnp.zeros((aw,), F32)])
        hcols = lambda a: a.reshape(a.shape[0], a_heads, HEAD).transpose(1, 2, 0)
        lora_t = jnp.concatenate([
            jnp.pad(hcols(rwkv_w2[l]), ((0, 0), (0, 0), (0, la + lg))),
            jnp.pad(hcols(rwkv_a2[l]), ((0, 0), (0, 0), (lw, lg))),
            jnp.pad(hcols(rwkv_g2[l]), ((0, 0), (0, 0), (lw + la, 0)))], axis=1)
        steps = dict(
            rwkv_vec_t=rwkv_vec[:7].reshape(7, a_heads, HEAD, 1), lora_t=lora_t.astype(BF16),
            conv_w_t=gdn_conv_w[l],
            gdn_scal=jnp.zeros((b_heads, 8, PAIR), F32)
                .at[:, 0].set(jnp.broadcast_to(gdn_A_log[l][:, None], (b_heads, PAIR)))
                .at[:, 1].set(jnp.broadcast_to(gdn_dt_bias[l][:, None], (b_heads, PAIR))),
            gdn_norm_t=gdn_norm_w[l].reshape(HEAD, 1),
            lb_logits_t=hgrn_lb_logits.reshape(nl, c_heads, HEAD, 1),
            hgrn_norm_t=hgrn_norm_w[l].reshape(HEAD, 1),
        )
        wts.append(dict(
            **steps,
            norm_mix=norm_mix_w[l].reshape(1, d),
            wta=wit[:a_cols], wtb=wtb, wtc=wit[a_cols + b_cols:], qkvw=qkvw,
            mu=rwkv_mu[l].reshape(1, a_cols), rwkv_vec=rwkv_vec, lora=lora.astype(BF16),
            gpair=gpair, bsel=bsel, gsel=gsel,
            conv_w=jnp.concatenate([gdn_conv_w[l], jnp.zeros((8 - CONV_K, qkvw), F32)], axis=0),
            gdn_tail=jnp.zeros((8, PAIR), F32).at[0].set(_lane_place(gdn_A_log[l], b_heads))
                                              .at[1].set(_lane_place(gdn_dt_bias[l], b_heads)),
            gdn_norm=jnp.tile(gdn_norm_w[l], b_heads).reshape(1, bw),
            lb_logits=hgrn_lb_logits,
            hgrn_norm=jnp.tile(hgrn_norm_w[l], c_heads).reshape(1, cw),
            w_out=w_out_bf, norm_ffn=norm_ffn_w[l].reshape(1, d),
            w_ffn_in=w_ffn_in_bf, w_ffn_out=w_ffn_out_bf,
            final_norm=final_norm_w.reshape(1, d),
        ))

    mod = _ada_call(jnp.concatenate([c_prompt, c_sample], axis=0), w_ada, b_ada)
    mod = mod.reshape(nl, bp + bs, 6, d)
    mods_p = [mod[l, :bp] for l in range(nl)]
    mods_s = [mod[l, bp:] for l in range(nl)]

    zeros = lambda s: jnp.zeros((nl, bp) + s.shape[2:], s.dtype)
    states_p = tuple(zeros(s) for s in (state_rwkv_shift, state_rwkv, state_gdn_conv, state_gdn, state_hgrn))
    states_s = (state_rwkv_shift, state_rwkv, state_gdn_conv, state_gdn, state_hgrn)

    th = w_ffn_out.shape[1] // FFN_HIDDEN_TILES
    y_p, new_p = _run_chunked(x_prompt, mods_p, states_p, wts, tm=TM_INPROJ, tm_ffn=TM_FFN, th=th, bt=SEQS_PER_STEP)
    y_s, new_s = _run_steps(x_sample, mods_s, states_s, wts, tm_ffn=TM_FFN_STEPS, th=th)
    return (y_p, y_s, *new_p, *new_s)
```

```python
import functools

import numpy as np
import jax
import jax.numpy as jnp
from jax import lax
from jax.experimental import pallas as pl
from jax.experimental.pallas import tpu as pltpu

F32 = jnp.float32
BF16 = jnp.bfloat16

HEAD = 64
PAIR = 2 * HEAD
CHUNK = 64
HBLK = 16
INV_BASE = 16
CONV_K = 4
RMS_EPS = 1e-6
L2_EPS = 1e-6
GN_EPS = 64e-5
VMEM_LIMIT = 56 * 1024 * 1024

TM_INPROJ = 512
TM_FFN = 512
TM_FFN_STEPS = 256
FFN_HIDDEN_TILES = 2
SEQS_PER_STEP = 8

_NN = (((1,), (0,)), ((), ()))
_NT = (((1,), (1,)), ((), ()))
_TN = (((0,), (0,)), ((), ()))


def _dot(a, b, dims=_NN):
    return lax.dot_general(a.astype(BF16), b.astype(BF16), dims, preferred_element_type=F32)


def _dot_sel(x, m, pieces=3):
    ps = _bf16_pieces(x, pieces)
    return lax.dot_general(jnp.concatenate(ps, axis=1), jnp.concatenate([m] * pieces, axis=0), _NN,
                           preferred_element_type=F32)


def _bf16_pieces(x, pieces):
    ps = []
    rem = x
    for i in range(pieces):
        p = rem.astype(BF16)
        ps.append(p)
        if i + 1 < pieces:
            rem = rem - p.astype(F32)
    return ps


def _seg_sum(x, gpair, pieces=2):
    return jnp.concatenate(
        [_dot_sel(x[:, i:i + PAIR], gpair, pieces) for i in range(0, x.shape[1], PAIR)], axis=1)


def _sigmoid(x):
    return jax.nn.sigmoid(x)


def _softplus(x):
    return jnp.maximum(x, 0.0) + jnp.log1p(jnp.exp(-jnp.abs(x)))


def _iota(shape, dim):
    return lax.broadcasted_iota(jnp.int32, shape, dim)


def _stack_heads(x):
    is_a = _iota((1, PAIR), 1) < HEAD
    return jnp.concatenate([jnp.where(is_a, x, 0.0), jnp.where(is_a, 0.0, x)], axis=0)


def _fold_heads(x, c):
    return x[0:c] + x[c:2 * c]


def _pair_blockdiag_mask():
    return (_iota((PAIR, 1), 0) >> 6) == (_iota((1, PAIR), 1) >> 6)


def _load_pair_state(s_ref, bi, pr):
    sa = s_ref[bi, 2 * pr]
    sb = s_ref[bi, 2 * pr + 1]
    z = jnp.zeros((HEAD, HEAD), F32)
    return jnp.concatenate(
        [jnp.concatenate([sa, z], axis=1), jnp.concatenate([z, sb], axis=1)], axis=0)


def _store_pair_state(s_ref, bi, pr, s):
    s_ref[bi, 2 * pr] = s[0:HEAD, 0:HEAD]
    s_ref[bi, 2 * pr + 1] = s[HEAD:PAIR, HEAD:PAIR]


def _dot_left01(m, x, pieces=3):
    return lax.dot_general(jnp.concatenate([m] * pieces, axis=1),
                           jnp.concatenate(_bf16_pieces(x, pieces), axis=0), _NN,
                           preferred_element_type=F32)


def _tri_inverse_many(ns, c):
    size = ns[0].shape[0]
    r = _iota((size, 1), 0)
    cc = _iota((1, size), 1)
    eye = (r == cc).astype(F32)
    sh = INV_BASE.bit_length() - 1
    diag = [jnp.where((r >> sh) == (cc >> sh), n, 0.0) for n in ns]
    invs = [eye + d for d in diag]
    pws = [_dot(d, d) for d in diag]
    span = 4
    while span < INV_BASE:
        both = [_dot(jnp.concatenate([inv, pw], axis=0), pw) for inv, pw in zip(invs, pws)]
        invs = [inv + b[0:size] for inv, b in zip(invs, both)]
        pws = [b[size:2 * size] for b in both]
        span *= 2
    invs = [inv + _dot(inv, pw) for inv, pw in zip(invs, pws)]
    blk = INV_BASE
    while blk < c:
        sh = blk.bit_length() - 1
        off = ((r >> (sh + 1)) == (cc >> (sh + 1))) & ((r >> sh) != (cc >> sh))
        low = [_dot(inv, jnp.where(off, n, 0.0)) for inv, n in zip(invs, ns)]
        invs = [inv + _dot(t, inv) for inv, t in zip(invs, low)]
        blk *= 2
    return invs


def _ada_kernel(c_ref, w_ref, b_ref, o_ref):
    c = c_ref[...]
    o_ref[0] = _dot(c * _sigmoid(c), w_ref[0]) + b_ref[0]


def _ada_call(c_all, w_ada, b_ada):
    nl, d, n6 = w_ada.shape
    rows = c_all.shape[0]
    tn = n6 // 4
    return pl.pallas_call(
        _ada_kernel,
        grid=(nl, n6 // tn),
        in_specs=[
            pl.BlockSpec((rows, d), lambda l, j: (0, 0)),
            pl.BlockSpec((1, d, tn), lambda l, j: (l, 0, j)),
            pl.BlockSpec((1, 1, tn), lambda l, j: (l, 0, j)),
        ],
        out_specs=pl.BlockSpec((1, rows, tn), lambda l, j: (l, 0, j)),
        out_shape=jax.ShapeDtypeStruct((nl, rows, n6), F32),
        compiler_params=pltpu.CompilerParams(
            dimension_semantics=("parallel", "parallel"), vmem_limit_bytes=VMEM_LIMIT),
        name="ada",
    )(c_all, w_ada, b_ada.reshape(nl, 1, n6))


def _norm_mod(x, nw, scale, shift):
    y = x * lax.rsqrt(jnp.mean(x * x, axis=-1, keepdims=True) + RMS_EPS) * nw
    return y * (1.0 + scale) + shift


def _inproj_kernel(x_ref, nw_ref, sc_ref, sh_ref, wa_ref, wb_ref, wc_ref, pa_ref, pb_ref, pc_ref):
    h = _norm_mod(x_ref[...], nw_ref[...], sc_ref[0], sh_ref[0]).astype(BF16)
    pa_ref[...] = lax.dot_general(h, wa_ref[...], _NT, preferred_element_type=F32)
    pb_ref[...] = lax.dot_general(h, wb_ref[...], _NT, preferred_element_type=F32)
    pc_ref[...] = lax.dot_general(h, wc_ref[...], _NT, preferred_element_type=F32)


def _mod_spec(mod, tm, tiles_per_seq):
    if mod.shape[1] == 1:
        return pl.BlockSpec((1, 1, mod.shape[2]), lambda i, *_: (i // tiles_per_seq, 0, 0))
    return pl.BlockSpec((1, tm, mod.shape[2]), lambda i, *_: (i, 0, 0))


def _inproj_call(x, nw, scale, shift, wa, wb, wc, tm, tiles_per_seq):
    rows, d = x.shape
    full = lambda a: pl.BlockSpec(a.shape, lambda i: (0, 0), pipeline_mode=pl.Buffered(1))
    outs = [jax.ShapeDtypeStruct((rows, w.shape[0]), F32) for w in (wa, wb, wc)]
    return pl.pallas_call(
        _inproj_kernel,
        grid=(rows // tm,),
        in_specs=[
            pl.BlockSpec((tm, d), lambda i: (i, 0)),
            full(nw),
            _mod_spec(scale, tm, tiles_per_seq),
            _mod_spec(shift, tm, tiles_per_seq),
            full(wa), full(wb), full(wc),
        ],
        out_specs=[pl.BlockSpec((tm, w.shape[0]), lambda i: (i, 0)) for w in (wa, wb, wc)],
        out_shape=outs,
        compiler_params=pltpu.CompilerParams(
            dimension_semantics=("parallel",), vmem_limit_bytes=VMEM_LIMIT),
        name="inproj",
    )(x, nw, scale, shift, wa, wb, wc)


def _lastrow_kernel(x_ref, nw_ref, sc_ref, sh_ref, hprev_ref, wa_ref, h_ref, pprev_ref):
    h_ref[...] = _norm_mod(x_ref[...], nw_ref[...], sc_ref[...], sh_ref[...])
    pprev_ref[...] = lax.dot_general(hprev_ref[...].astype(BF16), wa_ref[...], _NT,
                                     preferred_element_type=F32)


def _lastrow_call(x_last, nw, scale, shift, h_prev, wa):
    b, d = x_last.shape
    return pl.pallas_call(
        _lastrow_kernel,
        out_shape=[jax.ShapeDtypeStruct((b, d), F32), jax.ShapeDtypeStruct((b, wa.shape[0]), F32)],
        compiler_params=pltpu.CompilerParams(vmem_limit_bytes=VMEM_LIMIT),
        name="lastrow",
    )(x_last, nw, scale, shift, h_prev, wa)


def _rwkv_kernel(p_ref, pprev_ref, s0_ref, mu_ref, vec_ref, wl_ref, g_ref,
                 y_ref, sout_ref, s_scr, prev_scr, *, bt):
    c = CHUNK
    ci = pl.program_id(1)
    nci = pl.num_programs(1)
    aw = 3 * PAIR
    rows = _iota((c, 1), 0)
    lane = _iota((1, PAIR), 1)
    vec = vec_ref[...]
    w0, a0, k_k, k_a, r_k, ln_w, ln_b = (vec[i:i + 1] for i in range(7))
    gpair = g_ref[...]
    tri = (rows >= _iota((1, c), 1)).astype(BF16)
    r2 = _iota((2 * c, 1), 0) & (c - 1)
    c2 = _iota((1, 2 * c), 1) & (c - 1)
    bd = _pair_blockdiag_mask()
    strict = bd & (r2 > c2)
    incl = bd & (r2 >= c2)

    @pl.when(ci == 0)
    def _():
        for bi in range(bt):
            for pr in range(3):
                s_scr[bi, pr] = _load_pair_state(s0_ref, bi, pr)
            prev_scr[bi] = pprev_ref[bi]

    shifted = []
    for bi in range(bt):
        p = p_ref[bi]
        prev = pltpu.roll(p, 1, axis=0)
        prev = jnp.where(rows == 0, prev_scr[bi], prev)
        prev_scr[bi] = p[c - 1:c]
        shifted.append(p + mu_ref[...] * (prev - p))
    xs = jnp.concatenate(shifted, axis=0)
    r = xs[:, 0:aw]
    k = xs[:, aw:2 * aw]
    v = xs[:, 2 * aw:3 * aw]
    tail = xs[:, 3 * aw:3 * aw + PAIR]
    act = jnp.where(lane < 32, jnp.tanh(tail), jnp.where(lane < 64, tail, _sigmoid(tail)))
    lo = _dot(act, wl_ref[...])
    w = -_softplus(-(w0 + lo[:, 0:aw])) - 0.5
    ld = -jnp.exp(w)
    a = _sigmoid(a0 + lo[:, aw:2 * aw])
    g = lo[:, 2 * aw:3 * aw]
    kk = k * k_k
    kk = kk * lax.rsqrt(_seg_sum(kk * kk, gpair) + L2_EPS)
    k = k * (1.0 + (a - 1.0) * k_a)
    cums = [_dot_left01(tri, ld[c * bi:c * (bi + 1)]) for bi in range(bt)]
    cum = jnp.concatenate(cums, axis=0)
    cum_last = jnp.concatenate([jnp.broadcast_to(x[c - 1:c], x.shape) for x in cums], axis=0)
    e_neg = jnp.exp(-cum)
    e_end = jnp.exp(cum_last - cum)
    pre = dict(a_t=-kk * jnp.exp(cum - ld), b_t=kk * a * e_neg, k_t=k * e_neg, r_t=r * jnp.exp(cum), v=v,
               b_end=kk * a * e_end, k_end=k * e_end)

    chains = [(bi, pr) for bi in range(bt) for pr in range(3)]
    pair = lambda name: [pre[name][c * bi:c * (bi + 1), PAIR * pr:PAIR * (pr + 1)] for bi, pr in chains]
    a_t, b_t, k_t, r_t, v_p = pair("a_t"), pair("b_t"), pair("k_t"), pair("r_t"), pair("v")
    b_end, k_end = pair("b_end"), pair("k_end")
    d_end = [jnp.exp(cums[bi][c - 1:c, PAIR * pr:PAIR * (pr + 1)]) for bi, pr in chains]
    v_st = [_stack_heads(x) for x in v_p]
    gs = [_dot(jnp.concatenate([_stack_heads(a), _stack_heads(r)], axis=0),
               jnp.concatenate([b, b, k, k], axis=0), _NT)
          for a, r, b, k in zip(a_t, r_t, b_t, k_t)]
    a_ak = [jnp.where(strict, g[0:2 * c, 2 * c:4 * c], 0.0) for g in gs]
    p_rbk = [jnp.concatenate([jnp.where(incl, g[2 * c:4 * c, 0:2 * c], 0.0),
                              jnp.where(incl, g[2 * c:4 * c, 2 * c:4 * c], 0.0)], axis=1) for g in gs]
    tinv = _tri_inverse_many([jnp.where(strict, g[0:2 * c, 0:2 * c], 0.0) for g in gs], c)
    s = [s_scr[bi, pr] for bi, pr in chains]
    ars = [_dot(jnp.concatenate([a, r], axis=0), s_, _NT) for a, r, s_ in zip(a_t, r_t, s)]
    akv = [_dot(m, x) for m, x in zip(a_ak, v_st)]
    u_st = [_dot(t, _stack_heads(x[0:c]) + y) for t, x, y in zip(tinv, ars, akv)]
    y_st = [_dot(p, jnp.concatenate([u, x], axis=0)) for p, u, x in zip(p_rbk, u_st, v_st)]
    ds = [_dot(jnp.concatenate([_fold_heads(u, c), x], axis=0), jnp.concatenate([b_, k_], axis=0), _TN)
          for u, x, b_, k_ in zip(u_st, v_p, b_end, k_end)]
    for i, (bi, pr) in enumerate(chains):
        s_scr[bi, pr] = s[i] * d_end[i] + jnp.where(bd, ds[i], 0.0)

    y = jnp.concatenate(
        [jnp.concatenate([ars[3 * bi + pr][c:2 * c] + _fold_heads(y_st[3 * bi + pr], c) for pr in range(3)],
                         axis=1) for bi in range(bt)], axis=0)
    mean = _seg_sum(y, gpair) * (1.0 / HEAD)
    dy = y - mean
    var = _seg_sum(dy * dy, gpair) * (1.0 / HEAD)
    yn = dy * lax.rsqrt(var + GN_EPS) * ln_w + ln_b
    bonus = _seg_sum(r * k * r_k, gpair) * v
    y_ref[...] = ((yn + bonus) * g).reshape(bt, c, aw)

    @pl.when(ci == nci - 1)
    def _():
        for bi in range(bt):
            for pr in range(3):
                _store_pair_state(sout_ref, bi, pr, s_scr[bi, pr])


def _rwkv_call(p_a, p_prev, s0, mu, vec, wl, gmat, *, bt):
    b, tp, acols = p_a.shape
    nh = s0.shape[1]
    aw = nh * HEAD
    cin = CHUNK
    nc = tp // cin
    const = lambda a: pl.BlockSpec(a.shape, lambda i, j: (0,) * a.ndim)
    kern = functools.partial(_rwkv_kernel, bt=bt)
    return pl.pallas_call(
        kern,
        grid=(b // bt, nc),
        in_specs=[
            pl.BlockSpec((bt, cin, acols), lambda i, j: (i, j, 0)),
            pl.BlockSpec((bt, 1, acols), lambda i, j: (i, 0, 0)),
            pl.BlockSpec((bt, nh, HEAD, HEAD), lambda i, j: (i, 0, 0, 0)),
            const(mu), const(vec), const(wl), const(gmat),
        ],
        out_specs=[
            pl.BlockSpec((bt, cin, aw), lambda i, j: (i, j, 0)),
            pl.BlockSpec((bt, nh, HEAD, HEAD), lambda i, j: (i, 0, 0, 0)),
        ],
        out_shape=[jax.ShapeDtypeStruct((b, tp, aw), F32),
                   jax.ShapeDtypeStruct(s0.shape, F32)],
        scratch_shapes=[pltpu.VMEM((bt, nh // 2, PAIR, PAIR), F32),
                        pltpu.VMEM((bt, 1, acols), F32)],
        compiler_params=pltpu.CompilerParams(
            dimension_semantics=("parallel", "arbitrary"), vmem_limit_bytes=VMEM_LIMIT),
        name="rwkv",
    )(p_a, p_prev.reshape(b, 1, acols), s0, mu, vec, wl, gmat)


def _gdn_kernel(p_ref, conv0_ref, s0_ref, cw_ref, tp_ref, nw_ref, g_ref, bsel_ref, gsel_ref,
                y_ref, sout_ref, s_scr, cbuf, *, bt):
    c = CHUNK
    ci = pl.program_id(1)
    nci = pl.num_programs(1)
    bw = 3 * PAIR
    qkvw = 3 * bw
    rows = _iota((c, 1), 0)
    lane = _iota((1, PAIR), 1)
    nh = 2 * 3
    gpair = g_ref[...]
    tri = (rows >= _iota((1, c), 1)).astype(BF16)
    r2 = _iota((2 * c, 1), 0) & (c - 1)
    c2 = _iota((1, 2 * c), 1) & (c - 1)
    same =(_iota((2 * c, 1), 0) >> 6) == (_iota((1, 2 * c), 1) >> 6)
    strict = same & (r2 > c2)
    incl = same & (r2 >= c2)
    bd = _pair_blockdiag_mask()
    is_a = lane < HEAD
    a_log = tp_ref[0:1]
    dt_bias = tp_ref[1:2]
    is_beta = lane < nh
    is_g = (lane >= nh) & (lane < 2 * nh)

    @pl.when(ci == 0)
    def _():
        for bi in range(bt):
            for pr in range(3):
                s_scr[bi, pr] = _load_pair_state(s0_ref, bi, pr)
            cbuf[bi, 0:8] = jnp.zeros((8, qkvw), F32)
            cbuf[bi, 8 - (CONV_K - 1):8] = conv0_ref[bi]

    convs = []
    for bi in range(bt):
        cbuf[bi, 8:8 + c] = p_ref[bi, :, 0:qkvw]
        conv = None
        for j in range(CONV_K):
            t = cw_ref[j:j + 1] * cbuf[bi, pl.ds(8 - (CONV_K - 1) + j, c), :]
            conv = t if conv is None else conv + t
        cbuf[bi, 0:8] = cbuf[bi, c:c + 8]
        convs.append(conv)
    conv = jnp.concatenate(convs, axis=0)
    p = p_ref[...].reshape(bt * c, p_ref.shape[2])
    qkv = conv * _sigmoid(conv)
    q = qkv[:, 0:bw]
    k = qkv[:, bw:2 * bw]
    v = qkv[:, 2 * bw:3 * bw]
    z = p[:, qkvw:qkvw + bw]
    tail = p[:, qkvw + bw:qkvw + bw + PAIR]
    q = q * (lax.rsqrt(_seg_sum(q * q, gpair) + L2_EPS) * (HEAD ** -0.5))
    k = k * lax.rsqrt(_seg_sum(k * k, gpair) + L2_EPS)
    beta = jnp.where(is_beta, _sigmoid(tail), 0.0)
    gl = jnp.where(is_g, -jnp.exp(a_log) * _softplus(tail + dt_bias), 0.0)
    gcum128 = jnp.concatenate([_dot_left01(tri, gl[c * bi:c * (bi + 1)]) for bi in range(bt)], axis=0)
    gcum = _dot_sel(gcum128, gsel_ref[...])
    beta = _dot_sel(beta, bsel_ref[...])
    g_last = jnp.concatenate(
        [jnp.broadcast_to(gcum[c * (bi + 1) - 1:c * (bi + 1)], (c, bw)) for bi in range(bt)], axis=0)
    eg = jnp.exp(gcum)
    kb = k * beta
    pre = dict(q=q, k=k, kb=kb, gcum=gcum, kg=kb * eg, qg=q * eg, kd=k * jnp.exp(g_last - gcum),
               vb=v * beta, d_end=jnp.exp(g_last))

    chains = [(bi, pr) for bi in range(bt) for pr in range(3)]
    pair = lambda name: [pre[name][c * bi:c * (bi + 1), PAIR * pr:PAIR * (pr + 1)] for bi, pr in chains]
    q, k, kb, gcum, kg, qg = pair("q"), pair("k"), pair("kb"), pair("gcum"), pair("kg"), pair("qg")
    kd, vb = pair("kd"), pair("vb")
    d_end = [x[0:1] for x in pair("d_end")]

    def col(x):
        sw = pltpu.roll(x, HEAD, axis=1)
        return jnp.concatenate([jnp.where(is_a, x, sw), jnp.where(is_a, sw, x)], axis=0)

    gcol = [col(x) for x in gcum]
    diff = [x - x.T for x in gcol]
    dec = [jnp.exp(jnp.where(incl, x, 0.0)) for x in diff]
    dec_s = [jnp.where(strict, x, 0.0) for x in dec]
    dec_i = [jnp.where(incl, x, 0.0) for x in dec]
    sc = [_dot(jnp.concatenate([_stack_heads(x), _stack_heads(y)], axis=0),
               jnp.concatenate([w, w], axis=0), _NT)
          for x, y, w in zip(kb, q, k)]
    tinv = _tri_inverse_many([-(x[0:2 * c] * d) for x, d in zip(sc, dec_s)], c)
    qk = [x[2 * c:4 * c] * d for x, d in zip(sc, dec_i)]
    s = [s_scr[bi, pr] for bi, pr in chains]
    kqs = [_dot(jnp.concatenate([x, y], axis=0), s_) for x, y, s_ in zip(kg, qg, s)]
    v_new = [_dot(t, _stack_heads(x - y[0:c])) for t, x, y in zip(tinv, vb, kqs)]
    o_st = [_dot(x, y) for x, y in zip(qk, v_new)]
    ds = [_dot(x, _fold_heads(y, c), _TN) for x, y in zip(kd, v_new)]
    for i, (bi, pr) in enumerate(chains):
        s_scr[bi, pr] = s[i] * d_end[i] + jnp.where(bd, ds[i], 0.0)

    o = jnp.concatenate(
        [jnp.concatenate([kqs[3 * bi + pr][c:2 * c] + _fold_heads(o_st[3 * bi + pr], c) for pr in range(3)],
                         axis=1) for bi in range(bt)], axis=0)
    o = o * lax.rsqrt(_seg_sum(o * o, gpair) * (1.0 / HEAD) + RMS_EPS) * nw_ref[...]
    y_ref[...] = (o * (z * _sigmoid(z))).reshape(bt, c, bw)

    @pl.when(ci == nci - 1)
    def _():
        for bi in range(bt):
            for pr in range(3):
                _store_pair_state(sout_ref, bi, pr, s_scr[bi, pr])


def _gdn_call(p_b, conv0, s0, cw, tailp, nw, gmat, bsel, gsel, *, bt):
    b, tp, bcols = p_b.shape
    nh = s0.shape[1]
    bw = nh * HEAD
    cin = CHUNK
    nc = tp // cin
    const = lambda a: pl.BlockSpec(a.shape, lambda i, j: (0,) * a.ndim)
    kern = functools.partial(_gdn_kernel, bt=bt)
    return pl.pallas_call(
        kern,
        grid=(b // bt, nc),
        in_specs=[
            pl.BlockSpec((bt, cin, bcols), lambda i, j: (i, j, 0)),
            pl.BlockSpec((bt, CONV_K - 1, 3 * bw), lambda i, j: (i, 0, 0)),
            pl.BlockSpec((bt, nh, HEAD, HEAD), lambda i, j: (i, 0, 0, 0)),
            const(cw), const(tailp), const(nw), const(gmat), const(bsel), const(gsel),
        ],
        out_specs=[
            pl.BlockSpec((bt, cin, bw), lambda i, j: (i, j, 0)),
            pl.BlockSpec((bt, nh, HEAD, HEAD), lambda i, j: (i, 0, 0, 0)),
        ],
        out_shape=[jax.ShapeDtypeStruct((b, tp, bw), F32),
                   jax.ShapeDtypeStruct(s0.shape, F32)],
        scratch_shapes=[pltpu.VMEM((bt, nh // 2, PAIR, PAIR), F32),
                        pltpu.VMEM((bt, CHUNK + 8, 3 * bw), F32)],
        compiler_params=pltpu.CompilerParams(
            dimension_semantics=("parallel", "arbitrary"), vmem_limit_bytes=VMEM_LIMIT),
        name="gdn",
    )(p_b, conv0, s0, cw, tailp, nw, gmat, bsel, gsel)


def _hgrn_kernel(p_ref, s0_ref, lbl_ref, nw_ref, g_ref, y_ref, sout_ref, s_scr,
                 *, bt, layer):
    c = CHUNK
    ci = pl.program_id(1)
    nci = pl.num_programs(1)
    cw = 2 * PAIR
    rows = _iota((c, 1), 0)
    cols = _iota((1, c), 1)
    sameblk = (rows >> 4) == (cols >> 4)
    tri_tot = jnp.concatenate([sameblk & (rows >= cols), sameblk], axis=0).astype(BF16)
    rows_b = _iota((HBLK, 1), 0)
    bd = _pair_blockdiag_mask()
    gpair = g_ref[...]

    logits = lbl_ref[...]
    ex = jnp.exp(logits - jnp.max(logits, axis=0, keepdims=True))
    gam = ex / jnp.sum(ex, axis=0, keepdims=True)
    lb = jnp.sum(gam[0:layer + 1], axis=0, keepdims=True) - gam[0:1]

    @pl.when(ci == 0)
    def _():
        for bi in range(bt):
            for pr in range(2):
                s_scr[bi, pr] = _load_pair_state(s0_ref, bi, pr).T

    pre = []
    for bi in range(bt):
        p = p_ref[bi]
        qp = p[:, 0:cw]
        f = p[:, cw:2 * cw]
        v = p[:, 2 * cw:3 * cw]
        z = p[:, 3 * cw:4 * cw]
        q = qp * _sigmoid(qp)
        logf = jnp.log(lb + (1.0 - lb) * _sigmoid(f))
        k = (1.0 - lb) * _sigmoid(-f)
        cums = _dot_left01(tri_tot, logf)
        bl = cums[0:c]
        btot = cums[c:2 * c]
        pre.append(dict(q=q, k=k, v=v, z=z, bl=bl, q_in=q * jnp.exp(bl),
                        k_out=k * jnp.exp(btot - bl), d_blk=jnp.exp(btot)))

    chains = [(bi, pr) for bi in range(bt) for pr in range(2)]
    nblk = c // HBLK

    def blk_of(name, bi, pr, blk):
        return pre[bi][name][HBLK * blk:HBLK * (blk + 1), PAIR * pr:PAIR * (pr + 1)]

    intra = {}
    ds = {}
    for blk in range(nblk):
        for bi, pr in chains:
            qb, kb, vb, bb = (blk_of(n, bi, pr, blk) for n in ("q", "k", "v", "bl"))
            first = [0 if j < 8 else 8 for j in range(HBLK)]
            xs = []
            for j, lo in enumerate(first):
                e = jnp.exp(jnp.minimum(bb[lo:] - bb[j:j + 1], 0.0))
                xs.append(qb[lo:] * e * kb[j:j + 1])
            att = _dot_sel(jnp.concatenate(xs, axis=0), gpair, pieces=1)
            acc = [None, None]
            off = 0
            for j, lo in enumerate(first):
                n = HBLK - lo
                t = jnp.where(rows_b[lo:] >= j, att[off:off + n], 0.0) * vb[j:j + 1]
                off += n
                for half, part in enumerate([t[0:8], t[8:16]] if lo == 0 else [None, t]):
                    if part is not None:
                        acc[half] = part if acc[half] is None else acc[half] + part
            intra[bi, pr, blk] = jnp.concatenate(acc, axis=0)
            ds[bi, pr, blk] = jnp.where(bd, _dot(vb, blk_of("k_out", bi, pr, blk), _TN), 0.0)

    s = {ch: s_scr[ch] for ch in chains}
    outs = {}
    for blk in range(nblk):
        for bi, pr in chains:
            outs[bi, pr, blk] = intra[bi, pr, blk] + _dot(blk_of("q_in", bi, pr, blk), s[bi, pr], _NT)
            s[bi, pr] = s[bi, pr] * blk_of("d_blk", bi, pr, blk)[0:1] + ds[bi, pr, blk]
    for ch in chains:
        s_scr[ch] = s[ch]

    o = jnp.concatenate(
        [jnp.concatenate([outs[bi, pr, blk] for bi in range(bt) for blk in range(nblk)], axis=0)
         for pr in range(2)], axis=1)
    z = jnp.concatenate([pre[bi]["z"] for bi in range(bt)], axis=0)
    o = o * lax.rsqrt(_seg_sum(o * o, gpair) * (1.0 / HEAD) + RMS_EPS) * nw_ref[...]
    y_ref[...] = (o * _sigmoid(z)).reshape(bt, c, cw)

    @pl.when(ci == nci - 1)
    def _():
        for bi in range(bt):
            for pr in range(2):
                _store_pair_state(sout_ref, bi, pr, s_scr[bi, pr].T)


def _hgrn_call(p_c, s0, lbl, nw, gmat, *, bt, layer):
    b, tp, ccols = p_c.shape
    nh = s0.shape[1]
    cw = nh * HEAD
    cin = CHUNK
    nc = tp // cin
    const = lambda a: pl.BlockSpec(a.shape, lambda i, j: (0,) * a.ndim)
    kern = functools.partial(_hgrn_kernel, bt=bt, layer=layer)
    return pl.pallas_call(
        kern,
        grid=(b // bt, nc),
        in_specs=[
            pl.BlockSpec((bt, cin, ccols), lambda i, j: (i, j, 0)),
            pl.BlockSpec((bt, nh, HEAD, HEAD), lambda i, j: (i, 0, 0, 0)),
            const(lbl), const(nw), const(gmat),
        ],
        out_specs=[
            pl.BlockSpec((bt, cin, cw), lambda i, j: (i, j, 0)),
            pl.BlockSpec((bt, nh, HEAD, HEAD), lambda i, j: (i, 0, 0, 0)),
        ],
        out_shape=[jax.ShapeDtypeStruct((b, tp, cw), F32),
                   jax.ShapeDtypeStruct(s0.shape, F32)],
        scratch_shapes=[pltpu.VMEM((bt, nh // 2, PAIR, PAIR), F32)],
        compiler_params=pltpu.CompilerParams(
            dimension_semantics=("parallel", "arbitrary"), vmem_limit_bytes=VMEM_LIMIT),
        name="hgrn",
    )(p_c, s0, lbl, nw, gmat)


def _inproj_t_kernel(x_ref, nw_ref, sc_ref, sh_ref, hprev_ref, wta_ref, wtb_ref, wtc_ref,
                     pta_ref, ptb_ref, ptc_ref, rows_ref, hlast_ref, pprev_ref):
    h = _norm_mod(x_ref[...], nw_ref[...], sc_ref[...], sh_ref[...])
    hb = h.astype(BF16)
    nt = lambda w, a: lax.dot_general(w, a, _NT, preferred_element_type=F32)
    pta_ref[...] = nt(wta_ref[...], hb)
    ptb_ref[...] = nt(wtb_ref[...], hb)
    ptc_ref[...] = nt(wtc_ref[...], hb)
    rows_ref[...] = nt(hb, wtb_ref[0:rows_ref.shape[1], :])
    nb = hlast_ref.shape[0]
    hlast_ref[...] = h[h.shape[0] - nb:]
    pprev_ref[...] = nt(wta_ref[...], hprev_ref[...].astype(BF16))


def _inproj_t_call(x, nw, scale, shift, h_prev, wta, wtb, wtc, qkvw):
    rows, d = x.shape
    nb = h_prev.shape[0]
    sds = jax.ShapeDtypeStruct
    return pl.pallas_call(
        _inproj_t_kernel,
        out_shape=[sds((wta.shape[0], rows), F32), sds((wtb.shape[0], rows), F32),
                   sds((wtc.shape[0], rows), F32), sds((rows, qkvw), F32),
                   sds((nb, d), F32), sds((wta.shape[0], nb), F32)],
        compiler_params=pltpu.CompilerParams(vmem_limit_bytes=VMEM_LIMIT),
        name="inproj_t",
    )(x, nw, scale, shift, h_prev, wta, wtb, wtc)


def _col_sum(x):
    return jnp.sum(x, axis=0, keepdims=True)


def _rwkv_t_kernel(r_ref, k_ref, v_ref, tail_ref, rp_ref, kp_ref, vp_ref, tailp_ref,
                   mur_ref, muk_ref, muv_ref, mut_ref, vec_ref, wl_ref, s0_ref, *rest, n_prev):
    y_ref, sout_ref, v_scr, y_scr = _place_prev_states(rest, n_prev)
    nb = s0_ref.shape[-1]
    nt = r_ref.shape[-1] // nb
    w0, a0, k_k, k_a, r_k, ln_w, ln_b = (vec_ref[i, 0] for i in range(7))
    sout_ref[...] = s0_ref[...]

    def shifted(ref, pref, mu, idx, t):
        cur = ref[idx, :, t * nb:(t + 1) * nb]
        prv = pref[idx] if t == 0 else ref[idx, :, (t - 1) * nb:t * nb]
        return cur + mu * (prv - cur)

    for t in range(nt):
        r = shifted(r_ref, rp_ref, mur_ref[0], 0, t)
        k = shifted(k_ref, kp_ref, muk_ref[0], 0, t)
        v = shifted(v_ref, vp_ref, muv_ref[0], 0, t)
        t0 = shifted(tail_ref, tailp_ref, mut_ref[0], 0, t)
        t1 = shifted(tail_ref, tailp_ref, mut_ref[1], 1, t)
        act = jnp.concatenate([jnp.tanh(t0[0:32]), t0[32:64], _sigmoid(t1)], axis=0)
        lo = jnp.dot(wl_ref[0], act.astype(BF16), preferred_element_type=F32)
        w = -_softplus(-(w0 + lo[0:HEAD])) - 0.5
        dec = jnp.exp(-jnp.exp(w))
        a = _sigmoid(a0 + lo[HEAD:2 * HEAD])
        g = lo[2 * HEAD:3 * HEAD]
        kk = k * k_k
        kk = kk * lax.rsqrt(_col_sum(kk * kk) + L2_EPS)
        k = k * (1.0 + (a - 1.0) * k_a)
        a_t = -kk
        b_t = kk * a
        v_scr[...] = v

        def body(g, carry):
            base = pl.multiple_of(g * 8, 8)
            vg = v_scr[pl.ds(base, 8), :]
            ys = []
            for j in range(8):
                sv = sout_ref[0, base + j]
                sa = _col_sum(sv * a_t)
                sv = sv * dec + sa * b_t + vg[j:j + 1] * k
                sout_ref[0, base + j] = sv
                ys.append(_col_sum(sv * r))
            y_scr[pl.ds(base, 8), :] = jnp.concatenate(ys, axis=0)
            return carry

        lax.fori_loop(0, HEAD // 8, body, 0)
        y = y_scr[...]
        mean = _col_sum(y) * (1.0 / HEAD)
        dy = y - mean
        var = _col_sum(dy * dy) * (1.0 / HEAD)
        yn = dy * lax.rsqrt(var + GN_EPS) * ln_w + ln_b
        bonus = _col_sum(r * k * r_k) * v
        y_ref[0, :, t * nb:(t + 1) * nb] = (yn + bonus) * g


def _head_blocks(arr3, first, n=1):
    if n == 1:
        return pl.BlockSpec((1,) + arr3.shape[1:], lambda h: (first + h, 0, 0))
    return pl.BlockSpec((n,) + arr3.shape[1:], lambda h: (first // n, 0, 0))


def _place_prev_states(rest, n_prev):
    prev, (y_ref, sout_all), scratch = rest[:n_prev], rest[n_prev:n_prev + 2], rest[n_prev + 2:]
    for j, p in enumerate(prev):
        sout_all[j] = p[...]
    return (y_ref, sout_all.at[n_prev], *scratch)


def _layer_state_specs(s_all, layer, prev):
    nh, nb = s_all.shape[1], s_all.shape[-1]
    slots = len(prev) + 1
    spec_in = pl.BlockSpec((None, 1, HEAD, HEAD, nb), lambda h: (layer, h, 0, 0, 0))
    spec_prev = [pl.BlockSpec((1, HEAD, HEAD, nb), lambda h: (h, 0, 0, 0)) for _ in prev]
    spec_out = pl.BlockSpec((slots, 1, HEAD, HEAD, nb), lambda h: (0, h, 0, 0, 0))
    return spec_in, spec_prev, spec_out, jax.ShapeDtypeStruct((slots, nh, HEAD, HEAD, nb), F32)


def _rwkv_t_call(pta, pprev_t, s_all, layer, prev, mu, vec, wl_t):
    nh, nb = s_all.shape[1], s_all.shape[-1]
    rows = pta.shape[1]
    p3 = pta.reshape(-1, HEAD, rows)
    pp3 = pprev_t.reshape(-1, HEAD, nb)
    mu3 = mu.reshape(-1, HEAD, 1)
    tail0 = 3 * nh
    state, state_prev, state_out, state_shape = _layer_state_specs(s_all, layer, prev)
    return pl.pallas_call(
        functools.partial(_rwkv_t_kernel, n_prev=len(prev)),
        grid=(nh,),
        in_specs=[
            _head_blocks(p3, 0), _head_blocks(p3, nh), _head_blocks(p3, 2 * nh), _head_blocks(p3, tail0, 2),
            _head_blocks(pp3, 0), _head_blocks(pp3, nh), _head_blocks(pp3, 2 * nh), _head_blocks(pp3, tail0, 2),
            _head_blocks(mu3, 0), _head_blocks(mu3, nh), _head_blocks(mu3, 2 * nh), _head_blocks(mu3, tail0, 2),
            pl.BlockSpec((vec.shape[0], 1, HEAD, 1), lambda h: (0, h, 0, 0)),
            pl.BlockSpec((1,) + wl_t.shape[1:], lambda h: (h, 0, 0)),
            state, *state_prev,
        ],
        out_specs=[pl.BlockSpec((1, HEAD, rows), lambda h: (h, 0, 0)), state_out],
        out_shape=[jax.ShapeDtypeStruct((nh, HEAD, rows), F32), state_shape],
        scratch_shapes=[pltpu.VMEM((HEAD, nb), F32), pltpu.VMEM((HEAD, nb), F32)],
        compiler_params=pltpu.CompilerParams(
            dimension_semantics=("parallel",), vmem_limit_bytes=VMEM_LIMIT),
        name="rwkv_t",
    )(p3, p3, p3, p3, pp3, pp3, pp3, pp3, mu3, mu3, mu3, mu3, vec, wl_t, s_all, *prev)


def _gdn_t_kernel(q_ref, k_ref, v_ref, z_ref, tail_ref, cq_ref, ck_ref, cv_ref, wq_ref, wk_ref, wv_ref,
                  sc_ref, nw_ref, s0_ref, *rest, n_prev):
    y_ref, sout_ref, k_scr, q_scr = _place_prev_states(rest, n_prev)
    h = pl.program_id(0)
    nh = pl.num_programs(0)
    nb = s0_ref.shape[-1]
    nt = q_ref.shape[-1] // nb
    a_log = sc_ref[0, 0:1, 0:1]
    dt_bias = sc_ref[0, 1:2, 0:1]
    sout_ref[...] = s0_ref[...]

    def conv(x_ref, c_ref, w_ref, t):
        acc = None
        for j in range(CONV_K):
            u = t + j - (CONV_K - 1)
            tap = c_ref[u + CONV_K - 1, 0] if u < 0 else x_ref[0, :, u * nb:(u + 1) * nb]
            term = w_ref[j, 0] * tap
            acc = term if acc is None else acc + term
        return acc * _sigmoid(acc)

    for t in range(nt):
        sl = slice(t * nb, (t + 1) * nb)
        q = conv(q_ref, cq_ref, wq_ref, t)
        k = conv(k_ref, ck_ref, wk_ref, t)
        v = conv(v_ref, cv_ref, wv_ref, t)
        q = q * (lax.rsqrt(_col_sum(q * q) + L2_EPS) * (HEAD ** -0.5))
        k = k * lax.rsqrt(_col_sum(k * k) + L2_EPS)
        tl = tail_ref[0:2 * 8, sl]
        pick = lambda row: _col_sum(jnp.where(_iota((2 * 8, 1), 0) == row, tl, 0.0))
        beta = _sigmoid(pick(h))
        eg = jnp.exp(-jnp.exp(a_log) * _softplus(pick(nh + h) + dt_bias))
        k_scr[...] = k
        q_scr[...] = q

        def body1(g, acc):
            base = pl.multiple_of(g * 8, 8)
            kg = k_scr[pl.ds(base, 8), :]
            for j in range(8):
                acc = acc + sout_ref[0, base + j] * kg[j:j + 1]
            return acc

        sk = lax.fori_loop(0, HEAD // 8, body1, jnp.zeros((HEAD, nb), F32))
        v_new = beta * (v - eg * sk)

        def body2(g, acc):
            base = pl.multiple_of(g * 8, 8)
            kg = k_scr[pl.ds(base, 8), :]
            qg = q_scr[pl.ds(base, 8), :]
            for j in range(8):
                s = eg * sout_ref[0, base + j] + kg[j:j + 1] * v_new
                sout_ref[0, base + j] = s
                acc = acc + s * qg[j:j + 1]
            return acc

        o = lax.fori_loop(0, HEAD // 8, body2, jnp.zeros((HEAD, nb), F32))
        o = o * lax.rsqrt(_col_sum(o * o) * (1.0 / HEAD) + RMS_EPS) * nw_ref[...]
        z = z_ref[0, :, sl]
        y_ref[0, :, sl] = o * (z * _sigmoid(z))


def _gdn_t_call(ptb, conv_t, s_all, layer, prev, cw, scal, nw):
    nh, nb = s_all.shape[1], s_all.shape[-1]
    rows = ptb.shape[1]
    p3 = ptb.reshape(-1, HEAD, rows)
    c4 = conv_t.reshape(CONV_K - 1, -1, HEAD, nb)
    w4 = cw.reshape(CONV_K, -1, HEAD, 1)
    tap = lambda first: pl.BlockSpec((CONV_K - 1, 1, HEAD, nb), lambda h: (0, first + h, 0, 0))
    wsp = lambda first: pl.BlockSpec((CONV_K, 1, HEAD, 1), lambda h: (0, first + h, 0, 0))
    state, state_prev, state_out, state_shape = _layer_state_specs(s_all, layer, prev)
    tail_blk = 4 * nh * HEAD // PAIR
    return pl.pallas_call(
        functools.partial(_gdn_t_kernel, n_prev=len(prev)),
        grid=(nh,),
        in_specs=[
            _head_blocks(p3, 0), _head_blocks(p3, nh), _head_blocks(p3, 2 * nh), _head_blocks(p3, 3 * nh),
            pl.BlockSpec((PAIR, rows), lambda h: (tail_blk, 0)),
            tap(0), tap(nh), tap(2 * nh), wsp(0), wsp(nh), wsp(2 * nh),
            pl.BlockSpec((1,) + scal.shape[1:], lambda h: (h, 0, 0)),
            pl.BlockSpec(nw.shape, lambda h: (0, 0)),
            state, *state_prev,
        ],
        out_specs=[pl.BlockSpec((1, HEAD, rows), lambda h: (h, 0, 0)), state_out],
        out_shape=[jax.ShapeDtypeStruct((nh, HEAD, rows), F32), state_shape],
        scratch_shapes=[pltpu.VMEM((HEAD, nb), F32), pltpu.VMEM((HEAD, nb), F32)],
        compiler_params=pltpu.CompilerParams(
            dimension_semantics=("parallel",), vmem_limit_bytes=VMEM_LIMIT),
        name="gdn_t",
    )(p3, p3, p3, p3, ptb, c4, c4, c4, w4, w4, w4, scal, nw, s_all, *prev)


def _hgrn_t_kernel(q_ref, f_ref, v_ref, z_ref, lbl_ref, nw_ref, s0_ref, *rest, layer, n_prev):
    y_ref, sout_ref, f_scr, k_scr, q_scr = _place_prev_states(rest, n_prev)
    nb = s0_ref.shape[-1]
    nt = q_ref.shape[-1] // nb
    logits = lbl_ref[:, 0]
    ex = jnp.exp(logits - jnp.max(logits, axis=0, keepdims=True))
    gam = ex / jnp.sum(ex, axis=0, keepdims=True)
    lb = jnp.sum(gam[0:layer + 1], axis=0) - gam[0]
    sout_ref[...] = s0_ref[...]

    for t in range(nt):
        sl = slice(t * nb, (t + 1) * nb)
        qp = q_ref[0, :, sl]
        f = f_ref[0, :, sl]
        v = v_ref[0, :, sl]
        z = z_ref[0, :, sl]
        q_scr[...] = qp * _sigmoid(qp)
        f_scr[...] = lb + (1.0 - lb) * _sigmoid(f)
        k_scr[...] = (1.0 - lb) * _sigmoid(-f)

        def body(g, acc):
            base = pl.multiple_of(g * 8, 8)
            fg, kg, qg = (ref[pl.ds(base, 8), :] for ref in (f_scr, k_scr, q_scr))
            for j in range(8):
                s = fg[j:j + 1] * sout_ref[0, base + j] + kg[j:j + 1] * v
                sout_ref[0, base + j] = s
                acc = acc + qg[j:j + 1] * s
            return acc

        o = lax.fori_loop(0, HEAD // 8, body, jnp.zeros((HEAD, nb), F32))
        o = o * lax.rsqrt(_col_sum(o * o) * (1.0 / HEAD) + RMS_EPS) * nw_ref[...]
        y_ref[0, :, sl] = o * _sigmoid(z)


def _hgrn_t_call(ptc, s_all, lbl4, nw, layer, prev):
    nh, nb = s_all.shape[1], s_all.shape[-1]
    rows = ptc.shape[1]
    p3 = ptc.reshape(-1, HEAD, rows)
    state, state_prev, state_out, state_shape = _layer_state_specs(s_all, layer, prev)
    return pl.pallas_call(
        functools.partial(_hgrn_t_kernel, layer=layer, n_prev=len(prev)),
        grid=(nh,),
        in_specs=[
            _head_blocks(p3, 0), _head_blocks(p3, nh), _head_blocks(p3, 2 * nh), _head_blocks(p3, 3 * nh),
            pl.BlockSpec((lbl4.shape[0], 1, HEAD, 1), lambda h: (0, h, 0, 0)),
            pl.BlockSpec(nw.shape, lambda h: (0, 0)),
            state, *state_prev,
        ],
        out_specs=[pl.BlockSpec((1, HEAD, rows), lambda h: (h, 0, 0)), state_out],
        out_shape=[jax.ShapeDtypeStruct((nh, HEAD, rows), F32), state_shape],
        scratch_shapes=[pltpu.VMEM((HEAD, nb), F32)] * 3,
        compiler_params=pltpu.CompilerParams(
            dimension_semantics=("parallel",), vmem_limit_bytes=VMEM_LIMIT),
        name="hgrn_t",
    )(p3, p3, p3, p3, lbl4, nw, s_all, *prev)


def _ffn_kernel(x_ref, ya_ref, yb_ref, yc_ref, woa_ref, wob_ref, woc_ref, gm_ref, nw_ref,
                sc_ref, sh_ref, gf_ref, win_ref, wd_ref, fnw_ref, xo_ref, *yo_ref, th, y_transposed):
    ff = wd_ref.shape[0]
    dims = _TN if y_transposed else _NN
    proj = lambda y_ref, w_ref: lax.dot_general(y_ref[...].astype(BF16), w_ref[...], dims,
                                                preferred_element_type=F32)
    mix = proj(ya_ref, woa_ref) + proj(yb_ref, wob_ref) + proj(yc_ref, woc_ref)
    x1 = x_ref[...] + gm_ref[0] * mix
    h = _norm_mod(x1, nw_ref[...], sc_ref[0], sh_ref[0]).astype(BF16)
    acc = None
    for j in range(ff // th):
        gate = jnp.dot(h, win_ref[:, j * th:(j + 1) * th], preferred_element_type=F32)
        up = jnp.dot(h, win_ref[:, ff + j * th:ff + (j + 1) * th], preferred_element_type=F32)
        act = (gate * _sigmoid(gate) * up).astype(BF16)
        t = jnp.dot(act, wd_ref[j * th:(j + 1) * th, :], preferred_element_type=F32)
        acc = t if acc is None else acc + t
    xo = x1 + gf_ref[0] * acc
    xo_ref[...] = xo
    if yo_ref:
        yo_ref[0][...] = xo * lax.rsqrt(jnp.mean(xo * xo, axis=-1, keepdims=True) + RMS_EPS) * fnw_ref[...]


def _ffn_call(x, ya, yb, yc, w_out, gate_m, nw, scale_f, shift_f, gate_f,
              w_in, w_down, fnw, *, layer, tm, th, tiles_per_seq, final, y_transposed=False):
    rows, d = x.shape
    row = lambda a: pl.BlockSpec((tm, a.shape[1]), lambda i: (i, 0))
    yspec = (lambda a: pl.BlockSpec((a.shape[0], tm), lambda i: (0, i))) if y_transposed else row
    const = lambda a: pl.BlockSpec(a.shape, lambda i: (0, 0), pipeline_mode=pl.Buffered(1))
    of_layer = lambda a: pl.BlockSpec((None,) + a.shape[1:], lambda i: (layer, 0, 0),
                                      pipeline_mode=pl.Buffered(1))
    widths = [y.shape[0] if y_transposed else y.shape[1] for y in (ya, yb, yc)]
    starts = [0, widths[0], widths[0] + widths[1]]
    assert all(s % n == 0 for s, n in zip(starts, widths))
    wo_rows = lambda s, n: pl.BlockSpec((None, n, d), lambda i: (layer, s // n, 0), pipeline_mode=pl.Buffered(1))
    mod = lambda a: _mod_spec(a, tm, tiles_per_seq)
    n_out = 2 if final else 1
    return pl.pallas_call(
        functools.partial(_ffn_kernel, th=th, y_transposed=y_transposed),
        grid=(rows // tm,),
        in_specs=[
            row(x), yspec(ya), yspec(yb), yspec(yc), *[wo_rows(s, n) for s, n in zip(starts, widths)],
            mod(gate_m), const(nw), mod(scale_f), mod(shift_f), mod(gate_f),
            of_layer(w_in), of_layer(w_down), const(fnw),
        ],
        out_specs=[row(x)] * n_out,
        out_shape=[jax.ShapeDtypeStruct((rows, d), F32)] * n_out,
        compiler_params=pltpu.CompilerParams(
            dimension_semantics=("parallel",), vmem_limit_bytes=VMEM_LIMIT),
        name="ffn",
    )(x, ya, yb, yc, w_out, w_out, w_out, gate_m, nw, scale_f, shift_f, gate_f, w_in, w_down, fnw)


def _block_ones(n, blk):
    i = np.arange(n) // blk
    return jnp.asarray(i[:, None] == i[None, :], BF16)


def _head_select(first_lane, nh):
    m = np.zeros((PAIR, nh * HEAD), np.float32)
    for h in range(nh):
        m[first_lane + h, h * HEAD:(h + 1) * HEAD] = 1.0
    return jnp.asarray(m, BF16)


def _lane_place(x, first_lane):
    return jnp.zeros((PAIR,), F32).at[first_lane:first_lane + x.shape[0]].set(x)


def _run_chunked(x, mods, states, wts, *, tm, tm_ffn, th, bt):
    b, t, d = x.shape
    rows = b * t
    assert t % CHUNK == 0 and t % tm == 0 and t % tm_ffn == 0 and b % bt == 0 and t >= CONV_K - 1
    shift0, rwkv0, conv0, gdn0, hgrn0 = states
    nl = len(wts)
    new = ([], [], [], [], [])
    xr = x.reshape(rows, d)
    for l in range(nl):
        w = wts[l]
        m = mods[l]
        shift_m, scale_m, gate_m, shift_f, scale_f, gate_f = (m[:, i].reshape(b, 1, d) for i in range(6))
        p_a, p_b, p_c = _inproj_call(xr, w["norm_mix"], scale_m, shift_m, w["wta"], w["wtb"], w["wtc"],
                                     tm, t // tm)
        h_last, p_prev = _lastrow_call(xr.reshape(b, t, d)[:, t - 1], w["norm_mix"], m[:, 1], m[:, 0],
                                       shift0[l], w["wta"])
        p_a = p_a.reshape(b, t, -1)
        p_b = p_b.reshape(b, t, -1)
        p_c = p_c.reshape(b, t, -1)
        y_a, s_a = _rwkv_call(p_a, p_prev, rwkv0[l], w["mu"], w["rwkv_vec"], w["lora"], w["gpair"], bt=bt)
        y_b, s_b = _gdn_call(p_b, conv0[l], gdn0[l], w["conv_w"], w["gdn_tail"], w["gdn_norm"],
                             w["gpair"], w["bsel"], w["gsel"], bt=bt)
        y_c, s_c = _hgrn_call(p_c, hgrn0[l], w["lb_logits"], w["hgrn_norm"], w["gpair"], bt=bt, layer=l)
        conv_new = p_b[:, t - (CONV_K - 1):, :conv0.shape[-1]]
        res = _ffn_call(xr, y_a.reshape(rows, -1), y_b.reshape(rows, -1), y_c.reshape(rows, -1),
                        w["w_out"], gate_m, w["norm_ffn"], scale_f, shift_f, gate_f,
                        w["w_ffn_in"], w["w_ffn_out"], w["final_norm"],
                        layer=l, tm=tm_ffn, th=th, tiles_per_seq=t // tm_ffn, final=l == nl - 1)
        xr = res[0]
        for acc, s in zip(new, (h_last, s_a, conv_new, s_b, s_c)):
            acc.append(s)
    return res[1].reshape(b, t, d), [jnp.stack(acc) for acc in new]


def _run_steps(x, mods, states, wts, *, tm_ffn, th):
    b, t, d = x.shape
    rows = t * b
    tm_ffn = min(tm_ffn, rows)
    shift0, rwkv0, conv0, gdn0, hgrn0 = states
    to_lanes = lambda s: jnp.transpose(s, (0, 2, 3, 4, 1))
    rwkv_t, gdn_t, hgrn_t = to_lanes(rwkv0), to_lanes(gdn0), to_lanes(hgrn0)
    conv_t = jnp.transpose(conv0, (0, 2, 3, 1))
    nl = len(wts)
    new = ([], [], [], [], [])
    xr = jnp.transpose(x, (1, 0, 2)).reshape(rows, d)
    for l in range(nl):
        w = wts[l]
        m = mods[l]
        mod = lambda i, m=m: jnp.tile(m[:, i], (t, 1))
        tiled = lambda a: a.reshape(rows // tm_ffn, tm_ffn, d)
        pta, ptb, ptc, qkv_rows, h_last, pprev_t = _inproj_t_call(
            xr, w["norm_mix"], mod(1), mod(0), shift0[l], w["wta"], w["wtb"], w["wtc"], w["qkvw"])
        prev = [[s[0] for s in new[i]] if l == nl - 1 else [] for i in (1, 3, 4)]
        y_a, s_a = _rwkv_t_call(pta, pprev_t, rwkv_t, l, prev[0], w["mu"], w["rwkv_vec_t"], w["lora_t"])
        y_b, s_b = _gdn_t_call(ptb, conv_t[l], gdn_t, l, prev[1], w["conv_w_t"], w["gdn_scal"], w["gdn_norm_t"])
        y_c, s_c = _hgrn_t_call(ptc, hgrn_t, w["lb_logits_t"], w["hgrn_norm_t"], l, prev[2])
        xp = jnp.concatenate([jnp.transpose(conv0[l], (1, 0, 2)), qkv_rows.reshape(t, b, -1)], axis=0)
        conv_new = jnp.transpose(xp[-(CONV_K - 1):], (1, 0, 2))
        res = _ffn_call(xr, y_a.reshape(-1, rows), y_b.reshape(-1, rows), y_c.reshape(-1, rows),
                        w["w_out"], tiled(mod(2)), w["norm_ffn"],
                        tiled(mod(4)), tiled(mod(3)), tiled(mod(5)),
                        w["w_ffn_in"], w["w_ffn_out"], w["final_norm"],
                        layer=l, tm=tm_ffn, th=th, tiles_per_seq=1, final=l == nl - 1, y_transposed=True)
        xr = res[0]
        for acc, s in zip(new, (h_last, s_a, conv_new, s_b, s_c)):
            acc.append(s)
    y = jnp.transpose(res[1].reshape(t, b, d), (1, 0, 2))
    from_lanes = lambda s: jnp.transpose(s[-1], (0, 4, 1, 2, 3))
    return y, [jnp.stack(new[0]), from_lanes(new[1]), jnp.stack(new[2]), from_lanes(new[3]), from_lanes(new[4])]


def kernel(x_prompt, x_sample, c_prompt, c_sample, state_rwkv_shift, state_rwkv, state_gdn_conv, state_gdn, state_hgrn, w_ada, b_ada, norm_mix_w, w_in, rwkv_mu, rwkv_w0, rwkv_w2, rwkv_a0, rwkv_a2, rwkv_g2, rwkv_k_k, rwkv_k_a, rwkv_r_k, rwkv_ln_w, rwkv_ln_b, gdn_conv_w, gdn_A_log, gdn_dt_bias, gdn_norm_w, hgrn_lb_logits, hgrn_norm_w, w_out, norm_ffn_w, w_ffn_in, w_ffn_out, final_norm_w):
    nl, d, _ = w_in.shape
    a_heads = state_rwkv.shape[2]
    b_heads = state_gdn.shape[2]
    c_heads = state_hgrn.shape[2]
    aw, bw, cw = a_heads * HEAD, b_heads * HEAD, c_heads * HEAD
    lw, la, lg = rwkv_w2.shape[1], rwkv_a2.shape[1], rwkv_g2.shape[1]
    a_cols = 3 * aw + lw + la + lg
    qkvw = 3 * bw
    b_cols = qkvw + 2 * b_heads + bw
    assert (aw, bw, cw) == (3 * PAIR, 3 * PAIR, 2 * PAIR) and lw + la + lg == PAIR and lw == 32 and la == 32

    bp, tpr, _ = x_prompt.shape
    bs, ts, _ = x_sample.shape

    gpair = _block_ones(PAIR, HEAD)
    bsel = _head_select(0, b_heads)
    gsel = _head_select(b_heads, b_heads)
    w_out_bf, w_ffn_in_bf, w_ffn_out_bf = (a.astype(BF16) for a in (w_out, w_ffn_in, w_ffn_out))
    wts = []
    for l in range(nl):
        wit = jnp.swapaxes(w_in[l], 0, 1).astype(BF16)
        wbt = wit[a_cols:a_cols + b_cols]
        wtb = jnp.concatenate(
            [wbt[:qkvw], wbt[qkvw + 2 * b_heads:], wbt[qkvw:qkvw + 2 * b_heads],
             jnp.zeros((PAIR - 2 * b_heads, d), BF16)], axis=0)
        lora = jnp.zeros((PAIR, 3 * aw), F32)
        lora = lora.at[0:lw, 0:aw].set(rwkv_w2[l])
        lora = lora.at[lw:lw + la, aw:2 * aw].set(rwkv_a2[l])
        lora = lora.at[lw + la:, 2 * aw:].set(rwkv_g2[l])
        rwkv_vec = jnp.stack([rwkv_w0[l], rwkv_a0[l], rwkv_k_k[l], rwkv_k_a[l], rwkv_r_k[l].reshape(-1),
                              rwkv_ln_w[l], rwkv_ln_b[l], jnp.zeros((aw,), F32)])
        hcols = lambda a: a.reshape(a.shape[0], a_heads, HEAD).transpose(1, 2, 0)
        lora_t = jnp.concatenate([
            jnp.pad(hcols(rwkv_w2[l]), ((0, 0), (0, 0), (0, la + lg))),
            jnp.pad(hcols(rwkv_a2[l]), ((0, 0), (0, 0), (lw, lg))),
            jnp.pad(hcols(rwkv_g2[l]), ((0, 0), (0, 0), (lw + la, 0)))], axis=1)
        steps = dict(
            rwkv_vec_t=rwkv_vec[:7].reshape(7, a_heads, HEAD, 1), lora_t=lora_t.astype(BF16),
            conv_w_t=gdn_conv_w[l],
            gdn_scal=jnp.zeros((b_heads, 8, PAIR), F32)
                .at[:, 0].set(jnp.broadcast_to(gdn_A_log[l][:, None], (b_heads, PAIR)))
                .at[:, 1].set(jnp.broadcast_to(gdn_dt_bias[l][:, None], (b_heads, PAIR))),
            gdn_norm_t=gdn_norm_w[l].reshape(HEAD, 1),
            lb_logits_t=hgrn_lb_logits.reshape(nl, c_heads, HEAD, 1),
            hgrn_norm_t=hgrn_norm_w[l].reshape(HEAD, 1),
        )
        wts.append(dict(
            **steps,
            norm_mix=norm_mix_w[l].reshape(1, d),
            wta=wit[:a_cols], wtb=wtb, wtc=wit[a_cols + b_cols:], qkvw=qkvw,
            mu=rwkv_mu[l].reshape(1, a_cols), rwkv_vec=rwkv_vec, lora=lora.astype(BF16),
            gpair=gpair, bsel=bsel, gsel=gsel,
            conv_w=jnp.concatenate([gdn_conv_w[l], jnp.zeros((8 - CONV_K, qkvw), F32)], axis=0),
            gdn_tail=jnp.zeros((8, PAIR), F32).at[0].set(_lane_place(gdn_A_log[l], b_heads))
                                              .at[1].set(_lane_place(gdn_dt_bias[l], b_heads)),
            gdn_norm=jnp.tile(gdn_norm_w[l], b_heads).reshape(1, bw),
            lb_logits=hgrn_lb_logits,
            hgrn_norm=jnp.tile(hgrn_norm_w[l], c_heads).reshape(1, cw),
            w_out=w_out_bf, norm_ffn=norm_ffn_w[l].reshape(1, d),
            w_ffn_in=w_ffn_in_bf, w_ffn_out=w_ffn_out_bf,
            final_norm=final_norm_w.reshape(1, d),
        ))

    mod = _ada_call(jnp.concatenate([c_prompt, c_sample], axis=0), w_ada, b_ada)
    mod = mod.reshape(nl, bp + bs, 6, d)
    mods_p = [mod[l, :bp] for l in range(nl)]
    mods_s = [mod[l, bp:] for l in range(nl)]

    zeros = lambda s: jnp.zeros((nl, bp) + s.shape[2:], s.dtype)
    states_p = tuple(zeros(s) for s in (state_rwkv_shift, state_rwkv, state_gdn_conv, state_gdn, state_hgrn))
    states_s = (state_rwkv_shift, state_rwkv, state_gdn_conv, state_gdn, state_hgrn)

    th = w_ffn_out.shape[1] // FFN_HIDDEN_TILES
    y_p, new_p = _run_chunked(x_prompt, mods_p, states_p, wts, tm=TM_INPROJ, tm_ffn=TM_FFN, th=th, bt=SEQS_PER_STEP)
    y_s, new_s = _run_steps(x_sample, mods_s, states_s, wts, tm_ffn=TM_FFN_STEPS, th=th)
    return (y_p, y_s, *new_p, *new_s)
```
